```python
import math
import jax
import jax.numpy as jnp
from jax import lax
import numpy as np


D_MODEL = 2048
BATCH = 2
SEQ = 4096
DEPTH = 2
DEC_BATCH = 1
DEC_SEQ = 8192
PAST_LEN = 128

NORM_EPS = 1e-6
NEG_INF = -1e30

SSM_HEADDIM = 64
SSM_INNER = 3 * D_MODEL // 4
SSM_HEADS = SSM_INNER // SSM_HEADDIM
SSM_GROUPS = 4
SSM_STATE = 128
SSM_CONV = 5
SSM_CHUNK = 128
SSM_XBC = SSM_INNER + 2 * SSM_GROUPS * SSM_STATE
SSM_DT_MIN = 1e-3
SSM_DT_MAX = 1e-1

S5_WIDTH = D_MODEL // 2
S5_GROUP = 16
S5_GROUPS = S5_WIDTH // S5_GROUP
S5_STATE = 64
S5_DT_MIN = 1e-3
S5_DT_MAX = 1e-1

ATT_HEAD_DIM = 128
ATT_HEADS_PER_GROUP = 4
ATT_PATTERNS = ((128, 1), (512, 4), (2048, 16))
ATT_GROUPS = 3
ATT_HEADS = ATT_GROUPS * ATT_HEADS_PER_GROUP
ATT_WIDTH = ATT_HEADS * ATT_HEAD_DIM
ATT_OUT_WIDTH = ATT_HEADS_PER_GROUP * ATT_HEAD_DIM
REL_BUCKETS = 32
REL_MAX_DIST = 1024

N_BRANCHES = 3
IN_WIDTH = SSM_INNER + SSM_XBC + 2 * SSM_HEADS + S5_WIDTH + 3 * ATT_WIDTH + N_BRANCHES * D_MODEL

D_FF = ((8 * D_MODEL // 3 + 127) // 128) * 128
FFN_CONV = 3

kernel_name = "hybrid_ssd_s5_dilated_encoder"


def rms_norm(x, gain):
    xf = x.astype(jnp.float32)
    xf = xf * lax.rsqrt(jnp.mean(xf * xf, axis=-1, keepdims=True) + NORM_EPS)
    return (xf * gain.astype(jnp.float32)).astype(x.dtype)


def depthwise_conv_centred(x, w, b):
    width, ch = w.shape
    pad = width // 2
    y = lax.conv_general_dilated(x, w[:, None, :].astype(x.dtype), window_strides=(1,),
                                 padding=[(pad, pad)], dimension_numbers=('NWC', 'WIO', 'NWC'),
                                 feature_group_count=ch)
    return y + b.astype(x.dtype)


def ssd_chunked(xd, a_dt, b_mat, c_mat):
    bsz, seqlen, nh, hp = xd.shape
    ng, ns = b_mat.shape[2], b_mat.shape[3]
    nr = nh // ng
    tc = SSM_CHUNK
    nc = seqlen // tc
    xd = xd.reshape(bsz, nc, tc, ng, nr, hp)
    a = a_dt.astype(jnp.float32).reshape(bsz, nc, tc, ng, nr)
    bm = b_mat.reshape(bsz, nc, tc, ng, ns)
    cm = c_mat.reshape(bsz, nc, tc, ng, ns)
    a_cs = jnp.cumsum(a, axis=2)
    seg = a_cs[:, :, :, None] - a_cs[:, :, None, :]
    tril = np.tril(np.ones((tc, tc), dtype=bool))[:, :, None, None]
    decay = jnp.exp(jnp.where(tril, seg, -jnp.inf))
    cb = jnp.einsum('bclgn,bcsgn->bclsg', cm, bm)
    y_diag = jnp.einsum('bclsgr,bcsgrp->bclgrp', cb[..., None] * decay, xd)
    decay_to_end = jnp.exp(a_cs[:, :, -1:] - a_cs)
    states = jnp.einsum('bclgn,bclgrp->bcgrpn', bm, xd * decay_to_end[..., None])
    chunk_decay = jnp.exp(a_cs[:, :, -1])

    def step(h, inp):
        st, dec = inp
        return h * dec[..., None, None] + st, h

    _, h_in = lax.scan(step, jnp.zeros_like(states[:, 0]),
                       (jnp.moveaxis(states, 1, 0), jnp.moveaxis(chunk_decay, 1, 0)))
    h_in = jnp.moveaxis(h_in, 0, 1)
    y_off = jnp.einsum('bclgn,bcgrpn->bclgrp', cm, h_in) * jnp.exp(a_cs)[..., None]
    return (y_diag + y_off).reshape(bsz, seqlen, nh, hp)


def mamba2_bidirectional(z, xbc, dt_raw, conv_w, conv_b, a_log, dt_bias, d_skip, norm_g, w_out):
    bsz, seqlen, _ = z.shape
    xbc = jax.nn.silu(depthwise_conv_centred(xbc, conv_w, conv_b))
    xs, bm, cm = jnp.split(xbc, [SSM_INNER, SSM_INNER + SSM_GROUPS * SSM_STATE], axis=-1)
    xs = xs.reshape(bsz, seqlen, SSM_HEADS, SSM_HEADDIM)
    bm = bm.reshape(bsz, seqlen, SSM_GROUPS, SSM_STATE)
    cm = cm.reshape(bsz, seqlen, SSM_GROUPS, SSM_STATE)
    dt = jax.nn.softplus(dt_raw.astype(jnp.float32).reshape(bsz, seqlen, 2, SSM_HEADS)
                         + dt_bias.astype(jnp.float32))
    a = -jnp.exp(a_log.astype(jnp.float32))
    y_fwd = ssd_chunked(xs * dt[:, :, 0, :, None], a[0] * dt[:, :, 0], bm, cm)
    flip = lambda t: jnp.flip(t, axis=1)
    y_bwd = flip(ssd_chunked(flip(xs * dt[:, :, 1, :, None]), flip(a[1] * dt[:, :, 1]), flip(bm), flip(cm)))
    y = y_fwd + y_bwd + xs * d_skip[:, None]
    y = y.reshape(bsz, seqlen, SSM_INNER).astype(z.dtype) * jax.nn.silu(z)
    return rms_norm(y, norm_g) @ w_out


def s5_bidirectional(u, a_re, a_im, log_step, b_re, b_im, c_re, c_im, d_skip, w_glu):
    bsz, seqlen, _ = u.shape
    ug = u.astype(jnp.float32).reshape(bsz, seqlen, S5_GROUPS, S5_GROUP)
    lam = lax.complex(a_re.astype(jnp.float32), a_im.astype(jnp.float32))
    step = jnp.exp(log_step.astype(jnp.float32))[..., None]
    lam_bar = jnp.exp(lam * step)
    b = lax.complex(b_re.astype(jnp.float32), b_im.astype(jnp.float32))
    c = lax.complex(c_re.astype(jnp.float32), c_im.astype(jnp.float32))
    b_bar = ((lam_bar - 1.0) / lam)[..., None] * b

    def combine(e1, e2):
        a1, s1 = e1
        a2, s2 = e2
        return a1 * a2, a2 * s1 + s2

    def run(direction, reverse):
        bu = jnp.einsum('gps,blgs->blgp', b_bar[direction], ug)
        aa = jnp.broadcast_to(lam_bar[direction], bu.shape)
        _, states = lax.associative_scan(combine, (aa, bu), reverse=reverse, axis=1)
        return jnp.real(jnp.einsum('gsp,blgp->blgs', c[direction], states))

    y = run(0, False) + run(1, True) + ug * d_skip.astype(jnp.float32).reshape(S5_GROUPS, S5_GROUP)
    y = jax.nn.gelu(y.reshape(bsz, seqlen, S5_WIDTH)).astype(u.dtype)
    val, gate = jnp.split(y @ w_glu, 2, axis=-1)
    return val * jax.nn.sigmoid(gate)


def t5_bucket(rel):
    half = REL_BUCKETS // 2
    exact = half // 2
    sign = (rel > 0).astype(np.int32) * half
    n = np.abs(rel)
    large = exact + (np.log(np.maximum(n, 1) / exact) / np.log(REL_MAX_DIST / exact)
                     * (half - exact)).astype(np.int32)
    large = np.minimum(large, half - 1)
    return sign + np.where(n < exact, n, large)


def dilated_window_attention(q, k, v, bias_table, half, dil):
    bsz, seqlen, nh, dh = q.shape
    n = seqlen // dil
    blk = half
    nb = -(-n // blk)
    pad = nb * blk - n

    def to_sub(t):
        return t.reshape(bsz, n, dil, nh, dh).transpose(0, 2, 1, 3, 4)

    qs = jnp.pad(to_sub(q), ((0, 0), (0, 0), (0, pad), (0, 0), (0, 0))).reshape(bsz, dil, nb, blk, nh, dh)

    def windows(t):
        tp = jnp.pad(to_sub(t), ((0, 0), (0, 0), (blk, pad + blk), (0, 0), (0, 0)))
        tp = tp.reshape(bsz, dil, nb + 2, blk, nh, dh)
        return jnp.concatenate([tp[:, :, :-2], tp[:, :, 1:-1], tp[:, :, 2:]], axis=3)

    ks, vs = windows(k), windows(v)
    qi = np.arange(blk)[:, None]
    kj = np.arange(3 * blk)[None, :] - blk
    rel_sub = kj - qi
    key_idx = np.arange(nb)[:, None, None] * blk + kj[None]
    valid = (np.abs(rel_sub) <= half)[None] & (key_idx >= 0) & (key_idx < n)
    bias = jnp.moveaxis(bias_table[t5_bucket(rel_sub * dil)], -1, 0).astype(jnp.float32)
    logits = jnp.einsum('brnqhd,brnkhd->brnhqk', qs, ks,
                        preferred_element_type=jnp.float32) * (dh ** -0.5) + bias
    logits = jnp.where(valid[:, None], logits, NEG_INF)
    m = jnp.max(logits, axis=-1, keepdims=True)
    p = jnp.exp(logits - m)
    s = jnp.sum(p, axis=-1)
    o = jnp.einsum('brnhqk,brnkhd->brnqhd', p, vs.astype(jnp.float32))
    o = o / jnp.swapaxes(s, -1, -2)[..., None]
    lse = jnp.swapaxes(m[..., 0] + jnp.log(s), -1, -2)

    def from_sub(t):
        t = t.reshape((bsz, dil, nb * blk) + t.shape[4:])[:, :, :n]
        return jnp.swapaxes(t, 1, 2).reshape((bsz, seqlen) + t.shape[3:])

    return from_sub(o), from_sub(lse)


def dilated_attention_mixture(q, k, v, rel_bias, w_out):
    bsz, seqlen = q.shape[:2]
    outs, lses = [], []
    for g, (window, dil) in enumerate(ATT_PATTERNS):
        heads = slice(g * ATT_HEADS_PER_GROUP, (g + 1) * ATT_HEADS_PER_GROUP)
        o, lse = dilated_window_attention(q[:, :, heads], k[:, :, heads], v[:, :, heads],
                                          rel_bias[:, heads], window // (2 * dil), dil)
        outs.append(o)
        lses.append(lse)
    wts = jax.nn.softmax(jnp.stack(lses), axis=0)
    comb = jnp.sum(wts[..., None] * jnp.stack(outs), axis=0)
    return comb.reshape(bsz, seqlen, ATT_OUT_WIDTH).astype(q.dtype) @ w_out


def trunk(x, rel_bias, norm_mix, w_in, ssm_conv_w, ssm_conv_b, ssm_a_log, ssm_dt_bias, ssm_d,
          ssm_norm, ssm_w_out, s5_a_re, s5_a_im, s5_log_step, s5_b_re, s5_b_im, s5_c_re, s5_c_im,
          s5_d, s5_w_glu, att_w_out, w_o, norm_ffn, w_up, ffn_conv_w, ffn_conv_b, w_down, final_norm):
    bsz, seqlen, _ = x.shape
    splits = np.cumsum([SSM_INNER, SSM_XBC, 2 * SSM_HEADS, S5_WIDTH, ATT_WIDTH, ATT_WIDTH, ATT_WIDTH]).tolist()
    for i in range(DEPTH):
        h = rms_norm(x, norm_mix[i])
        z, xbc, dt_raw, u, q, k, v, gates = jnp.split(h @ w_in[i], splits, axis=-1)
        a_out = mamba2_bidirectional(z, xbc, dt_raw, ssm_conv_w[i], ssm_conv_b[i], ssm_a_log[i],
                                     ssm_dt_bias[i], ssm_d[i], ssm_norm[i], ssm_w_out[i]).astype(x.dtype)
        b_out = s5_bidirectional(u, s5_a_re[i], s5_a_im[i], s5_log_step[i], s5_b_re[i], s5_b_im[i],
                                 s5_c_re[i], s5_c_im[i], s5_d[i], s5_w_glu[i]).astype(x.dtype)
        head_shape = (bsz, seqlen, ATT_HEADS, ATT_HEAD_DIM)
        c_out = dilated_attention_mixture(q.reshape(head_shape), k.reshape(head_shape), v.reshape(head_shape),
                                          rel_bias, att_w_out[i]).astype(x.dtype)
        g = jax.nn.sigmoid(gates.astype(jnp.float32)).reshape(bsz, seqlen, N_BRANCHES, D_MODEL).astype(x.dtype)
        merged = g[:, :, 0] * a_out + g[:, :, 1] * b_out + g[:, :, 2] * c_out
        x = x + merged @ w_o[i]
        h = rms_norm(x, norm_ffn[i])
        gate, val = jnp.split(depthwise_conv_centred(h @ w_up[i], ffn_conv_w[i], ffn_conv_b[i]), 2, axis=-1)
        x = x + (jax.nn.silu(gate) * val) @ w_down[i]
    return rms_norm(x, final_norm)


def setup_inputs(seed: int = 0) -> dict:
    key = jax.random.key(seed)
    keys = iter(jax.random.split(key, 48))
    nrm = lambda shape, scale: scale * jax.random.normal(next(keys), shape, jnp.float32)
    unif = lambda shape, lo, hi: jax.random.uniform(next(keys), shape, jnp.float32, lo, hi)
    ssm_dt = jnp.exp(unif((DEPTH, 2, SSM_HEADS), math.log(SSM_DT_MIN), math.log(SSM_DT_MAX)))
    s5_im = jnp.broadcast_to(math.pi * jnp.arange(S5_STATE, dtype=jnp.float32), (DEPTH, 2, S5_GROUPS, S5_STATE))
    return {
        'x_prompt': nrm((BATCH, SEQ, D_MODEL), 1.0),
        'x_sample': nrm((DEC_BATCH, DEC_SEQ, D_MODEL), 1.0),
        'rel_bias': nrm((REL_BUCKETS, ATT_HEADS), 0.5),
        'norm_mix': 1.0 + nrm((DEPTH, D_MODEL), 0.02),
        'w_in': nrm((DEPTH, D_MODEL, IN_WIDTH), D_MODEL ** -0.5),
        'ssm_conv_w': nrm((DEPTH, SSM_CONV, SSM_XBC), SSM_CONV ** -0.5),
        'ssm_conv_b': nrm((DEPTH, SSM_XBC), 0.02),
        'ssm_a_log': jnp.log(unif((DEPTH, 2, SSM_HEADS), 1.0, 16.0)),
        'ssm_dt_bias': jnp.log(jnp.expm1(ssm_dt)),
        'ssm_d': 1.0 + nrm((DEPTH, SSM_HEADS), 0.1),
        'ssm_norm': 1.0 + nrm((DEPTH, SSM_INNER), 0.02),
        'ssm_w_out': nrm((DEPTH, SSM_INNER, D_MODEL), SSM_INNER ** -0.5),
        's5_a_re': -0.5 + nrm((DEPTH, 2, S5_GROUPS, S5_STATE), 0.01),
        's5_a_im': s5_im + nrm((DEPTH, 2, S5_GROUPS, S5_STATE), 0.01),
        's5_log_step': unif((DEPTH, 2, S5_GROUPS), math.log(S5_DT_MIN), math.log(S5_DT_MAX)),
        's5_b_re': nrm((DEPTH, S5_GROUPS, S5_STATE, S5_GROUP), (2 * S5_GROUP) ** -0.5),
        's5_b_im': nrm((DEPTH, S5_GROUPS, S5_STATE, S5_GROUP), (2 * S5_GROUP) ** -0.5),
        's5_c_re': nrm((DEPTH, 2, S5_GROUPS, S5_GROUP, S5_STATE), (2 * S5_STATE) ** -0.5),
        's5_c_im': nrm((DEPTH, 2, S5_GROUPS, S5_GROUP, S5_STATE), (2 * S5_STATE) ** -0.5),
        's5_d': nrm((DEPTH, S5_WIDTH), 1.0),
        's5_w_glu': nrm((DEPTH, S5_WIDTH, 2 * D_MODEL), S5_WIDTH ** -0.5),
        'att_w_out': nrm((DEPTH, ATT_OUT_WIDTH, D_MODEL), ATT_OUT_WIDTH ** -0.5),
        'w_o': nrm((DEPTH, D_MODEL, D_MODEL), D_MODEL ** -0.5),
        'norm_ffn': 1.0 + nrm((DEPTH, D_MODEL), 0.02),
        'w_up': nrm((DEPTH, D_MODEL, 2 * D_FF), D_MODEL ** -0.5),
        'ffn_conv_w': nrm((DEPTH, FFN_CONV, 2 * D_FF), FFN_CONV ** -0.5),
        'ffn_conv_b': nrm((DEPTH, 2 * D_FF), 0.02),
        'w_down': nrm((DEPTH, D_FF, D_MODEL), D_FF ** -0.5),
        'final_norm': 1.0 + nrm((D_MODEL,), 0.02),
    }


def reference(x_prompt, x_sample, rel_bias, norm_mix, w_in, ssm_conv_w, ssm_conv_b, ssm_a_log,
              ssm_dt_bias, ssm_d, ssm_norm, ssm_w_out, s5_a_re, s5_a_im, s5_log_step, s5_b_re,
              s5_b_im, s5_c_re, s5_c_im, s5_d, s5_w_glu, att_w_out, w_o, norm_ffn, w_up,
              ffn_conv_w, ffn_conv_b, w_down, final_norm):
    y_prompt = trunk(x_prompt, rel_bias, norm_mix, w_in, ssm_conv_w, ssm_conv_b, ssm_a_log, ssm_dt_bias,
                     ssm_d, ssm_norm, ssm_w_out, s5_a_re, s5_a_im, s5_log_step, s5_b_re, s5_b_im,
                     s5_c_re, s5_c_im, s5_d, s5_w_glu, att_w_out, w_o, norm_ffn, w_up, ffn_conv_w,
                     ffn_conv_b, w_down, final_norm)
    y_sample = trunk(x_sample, rel_bias, norm_mix, w_in, ssm_conv_w, ssm_conv_b, ssm_a_log, ssm_dt_bias,
                     ssm_d, ssm_norm, ssm_w_out, s5_a_re, s5_a_im, s5_log_step, s5_b_re, s5_b_im,
                     s5_c_re, s5_c_im, s5_d, s5_w_glu, att_w_out, w_o, norm_ffn, w_up, ffn_conv_w,
                     ffn_conv_b, w_down, final_norm)
    return (y_prompt, y_sample)
```

```python
import functools
import math

import numpy as np
import jax
import jax.numpy as jnp
from jax import lax
from jax.experimental import pallas as pl
from jax.experimental.pallas import tpu as pltpu

F32 = jnp.float32
BF16 = jnp.bfloat16

D_MODEL = 2048
NORM_EPS = 1e-6
NEG_INF = -1e30

SSM_HEADDIM = 64
SSM_INNER = 1536
SSM_HEADS = 24
SSM_GROUPS = 4
SSM_HPG = SSM_HEADS // SSM_GROUPS
SSM_STATE = 128
SSM_CONV = 5
SSM_CHUNK = 128
SSM_XBC = 2560

S5_WIDTH = 1024
S5_GROUP = 16
S5_GROUPS = 64
S5_STATE = 64
S5_CHUNK = 32
S5_CW = S5_CHUNK * S5_GROUP
S5_LAGS = 128
S5_SCAN_STEPS = 8

ATT_HEAD_DIM = 128
ATT_HPG = 4
ATT_PATTERNS = ((128, 1), (512, 4), (2048, 16))
ATT_GW = ATT_HPG * ATT_HEAD_DIM
ATT_HALF = 64
REL_BUCKETS = 32
REL_MAX_DIST = 1024

D_FF = 5504
D_FF_PAD = 5632
FFN_CONV = 3

COL_Z = 0
COL_XBC = 1536
COL_U = 4096
COL_Q = 5120
COL_K = 6656
COL_V = 8192
COL_GATES = 9728
IN_COLS = 15872
DT_COLS = 256

V7X_VMEM_BYTES = 64 * 1024 * 1024


def _cparams(sem, vmem_mb):
    return pltpu.CompilerParams(dimension_semantics=sem, vmem_limit_bytes=vmem_mb * 1024 * 1024)


def _sigmoid(x):
    return 1.0 / (1.0 + jnp.exp(-x))


def _silu(x):
    return x * _sigmoid(x)


def _split3(x):
    hi = x.astype(BF16)
    r1 = x - hi.astype(F32)
    mid = r1.astype(BF16)
    lo = (r1 - mid.astype(F32)).astype(BF16)
    return hi, mid, lo


def _dot_sel(x, sel):
    hi, mid, lo = _split3(x)
    d = lambda a: jnp.dot(a, sel, preferred_element_type=F32)
    return d(hi) + d(mid) + d(lo)


def _sel_dot(sel, x):
    hi, mid, lo = _split3(x)
    d = lambda a: jnp.dot(sel, a, preferred_element_type=F32)
    return d(hi) + d(mid) + d(lo)


def _dot_f32(a, b):
    ah, am, _ = _split3(a)
    bh, bm, _ = _split3(b)
    d = lambda x, y: jnp.dot(x, y, preferred_element_type=F32)
    return d(ah, bh) + d(ah, bm) + d(am, bh)


def _rmsnorm_kernel(x_ref, g_ref, o_ref):
    x = x_ref[...]
    ms = jnp.mean(x * x, axis=-1, keepdims=True)
    o_ref[...] = (x * lax.rsqrt(ms + NORM_EPS) * g_ref[...]).astype(o_ref.dtype)


def _rmsnorm(x, gain, out_dtype, tr=512):
    m, d = x.shape
    return pl.pallas_call(
        _rmsnorm_kernel,
        grid=(m // tr,),
        in_specs=[pl.BlockSpec((tr, d), lambda i: (i, 0)),
                  pl.BlockSpec((1, d), lambda i: (0, 0))],
        out_specs=pl.BlockSpec((tr, d), lambda i: (i, 0)),
        out_shape=jax.ShapeDtypeStruct((m, d), out_dtype),
        compiler_params=_cparams(("parallel",), 32),
        name="rmsnorm",
    )(x, gain.reshape(1, d))


def _mm_kernel(epilogue, n_extra, a_ref, b_ref, *rest):
    extras = rest[:n_extra]
    o_ref = rest[n_extra]
    acc = jnp.dot(a_ref[...], b_ref[...], preferred_element_type=F32)
    if epilogue is not None:
        acc = epilogue(acc, *[e[...] for e in extras])
    o_ref[...] = acc.astype(o_ref.dtype)


def _matmul(a, b, *, tm, tn, out_dtype, k=None, a_kblk=0, epilogue=None, extras=(), vmem_mb=48,
            name="matmul"):
    m = a.shape[0]
    kk, n = b.shape
    if k is None:
        k = a.shape[1]
    assert k == kk and m % tm == 0 and n % tn == 0
    in_specs = [pl.BlockSpec((tm, k), lambda i, j: (i, a_kblk)),
                pl.BlockSpec((k, tn), lambda i, j: (0, j))]
    args = [a, b]
    for arr, off in extras:
        in_specs.append(pl.BlockSpec((tm, tn), functools.partial(lambda i, j, o: (i, j + o), o=off)))
        args.append(arr)
    return pl.pallas_call(
        functools.partial(_mm_kernel, epilogue, len(extras)),
        grid=(m // tm, n // tn),
        in_specs=in_specs,
        out_specs=pl.BlockSpec((tm, tn), lambda i, j: (i, j)),
        out_shape=jax.ShapeDtypeStruct((m, n), out_dtype),
        compiler_params=_cparams(("parallel", "parallel"), vmem_mb),
        name=name,
    )(*args)


def _glu_kernel(a_ref, bv_ref, bg_ref, gate_ref, acc_ref, o_ref):
    a = a_ref[...]
    val = jnp.dot(a, bv_ref[...], preferred_element_type=F32)
    gate = jnp.dot(a, bg_ref[...], preferred_element_type=F32)
    branch = val * _sigmoid(gate)
    o_ref[...] = acc_ref[...] + _sigmoid(gate_ref[...].astype(F32)) * branch


def _glu_matmul(a, w, proj, merged, *, tm, tn, gate_blk):
    m, k = a.shape
    n = w.shape[1] // 2
    nb = n // tn
    return pl.pallas_call(
        _glu_kernel,
        grid=(m // tm, nb),
        in_specs=[pl.BlockSpec((tm, k), lambda i, j: (i, 0)),
                  pl.BlockSpec((k, tn), lambda i, j: (0, j)),
                  pl.BlockSpec((k, tn), lambda i, j: (0, j + nb)),
                  pl.BlockSpec((tm, tn), lambda i, j: (i, j + gate_blk)),
                  pl.BlockSpec((tm, tn), lambda i, j: (i, j))],
        out_specs=pl.BlockSpec((tm, tn), lambda i, j: (i, j)),
        out_shape=jax.ShapeDtypeStruct((m, n), F32),
        compiler_params=_cparams(("parallel", "parallel"), 48),
        name="s5_glu_matmul",
    )(a, w, w, proj, merged)


HALO = 16


def _seq_flags(row0, nrows, seq_bounds):
    starts = [s for s, _ in seq_bounds]
    ends = [s + l for s, l in seq_bounds]
    is_start = functools.reduce(jnp.logical_or, [row0 == s for s in starts])
    is_end = functools.reduce(jnp.logical_or, [row0 + nrows == e for e in ends])
    return is_start, is_end


def _conv_taps(buf_ref, w_ref, b_ref, width, tr):
    pad = width // 2
    acc = None
    for kk in range(width):
        term = buf_ref[pl.ds(HALO - pad + kk, tr), :] * w_ref[kk:kk + 1, :]
        acc = term if acc is None else acc + term
    return acc + b_ref[...]


def _fill_halo_buf(buf_ref, prev_ref, main_ref, next_ref, tr, seq_bounds):
    i = pl.program_id(0)
    is_start, is_end = _seq_flags(i * tr, tr, seq_bounds)
    prev = prev_ref[...].astype(F32)
    nxt = next_ref[...].astype(F32)
    buf_ref[0:HALO, :] = jnp.where(is_start, 0.0, prev)
    buf_ref[HALO:HALO + tr, :] = main_ref[...].astype(F32)
    buf_ref[HALO + tr:HALO + tr + HALO, :] = jnp.where(is_end, 0.0, nxt)


def _conv_silu_kernel(width, tr, seq_bounds, prev_ref, main_ref, next_ref, w_ref, b_ref, o_ref, buf_ref):
    _fill_halo_buf(buf_ref, prev_ref, main_ref, next_ref, tr, seq_bounds)
    o_ref[...] = _silu(_conv_taps(buf_ref, w_ref, b_ref, width, tr)).astype(o_ref.dtype)


def _halo_specs(tr, tc, col_off, nrows):
    hb = tr // HALO
    last = nrows // HALO - 1
    return [pl.BlockSpec((HALO, tc), lambda i, j: (jnp.maximum(i * hb - 1, 0), j + col_off)),
            pl.BlockSpec((tr, tc), lambda i, j: (i, j + col_off)),
            pl.BlockSpec((HALO, tc), lambda i, j: (jnp.minimum((i + 1) * hb, last), j + col_off))]


def _conv_silu(src, col_blk, ncols, w, b, seq_bounds, tr=512, tc=512):
    m = src.shape[0]
    width = w.shape[0]
    return pl.pallas_call(
        functools.partial(_conv_silu_kernel, width, tr, seq_bounds),
        grid=(m // tr, ncols // tc),
        in_specs=_halo_specs(tr, tc, col_blk, m) + [
            pl.BlockSpec((width, tc), lambda i, j: (0, j)),
            pl.BlockSpec((1, tc), lambda i, j: (0, j))],
        out_specs=pl.BlockSpec((tr, tc), lambda i, j: (i, j)),
        out_shape=jax.ShapeDtypeStruct((m, ncols), BF16),
        scratch_shapes=[pltpu.VMEM((tr + 2 * HALO, tc), F32)],
        compiler_params=_cparams(("parallel", "parallel"), 32),
        name="ssm_conv_silu",
    )(src, src, src, w, b.reshape(1, ncols))


def _ffn_act_kernel(tr, seq_bounds, gp_ref, gm_ref, gn_ref, vp_ref, vm_ref, vn_ref,
                    wg_ref, bg_ref, wv_ref, bv_ref, o_ref, gbuf_ref, vbuf_ref):
    _fill_halo_buf(gbuf_ref, gp_ref, gm_ref, gn_ref, tr, seq_bounds)
    _fill_halo_buf(vbuf_ref, vp_ref, vm_ref, vn_ref, tr, seq_bounds)
    gate = _conv_taps(gbuf_ref, wg_ref, bg_ref, FFN_CONV, tr)
    val = _conv_taps(vbuf_ref, wv_ref, bv_ref, FFN_CONV, tr)
    o_ref[...] = (_silu(gate) * val).astype(o_ref.dtype)


def _ffn_act(up, w, b, seq_bounds, tr=512, tc=512):
    m = up.shape[0]
    nb = D_FF_PAD // tc
    wspec = lambda off: pl.BlockSpec((FFN_CONV, tc), functools.partial(lambda i, j, o: (0, j + o), o=off))
    bspec = lambda off: pl.BlockSpec((1, tc), functools.partial(lambda i, j, o: (0, j + o), o=off))
    b2 = b.reshape(1, -1)
    return pl.pallas_call(
        functools.partial(_ffn_act_kernel, tr, seq_bounds),
        grid=(m // tr, nb),
        in_specs=_halo_specs(tr, tc, 0, m) + _halo_specs(tr, tc, nb, m) + [
            wspec(0), bspec(0), wspec(nb), bspec(nb)],
        out_specs=pl.BlockSpec((tr, tc), lambda i, j: (i, j)),
        out_shape=jax.ShapeDtypeStruct((m, D_FF_PAD), BF16),
        scratch_shapes=[pltpu.VMEM((tr + 2 * HALO, tc), F32), pltpu.VMEM((tr + 2 * HALO, tc), F32)],
        compiler_params=_cparams(("parallel", "parallel"), 32),
        name="ffn_conv_act",
    )(up, up, up, up, up, up, w, b2, w, b2)


def _softplus(x):
    return jnp.maximum(x, 0.0) + jnp.log1p(jnp.exp(-jnp.abs(x)))


def _ssd_kernel(nchunks, seq_bounds, xbc_ref, dt_ref, bias_ref, alog_ref, y_ref, h_ref):
    t = SSM_CHUNK
    n = SSM_STATE
    p = SSM_HEADDIM
    d = pl.program_id(0)
    i = pl.program_id(1)
    chunk = i + d * (nchunks - 1 - 2 * i)
    is_start, is_end = _seq_flags(chunk * t, t, seq_bounds)
    reset = jnp.where(d == 0, is_start, is_end)

    @pl.when(reset)
    def _():
        h_ref[...] = jnp.zeros_like(h_ref)

    dt = _softplus(dt_ref[...] + bias_ref[0])
    da = dt * (-jnp.exp(alog_ref[0]))
    row = lax.broadcasted_iota(jnp.int32, (t, t), 0)
    col = lax.broadcasted_iota(jnp.int32, (t, t), 1)
    causal = ((row - col) * (1 - 2 * d)) >= 0
    tri = jnp.where(causal, 1.0, 0.0).astype(BF16)
    cs = _sel_dot(tri, da)
    cs_t = cs.T
    dt_t = dt.T
    tot = jnp.where(d == 0, cs[t - 1:t, :], cs[0:1, :])
    tot_t = jnp.where(d == 0, cs_t[:, t - 1:t], cs_t[:, 0:1])
    w_t = jnp.exp(tot_t - cs_t) * dt_t
    ecs = jnp.exp(cs)
    etot = jnp.exp(tot)

    for g in range(SSM_GROUPS):
        bg = xbc_ref[:, SSM_INNER + g * n:SSM_INNER + (g + 1) * n]
        cg = xbc_ref[:, SSM_INNER + SSM_GROUPS * n + g * n:SSM_INNER + SSM_GROUPS * n + (g + 1) * n]
        cb = lax.dot_general(cg, bg, (((1,), (1,)), ((), ())), preferred_element_type=F32)
        bg_t = bg.astype(F32).T
        cg_f = cg.astype(F32)
        for hh in range(SSM_HPG):
            h = g * SSM_HPG + hh
            colb = jnp.broadcast_to(cs[:, h:h + 1], (t, t))
            rowb = jnp.broadcast_to(cs_t[h:h + 1, :], (t, t))
            decay = jnp.exp(jnp.where(causal, colb - rowb, NEG_INF))
            mh = decay * cb * jnp.broadcast_to(dt_t[h:h + 1, :], (t, t))
            ce = cg_f * jnp.broadcast_to(ecs[:, h:h + 1], (t, n))
            lhs = jnp.concatenate([mh, ce], axis=1).astype(BF16)
            xh = xbc_ref[:, h * p:(h + 1) * p]
            hin = h_ref[g, :, hh * p:(hh + 1) * p]
            rhs = jnp.concatenate([xh, hin.astype(BF16)], axis=0)
            y_ref[0, :, h * p:(h + 1) * p] = jnp.dot(lhs, rhs, preferred_element_type=F32)
            bt = (bg_t * jnp.broadcast_to(w_t[h:h + 1, :], (n, t))).astype(BF16)
            st = jnp.dot(bt, xh, preferred_element_type=F32)
            h_ref[g, :, hh * p:(hh + 1) * p] = hin * jnp.broadcast_to(etot[:, h:h + 1], (n, p)) + st


def _ssd(xbc, dtraw, dt_bias, a_log, seq_bounds):
    m = xbc.shape[0]
    t = SSM_CHUNK
    nchunks = m // t
    cidx = lambda d, i: i + d * (nchunks - 1 - 2 * i)
    pad = lambda v: jnp.pad(v.astype(F32), ((0, 0), (0, 128 - SSM_HEADS))).reshape(2, 1, 128)
    return pl.pallas_call(
        functools.partial(_ssd_kernel, nchunks, seq_bounds),
        grid=(2, nchunks),
        in_specs=[pl.BlockSpec((t, SSM_XBC), lambda d, i: (cidx(d, i), 0)),
                  pl.BlockSpec((t, 128), lambda d, i: (cidx(d, i), d)),
                  pl.BlockSpec((1, 1, 128), lambda d, i: (d, 0, 0)),
                  pl.BlockSpec((1, 1, 128), lambda d, i: (d, 0, 0))],
        out_specs=pl.BlockSpec((1, t, SSM_INNER), lambda d, i: (d, cidx(d, i), 0)),
        out_shape=jax.ShapeDtypeStruct((2, m, SSM_INNER), F32),
        scratch_shapes=[pltpu.VMEM((SSM_GROUPS, SSM_STATE, SSM_HPG * SSM_HEADDIM), F32)],
        compiler_params=_cparams(("arbitrary", "arbitrary"), 32),
        name="ssd_scan",
    )(xbc, dtraw, pad(dt_bias), pad(a_log))


def _ssd_gate_kernel(y_ref, xs_ref, z_ref, d_ref, g_ref, o_ref):
    y = y_ref[0] + y_ref[1] + xs_ref[...].astype(F32) * d_ref[...]
    y = y * _silu(z_ref[...].astype(F32))
    ms = jnp.mean(y * y, axis=-1, keepdims=True)
    o_ref[...] = (y * lax.rsqrt(ms + NORM_EPS) * g_ref[...]).astype(o_ref.dtype)


def _ssd_gate(y, xbc, proj, d_skip, norm_g, tr=256):
    m = xbc.shape[0]
    w = SSM_INNER
    return pl.pallas_call(
        _ssd_gate_kernel,
        grid=(m // tr,),
        in_specs=[pl.BlockSpec((2, tr, w), lambda i: (0, i, 0)),
                  pl.BlockSpec((tr, w), lambda i: (i, 0)),
                  pl.BlockSpec((tr, w), lambda i: (i, COL_Z // w)),
                  pl.BlockSpec((1, w), lambda i: (0, 0)),
                  pl.BlockSpec((1, w), lambda i: (0, 0))],
        out_specs=pl.BlockSpec((tr, w), lambda i: (i, 0)),
        out_shape=jax.ShapeDtypeStruct((m, w), BF16),
        compiler_params=_cparams(("parallel",), 32),
        name="ssd_gate_norm",
    )(y, xbc, proj, jnp.repeat(d_skip.astype(F32), SSM_HEADDIM).reshape(1, w), norm_g.reshape(1, w))


def _s5_selectors():
    t, c = S5_CHUNK, S5_GROUP
    lag = np.arange(S5_LAGS)[:, None]
    tok = (np.arange(S5_CW) // c)[None, :]
    e_exit_f = (lag == t - 1 - tok)
    e_exit_b = (lag == tok)
    e_in_f = (lag == tok + 1)
    e_in_b = (lag == t - tok)
    lagidx = (np.arange(2 * S5_CW) // c)[None, :]
    rel = lagidx - (t - 1)
    inb = lagidx <= 2 * t - 2
    e_k_f = (lag == rel) & (rel >= 0) & inb
    e_k_b = (lag == -rel) & (rel <= 0) & inb
    tile = (np.arange(c)[:, None] == (np.arange(2 * S5_CW) % c)[None, :])
    sel = np.stack([np.concatenate([a, b], axis=1) for a, b in
                    ((e_exit_f, e_in_f), (e_exit_b, e_in_b))])
    selk = np.stack([e_k_f, e_k_b])
    return (jnp.asarray(sel, BF16), jnp.asarray(selk, BF16), jnp.asarray(tile, BF16))


def _s5_prep_kernel(acol_ref, arow_ref, ls_ref, b_ref, bt_ref, ct_ref, sel_ref, selk_ref, tile_ref,
                    sign_ref, w_ref, ws_ref, wo_ref, lpa_ref, lpb_ref):
    t = S5_CHUNK
    cw = S5_CW
    p = S5_STATE
    tile = tile_ref[...]
    lagf = lax.broadcasted_iota(jnp.int32, (p, S5_LAGS), 1).astype(F32)
    kt = jnp.zeros((S5_GROUP, 2 * cw), F32)
    for d in range(2):
        step = jnp.exp(ls_ref[0, 0, d])
        are = acol_ref[0, 0, d, 0]
        aim = acol_ref[0, 0, d, 1]
        mag = are * step
        th = aim * step
        amp = jnp.exp(lagf * mag)
        pwr = amp * jnp.cos(lagf * th)
        pwi = amp * jnp.sin(lagf * th)
        lbr = jnp.exp(mag) * jnp.cos(th)
        lbi = jnp.exp(mag) * jnp.sin(th)
        den = are * are + aim * aim
        cfr = ((lbr - 1.0) * are + lbi * aim) / den
        cfi = (lbi * are - (lbr - 1.0) * aim) / den
        bre = b_ref[0, 0, 0]
        bim = b_ref[0, 0, 1]
        bbr = cfr * bre - cfi * bim
        bbi = cfr * bim + cfi * bre
        step_r = step
        are_r = arow_ref[0, 0, d, 0:1, 0:p]
        aim_r = arow_ref[0, 0, d, 1:2, 0:p]
        mag_r = are_r * step_r
        th_r = aim_r * step_r
        lbr_r = jnp.exp(mag_r) * jnp.cos(th_r)
        lbi_r = jnp.exp(mag_r) * jnp.sin(th_r)
        den_r = are_r * are_r + aim_r * aim_r
        cfr_r = ((lbr_r - 1.0) * are_r + lbi_r * aim_r) / den_r
        cfi_r = (lbi_r * are_r - (lbr_r - 1.0) * aim_r) / den_r
        btr = bt_ref[0, 0, 0]
        bti = bt_ref[0, 0, 1]
        bbr_t = cfr_r * btr - cfi_r * bti
        bbi_t = cfr_r * bti + cfi_r * btr
        ctr = ct_ref[0, 0, d, 0]
        cti = ct_ref[0, 0, d, 1]

        sel = sel_ref[d]
        er = _dot_sel(pwr, sel)
        ei = _dot_sel(pwi, sel)
        tb_r = _dot_sel(bbr, tile[:, 0:cw])
        tb_i = _dot_sel(bbi, tile[:, 0:cw])
        tc_r = _dot_sel(ctr, tile)
        tc_i = _dot_sel(cti, tile)
        ws_ref[0, 0, d * 2 * p:d * 2 * p + p, :] = (er[:, 0:cw] * tb_r - ei[:, 0:cw] * tb_i).astype(BF16)
        ws_ref[0, 0, d * 2 * p + p:(d + 1) * 2 * p, :] = (er[:, 0:cw] * tb_i + ei[:, 0:cw] * tb_r).astype(BF16)
        zr = er[:, cw:] * tc_r[:, 0:cw] - ei[:, cw:] * tc_i[:, 0:cw]
        zi = er[:, cw:] * tc_i[:, 0:cw] + ei[:, cw:] * tc_r[:, 0:cw]
        wo_ref[0, 0, d * 2 * p:d * 2 * p + p, :] = zr.astype(BF16)
        wo_ref[0, 0, d * 2 * p + p:(d + 1) * 2 * p, :] = (-zi).astype(BF16)
        selk = selk_ref[d]
        qr = _dot_sel(pwr, selk) * tc_r - _dot_sel(pwi, selk) * tc_i
        qi = _dot_sel(pwr, selk) * tc_i + _dot_sel(pwi, selk) * tc_r
        kt = kt + _dot_f32(bbr_t, qr) - _dot_f32(bbi_t, qi)
        are2 = arow_ref[0, 0, d, 0:1, :]
        aim2 = arow_ref[0, 0, d, 1:2, :]
        ampt = jnp.exp(are2 * step_r * float(t))
        zr2 = ampt * jnp.cos(aim2 * step_r * float(t))
        zi2 = ampt * jnp.sin(aim2 * step_r * float(t))
        sign = sign_ref[...]
        for k in range(S5_SCAN_STEPS):
            lpa_ref[0, 0, d, k:k + 1, :] = zr2
            lpb_ref[0, 0, d, k:k + 1, :] = sign * zi2
            zr2, zi2 = zr2 * zr2 - zi2 * zi2, 2.0 * zr2 * zi2
    for s in range(t):
        off = (t - 1 - s) * S5_GROUP
        w_ref[0, 0, s * S5_GROUP:(s + 1) * S5_GROUP, :] = kt[:, off:off + cw].astype(BF16)


def _s5_prep(a_re, a_im, log_step, b_re, b_im, c_re, c_im):
    depth = a_re.shape[0]
    g, p, c = S5_GROUPS, S5_STATE, S5_GROUP
    a = jnp.stack([a_re, a_im], axis=2).astype(F32)
    a = a.transpose(0, 3, 1, 2, 4)
    acol = a[..., None]
    arow = jnp.concatenate([a, a], axis=-1)
    ls = log_step.astype(F32).transpose(0, 2, 1).reshape(depth, g, 2, 1, 1)
    b = jnp.stack([b_re, b_im], axis=2).astype(F32)
    bt = b.transpose(0, 1, 2, 4, 3)
    ct = jnp.stack([c_re, c_im], axis=3).astype(F32)
    ct = ct.transpose(0, 2, 1, 3, 5, 4)
    sel, selk, tile = _s5_selectors()
    sign = jnp.concatenate([-jnp.ones((1, p), F32), jnp.ones((1, p), F32)], axis=1)
    full = lambda shp: pl.BlockSpec(shp, lambda l, j: (0,) * len(shp))
    per = lambda shp: pl.BlockSpec((1, 1) + shp, lambda l, j: (l, j) + (0,) * len(shp))
    cw = S5_CW
    return pl.pallas_call(
        _s5_prep_kernel,
        grid=(depth, g),
        in_specs=[per((2, 2, p, 1)), per((2, 2, 2 * p)), per((2, 1, 1)), per((2, p, c)), per((2, c, p)),
                  per((2, 2, p, c)), full((2, S5_LAGS, 2 * cw)), full((2, S5_LAGS, 2 * cw)),
                  full((c, 2 * cw)), full((1, 2 * p))],
        out_specs=[per((cw, cw)), per((4 * p, cw)), per((4 * p, cw)),
                   per((2, S5_SCAN_STEPS, 2 * p)), per((2, S5_SCAN_STEPS, 2 * p))],
        out_shape=[jax.ShapeDtypeStruct((depth, g, cw, cw), BF16),
                   jax.ShapeDtypeStruct((depth, g, 4 * p, cw), BF16),
                   jax.ShapeDtypeStruct((depth, g, 4 * p, cw), BF16),
                   jax.ShapeDtypeStruct((depth, g, 2, S5_SCAN_STEPS, 2 * p), F32),
                   jax.ShapeDtypeStruct((depth, g, 2, S5_SCAN_STEPS, 2 * p), F32)],
        compiler_params=_cparams(("parallel", "parallel"), 32),
        name="s5_prep",
    )(acol, arow, ls, b, bt, ct, sel, selk, tile, sign)


def _gelu_tanh(x):
    return 0.5 * x * (1.0 + jnp.tanh(math.sqrt(2.0 / math.pi) * (x + 0.044715 * (x * x * x))))


def _s5_kernel(chunk_bounds, u_ref, w_ref, ws_ref, wo_ref, lpa_ref, lpb_ref, d_ref, o_ref):
    r = u_ref.shape[1]
    p2 = 2 * S5_STATE
    u = u_ref[0]
    y = jnp.dot(u, w_ref[0], preferred_element_type=F32)
    st = lax.dot_general(u, ws_ref[0], (((1,), (1,)), ((), ())), preferred_element_type=F32)
    ridx = lax.broadcasted_iota(jnp.int32, (r, p2), 0)
    rloc = jnp.zeros((r, p2), jnp.int32)
    rlen = jnp.zeros((r, p2), jnp.int32)
    for s0, ln in chunk_bounds:
        inside = (ridx >= s0) & (ridx < s0 + ln)
        rloc = jnp.where(inside, ridx - s0, rloc)
        rlen = jnp.where(inside, ln, rlen)
    xin = []
    for d in range(2):
        x = st[:, d * p2:(d + 1) * p2]
        for k in range(S5_SCAN_STEPS):
            sh = 1 << k
            if d == 0:
                prev = jnp.where(rloc >= sh, pltpu.roll(x, sh, 0), 0.0)
            else:
                prev = jnp.where(rloc < rlen - sh, pltpu.roll(x, r - sh, 0), 0.0)
            x = x + lpa_ref[0, d, k:k + 1, :] * prev + lpb_ref[0, d, k:k + 1, :] * pltpu.roll(prev, S5_STATE, 1)
        if d == 0:
            xin.append(jnp.where(rloc >= 1, pltpu.roll(x, 1, 0), 0.0))
        else:
            xin.append(jnp.where(rloc < rlen - 1, pltpu.roll(x, r - 1, 0), 0.0))
    xin = jnp.concatenate(xin, axis=1).astype(BF16)
    y = y + jnp.dot(xin, wo_ref[0], preferred_element_type=F32)
    y = y + u.astype(F32) * d_ref[0]
    o_ref[0] = _gelu_tanh(y).astype(o_ref.dtype)


def _s5(u_chunks, w, ws, wo, lpa, lpb, d_tiled, chunk_bounds):
    g, r, cw = u_chunks.shape
    p = S5_STATE
    per = lambda shp: pl.BlockSpec((1,) + shp, lambda j: (j,) + (0,) * len(shp))
    return pl.pallas_call(
        functools.partial(_s5_kernel, chunk_bounds),
        grid=(g,),
        in_specs=[per((r, cw)), per((cw, cw)), per((4 * p, cw)), per((4 * p, cw)),
                  per((2, S5_SCAN_STEPS, 2 * p)), per((2, S5_SCAN_STEPS, 2 * p)), per((1, cw))],
        out_specs=per((r, cw)),
        out_shape=jax.ShapeDtypeStruct((g, r, cw), BF16),
        compiler_params=_cparams(("parallel",), 32),
        name="s5_mix",
    )(u_chunks, w, ws, wo, lpa, lpb, d_tiled)


def _t5_bucket(rel):
    half = REL_BUCKETS // 2
    exact = half // 2
    sign = (rel > 0).astype(np.int32) * half
    n = np.abs(rel)
    large = exact + (np.log(np.maximum(n, 1) / exact) / np.log(REL_MAX_DIST / exact)
                     * (half - exact)).astype(np.int32)
    large = np.minimum(large, half - 1)
    return sign + np.where(n < exact, n, large)


def _attn_kernel(tq, blk_bounds, q_ref, kp_ref, km_ref, kn_ref, vp_ref, vm_ref, vn_ref, bias_ref,
                 o_ref, lse_ref):
    jb = pl.program_id(1)
    is_first, is_last = _seq_flags(jb, 1, blk_bounds)
    tk = tq + 2 * ATT_HALF
    colk = lax.broadcasted_iota(jnp.int32, (tq, tk), 1)
    valid = jnp.logical_and(jnp.logical_or(colk >= ATT_HALF, jnp.logical_not(is_first)),
                            jnp.logical_or(colk < tq + ATT_HALF, jnp.logical_not(is_last)))
    kcat = jnp.concatenate([kp_ref[...], km_ref[...], kn_ref[...]], axis=0)
    vcat = jnp.concatenate([vp_ref[...], vm_ref[...], vn_ref[...]], axis=0)
    scale = ATT_HEAD_DIM ** -0.5
    for h in range(ATT_HPG):
        sl = slice(h * ATT_HEAD_DIM, (h + 1) * ATT_HEAD_DIM)
        s = lax.dot_general(q_ref[:, sl], kcat[:, sl], (((1,), (1,)), ((), ())),
                            preferred_element_type=F32)
        s = jnp.where(valid, s * scale + bias_ref[h], NEG_INF)
        mx = jnp.max(s, axis=-1, keepdims=True)
        pr = jnp.exp(s - mx)
        den = jnp.sum(pr, axis=-1, keepdims=True)
        o = jnp.dot(pr.astype(BF16), vcat[:, sl], preferred_element_type=F32)
        o_ref[:, sl] = o / den
        lse_ref[:, sl] = jnp.broadcast_to(mx + jnp.log(den), (tq, ATT_HEAD_DIM))


def _attention_group(proj, rel_bias, gi, seq_bounds):
    window, dil = ATT_PATTERNS[gi]
    assert window // (2 * dil) == ATT_HALF
    m = proj.shape[0]
    nview = m // dil
    tq = min(256, min(l for _, l in seq_bounds) // dil)
    assert tq % ATT_HALF == 0 and all((s // dil) % tq == 0 and (l // dil) % tq == 0 for s, l in seq_bounds)
    blk_bounds = tuple((s // dil // tq, l // dil // tq) for s, l in seq_bounds)
    tk = tq + 2 * ATT_HALF
    rel = (np.arange(tk)[None, :] - ATT_HALF) - np.arange(tq)[:, None]
    band = np.abs(rel) <= ATT_HALF
    heads = slice(gi * ATT_HPG, (gi + 1) * ATT_HPG)
    table = rel_bias[:, heads].astype(F32)[_t5_bucket(rel * dil)]
    bias = jnp.where(band[None], jnp.moveaxis(table, -1, 0), NEG_INF)

    pv = proj.reshape(nview, dil * IN_COLS)
    cpr = IN_COLS // ATT_GW
    hb = tq // ATT_HALF
    lasth = nview // ATT_HALF - 1

    def main(col):
        return pl.BlockSpec((tq, ATT_GW), lambda r, jb: (jb, r * cpr + col + gi))

    def prev(col):
        return pl.BlockSpec((ATT_HALF, ATT_GW), lambda r, jb: (jnp.maximum(jb * hb - 1, 0), r * cpr + col + gi))

    def nxt(col):
        return pl.BlockSpec((ATT_HALF, ATT_GW), lambda r, jb: (jnp.minimum((jb + 1) * hb, lasth), r * cpr + col + gi))

    cq, ck, cv = COL_Q // ATT_GW, COL_K // ATT_GW, COL_V // ATT_GW
    out_spec = pl.BlockSpec((tq, ATT_GW), lambda r, jb: (jb, r))
    o, lse = pl.pallas_call(
        functools.partial(_attn_kernel, tq, blk_bounds),
        grid=(dil, nview // tq),
        in_specs=[main(cq), prev(ck), main(ck), nxt(ck), prev(cv), main(cv), nxt(cv),
                  pl.BlockSpec((ATT_HPG, tq, tk), lambda r, jb: (0, 0, 0))],
        out_specs=[out_spec, out_spec],
        out_shape=[jax.ShapeDtypeStruct((nview, dil * ATT_GW), F32)] * 2,
        compiler_params=_cparams(("parallel", "parallel"), 32),
        name=f"dilated_attention_{gi}",
    )(pv, pv, pv, pv, pv, pv, pv, bias)
    return o.reshape(m, ATT_GW), lse.reshape(m, ATT_GW)


def _att_combine_kernel(o0, l0, o1, l1, o2, l2, out_ref):
    a, b, c = l0[...], l1[...], l2[...]
    mx = jnp.maximum(jnp.maximum(a, b), c)
    ea, eb, ec = jnp.exp(a - mx), jnp.exp(b - mx), jnp.exp(c - mx)
    tot = ea + eb + ec
    out_ref[...] = ((ea * o0[...] + eb * o1[...] + ec * o2[...]) / tot).astype(out_ref.dtype)


def _att_combine(outs, lses, tr=512):
    m, w = outs[0].shape
    spec = pl.BlockSpec((tr, w), lambda i: (i, 0))
    args = [x for pair in zip(outs, lses) for x in pair]
    return pl.pallas_call(
        _att_combine_kernel,
        grid=(m // tr,),
        in_specs=[spec] * 6,
        out_specs=spec,
        out_shape=jax.ShapeDtypeStruct((m, w), BF16),
        compiler_params=_cparams(("parallel",), 32),
        name="attention_combine",
    )(*args)


def _pack_w_in(w):
    splits = np.cumsum([SSM_INNER, SSM_XBC, 2 * SSM_HEADS, S5_WIDTH, 3 * 1536]).tolist()
    z, xbc, dt, u, qkv, gates = jnp.split(w, splits, axis=-1)
    main = jnp.concatenate([z, xbc, u, qkv, gates], axis=-1).astype(BF16)
    zpad = jnp.zeros((w.shape[0], 128 - SSM_HEADS), w.dtype)
    wdt = jnp.concatenate([dt[:, :SSM_HEADS], zpad, dt[:, SSM_HEADS:], zpad], axis=-1).astype(BF16)
    return main, wdt


def _pack_ffn(w_up, conv_w, conv_b, w_down):
    padc = lambda a: jnp.pad(a, [(0, 0)] * (a.ndim - 1) + [(0, D_FF_PAD - D_FF)])
    both = lambda a: jnp.concatenate([padc(a[..., :D_FF]), padc(a[..., D_FF:])], axis=-1)
    w_down_p = jnp.pad(w_down, ((0, D_FF_PAD - D_FF), (0, 0))).astype(BF16)
    return both(w_up).astype(BF16), both(conv_w).astype(F32), both(conv_b).astype(F32), w_down_p


def _trunk(x, seq_bounds, rel_bias, norm_mix, w_in, ssm_conv_w, ssm_conv_b, ssm_a_log, ssm_dt_bias, ssm_d,
           ssm_norm, ssm_w_out, s5_tables, s5_d, s5_w_glu, att_w_out, w_o, norm_ffn, w_up, ffn_conv_w,
           ffn_conv_b, w_down, final_norm):
    m = x.shape[0]
    depth = w_in.shape[0]
    tm = min(1024, m)
    tm2 = min(512, m)
    gate_blk = lambda b, tn: (COL_GATES + b * D_MODEL) // tn
    chunk_bounds = tuple((s // S5_CHUNK, l // S5_CHUNK) for s, l in seq_bounds)
    r_chunks = m // S5_CHUNK
    gated = lambda acc, gate: _sigmoid(gate.astype(F32)) * acc
    gated_add = lambda acc, gate, prev: prev + _sigmoid(gate.astype(F32)) * acc
    resid = lambda acc, res: res + acc

    for li in range(depth):
        h = _rmsnorm(x, norm_mix[li], BF16)
        w_main, w_dt = _pack_w_in(w_in[li])
        proj = _matmul(h, w_main, tm=tm, tn=512, out_dtype=BF16, name="in_proj")
        dtraw = _matmul(h, w_dt, tm=tm, tn=DT_COLS, out_dtype=F32, name="dt_proj")

        xbc = _conv_silu(proj, COL_XBC // 512, SSM_XBC, ssm_conv_w[li].astype(F32), ssm_conv_b[li].astype(F32),
                         seq_bounds, tr=tm2)
        y_ssd = _ssd(xbc, dtraw, ssm_dt_bias[li], ssm_a_log[li], seq_bounds)
        y_a = _ssd_gate(y_ssd, xbc, proj, ssm_d[li], ssm_norm[li].astype(F32), tr=min(256, m))
        merged = _matmul(y_a, ssm_w_out[li].astype(BF16), tm=tm, tn=512, out_dtype=F32, epilogue=gated,
                         extras=((proj, gate_blk(0, 512)),), name="ssm_out_proj")

        w5, ws5, wo5, lpa, lpb = (tbl[li] for tbl in s5_tables)
        u = proj[:, COL_U:COL_U + S5_WIDTH].reshape(r_chunks, S5_CHUNK, S5_GROUPS, S5_GROUP)
        u = u.transpose(2, 0, 1, 3).reshape(S5_GROUPS, r_chunks, S5_CW)
        d_tiled = jnp.tile(s5_d[li].astype(F32).reshape(S5_GROUPS, 1, S5_GROUP), (1, 1, S5_CHUNK))
        y_b = _s5(u, w5, ws5, wo5, lpa, lpb, d_tiled, chunk_bounds)
        y_b = y_b.reshape(S5_GROUPS, r_chunks, S5_CHUNK, S5_GROUP).transpose(1, 2, 0, 3).reshape(m, S5_WIDTH)
        merged = _glu_matmul(y_b, s5_w_glu[li].astype(BF16), proj, merged, tm=tm, tn=512,
                             gate_blk=gate_blk(1, 512))

        outs, lses = zip(*[_attention_group(proj, rel_bias, gi, seq_bounds) for gi in range(len(ATT_PATTERNS))])
        comb = _att_combine(outs, lses, tr=tm2)
        merged = _matmul(comb, att_w_out[li].astype(BF16), tm=tm, tn=512, out_dtype=BF16, epilogue=gated_add,
                         extras=((proj, gate_blk(2, 512)), (merged, 0)), name="att_out_proj")

        x = _matmul(merged, w_o[li].astype(BF16), tm=tm, tn=512, out_dtype=F32, epilogue=resid,
                    extras=((x, 0),), name="mix_out_proj")

        h = _rmsnorm(x, norm_ffn[li], BF16)
        w_up_p, cw_p, cb_p, w_down_p = _pack_ffn(w_up[li], ffn_conv_w[li], ffn_conv_b[li], w_down[li])
        up = _matmul(h, w_up_p, tm=tm, tn=512, out_dtype=BF16, name="ffn_up")
        act = _ffn_act(up, cw_p, cb_p, seq_bounds, tr=tm2)
        x = _matmul(act, w_down_p, tm=tm2, tn=512, out_dtype=F32, epilogue=resid, extras=((x, 0),),
                    name="ffn_down")
    return _rmsnorm(x, final_norm, F32)


def kernel(x_prompt, x_sample, rel_bias, norm_mix, w_in, ssm_conv_w, ssm_conv_b, ssm_a_log, ssm_dt_bias, ssm_d, ssm_norm, ssm_w_out, s5_a_re, s5_a_im, s5_log_step, s5_b_re, s5_b_im, s5_c_re, s5_c_im, s5_d, s5_w_glu, att_w_out, w_o, norm_ffn, w_up, ffn_conv_w, ffn_conv_b, w_down, final_norm):
    d = x_prompt.shape[-1]
    seq_bounds = []
    row = 0
    for arr in (x_prompt, x_sample):
        for _ in range(arr.shape[0]):
            seq_bounds.append((row, arr.shape[1]))
            row += arr.shape[1]
    seq_bounds = tuple(seq_bounds)
    x = jnp.concatenate([x_prompt.reshape(-1, d), x_sample.reshape(-1, d)], axis=0)
    s5_tables = _s5_prep(s5_a_re, s5_a_im, s5_log_step, s5_b_re, s5_b_im, s5_c_re, s5_c_im)
    y = _trunk(x, seq_bounds, rel_bias, norm_mix, w_in, ssm_conv_w, ssm_conv_b, ssm_a_log, ssm_dt_bias, ssm_d,
               ssm_norm, ssm_w_out, s5_tables, s5_d, s5_w_glu, att_w_out, w_o, norm_ffn, w_up, ffn_conv_w,
               ffn_conv_b, w_down, final_norm)
    n_prompt = x_prompt.shape[0] * x_prompt.shape[1]
    return (y[:n_prompt].reshape(x_prompt.shape), y[n_prompt:].reshape(x_sample.shape))
```

```python
import functools
import math

import numpy as np
import jax
import jax.numpy as jnp
from jax import lax
from jax.experimental import pallas as pl
from jax.experimental.pallas import tpu as pltpu

F32 = jnp.float32
BF16 = jnp.bfloat16

D_MODEL = 2048
NORM_EPS = 1e-6
NEG_INF = -1e30

SSM_HEADDIM = 64
SSM_INNER = 1536
SSM_HEADS = 24
SSM_GROUPS = 4
SSM_HPG = SSM_HEADS // SSM_GROUPS
SSM_STATE = 128
SSM_CONV = 5
SSM_CHUNK = 128
SSM_XBC = 2560

S5_WIDTH = 1024
S5_GROUP = 16
S5_GROUPS = 64
S5_STATE = 64
S5_CHUNK = 32
S5_CW = S5_CHUNK * S5_GROUP
S5_LAGS = 128
S5_SCAN_STEPS = 8

ATT_HEAD_DIM = 128
ATT_HPG = 4
ATT_PATTERNS = ((128, 1), (512, 4), (2048, 16))
ATT_GW = ATT_HPG * ATT_HEAD_DIM
ATT_HALF = 64
REL_BUCKETS = 32
REL_MAX_DIST = 1024

D_FF = 5504
D_FF_PAD = 5632
FFN_CONV = 3

COL_Z = 0
COL_XBC = 1536
COL_U = 4096
COL_QKV = 5120
COL_GATES = 9728
IN_COLS = 15872
DT_COLS = 256

V7X_VMEM_BYTES = 64 * 1024 * 1024


def _cparams(sem, vmem_mb):
    return pltpu.CompilerParams(dimension_semantics=sem, vmem_limit_bytes=vmem_mb * 1024 * 1024)


def _sigmoid(x):
    return 1.0 / (1.0 + jnp.exp(-x))


def _silu(x):
    return x * _sigmoid(x)


def _split3(x):
    hi = x.astype(BF16)
    r1 = x - hi.astype(F32)
    mid = r1.astype(BF16)
    lo = (r1 - mid.astype(F32)).astype(BF16)
    return hi, mid, lo


def _dot_sel(x, sel):
    hi, mid, lo = _split3(x)
    d = lambda a: jnp.dot(a, sel, preferred_element_type=F32)
    return d(hi) + d(mid) + d(lo)


def _sel_dot(sel, x):
    hi, mid, lo = _split3(x)
    d = lambda a: jnp.dot(sel, a, preferred_element_type=F32)
    return d(hi) + d(mid) + d(lo)


def _dot_f32(a, b):
    ah, am, _ = _split3(a)
    bh, bm, _ = _split3(b)
    d = lambda x, y: jnp.dot(x, y, preferred_element_type=F32)
    return d(ah, bh) + d(ah, bm) + d(am, bh)


def _rmsnorm_kernel(x_ref, g_ref, o_ref):
    x = x_ref[...]
    ms = jnp.mean(x * x, axis=-1, keepdims=True)
    o_ref[...] = (x * lax.rsqrt(ms + NORM_EPS) * g_ref[...]).astype(o_ref.dtype)


def _rmsnorm(x, gain, out_dtype, tr=512):
    m, d = x.shape
    return pl.pallas_call(
        _rmsnorm_kernel,
        grid=(m // tr,),
        in_specs=[pl.BlockSpec((tr, d), lambda i: (i, 0)),
                  pl.BlockSpec((1, d), lambda i: (0, 0))],
        out_specs=pl.BlockSpec((tr, d), lambda i: (i, 0)),
        out_shape=jax.ShapeDtypeStruct((m, d), out_dtype),
        compiler_params=_cparams(("parallel",), 32),
        name="rmsnorm",
    )(x, gain.reshape(1, d))


def _mm_kernel(epilogue, n_extra, a_ref, b_ref, *rest):
    extras = rest[:n_extra]
    o_ref = rest[n_extra]
    acc = jnp.dot(a_ref[...], b_ref[...], preferred_element_type=F32)
    if epilogue is not None:
        acc = epilogue(acc, *[e[...] for e in extras])
    o_ref[...] = acc.astype(o_ref.dtype)


def _matmul(a, b, *, tm, tn, out_dtype, k=None, a_kblk=0, epilogue=None, extras=(), vmem_mb=48,
            name="matmul"):
    m = a.shape[0]
    kk, n = b.shape
    if k is None:
        k = a.shape[1]
    assert k == kk and m % tm == 0 and n % tn == 0
    in_specs = [pl.BlockSpec((tm, k), lambda i, j: (i, a_kblk)),
                pl.BlockSpec((k, tn), lambda i, j: (0, j))]
    args = [a, b]
    for arr, off in extras:
        in_specs.append(pl.BlockSpec((tm, tn), functools.partial(lambda i, j, o: (i, j + o), o=off)))
        args.append(arr)
    return pl.pallas_call(
        functools.partial(_mm_kernel, epilogue, len(extras)),
        grid=(m // tm, n // tn),
        in_specs=in_specs,
        out_specs=pl.BlockSpec((tm, tn), lambda i, j: (i, j)),
        out_shape=jax.ShapeDtypeStruct((m, n), out_dtype),
        compiler_params=_cparams(("parallel", "parallel"), vmem_mb),
        name=name,
    )(*args)


def _glu_kernel(a_ref, bv_ref, bg_ref, gate_ref, acc_ref, o_ref):
    a = a_ref[...]
    val = jnp.dot(a, bv_ref[...], preferred_element_type=F32)
    gate = jnp.dot(a, bg_ref[...], preferred_element_type=F32)
    branch = val * _sigmoid(gate)
    o_ref[...] = acc_ref[...] + _sigmoid(gate_ref[...].astype(F32)) * branch


def _glu_matmul(a, w, proj, merged, *, tm, tn, gate_blk):
    m, k = a.shape
    n = w.shape[1] // 2
    nb = n // tn
    return pl.pallas_call(
        _glu_kernel,
        grid=(m // tm, nb),
        in_specs=[pl.BlockSpec((tm, k), lambda i, j: (i, 0)),
                  pl.BlockSpec((k, tn), lambda i, j: (0, j)),
                  pl.BlockSpec((k, tn), lambda i, j: (0, j + nb)),
                  pl.BlockSpec((tm, tn), lambda i, j: (i, j + gate_blk)),
                  pl.BlockSpec((tm, tn), lambda i, j: (i, j))],
        out_specs=pl.BlockSpec((tm, tn), lambda i, j: (i, j)),
        out_shape=jax.ShapeDtypeStruct((m, n), F32),
        compiler_params=_cparams(("parallel", "parallel"), 48),
        name="s5_glu_matmul",
    )(a, w, w, proj, merged)


HALO = 16


def _seq_flags(row0, nrows, seq_bounds):
    starts = [s for s, _ in seq_bounds]
    ends = [s + l for s, l in seq_bounds]
    is_start = functools.reduce(jnp.logical_or, [row0 == s for s in starts])
    is_end = functools.reduce(jnp.logical_or, [row0 + nrows == e for e in ends])
    return is_start, is_end


def _conv_taps(buf_ref, w_ref, b_ref, width, tr):
    pad = width // 2
    acc = None
    for kk in range(width):
        term = buf_ref[pl.ds(HALO - pad + kk, tr), :] * w_ref[kk:kk + 1, :]
        acc = term if acc is None else acc + term
    return acc + b_ref[...]


def _fill_halo_buf(buf_ref, prev_ref, main_ref, next_ref, tr, seq_bounds):
    i = pl.program_id(0)
    is_start, is_end = _seq_flags(i * tr, tr, seq_bounds)
    prev = prev_ref[...].astype(F32)
    nxt = next_ref[...].astype(F32)
    buf_ref[0:HALO, :] = jnp.where(is_start, 0.0, prev)
    buf_ref[HALO:HALO + tr, :] = main_ref[...].astype(F32)
    buf_ref[HALO + tr:HALO + tr + HALO, :] = jnp.where(is_end, 0.0, nxt)


def _conv_silu_kernel(width, tr, seq_bounds, prev_ref, main_ref, next_ref, w_ref, b_ref, o_ref, buf_ref):
    _fill_halo_buf(buf_ref, prev_ref, main_ref, next_ref, tr, seq_bounds)
    o_ref[...] = _silu(_conv_taps(buf_ref, w_ref, b_ref, width, tr)).astype(o_ref.dtype)


def _halo_specs(tr, tc, col_off, nrows):
    hb = tr // HALO
    last = nrows // HALO - 1
    return [pl.BlockSpec((HALO, tc), lambda i, j: (jnp.maximum(i * hb - 1, 0), j + col_off)),
            pl.BlockSpec((tr, tc), lambda i, j: (i, j + col_off)),
            pl.BlockSpec((HALO, tc), lambda i, j: (jnp.minimum((i + 1) * hb, last), j + col_off))]


def _conv_silu(src, col_blk, ncols, w, b, seq_bounds, tr=512, tc=512):
    m = src.shape[0]
    width = w.shape[0]
    return pl.pallas_call(
        functools.partial(_conv_silu_kernel, width, tr, seq_bounds),
        grid=(m // tr, ncols // tc),
        in_specs=_halo_specs(tr, tc, col_blk, m) + [
            pl.BlockSpec((width, tc), lambda i, j: (0, j)),
            pl.BlockSpec((1, tc), lambda i, j: (0, j))],
        out_specs=pl.BlockSpec((tr, tc), lambda i, j: (i, j)),
        out_shape=jax.ShapeDtypeStruct((m, ncols), BF16),
        scratch_shapes=[pltpu.VMEM((tr + 2 * HALO, tc), F32)],
        compiler_params=_cparams(("parallel", "parallel"), 32),
        name="ssm_conv_silu",
    )(src, src, src, w, b.reshape(1, ncols))


def _ffn_act_kernel(tr, seq_bounds, gp_ref, gm_ref, gn_ref, vp_ref, vm_ref, vn_ref,
                    wg_ref, bg_ref, wv_ref, bv_ref, o_ref, gbuf_ref, vbuf_ref):
    _fill_halo_buf(gbuf_ref, gp_ref, gm_ref, gn_ref, tr, seq_bounds)
    _fill_halo_buf(vbuf_ref, vp_ref, vm_ref, vn_ref, tr, seq_bounds)
    gate = _conv_taps(gbuf_ref, wg_ref, bg_ref, FFN_CONV, tr)
    val = _conv_taps(vbuf_ref, wv_ref, bv_ref, FFN_CONV, tr)
    o_ref[...] = (_silu(gate) * val).astype(o_ref.dtype)


def _ffn_act(up, w, b, seq_bounds, tr=512, tc=512):
    m = up.shape[0]
    nb = D_FF_PAD // tc
    wspec = lambda off: pl.BlockSpec((FFN_CONV, tc), functools.partial(lambda i, j, o: (0, j + o), o=off))
    bspec = lambda off: pl.BlockSpec((1, tc), functools.partial(lambda i, j, o: (0, j + o), o=off))
    b2 = b.reshape(1, -1)
    return pl.pallas_call(
        functools.partial(_ffn_act_kernel, tr, seq_bounds),
        grid=(m // tr, nb),
        in_specs=_halo_specs(tr, tc, 0, m) + _halo_specs(tr, tc, nb, m) + [
            wspec(0), bspec(0), wspec(nb), bspec(nb)],
        out_specs=pl.BlockSpec((tr, tc), lambda i, j: (i, j)),
        out_shape=jax.ShapeDtypeStruct((m, D_FF_PAD), BF16),
        scratch_shapes=[pltpu.VMEM((tr + 2 * HALO, tc), F32), pltpu.VMEM((tr + 2 * HALO, tc), F32)],
        compiler_params=_cparams(("parallel", "parallel"), 32),
        name="ffn_conv_act",
    )(up, up, up, up, up, up, w, b2, w, b2)


def _softplus(x):
    return jnp.maximum(x, 0.0) + jnp.log1p(jnp.exp(-jnp.abs(x)))


def _ssd_cumsums(dt_ref, bias_ref, alog_ref, d):
    t = SSM_CHUNK
    dt = _softplus(dt_ref[:, d * 128:(d + 1) * 128] + bias_ref[d])
    da = dt * (-jnp.exp(alog_ref[d]))
    row = lax.broadcasted_iota(jnp.int32, (t, t), 0)
    col = lax.broadcasted_iota(jnp.int32, (t, t), 1)
    tri = jnp.where((col <= row) if d == 0 else (col >= row), 1.0, 0.0).astype(BF16)
    return dt, _sel_dot(tri, da)


def _ssd_state_kernel(nchunks, seq_bounds, xf_ref, xb_ref, dtf_ref, dtb_ref, bias_ref, alog_ref,
                      hf_ref, hb_ref, h_ref):
    t = SSM_CHUNK
    n = SSM_STATE
    p = SSM_HEADDIM
    i = pl.program_id(0)
    is_start, _ = _seq_flags(i * t, t, seq_bounds)
    _, is_end = _seq_flags((nchunks - 1 - i) * t, t, seq_bounds)

    @pl.when(is_start)
    def _():
        h_ref[0] = jnp.zeros(h_ref.shape[1:], F32)

    @pl.when(is_end)
    def _():
        h_ref[1] = jnp.zeros(h_ref.shape[1:], F32)

    for d, (x_ref, dt_ref, out_ref) in enumerate(((xf_ref, dtf_ref, hf_ref), (xb_ref, dtb_ref, hb_ref))):
        dt, cs = _ssd_cumsums(dt_ref, bias_ref, alog_ref, d)
        tot = cs[t - 1:t, :] if d == 0 else cs[0:1, :]
        w = jnp.exp(tot - cs) * dt
        etot = jnp.exp(tot)
        for g in range(SSM_GROUPS):
            bg_t = x_ref[:, SSM_INNER + g * n:SSM_INNER + (g + 1) * n].astype(F32).T.astype(BF16)
            xw, dec = [], []
            for hh in range(SSM_HPG):
                h = g * SSM_HPG + hh
                xh = x_ref[:, h * p:(h + 1) * p].astype(F32)
                xw.append((xh * jnp.broadcast_to(w[:, h:h + 1], (t, p))).astype(BF16))
                dec.append(jnp.broadcast_to(etot[:, h:h + 1], (n, p)))
            st = jnp.dot(bg_t, jnp.concatenate(xw, axis=1), preferred_element_type=F32)
            hin = h_ref[d, g]
            out_ref[0, g] = hin.astype(BF16)
            h_ref[d, g] = hin * jnp.concatenate(dec, axis=1) + st


def _ssd_states(xbc, dtraw, dt_bias, a_log, seq_bounds):
    m = xbc.shape[0]
    t = SSM_CHUNK
    nchunks = m // t
    hw = SSM_HPG * SSM_HEADDIM
    hshape = (nchunks, SSM_GROUPS, SSM_STATE, hw)
    return pl.pallas_call(
        functools.partial(_ssd_state_kernel, nchunks, seq_bounds),
        grid=(nchunks,),
        in_specs=[pl.BlockSpec((t, SSM_XBC), lambda i: (i, 0)),
                  pl.BlockSpec((t, SSM_XBC), lambda i: (nchunks - 1 - i, 0)),
                  pl.BlockSpec((t, DT_COLS), lambda i: (i, 0)),
                  pl.BlockSpec((t, DT_COLS), lambda i: (nchunks - 1 - i, 0)),
                  pl.BlockSpec((2, 1, 128), lambda i: (0, 0, 0)),
                  pl.BlockSpec((2, 1, 128), lambda i: (0, 0, 0))],
        out_specs=[pl.BlockSpec((1, SSM_GROUPS, SSM_STATE, hw), lambda i: (i, 0, 0, 0)),
                   pl.BlockSpec((1, SSM_GROUPS, SSM_STATE, hw), lambda i: (nchunks - 1 - i, 0, 0, 0))],
        out_shape=[jax.ShapeDtypeStruct(hshape, BF16)] * 2,
        scratch_shapes=[pltpu.VMEM((2, SSM_GROUPS, SSM_STATE, hw), F32)],
        compiler_params=_cparams(("arbitrary",), 32),
        name="ssd_states",
    )(xbc, xbc, dtraw, dtraw, dt_bias, a_log)


def _ssd_out_kernel(x_ref, dt_ref, bias_ref, alog_ref, hf_ref, hb_ref, z_ref, dskip_ref, gain_ref,
                    o_ref, y_ref):
    t = SSM_CHUNK
    n = SSM_STATE
    p = SSM_HEADDIM
    dtf, csf = _ssd_cumsums(dt_ref, bias_ref, alog_ref, 0)
    dtb, csb = _ssd_cumsums(dt_ref, bias_ref, alog_ref, 1)
    csf_t, csb_t, dtf_t, dtb_t = csf.T, csb.T, dtf.T, dtb.T
    dts_t = dtf_t + dtb_t
    row = lax.broadcasted_iota(jnp.int32, (t, t), 0)
    col = lax.broadcasted_iota(jnp.int32, (t, t), 1)
    low = col <= row
    low_strict = col < row
    up_strict = col > row
    for g in range(SSM_GROUPS):
        bg = x_ref[:, SSM_INNER + g * n:SSM_INNER + (g + 1) * n]
        cg = x_ref[:, SSM_INNER + SSM_GROUPS * n + g * n:SSM_INNER + SSM_GROUPS * n + (g + 1) * n]
        cb = lax.dot_general(cg, bg, (((1,), (1,)), ((), ())), preferred_element_type=F32)
        cg_f = cg.astype(F32)
        for hh in range(SSM_HPG):
            h = g * SSM_HPG + hh
            colf = jnp.broadcast_to(csf[:, h:h + 1], (t, t))
            colb = jnp.broadcast_to(csb[:, h:h + 1], (t, t))
            seg = jnp.where(low, colf - jnp.broadcast_to(csf_t[h:h + 1, :], (t, t)),
                            colb - jnp.broadcast_to(csb_t[h:h + 1, :], (t, t)))
            coef = jnp.where(low_strict, jnp.broadcast_to(dtf_t[h:h + 1, :], (t, t)),
                             jnp.where(up_strict, jnp.broadcast_to(dtb_t[h:h + 1, :], (t, t)),
                                       jnp.broadcast_to(dts_t[h:h + 1, :], (t, t))))
            mh = jnp.exp(seg) * cb * coef
            lhs = jnp.concatenate([mh, cg_f * jnp.exp(colf), cg_f * jnp.exp(colb)], axis=1).astype(BF16)
            rhs = jnp.concatenate([x_ref[:, h * p:(h + 1) * p],
                                   hf_ref[0, g, :, hh * p:(hh + 1) * p],
                                   hb_ref[0, g, :, hh * p:(hh + 1) * p]], axis=0)
            y_ref[:, h * p:(h + 1) * p] = jnp.dot(lhs, rhs, preferred_element_type=F32)
    y = y_ref[...] + x_ref[:, 0:SSM_INNER].astype(F32) * dskip_ref[...]
    y = y * _silu(z_ref[...].astype(F32))
    ms = jnp.mean(y * y, axis=-1, keepdims=True)
    o_ref[...] = (y * lax.rsqrt(ms + NORM_EPS) * gain_ref[...]).astype(o_ref.dtype)


def _ssd_out(xbc, dtraw, dt_bias, a_log, h_f, h_b, proj, d_skip, norm_g):
    m = xbc.shape[0]
    t = SSM_CHUNK
    w = SSM_INNER
    hw = SSM_HPG * SSM_HEADDIM
    hspec = pl.BlockSpec((1, SSM_GROUPS, SSM_STATE, hw), lambda i: (i, 0, 0, 0))
    return pl.pallas_call(
        _ssd_out_kernel,
        grid=(m // t,),
        in_specs=[pl.BlockSpec((t, SSM_XBC), lambda i: (i, 0)),
                  pl.BlockSpec((t, DT_COLS), lambda i: (i, 0)),
                  pl.BlockSpec((2, 1, 128), lambda i: (0, 0, 0)),
                  pl.BlockSpec((2, 1, 128), lambda i: (0, 0, 0)),
                  hspec, hspec,
                  pl.BlockSpec((t, w), lambda i: (i, COL_Z // w)),
                  pl.BlockSpec((1, w), lambda i: (0, 0)),
                  pl.BlockSpec((1, w), lambda i: (0, 0))],
        out_specs=pl.BlockSpec((t, w), lambda i: (i, 0)),
        out_shape=jax.ShapeDtypeStruct((m, w), BF16),
        scratch_shapes=[pltpu.VMEM((t, w), F32)],
        compiler_params=_cparams(("parallel",), 32),
        name="ssd_out",
    )(xbc, dtraw, dt_bias, a_log, h_f, h_b, proj,
      jnp.repeat(d_skip.astype(F32), SSM_HEADDIM).reshape(1, w), norm_g.reshape(1, w))


def _ssd(xbc, dtraw, dt_bias, a_log, proj, d_skip, norm_g, seq_bounds):
    pad = lambda v: jnp.pad(v.astype(F32), ((0, 0), (0, 128 - SSM_HEADS))).reshape(2, 1, 128)
    dt_bias, a_log = pad(dt_bias), pad(a_log)
    h_f, h_b = _ssd_states(xbc, dtraw, dt_bias, a_log, seq_bounds)
    return _ssd_out(xbc, dtraw, dt_bias, a_log, h_f, h_b, proj, d_skip, norm_g)


def _s5_selectors():
    t, c = S5_CHUNK, S5_GROUP
    lag = np.arange(S5_LAGS)[:, None]
    tok = (np.arange(S5_CW) // c)[None, :]
    e_exit_f = (lag == t - 1 - tok)
    e_exit_b = (lag == tok)
    e_in_f = (lag == tok + 1)
    e_in_b = (lag == t - tok)
    lagidx = (np.arange(2 * S5_CW) // c)[None, :]
    rel = lagidx - (t - 1)
    inb = lagidx <= 2 * t - 2
    e_k_f = (lag == rel) & (rel >= 0) & inb
    e_k_b = (lag == -rel) & (rel <= 0) & inb
    tile = (np.arange(c)[:, None] == (np.arange(2 * S5_CW) % c)[None, :])
    sel = np.stack([np.concatenate([a, b], axis=1) for a, b in
                    ((e_exit_f, e_in_f), (e_exit_b, e_in_b))])
    selk = np.stack([e_k_f, e_k_b])
    return (jnp.asarray(sel, BF16), jnp.asarray(selk, BF16), jnp.asarray(tile, BF16))


def _s5_prep_kernel(acol_ref, arow_ref, ls_ref, b_ref, bt_ref, ct_ref, sel_ref, selk_ref, tile_ref,
                    sign_ref, w_ref, ws_ref, wo_ref, lpa_ref, lpb_ref):
    t = S5_CHUNK
    cw = S5_CW
    p = S5_STATE
    tile = tile_ref[...]
    lagf = lax.broadcasted_iota(jnp.int32, (p, S5_LAGS), 1).astype(F32)
    kt = jnp.zeros((S5_GROUP, 2 * cw), F32)
    for d in range(2):
        step = jnp.exp(ls_ref[0, 0, d])
        are = acol_ref[0, 0, d, 0]
        aim = acol_ref[0, 0, d, 1]
        mag = are * step
        th = aim * step
        amp = jnp.exp(lagf * mag)
        pwr = amp * jnp.cos(lagf * th)
        pwi = amp * jnp.sin(lagf * th)
        lbr = jnp.exp(mag) * jnp.cos(th)
        lbi = jnp.exp(mag) * jnp.sin(th)
        den = are * are + aim * aim
        cfr = ((lbr - 1.0) * are + lbi * aim) / den
        cfi = (lbi * are - (lbr - 1.0) * aim) / den
        bre = b_ref[0, 0, 0]
        bim = b_ref[0, 0, 1]
        bbr = cfr * bre - cfi * bim
        bbi = cfr * bim + cfi * bre
        step_r = step
        are_r = arow_ref[0, 0, d, 0:1, 0:p]
        aim_r = arow_ref[0, 0, d, 1:2, 0:p]
        mag_r = are_r * step_r
        th_r = aim_r * step_r
        lbr_r = jnp.exp(mag_r) * jnp.cos(th_r)
        lbi_r = jnp.exp(mag_r) * jnp.sin(th_r)
        den_r = are_r * are_r + aim_r * aim_r
        cfr_r = ((lbr_r - 1.0) * are_r + lbi_r * aim_r) / den_r
        cfi_r = (lbi_r * are_r - (lbr_r - 1.0) * aim_r) / den_r
        btr = bt_ref[0, 0, 0]
        bti = bt_ref[0, 0, 1]
        bbr_t = cfr_r * btr - cfi_r * bti
        bbi_t = cfr_r * bti + cfi_r * btr
        ctr = ct_ref[0, 0, d, 0]
        cti = ct_ref[0, 0, d, 1]

        sel = sel_ref[d]
        er = _dot_sel(pwr, sel)
        ei = _dot_sel(pwi, sel)
        tb_r = _dot_sel(bbr, tile[:, 0:cw])
        tb_i = _dot_sel(bbi, tile[:, 0:cw])
        tc_r = _dot_sel(ctr, tile)
        tc_i = _dot_sel(cti, tile)
        ws_ref[0, 0, d * 2 * p:d * 2 * p + p, :] = (er[:, 0:cw] * tb_r - ei[:, 0:cw] * tb_i).astype(BF16)
        ws_ref[0, 0, d * 2 * p + p:(d + 1) * 2 * p, :] = (er[:, 0:cw] * tb_i + ei[:, 0:cw] * tb_r).astype(BF16)
        zr = er[:, cw:] * tc_r[:, 0:cw] - ei[:, cw:] * tc_i[:, 0:cw]
        zi = er[:, cw:] * tc_i[:, 0:cw] + ei[:, cw:] * tc_r[:, 0:cw]
        wo_ref[0, 0, d * 2 * p:d * 2 * p + p, :] = zr.astype(BF16)
        wo_ref[0, 0, d * 2 * p + p:(d + 1) * 2 * p, :] = (-zi).astype(BF16)
        selk = selk_ref[d]
        qr = _dot_sel(pwr, selk) * tc_r - _dot_sel(pwi, selk) * tc_i
        qi = _dot_sel(pwr, selk) * tc_i + _dot_sel(pwi, selk) * tc_r
        kt = kt + _dot_f32(bbr_t, qr) - _dot_f32(bbi_t, qi)
        are2 = arow_ref[0, 0, d, 0:1, :]
        aim2 = arow_ref[0, 0, d, 1:2, :]
        ampt = jnp.exp(are2 * step_r * float(t))
        zr2 = ampt * jnp.cos(aim2 * step_r * float(t))
        zi2 = ampt * jnp.sin(aim2 * step_r * float(t))
        sign = sign_ref[...]
        for k in range(S5_SCAN_STEPS):
            lpa_ref[0, 0, d, k:k + 1, :] = zr2
            lpb_ref[0, 0, d, k:k + 1, :] = sign * zi2
            zr2, zi2 = zr2 * zr2 - zi2 * zi2, 2.0 * zr2 * zi2
    for s in range(t):
        off = (t - 1 - s) * S5_GROUP
        w_ref[0, 0, s * S5_GROUP:(s + 1) * S5_GROUP, :] = kt[:, off:off + cw].astype(BF16)


def _s5_prep(a_re, a_im, log_step, b_re, b_im, c_re, c_im):
    depth = a_re.shape[0]
    g, p, c = S5_GROUPS, S5_STATE, S5_GROUP
    a = jnp.stack([a_re, a_im], axis=2).astype(F32)
    a = a.transpose(0, 3, 1, 2, 4)
    acol = a[..., None]
    arow = jnp.concatenate([a, a], axis=-1)
    ls = log_step.astype(F32).transpose(0, 2, 1).reshape(depth, g, 2, 1, 1)
    b = jnp.stack([b_re, b_im], axis=2).astype(F32)
    bt = b.transpose(0, 1, 2, 4, 3)
    ct = jnp.stack([c_re, c_im], axis=3).astype(F32)
    ct = ct.transpose(0, 2, 1, 3, 5, 4)
    sel, selk, tile = _s5_selectors()
    sign = jnp.concatenate([-jnp.ones((1, p), F32), jnp.ones((1, p), F32)], axis=1)
    full = lambda shp: pl.BlockSpec(shp, lambda l, j: (0,) * len(shp))
    per = lambda shp: pl.BlockSpec((1, 1) + shp, lambda l, j: (l, j) + (0,) * len(shp))
    cw = S5_CW
    return pl.pallas_call(
        _s5_prep_kernel,
        grid=(depth, g),
        in_specs=[per((2, 2, p, 1)), per((2, 2, 2 * p)), per((2, 1, 1)), per((2, p, c)), per((2, c, p)),
                  per((2, 2, p, c)), full((2, S5_LAGS, 2 * cw)), full((2, S5_LAGS, 2 * cw)),
                  full((c, 2 * cw)), full((1, 2 * p))],
        out_specs=[per((cw, cw)), per((4 * p, cw)), per((4 * p, cw)),
                   per((2, S5_SCAN_STEPS, 2 * p)), per((2, S5_SCAN_STEPS, 2 * p))],
        out_shape=[jax.ShapeDtypeStruct((depth, g, cw, cw), BF16),
                   jax.ShapeDtypeStruct((depth, g, 4 * p, cw), BF16),
                   jax.ShapeDtypeStruct((depth, g, 4 * p, cw), BF16),
                   jax.ShapeDtypeStruct((depth, g, 2, S5_SCAN_STEPS, 2 * p), F32),
                   jax.ShapeDtypeStruct((depth, g, 2, S5_SCAN_STEPS, 2 * p), F32)],
        compiler_params=_cparams(("parallel", "parallel"), 32),
        name="s5_prep",
    )(acol, arow, ls, b, bt, ct, sel, selk, tile, sign)


def _gelu_tanh(x):
    return 0.5 * x * (1.0 + jnp.tanh(math.sqrt(2.0 / math.pi) * (x + 0.044715 * (x * x * x))))


def _s5_kernel(chunk_bounds, u_ref, w_ref, ws_ref, wo_ref, lpa_ref, lpb_ref, d_ref, o_ref):
    r = u_ref.shape[1]
    p2 = 2 * S5_STATE
    u = u_ref[0]
    y = jnp.dot(u, w_ref[0], preferred_element_type=F32)
    st = lax.dot_general(u, ws_ref[0], (((1,), (1,)), ((), ())), preferred_element_type=F32)
    ridx = lax.broadcasted_iota(jnp.int32, (r, p2), 0)
    rloc = jnp.zeros((r, p2), jnp.int32)
    rlen = jnp.zeros((r, p2), jnp.int32)
    for s0, ln in chunk_bounds:
        inside = (ridx >= s0) & (ridx < s0 + ln)
        rloc = jnp.where(inside, ridx - s0, rloc)
        rlen = jnp.where(inside, ln, rlen)
    xin = []
    for d in range(2):
        x = st[:, d * p2:(d + 1) * p2]
        for k in range(S5_SCAN_STEPS):
            sh = 1 << k
            if d == 0:
                prev = jnp.where(rloc >= sh, pltpu.roll(x, sh, 0), 0.0)
            else:
                prev = jnp.where(rloc < rlen - sh, pltpu.roll(x, r - sh, 0), 0.0)
            x = x + lpa_ref[0, d, k:k + 1, :] * prev + lpb_ref[0, d, k:k + 1, :] * pltpu.roll(prev, S5_STATE, 1)
        if d == 0:
            xin.append(jnp.where(rloc >= 1, pltpu.roll(x, 1, 0), 0.0))
        else:
            xin.append(jnp.where(rloc < rlen - 1, pltpu.roll(x, r - 1, 0), 0.0))
    xin = jnp.concatenate(xin, axis=1).astype(BF16)
    y = y + jnp.dot(xin, wo_ref[0], preferred_element_type=F32)
    y = y + u.astype(F32) * d_ref[0]
    o_ref[0] = _gelu_tanh(y).astype(o_ref.dtype)


def _s5(u_chunks, w, ws, wo, lpa, lpb, d_tiled, chunk_bounds):
    g, r, cw = u_chunks.shape
    p = S5_STATE
    per = lambda shp: pl.BlockSpec((1,) + shp, lambda j: (j,) + (0,) * len(shp))
    return pl.pallas_call(
        functools.partial(_s5_kernel, chunk_bounds),
        grid=(g,),
        in_specs=[per((r, cw)), per((cw, cw)), per((4 * p, cw)), per((4 * p, cw)),
                  per((2, S5_SCAN_STEPS, 2 * p)), per((2, S5_SCAN_STEPS, 2 * p)), per((1, cw))],
        out_specs=per((r, cw)),
        out_shape=jax.ShapeDtypeStruct((g, r, cw), BF16),
        compiler_params=_cparams(("parallel",), 32),
        name="s5_mix",
    )(u_chunks, w, ws, wo, lpa, lpb, d_tiled)


def _t5_bucket(rel):
    half = REL_BUCKETS // 2
    exact = half // 2
    sign = (rel > 0).astype(np.int32) * half
    n = np.abs(rel)
    large = exact + (np.log(np.maximum(n, 1) / exact) / np.log(REL_MAX_DIST / exact)
                     * (half - exact)).astype(np.int32)
    large = np.minimum(large, half - 1)
    return sign + np.where(n < exact, n, large)


def _att_bias_kernel(gi, idx_ref, tbl_ref, o_ref):
    idx = idx_ref[...]
    for h in range(ATT_HPG):
        acc = jnp.full(idx.shape, NEG_INF, F32)
        for b in range(REL_BUCKETS):
            acc = jnp.where(idx == b, tbl_ref[b, gi * ATT_HPG + h], acc)
        o_ref[h] = acc


def _att_tile(seq_bounds, dil):
    return min(256, min(l for _, l in seq_bounds) // dil)


def _att_bias(rel_bias, gi, tq):
    dil = ATT_PATTERNS[gi][1]
    tk = tq + 2 * ATT_HALF
    rel = (np.arange(tk)[None, :] - ATT_HALF) - np.arange(tq)[:, None]
    idx = np.where(np.abs(rel) <= ATT_HALF, _t5_bucket(rel * dil), -1).astype(np.int32)
    return pl.pallas_call(
        functools.partial(_att_bias_kernel, gi),
        in_specs=[pl.BlockSpec(memory_space=pltpu.VMEM), pl.BlockSpec(memory_space=pltpu.SMEM)],
        out_specs=pl.BlockSpec(memory_space=pltpu.VMEM),
        out_shape=jax.ShapeDtypeStruct((ATT_HPG, tq, tk), F32),
        name=f"attention_bias_{gi}",
    )(jnp.asarray(idx), rel_bias.astype(F32))


def _attn_kernel(tq, blk_bounds, q_ref, kp_ref, km_ref, kn_ref, vp_ref, vm_ref, vn_ref, bias_ref,
                 o_ref, lse_ref):
    jb = pl.program_id(1)
    is_first, is_last = _seq_flags(jb, 1, blk_bounds)
    tk = tq + 2 * ATT_HALF
    colk = lax.broadcasted_iota(jnp.int32, (tq, tk), 1)
    valid = jnp.logical_and(jnp.logical_or(colk >= ATT_HALF, jnp.logical_not(is_first)),
                            jnp.logical_or(colk < tq + ATT_HALF, jnp.logical_not(is_last)))
    kcat = jnp.concatenate([kp_ref[...], km_ref[...], kn_ref[...]], axis=0)
    vcat = jnp.concatenate([vp_ref[...], vm_ref[...], vn_ref[...]], axis=0)
    scale = ATT_HEAD_DIM ** -0.5
    lane = lax.broadcasted_iota(jnp.int32, (tq, ATT_HEAD_DIM), 1)
    lse_tile = jnp.zeros((tq, ATT_HEAD_DIM), F32)
    for h in range(ATT_HPG):
        sl = slice(h * ATT_HEAD_DIM, (h + 1) * ATT_HEAD_DIM)
        s = lax.dot_general(q_ref[:, sl], kcat[:, sl], (((1,), (1,)), ((), ())),
                            preferred_element_type=F32)
        s = jnp.where(valid, s * scale + bias_ref[h], NEG_INF)
        mx = jnp.max(s, axis=-1, keepdims=True)
        pr = jnp.exp(s - mx)
        den = jnp.sum(pr, axis=-1, keepdims=True)
        o = jnp.dot(pr.astype(BF16), vcat[:, sl], preferred_element_type=F32)
        o_ref[:, sl] = (o / den).astype(o_ref.dtype)
        lse_tile = jnp.where(lane == h, mx + jnp.log(den), lse_tile)
    lse_ref[...] = lse_tile


def _attention_group(proj, bias, gi, seq_bounds):
    window, dil = ATT_PATTERNS[gi]
    assert window // (2 * dil) == ATT_HALF
    m = proj.shape[0]
    nview = m // dil
    tq = bias.shape[1]
    assert tq % ATT_HALF == 0 and all((s // dil) % tq == 0 and (l // dil) % tq == 0 for s, l in seq_bounds)
    blk_bounds = tuple((s // dil // tq, l // dil // tq) for s, l in seq_bounds)
    tk = tq + 2 * ATT_HALF
    col0 = COL_QKV + gi * 3 * ATT_GW
    if dil == 1:
        src, cb = proj, col0 // ATT_GW
    else:
        src = proj[:, col0:col0 + 3 * ATT_GW].reshape(nview, dil, 3 * ATT_GW)
        src, cb = src.transpose(1, 0, 2).reshape(m, 3 * ATT_GW), 0
    nb = nview // tq
    hb = tq // ATT_HALF
    lasth = m // ATT_HALF - 1

    def main(col):
        return pl.BlockSpec((tq, ATT_GW), lambda r, jb: (r * nb + jb, cb + col))

    def prev(col):
        return pl.BlockSpec((ATT_HALF, ATT_GW), lambda r, jb: (jnp.maximum((r * nb + jb) * hb - 1, 0), cb + col))

    def nxt(col):
        return pl.BlockSpec((ATT_HALF, ATT_GW), lambda r, jb: (jnp.minimum((r * nb + jb + 1) * hb, lasth), cb + col))

    o, lse = pl.pallas_call(
        functools.partial(_attn_kernel, tq, blk_bounds),
        grid=(dil, nb),
        in_specs=[main(0), prev(1), main(1), nxt(1), prev(2), main(2), nxt(2),
                  pl.BlockSpec((ATT_HPG, tq, tk), lambda r, jb: (0, 0, 0))],
        out_specs=[pl.BlockSpec((tq, ATT_GW), lambda r, jb: (r * nb + jb, 0)),
                   pl.BlockSpec((tq, ATT_HEAD_DIM), lambda r, jb: (r * nb + jb, 0))],
        out_shape=[jax.ShapeDtypeStruct((m, ATT_GW), BF16), jax.ShapeDtypeStruct((m, ATT_HEAD_DIM), F32)],
        compiler_params=_cparams(("parallel", "parallel"), 32),
        name=f"dilated_attention_{gi}",
    )(src, src, src, src, src, src, src, bias)
    if dil > 1:
        unperm = lambda a: a.reshape(dil, nview, a.shape[-1]).transpose(1, 0, 2).reshape(m, a.shape[-1])
        o, lse = unperm(o), unperm(lse)
    return o, lse


def _att_combine_kernel(o0, l0, o1, l1, o2, l2, out_ref):
    a, b, c = l0[...], l1[...], l2[...]
    mx = jnp.maximum(jnp.maximum(a, b), c)
    ea, eb, ec = jnp.exp(a - mx), jnp.exp(b - mx), jnp.exp(c - mx)
    inv = 1.0 / (ea + eb + ec)
    wa, wb, wc = ea * inv, eb * inv, ec * inv
    tr = out_ref.shape[0]
    for h in range(ATT_HPG):
        sl = slice(h * ATT_HEAD_DIM, (h + 1) * ATT_HEAD_DIM)
        bc = lambda w: jnp.broadcast_to(w[:, h:h + 1], (tr, ATT_HEAD_DIM))
        out_ref[:, sl] = (bc(wa) * o0[:, sl].astype(F32) + bc(wb) * o1[:, sl].astype(F32)
                          + bc(wc) * o2[:, sl].astype(F32)).astype(out_ref.dtype)


def _att_combine(outs, lses, tr=512):
    m, w = outs[0].shape
    ospec = pl.BlockSpec((tr, w), lambda i: (i, 0))
    lspec = pl.BlockSpec((tr, ATT_HEAD_DIM), lambda i: (i, 0))
    args = [x for pair in zip(outs, lses) for x in pair]
    return pl.pallas_call(
        _att_combine_kernel,
        grid=(m // tr,),
        in_specs=[ospec, lspec] * 3,
        out_specs=ospec,
        out_shape=jax.ShapeDtypeStruct((m, w), BF16),
        compiler_params=_cparams(("parallel",), 32),
        name="attention_combine",
    )(*args)


def _pack_w_in(w):
    splits = np.cumsum([SSM_INNER, SSM_XBC, 2 * SSM_HEADS, S5_WIDTH, 3 * 1536]).tolist()
    z, xbc, dt, u, qkv, gates = jnp.split(w, splits, axis=-1)
    ng = len(ATT_PATTERNS)
    qkv = qkv.reshape(w.shape[0], 3, ng, ATT_GW).transpose(0, 2, 1, 3).reshape(w.shape[0], 3 * ng * ATT_GW)
    main = jnp.concatenate([z, xbc, u, qkv, gates], axis=-1).astype(BF16)
    zpad = jnp.zeros((w.shape[0], 128 - SSM_HEADS), w.dtype)
    wdt = jnp.concatenate([dt[:, :SSM_HEADS], zpad, dt[:, SSM_HEADS:], zpad], axis=-1).astype(BF16)
    return main, wdt


def _pack_ffn(w_up, conv_w, conv_b, w_down):
    padc = lambda a: jnp.pad(a, [(0, 0)] * (a.ndim - 1) + [(0, D_FF_PAD - D_FF)])
    both = lambda a: jnp.concatenate([padc(a[..., :D_FF]), padc(a[..., D_FF:])], axis=-1)
    w_down_p = jnp.pad(w_down, ((0, D_FF_PAD - D_FF), (0, 0))).astype(BF16)
    return both(w_up).astype(BF16), both(conv_w).astype(F32), both(conv_b).astype(F32), w_down_p


def _trunk(x, seq_bounds, rel_bias, norm_mix, w_in, ssm_conv_w, ssm_conv_b, ssm_a_log, ssm_dt_bias, ssm_d,
           ssm_norm, ssm_w_out, s5_tables, s5_d, s5_w_glu, att_w_out, w_o, norm_ffn, w_up, ffn_conv_w,
           ffn_conv_b, w_down, final_norm):
    m = x.shape[0]
    depth = w_in.shape[0]
    tm = min(1024, m)
    tm2 = min(512, m)
    gate_blk = lambda b, tn: (COL_GATES + b * D_MODEL) // tn
    chunk_bounds = tuple((s // S5_CHUNK, l // S5_CHUNK) for s, l in seq_bounds)
    r_chunks = m // S5_CHUNK
    att_bias = [_att_bias(rel_bias, gi, _att_tile(seq_bounds, dil)) for gi, (_, dil) in enumerate(ATT_PATTERNS)]
    gated =lambda acc, gate: _sigmoid(gate.astype(F32)) * acc
    gated_add = lambda acc, gate, prev: prev + _sigmoid(gate.astype(F32)) * acc
    resid = lambda acc, res: res + acc

    for li in range(depth):
        h = _rmsnorm(x, norm_mix[li], BF16)
        w_main, w_dt = _pack_w_in(w_in[li])
        proj = _matmul(h, w_main, tm=tm, tn=512, out_dtype=BF16, name="in_proj")
        dtraw = _matmul(h, w_dt, tm=tm, tn=DT_COLS, out_dtype=F32, name="dt_proj")

        xbc = _conv_silu(proj, COL_XBC // 512, SSM_XBC, ssm_conv_w[li].astype(F32), ssm_conv_b[li].astype(F32),
                         seq_bounds, tr=tm2)
        y_a = _ssd(xbc, dtraw, ssm_dt_bias[li], ssm_a_log[li], proj, ssm_d[li], ssm_norm[li].astype(F32),
                   seq_bounds)
        merged = _matmul(y_a, ssm_w_out[li].astype(BF16), tm=tm, tn=512, out_dtype=F32, epilogue=gated,
                         extras=((proj, gate_blk(0, 512)),), name="ssm_out_proj")

        w5, ws5, wo5, lpa, lpb = (tbl[li] for tbl in s5_tables)
        u = proj[:, COL_U:COL_U + S5_WIDTH].reshape(r_chunks, S5_CHUNK, S5_GROUPS, S5_GROUP)
        u = u.transpose(2, 0, 1, 3).reshape(S5_GROUPS, r_chunks, S5_CW)
        d_tiled = jnp.tile(s5_d[li].astype(F32).reshape(S5_GROUPS, 1, S5_GROUP), (1, 1, S5_CHUNK))
        y_b = _s5(u, w5, ws5, wo5, lpa, lpb, d_tiled, chunk_bounds)
        y_b = y_b.reshape(S5_GROUPS, r_chunks, S5_CHUNK, S5_GROUP).transpose(1, 2, 0, 3).reshape(m, S5_WIDTH)
        merged = _glu_matmul(y_b, s5_w_glu[li].astype(BF16), proj, merged, tm=tm, tn=512,
                             gate_blk=gate_blk(1, 512))

        outs, lses = zip(*[_attention_group(proj, att_bias[gi], gi, seq_bounds)
                           for gi in range(len(ATT_PATTERNS))])
        comb = _att_combine(outs, lses, tr=tm2)
        merged = _matmul(comb, att_w_out[li].astype(BF16), tm=tm, tn=512, out_dtype=BF16, epilogue=gated_add,
                         extras=((proj, gate_blk(2, 512)), (merged, 0)), name="att_out_proj")

        x = _matmul(merged, w_o[li].astype(BF16), tm=tm, tn=512, out_dtype=F32, epilogue=resid,
                    extras=((x, 0),), name="mix_out_proj")

        h = _rmsnorm(x, norm_ffn[li], BF16)
        w_up_p, cw_p, cb_p, w_down_p = _pack_ffn(w_up[li], ffn_conv_w[li], ffn_conv_b[li], w_down[li])
        up = _matmul(h, w_up_p, tm=tm, tn=512, out_dtype=BF16, name="ffn_up")
        act = _ffn_act(up, cw_p, cb_p, seq_bounds, tr=tm2)
        x = _matmul(act, w_down_p, tm=tm2, tn=512, out_dtype=F32, epilogue=resid, extras=((x, 0),),
                    name="ffn_down")
    return _rmsnorm(x, final_norm, F32)


def kernel(x_prompt, x_sample, rel_bias, norm_mix, w_in, ssm_conv_w, ssm_conv_b, ssm_a_log, ssm_dt_bias, ssm_d, ssm_norm, ssm_w_out, s5_a_re, s5_a_im, s5_log_step, s5_b_re, s5_b_im, s5_c_re, s5_c_im, s5_d, s5_w_glu, att_w_out, w_o, norm_ffn, w_up, ffn_conv_w, ffn_conv_b, w_down, final_norm):
    d = x_prompt.shape[-1]
    seq_bounds = []
    row = 0
    for arr in (x_prompt, x_sample):
        for _ in range(arr.shape[0]):
            seq_bounds.append((row, arr.shape[1]))
            row += arr.shape[1]
    seq_bounds = tuple(seq_bounds)
    x = jnp.concatenate([x_prompt.reshape(-1, d), x_sample.reshape(-1, d)], axis=0)
    s5_tables = _s5_prep(s5_a_re, s5_a_im, s5_log_step, s5_b_re, s5_b_im, s5_c_re, s5_c_im)
    y = _trunk(x, seq_bounds, rel_bias, norm_mix, w_in, ssm_conv_w, ssm_conv_b, ssm_a_log, ssm_dt_bias, ssm_d,
               ssm_norm, ssm_w_out, s5_tables, s5_d, s5_w_glu, att_w_out, w_o, norm_ffn, w_up, ffn_conv_w,
               ffn_conv_b, w_down, final_norm)
    n_prompt = x_prompt.shape[0] * x_prompt.shape[1]
    return (y[:n_prompt].reshape(x_prompt.shape), y[n_prompt:].reshape(x_sample.shape))
```

```python
import functools
import math

import numpy as np
import jax
import jax.numpy as jnp
from jax import lax
from jax.experimental import pallas as pl
from jax.experimental.pallas import tpu as pltpu

F32 = jnp.float32
BF16 = jnp.bfloat16

D_MODEL = 2048
NORM_EPS = 1e-6
NEG_INF = -1e30

SSM_HEADDIM = 64
SSM_INNER = 1536
SSM_HEADS = 24
SSM_GROUPS = 4
SSM_HPG = SSM_HEADS // SSM_GROUPS
SSM_STATE = 128
SSM_CONV = 5
SSM_CHUNK = 128
SSM_XBC = 2560

S5_WIDTH = 1024
S5_GROUP = 16
S5_GROUPS = 64
S5_STATE = 64
S5_CHUNK = 32
S5_CW = S5_CHUNK * S5_GROUP
S5_LAGS = 128
S5_SCAN_STEPS = 8

ATT_HEAD_DIM = 128
ATT_HPG = 4
ATT_PATTERNS = ((128, 1), (512, 4), (2048, 16))
ATT_GW = ATT_HPG * ATT_HEAD_DIM
ATT_HALF = 64
REL_BUCKETS = 32
REL_MAX_DIST = 1024

D_FF = 5504
D_FF_PAD = 5632
FFN_CONV = 3

COL_Z = 0
COL_XBC = 1536
COL_U = 4096
COL_QKV = 5120
COL_GATES = 9728
IN_COLS = 15872
DT_COLS = 256

V7X_VMEM_BYTES = 64 * 1024 * 1024


def _cparams(sem, vmem_mb):
    return pltpu.CompilerParams(dimension_semantics=sem, vmem_limit_bytes=vmem_mb * 1024 * 1024)


def _sigmoid(x):
    return 1.0 / (1.0 + jnp.exp(-x))


def _silu(x):
    return x * _sigmoid(x)


def _split3(x):
    hi = x.astype(BF16)
    r1 = x - hi.astype(F32)
    mid = r1.astype(BF16)
    lo = (r1 - mid.astype(F32)).astype(BF16)
    return hi, mid, lo


def _dot_sel(x, sel):
    hi, mid, lo = _split3(x)
    d = lambda a: jnp.dot(a, sel, preferred_element_type=F32)
    return d(hi) + d(mid) + d(lo)


def _sel_dot(sel, x):
    hi, mid, lo = _split3(x)
    d = lambda a: jnp.dot(sel, a, preferred_element_type=F32)
    return d(hi) + d(mid) + d(lo)


def _dot_f32(a, b):
    ah, am, _ = _split3(a)
    bh, bm, _ = _split3(b)
    d = lambda x, y: jnp.dot(x, y, preferred_element_type=F32)
    return d(ah, bh) + d(ah, bm) + d(am, bh)


def _rmsnorm_kernel(x_ref, g_ref, o_ref):
    x = x_ref[...]
    ms = jnp.mean(x * x, axis=-1, keepdims=True)
    o_ref[...] = (x * lax.rsqrt(ms + NORM_EPS) * g_ref[...]).astype(o_ref.dtype)


def _rmsnorm(x, gain, out_dtype, tr=512):
    m, d = x.shape
    return pl.pallas_call(
        _rmsnorm_kernel,
        grid=(m // tr,),
        in_specs=[pl.BlockSpec((tr, d), lambda i: (i, 0)),
                  pl.BlockSpec((1, d), lambda i: (0, 0))],
        out_specs=pl.BlockSpec((tr, d), lambda i: (i, 0)),
        out_shape=jax.ShapeDtypeStruct((m, d), out_dtype),
        compiler_params=_cparams(("parallel",), 32),
        name="rmsnorm",
    )(x, gain.reshape(1, d))


def _mm_kernel(epilogue, n_extra, a_ref, b_ref, *rest):
    extras = rest[:n_extra]
    o_ref = rest[n_extra]
    acc = jnp.dot(a_ref[...], b_ref[...], preferred_element_type=F32)
    if epilogue is not None:
        acc = epilogue(acc, *[e[...] for e in extras])
    o_ref[...] = acc.astype(o_ref.dtype)


def _matmul(a, b, *, tm, tn, out_dtype, k=None, a_kblk=0, epilogue=None, extras=(), vmem_mb=48,
            name="matmul"):
    m = a.shape[0]
    kk, n = b.shape
    if k is None:
        k = a.shape[1]
    assert k == kk and m % tm == 0 and n % tn == 0
    in_specs = [pl.BlockSpec((tm, k), lambda i, j: (i, a_kblk)),
                pl.BlockSpec((k, tn), lambda i, j: (0, j))]
    args = [a, b]
    for arr, off in extras:
        in_specs.append(pl.BlockSpec((tm, tn), functools.partial(lambda i, j, o: (i, j + o), o=off)))
        args.append(arr)
    return pl.pallas_call(
        functools.partial(_mm_kernel, epilogue, len(extras)),
        grid=(m // tm, n // tn),
        in_specs=in_specs,
        out_specs=pl.BlockSpec((tm, tn), lambda i, j: (i, j)),
        out_shape=jax.ShapeDtypeStruct((m, n), out_dtype),
        compiler_params=_cparams(("parallel", "parallel"), vmem_mb),
        name=name,
    )(*args)


def _glu_kernel(a_ref, bv_ref, bg_ref, gate_ref, acc_ref, o_ref):
    a = a_ref[...]
    val = jnp.dot(a, bv_ref[...], preferred_element_type=F32)
    gate = jnp.dot(a, bg_ref[...], preferred_element_type=F32)
    branch = val * _sigmoid(gate)
    o_ref[...] = acc_ref[...] + _sigmoid(gate_ref[...].astype(F32)) * branch


def _glu_matmul(a, w, proj, merged, *, tm, tn, gate_blk):
    m, k = a.shape
    n = w.shape[1] // 2
    nb = n // tn
    return pl.pallas_call(
        _glu_kernel,
        grid=(m // tm, nb),
        in_specs=[pl.BlockSpec((tm, k), lambda i, j: (i, 0)),
                  pl.BlockSpec((k, tn), lambda i, j: (0, j)),
                  pl.BlockSpec((k, tn), lambda i, j: (0, j + nb)),
                  pl.BlockSpec((tm, tn), lambda i, j: (i, j + gate_blk)),
                  pl.BlockSpec((tm, tn), lambda i, j: (i, j))],
        out_specs=pl.BlockSpec((tm, tn), lambda i, j: (i, j)),
        out_shape=jax.ShapeDtypeStruct((m, n), F32),
        compiler_params=_cparams(("parallel", "parallel"), 48),
        name="s5_glu_matmul",
    )(a, w, w, proj, merged)


HALO = 16


def _seq_flags(row0, nrows, seq_bounds):
    starts = [s for s, _ in seq_bounds]
    ends = [s + l for s, l in seq_bounds]
    is_start = functools.reduce(jnp.logical_or, [row0 == s for s in starts])
    is_end = functools.reduce(jnp.logical_or, [row0 + nrows == e for e in ends])
    return is_start, is_end


def _conv_taps(buf_ref, w_ref, b_ref, width, tr):
    pad = width // 2
    acc = None
    for kk in range(width):
        term = buf_ref[pl.ds(HALO - pad + kk, tr), :] * w_ref[kk:kk + 1, :]
        acc = term if acc is None else acc + term
    return acc + b_ref[...]


def _norm_rows(x, gain):
    ms = jnp.mean(x * x, axis=-1, keepdims=True)
    return (x * lax.rsqrt(ms + NORM_EPS) * gain).astype(BF16)


def _fill_normed(hn_ref, xp_ref, xm_ref, xn_ref, gain_ref, tm, seq_bounds):
    is_start, is_end = _seq_flags(pl.program_id(0) * tm, tm, seq_bounds)
    gain = gain_ref[...]
    hn_ref[0:HALO, :] = jnp.where(is_start, 0.0, _norm_rows(xp_ref[...], gain)).astype(BF16)
    hn_ref[HALO:HALO + tm, :] = _norm_rows(xm_ref[...], gain)
    hn_ref[HALO + tm:HALO + tm + HALO, :] = jnp.where(is_end, 0.0, _norm_rows(xn_ref[...], gain)).astype(BF16)


def _row_halo_specs(tm, d, nrows):
    hb = tm // HALO
    last = nrows // HALO - 1
    return [pl.BlockSpec((HALO, d), lambda i, j: (jnp.maximum(i * hb - 1, 0), 0)),
            pl.BlockSpec((tm, d), lambda i, j: (i, 0)),
            pl.BlockSpec((HALO, d), lambda i, j: (jnp.minimum((i + 1) * hb, last), 0))]


def _in_proj_kernel(tm, seq_bounds, conv_lo, conv_hi, xp_ref, xm_ref, xn_ref, gain_ref, w_ref, wdt_ref,
                    cw_ref, cb_ref, o_ref, dt_ref, hn_ref, acc_ref):
    j = pl.program_id(1)

    @pl.when(j == 0)
    def _():
        _fill_normed(hn_ref, xp_ref, xm_ref, xn_ref, gain_ref, tm, seq_bounds)
        dt_ref[...] = jnp.dot(hn_ref[HALO:HALO + tm, :], wdt_ref[...], preferred_element_type=F32)

    is_conv = jnp.logical_and(j >= conv_lo, j < conv_hi)

    @pl.when(is_conv)
    def _():
        acc_ref[...] = jnp.dot(hn_ref[...], w_ref[...], preferred_element_type=F32)
        o_ref[...] = _silu(_conv_taps(acc_ref, cw_ref, cb_ref, SSM_CONV, tm)).astype(o_ref.dtype)

    @pl.when(jnp.logical_not(is_conv))
    def _():
        o_ref[...] = jnp.dot(hn_ref[HALO:HALO + tm, :], w_ref[...],
                             preferred_element_type=F32).astype(o_ref.dtype)


def _in_proj(x, gain, w_main, w_dt, conv_w, conv_b, seq_bounds, tm, tn=512):
    m, d = x.shape
    n = w_main.shape[1]
    conv_lo, conv_hi = COL_XBC // tn, (COL_XBC + SSM_XBC) // tn
    cidx = lambda i, j: (0, jnp.clip(j - conv_lo, 0, conv_hi - conv_lo - 1))
    return pl.pallas_call(
        functools.partial(_in_proj_kernel, tm, seq_bounds, conv_lo, conv_hi),
        grid=(m // tm, n // tn),
        in_specs=_row_halo_specs(tm, d, m) + [
            pl.BlockSpec((1, d), lambda i, j: (0, 0)),
            pl.BlockSpec((d, tn), lambda i, j: (0, j)),
            pl.BlockSpec((d, DT_COLS), lambda i, j: (0, 0)),
            pl.BlockSpec((SSM_CONV, tn), cidx),
            pl.BlockSpec((1, tn), cidx)],
        out_specs=[pl.BlockSpec((tm, tn), lambda i, j: (i, j)),
                   pl.BlockSpec((tm, DT_COLS), lambda i, j: (i, 0))],
        out_shape=[jax.ShapeDtypeStruct((m, n), BF16), jax.ShapeDtypeStruct((m, DT_COLS), F32)],
        scratch_shapes=[pltpu.VMEM((tm + 2 * HALO, d), BF16), pltpu.VMEM((tm + 2 * HALO, tn), F32)],
        compiler_params=_cparams(("parallel", "arbitrary"), 56),
        name="in_proj",
    )(x, x, x, gain.reshape(1, d), w_main, w_dt, conv_w, conv_b.reshape(1, -1))


def _ffn_up_kernel(tm, seq_bounds, xp_ref, xm_ref, xn_ref, gain_ref, wg_ref, wv_ref, cwg_ref, cbg_ref,
                   cwv_ref, cbv_ref, o_ref, hn_ref, sg_ref, sv_ref):
    @pl.when(pl.program_id(1) == 0)
    def _():
        _fill_normed(hn_ref, xp_ref, xm_ref, xn_ref, gain_ref, tm, seq_bounds)

    hn = hn_ref[...]
    sg_ref[...] = jnp.dot(hn, wg_ref[...], preferred_element_type=F32)
    sv_ref[...] = jnp.dot(hn, wv_ref[...], preferred_element_type=F32)
    gate = _conv_taps(sg_ref, cwg_ref, cbg_ref, FFN_CONV, tm)
    val = _conv_taps(sv_ref, cwv_ref, cbv_ref, FFN_CONV, tm)
    o_ref[...] = (_silu(gate) * val).astype(o_ref.dtype)


def _ffn_up(x, gain, w, conv_w, conv_b, seq_bounds, tm, tn=512):
    m, d = x.shape
    nb = D_FF_PAD // tn
    col = lambda rows, off: pl.BlockSpec((rows, tn), functools.partial(lambda i, j, o: (0, j + o), o=off))
    cb2 = conv_b.reshape(1, -1)
    return pl.pallas_call(
        functools.partial(_ffn_up_kernel, tm, seq_bounds),
        grid=(m // tm, nb),
        in_specs=_row_halo_specs(tm, d, m) + [
            pl.BlockSpec((1, d), lambda i, j: (0, 0)),
            col(d, 0), col(d, nb), col(FFN_CONV, 0), col(1, 0), col(FFN_CONV, nb), col(1, nb)],
        out_specs=pl.BlockSpec((tm, tn), lambda i, j: (i, j)),
        out_shape=jax.ShapeDtypeStruct((m, D_FF_PAD), BF16),
        scratch_shapes=[pltpu.VMEM((tm + 2 * HALO, d), BF16), pltpu.VMEM((tm + 2 * HALO, tn), F32),
                        pltpu.VMEM((tm + 2 * HALO, tn), F32)],
        compiler_params=_cparams(("parallel", "arbitrary"), 56),
        name="ffn_up",
    )(x, x, x, gain.reshape(1, d), w, w, conv_w, cb2, conv_w, cb2)


def _softplus(x):
    return jnp.maximum(x, 0.0) + jnp.log1p(jnp.exp(-jnp.abs(x)))


def _ssd_cumsums(dt_ref, bias_ref, alog_ref, d, row0=0):
    t = SSM_CHUNK
    dt = _softplus(dt_ref[row0:row0 + t, d * 128:(d + 1) * 128] + bias_ref[d])
    da = dt * (-jnp.exp(alog_ref[d]))
    row = lax.broadcasted_iota(jnp.int32, (t, t), 0)
    col = lax.broadcasted_iota(jnp.int32, (t, t), 1)
    tri = jnp.where((col <= row) if d == 0 else (col >= row), 1.0, 0.0).astype(BF16)
    return dt, _sel_dot(tri, da)


SSD_SCAN_CHUNKS = 2


def _ssd_state_kernel(nsteps, seq_bounds, xf_ref, bf_ref, dtf_ref, xb_ref, bb_ref, dtb_ref, bias_ref, alog_ref,
                      hf_ref, hb_ref, h_ref):
    t = SSM_CHUNK
    n = SSM_STATE
    p = SSM_HEADDIM
    kc = SSD_SCAN_CHUNKS
    i = pl.program_id(0)

    @pl.when(i == 0)
    def _():
        h_ref[...] = jnp.zeros(h_ref.shape, F32)

    for d, (x_ref, b_ref, dt_ref, out_ref) in enumerate(((xf_ref, bf_ref, dtf_ref, hf_ref),
                                                         (xb_ref, bb_ref, dtb_ref, hb_ref))):
        blk = i if d == 0 else nsteps - 1 - i
        contrib = []
        for c in range(kc):
            r0 = c * t
            dt, cs = _ssd_cumsums(dt_ref, bias_ref, alog_ref, d, r0)
            tot = cs[t - 1:t, :] if d == 0 else cs[0:1, :]
            w = jnp.exp(tot - cs) * dt
            etot = jnp.exp(tot)
            per_group = []
            for g in range(SSM_GROUPS):
                bg_t = b_ref[r0:r0 + t, g * n:(g + 1) * n].astype(F32).T.astype(BF16)
                xw, dec = [], []
                for hh in range(SSM_HPG):
                    h = g * SSM_HPG + hh
                    xh = x_ref[r0:r0 + t, h * p:(h + 1) * p].astype(F32)
                    xw.append((xh * jnp.broadcast_to(w[:, h:h + 1], (t, p))).astype(BF16))
                    dec.append(jnp.broadcast_to(etot[:, h:h + 1], (n, p)))
                st = jnp.dot(bg_t, jnp.concatenate(xw, axis=1), preferred_element_type=F32)
                per_group.append((st, jnp.concatenate(dec, axis=1)))
            contrib.append(per_group)
        for c in (range(kc) if d == 0 else reversed(range(kc))):
            is_start, is_end = _seq_flags((blk * kc + c) * t, t, seq_bounds)
            reset = is_start if d == 0 else is_end
            for g in range(SSM_GROUPS):
                st, dec = contrib[c][g]
                hin = jnp.where(reset, 0.0, h_ref[d, g])
                out_ref[c, g] = hin.astype(BF16)
                h_ref[d, g] = hin * dec + st


def _ssd_states(proj, dtraw, dt_bias, a_log, seq_bounds):
    m = proj.shape[0]
    t = SSM_CHUNK
    kc = SSD_SCAN_CHUNKS
    nchunks = m // t
    nsteps = nchunks // kc
    hw = SSM_HPG * SSM_HEADDIM
    bw = SSM_GROUPS * SSM_STATE
    fwd = lambda i: i
    bwd = lambda i: nsteps - 1 - i
    xspec = lambda f: pl.BlockSpec((kc * t, SSM_INNER), lambda i: (f(i), COL_XBC // SSM_INNER))
    bspec = lambda f: pl.BlockSpec((kc * t, bw), lambda i: (f(i), (COL_XBC + SSM_INNER) // bw))
    dspec = lambda f: pl.BlockSpec((kc * t, DT_COLS), lambda i: (f(i), 0))
    hspec = lambda f: pl.BlockSpec((kc, SSM_GROUPS, SSM_STATE, hw), lambda i: (f(i), 0, 0, 0))
    const = pl.BlockSpec((2, 1, 128), lambda i: (0, 0, 0))
    return pl.pallas_call(
        functools.partial(_ssd_state_kernel, nsteps, seq_bounds),
        grid=(nsteps,),
        in_specs=[xspec(fwd), bspec(fwd), dspec(fwd), xspec(bwd), bspec(bwd), dspec(bwd), const, const],
        out_specs=[hspec(fwd), hspec(bwd)],
        out_shape=[jax.ShapeDtypeStruct((nchunks, SSM_GROUPS, SSM_STATE, hw), BF16)] * 2,
        scratch_shapes=[pltpu.VMEM((2, SSM_GROUPS, SSM_STATE, hw), F32)],
        compiler_params=_cparams(("arbitrary",), 32),
        name="ssd_states",
    )(proj, proj, dtraw, proj, proj, dtraw, dt_bias, a_log)


def _ssd_out_kernel(x_ref, b_ref, c_ref, dt_ref, bias_ref, alog_ref, hf_ref, hb_ref, z_ref, dskip_ref,
                    gain_ref, o_ref, y_ref):
    t = SSM_CHUNK
    n = SSM_STATE
    p = SSM_HEADDIM
    dtf, csf = _ssd_cumsums(dt_ref, bias_ref, alog_ref, 0)
    dtb, csb = _ssd_cumsums(dt_ref, bias_ref, alog_ref, 1)
    csf_t, csb_t, dtf_t, dtb_t = csf.T, csb.T, dtf.T, dtb.T
    dts_t = dtf_t + dtb_t
    row = lax.broadcasted_iota(jnp.int32, (t, t), 0)
    col = lax.broadcasted_iota(jnp.int32, (t, t), 1)
    low = col <= row
    low_strict = col < row
    up_strict = col > row
    for g in range(SSM_GROUPS):
        bg = b_ref[:, g * n:(g + 1) * n]
        cg = c_ref[:, g * n:(g + 1) * n]
        cb = lax.dot_general(cg, bg, (((1,), (1,)), ((), ())), preferred_element_type=F32)
        cg_f = cg.astype(F32)
        for hh in range(SSM_HPG):
            h = g * SSM_HPG + hh
            colf = jnp.broadcast_to(csf[:, h:h + 1], (t, t))
            colb = jnp.broadcast_to(csb[:, h:h + 1], (t, t))
            seg = jnp.where(low, colf - jnp.broadcast_to(csf_t[h:h + 1, :], (t, t)),
                            colb - jnp.broadcast_to(csb_t[h:h + 1, :], (t, t)))
            coef = jnp.where(low_strict, jnp.broadcast_to(dtf_t[h:h + 1, :], (t, t)),
                             jnp.where(up_strict, jnp.broadcast_to(dtb_t[h:h + 1, :], (t, t)),
                                       jnp.broadcast_to(dts_t[h:h + 1, :], (t, t))))
            mh = jnp.exp(seg) * cb * coef
            lhs = jnp.concatenate([mh, cg_f * jnp.exp(colf), cg_f * jnp.exp(colb)], axis=1).astype(BF16)
            rhs = jnp.concatenate([x_ref[:, h * p:(h + 1) * p],
                                   hf_ref[0, g, :, hh * p:(hh + 1) * p],
                                   hb_ref[0, g, :, hh * p:(hh + 1) * p]], axis=0)
            y_ref[:, h * p:(h + 1) * p] = jnp.dot(lhs, rhs, preferred_element_type=F32)
    y = y_ref[...] + x_ref[...].astype(F32) * dskip_ref[...]
    y = y * _silu(z_ref[...].astype(F32))
    ms = jnp.mean(y * y, axis=-1, keepdims=True)
    o_ref[...] = (y * lax.rsqrt(ms + NORM_EPS) * gain_ref[...]).astype(o_ref.dtype)


def _ssd_out(proj, dtraw, dt_bias, a_log, h_f, h_b, d_skip, norm_g):
    m = proj.shape[0]
    t = SSM_CHUNK
    w = SSM_INNER
    hw = SSM_HPG * SSM_HEADDIM
    bw = SSM_GROUPS * SSM_STATE
    hspec = pl.BlockSpec((1, SSM_GROUPS, SSM_STATE, hw), lambda i: (i, 0, 0, 0))
    return pl.pallas_call(
        _ssd_out_kernel,
        grid=(m // t,),
        in_specs=[pl.BlockSpec((t, w), lambda i: (i, COL_XBC // w)),
                  pl.BlockSpec((t, bw), lambda i: (i, (COL_XBC + w) // bw)),
                  pl.BlockSpec((t, bw), lambda i: (i, (COL_XBC + w) // bw + 1)),
                  pl.BlockSpec((t, DT_COLS), lambda i: (i, 0)),
                  pl.BlockSpec((2, 1, 128), lambda i: (0, 0, 0)),
                  pl.BlockSpec((2, 1, 128), lambda i: (0, 0, 0)),
                  hspec, hspec,
                  pl.BlockSpec((t, w), lambda i: (i, COL_Z // w)),
                  pl.BlockSpec((1, w), lambda i: (0, 0)),
                  pl.BlockSpec((1, w), lambda i: (0, 0))],
        out_specs=pl.BlockSpec((t, w), lambda i: (i, 0)),
        out_shape=jax.ShapeDtypeStruct((m, w), BF16),
        scratch_shapes=[pltpu.VMEM((t, w), F32)],
        compiler_params=_cparams(("parallel",), 32),
        name="ssd_out",
    )(proj, proj, proj, dtraw, dt_bias, a_log, h_f, h_b, proj,
      jnp.repeat(d_skip.astype(F32), SSM_HEADDIM).reshape(1, w), norm_g.reshape(1, w))


def _ssd(proj, dtraw, dt_bias, a_log, d_skip, norm_g, seq_bounds):
    pad = lambda v: jnp.pad(v.astype(F32), ((0, 0), (0, 128 - SSM_HEADS))).reshape(2, 1, 128)
    dt_bias, a_log = pad(dt_bias), pad(a_log)
    h_f, h_b = _ssd_states(proj, dtraw, dt_bias, a_log, seq_bounds)
    return _ssd_out(proj, dtraw, dt_bias, a_log, h_f, h_b, d_skip, norm_g)


def _s5_selectors():
    t, c = S5_CHUNK, S5_GROUP
    lag = np.arange(S5_LAGS)[:, None]
    tok = (np.arange(S5_CW) // c)[None, :]
    e_exit_f = (lag == t - 1 - tok)
    e_exit_b = (lag == tok)
    e_in_f = (lag == tok + 1)
    e_in_b = (lag == t - tok)
    lagidx = (np.arange(2 * S5_CW) // c)[None, :]
    rel = lagidx - (t - 1)
    inb = lagidx <= 2 * t - 2
    e_k_f = (lag == rel) & (rel >= 0) & inb
    e_k_b = (lag == -rel) & (rel <= 0) & inb
    tile = (np.arange(c)[:, None] == (np.arange(2 * S5_CW) % c)[None, :])
    sel = np.stack([np.concatenate([a, b], axis=1) for a, b in
                    ((e_exit_f, e_in_f), (e_exit_b, e_in_b))])
    selk = np.stack([e_k_f, e_k_b])
    return (jnp.asarray(sel, BF16), jnp.asarray(selk, BF16), jnp.asarray(tile, BF16))


def _s5_prep_kernel(acol_ref, arow_ref, ls_ref, b_ref, bt_ref, ct_ref, sel_ref, selk_ref, tile_ref,
                    sign_ref, w_ref, ws_ref, wo_ref, lpa_ref, lpb_ref):
    t = S5_CHUNK
    cw = S5_CW
    p = S5_STATE
    tile = tile_ref[...]
    lagf = lax.broadcasted_iota(jnp.int32, (p, S5_LAGS), 1).astype(F32)
    kt = jnp.zeros((S5_GROUP, 2 * cw), F32)
    for d in range(2):
        step = jnp.exp(ls_ref[0, 0, d])
        are = acol_ref[0, 0, d, 0]
        aim = acol_ref[0, 0, d, 1]
        mag = are * step
        th = aim * step
        amp = jnp.exp(lagf * mag)
        pwr = amp * jnp.cos(lagf * th)
        pwi = amp * jnp.sin(lagf * th)
        lbr = jnp.exp(mag) * jnp.cos(th)
        lbi = jnp.exp(mag) * jnp.sin(th)
        den = are * are + aim * aim
        cfr = ((lbr - 1.0) * are + lbi * aim) / den
        cfi = (lbi * are - (lbr - 1.0) * aim) / den
        bre = b_ref[0, 0, 0]
        bim = b_ref[0, 0, 1]
        bbr = cfr * bre - cfi * bim
        bbi = cfr * bim + cfi * bre
        step_r = step
        are_r = arow_ref[0, 0, d, 0:1, 0:p]
        aim_r = arow_ref[0, 0, d, 1:2, 0:p]
        mag_r = are_r * step_r
        th_r = aim_r * step_r
        lbr_r = jnp.exp(mag_r) * jnp.cos(th_r)
        lbi_r = jnp.exp(mag_r) * jnp.sin(th_r)
        den_r = are_r * are_r + aim_r * aim_r
        cfr_r = ((lbr_r - 1.0) * are_r + lbi_r * aim_r) / den_r
        cfi_r = (lbi_r * are_r - (lbr_r - 1.0) * aim_r) / den_r
        btr = bt_ref[0, 0, 0]
        bti = bt_ref[0, 0, 1]
        bbr_t = cfr_r * btr - cfi_r * bti
        bbi_t = cfr_r * bti + cfi_r * btr
        ctr = ct_ref[0, 0, d, 0]
        cti = ct_ref[0, 0, d, 1]

        sel = sel_ref[d]
        er = _dot_sel(pwr, sel)
        ei = _dot_sel(pwi, sel)
        tb_r = _dot_sel(bbr, tile[:, 0:cw])
        tb_i = _dot_sel(bbi, tile[:, 0:cw])
        tc_r = _dot_sel(ctr, tile)
        tc_i = _dot_sel(cti, tile)
        ws_ref[0, 0, d * 2 * p:d * 2 * p + p, :] = (er[:, 0:cw] * tb_r - ei[:, 0:cw] * tb_i).astype(BF16)
        ws_ref[0, 0, d * 2 * p + p:(d + 1) * 2 * p, :] = (er[:, 0:cw] * tb_i + ei[:, 0:cw] * tb_r).astype(BF16)
        zr = er[:, cw:] * tc_r[:, 0:cw] - ei[:, cw:] * tc_i[:, 0:cw]
        zi = er[:, cw:] * tc_i[:, 0:cw] + ei[:, cw:] * tc_r[:, 0:cw]
        wo_ref[0, 0, d * 2 * p:d * 2 * p + p, :] = zr.astype(BF16)
        wo_ref[0, 0, d * 2 * p + p:(d + 1) * 2 * p, :] = (-zi).astype(BF16)
        selk = selk_ref[d]
        qr = _dot_sel(pwr, selk) * tc_r - _dot_sel(pwi, selk) * tc_i
        qi = _dot_sel(pwr, selk) * tc_i + _dot_sel(pwi, selk) * tc_r
        kt = kt + _dot_f32(bbr_t, qr) - _dot_f32(bbi_t, qi)
        are2 = arow_ref[0, 0, d, 0:1, :]
        aim2 = arow_ref[0, 0, d, 1:2, :]
        ampt = jnp.exp(are2 * step_r * float(t))
        zr2 = ampt * jnp.cos(aim2 * step_r * float(t))
        zi2 = ampt * jnp.sin(aim2 * step_r * float(t))
        sign = sign_ref[...]
        for k in range(S5_SCAN_STEPS):
            lpa_ref[0, 0, d, k:k + 1, :] = zr2
            lpb_ref[0, 0, d, k:k + 1, :] = sign * zi2
            zr2, zi2 = zr2 * zr2 - zi2 * zi2, 2.0 * zr2 * zi2
    for s in range(t):
        off = (t - 1 - s) * S5_GROUP
        w_ref[0, 0, s * S5_GROUP:(s + 1) * S5_GROUP, :] = kt[:, off:off + cw].astype(BF16)


def _s5_prep(a_re, a_im, log_step, b_re, b_im, c_re, c_im):
    depth = a_re.shape[0]
    g, p, c = S5_GROUPS, S5_STATE, S5_GROUP
    a = jnp.stack([a_re, a_im], axis=2).astype(F32)
    a = a.transpose(0, 3, 1, 2, 4)
    acol = a[..., None]
    arow = jnp.concatenate([a, a], axis=-1)
    ls = log_step.astype(F32).transpose(0, 2, 1).reshape(depth, g, 2, 1, 1)
    b = jnp.stack([b_re, b_im], axis=2).astype(F32)
    bt = b.transpose(0, 1, 2, 4, 3)
    ct = jnp.stack([c_re, c_im], axis=3).astype(F32)
    ct = ct.transpose(0, 2, 1, 3, 5, 4)
    sel, selk, tile = _s5_selectors()
    sign = jnp.concatenate([-jnp.ones((1, p), F32), jnp.ones((1, p), F32)], axis=1)
    full = lambda shp: pl.BlockSpec(shp, lambda l, j: (0,) * len(shp))
    per = lambda shp: pl.BlockSpec((1, 1) + shp, lambda l, j: (l, j) + (0,) * len(shp))
    cw = S5_CW
    return pl.pallas_call(
        _s5_prep_kernel,
        grid=(depth, g),
        in_specs=[per((2, 2, p, 1)), per((2, 2, 2 * p)), per((2, 1, 1)), per((2, p, c)), per((2, c, p)),
                  per((2, 2, p, c)), full((2, S5_LAGS, 2 * cw)), full((2, S5_LAGS, 2 * cw)),
                  full((c, 2 * cw)), full((1, 2 * p))],
        out_specs=[per((cw, cw)), per((4 * p, cw)), per((4 * p, cw)),
                   per((2, S5_SCAN_STEPS, 2 * p)), per((2, S5_SCAN_STEPS, 2 * p))],
        out_shape=[jax.ShapeDtypeStruct((depth, g, cw, cw), BF16),
                   jax.ShapeDtypeStruct((depth, g, 4 * p, cw), BF16),
                   jax.ShapeDtypeStruct((depth, g, 4 * p, cw), BF16),
                   jax.ShapeDtypeStruct((depth, g, 2, S5_SCAN_STEPS, 2 * p), F32),
                   jax.ShapeDtypeStruct((depth, g, 2, S5_SCAN_STEPS, 2 * p), F32)],
        compiler_params=_cparams(("parallel", "parallel"), 32),
        name="s5_prep",
    )(acol, arow, ls, b, bt, ct, sel, selk, tile, sign)


def _gelu_tanh(x):
    return 0.5 * x * (1.0 + jnp.tanh(math.sqrt(2.0 / math.pi) * (x + 0.044715 * (x * x * x))))


def _s5_kernel(chunk_bounds, u_ref, w_ref, ws_ref, wo_ref, lpa_ref, lpb_ref, d_ref, o_ref):
    r = u_ref.shape[1]
    p2 = 2 * S5_STATE
    u = u_ref[0]
    y = jnp.dot(u, w_ref[0], preferred_element_type=F32)
    st = lax.dot_general(u, ws_ref[0], (((1,), (1,)), ((), ())), preferred_element_type=F32)
    ridx = lax.broadcasted_iota(jnp.int32, (r, p2), 0)
    rloc = jnp.zeros((r, p2), jnp.int32)
    rlen = jnp.zeros((r, p2), jnp.int32)
    for s0, ln in chunk_bounds:
        inside = (ridx >= s0) & (ridx < s0 + ln)
        rloc = jnp.where(inside, ridx - s0, rloc)
        rlen = jnp.where(inside, ln, rlen)
    xin = []
    for d in range(2):
        x = st[:, d * p2:(d + 1) * p2]
        for k in range(S5_SCAN_STEPS):
            sh = 1 << k
            if d == 0:
                prev = jnp.where(rloc >= sh, pltpu.roll(x, sh, 0), 0.0)
            else:
                prev = jnp.where(rloc < rlen - sh, pltpu.roll(x, r - sh, 0), 0.0)
            x = x + lpa_ref[0, d, k:k + 1, :] * prev + lpb_ref[0, d, k:k + 1, :] * pltpu.roll(prev, S5_STATE, 1)
        if d == 0:
            xin.append(jnp.where(rloc >= 1, pltpu.roll(x, 1, 0), 0.0))
        else:
            xin.append(jnp.where(rloc < rlen - 1, pltpu.roll(x, r - 1, 0), 0.0))
    xin = jnp.concatenate(xin, axis=1).astype(BF16)
    y = y + jnp.dot(xin, wo_ref[0], preferred_element_type=F32)
    y = y + u.astype(F32) * d_ref[0]
    o_ref[0] = _gelu_tanh(y).astype(o_ref.dtype)


def _s5(u_chunks, w, ws, wo, lpa, lpb, d_tiled, chunk_bounds):
    g, r, cw = u_chunks.shape
    p = S5_STATE
    per = lambda shp: pl.BlockSpec((1,) + shp, lambda j: (j,) + (0,) * len(shp))
    return pl.pallas_call(
        functools.partial(_s5_kernel, chunk_bounds),
        grid=(g,),
        in_specs=[per((r, cw)), per((cw, cw)), per((4 * p, cw)), per((4 * p, cw)),
                  per((2, S5_SCAN_STEPS, 2 * p)), per((2, S5_SCAN_STEPS, 2 * p)), per((1, cw))],
        out_specs=per((r, cw)),
        out_shape=jax.ShapeDtypeStruct((g, r, cw), BF16),
        compiler_params=_cparams(("parallel",), 32),
        name="s5_mix",
    )(u_chunks, w, ws, wo, lpa, lpb, d_tiled)


def _t5_bucket(rel):
    half = REL_BUCKETS // 2
    exact = half // 2
    sign = (rel > 0).astype(np.int32) * half
    n = np.abs(rel)
    large = exact + (np.log(np.maximum(n, 1) / exact) / np.log(REL_MAX_DIST / exact)
                     * (half - exact)).astype(np.int32)
    large = np.minimum(large, half - 1)
    return sign + np.where(n < exact, n, large)


def _att_bias_kernel(gi, idx_ref, tbl_ref, o_ref):
    idx = idx_ref[...]
    for h in range(ATT_HPG):
        acc = jnp.full(idx.shape, NEG_INF, F32)
        for b in range(REL_BUCKETS):
            acc = jnp.where(idx == b, tbl_ref[b, gi * ATT_HPG + h], acc)
        o_ref[h] = acc


def _att_tile(seq_bounds, dil):
    return min(256, min(l for _, l in seq_bounds) // dil)


def _att_bias(rel_bias, gi, tq):
    dil = ATT_PATTERNS[gi][1]
    tk = tq + 2 * ATT_HALF
    rel = (np.arange(tk)[None, :] - ATT_HALF) - np.arange(tq)[:, None]
    idx = np.where(np.abs(rel) <= ATT_HALF, _t5_bucket(rel * dil), -1).astype(np.int32)
    return pl.pallas_call(
        functools.partial(_att_bias_kernel, gi),
        in_specs=[pl.BlockSpec(memory_space=pltpu.VMEM), pl.BlockSpec(memory_space=pltpu.SMEM)],
        out_specs=pl.BlockSpec(memory_space=pltpu.VMEM),
        out_shape=jax.ShapeDtypeStruct((ATT_HPG, tq, tk), F32),
        name=f"attention_bias_{gi}",
    )(jnp.asarray(idx), rel_bias.astype(F32))


def _attn_kernel(tq, blk_bounds, q_ref, kp_ref, km_ref, kn_ref, vp_ref, vm_ref, vn_ref, bias_ref,
                 o_ref, lse_ref):
    jb = pl.program_id(1)
    is_first, is_last = _seq_flags(jb, 1, blk_bounds)
    tk = tq + 2 * ATT_HALF
    colk = lax.broadcasted_iota(jnp.int32, (tq, tk), 1)
    valid = jnp.logical_and(jnp.logical_or(colk >= ATT_HALF, jnp.logical_not(is_first)),
                            jnp.logical_or(colk < tq + ATT_HALF, jnp.logical_not(is_last)))
    kcat = jnp.concatenate([kp_ref[...], km_ref[...], kn_ref[...]], axis=0)
    vcat = jnp.concatenate([vp_ref[...], vm_ref[...], vn_ref[...]], axis=0)
    scale = ATT_HEAD_DIM ** -0.5
    lane = lax.broadcasted_iota(jnp.int32, (tq, ATT_HEAD_DIM), 1)
    lse_tile = jnp.zeros((tq, ATT_HEAD_DIM), F32)
    for h in range(ATT_HPG):
        sl = slice(h * ATT_HEAD_DIM, (h + 1) * ATT_HEAD_DIM)
        s = lax.dot_general(q_ref[:, sl], kcat[:, sl], (((1,), (1,)), ((), ())),
                            preferred_element_type=F32)
        s = jnp.where(valid, s * scale + bias_ref[h], NEG_INF)
        mx = jnp.max(s, axis=-1, keepdims=True)
        pr = jnp.exp(s - mx)
        den = jnp.sum(pr, axis=-1, keepdims=True)
        o = jnp.dot(pr.astype(BF16), vcat[:, sl], preferred_element_type=F32)
        o_ref[:, sl] = (o / den).astype(o_ref.dtype)
        lse_tile = jnp.where(lane == h, mx + jnp.log(den), lse_tile)
    lse_ref[...] = lse_tile


def _attention_group(proj, bias, gi, seq_bounds):
    window, dil = ATT_PATTERNS[gi]
    assert window // (2 * dil) == ATT_HALF
    m = proj.shape[0]
    nview = m // dil
    tq = bias.shape[1]
    assert tq % ATT_HALF == 0 and all((s // dil) % tq == 0 and (l // dil) % tq == 0 for s, l in seq_bounds)
    blk_bounds = tuple((s // dil // tq, l // dil // tq) for s, l in seq_bounds)
    tk = tq + 2 * ATT_HALF
    col0 = COL_QKV + gi * 3 * ATT_GW
    if dil == 1:
        src, cb = proj, col0 // ATT_GW
    else:
        src = proj[:, col0:col0 + 3 * ATT_GW].reshape(nview, dil, 3 * ATT_GW)
        src, cb = src.transpose(1, 0, 2).reshape(m, 3 * ATT_GW), 0
    nb = nview // tq
    hb = tq // ATT_HALF
    lasth = m // ATT_HALF - 1

    def main(col):
        return pl.BlockSpec((tq, ATT_GW), lambda r, jb: (r * nb + jb, cb + col))

    def prev(col):
        return pl.BlockSpec((ATT_HALF, ATT_GW), lambda r, jb: (jnp.maximum((r * nb + jb) * hb - 1, 0), cb + col))

    def nxt(col):
        return pl.BlockSpec((ATT_HALF, ATT_GW), lambda r, jb: (jnp.minimum((r * nb + jb + 1) * hb, lasth), cb + col))

    o, lse = pl.pallas_call(
        functools.partial(_attn_kernel, tq, blk_bounds),
        grid=(dil, nb),
        in_specs=[main(0), prev(1), main(1), nxt(1), prev(2), main(2), nxt(2),
                  pl.BlockSpec((ATT_HPG, tq, tk), lambda r, jb: (0, 0, 0))],
        out_specs=[pl.BlockSpec((tq, ATT_GW), lambda r, jb: (r * nb + jb, 0)),
                   pl.BlockSpec((tq, ATT_HEAD_DIM), lambda r, jb: (r * nb + jb, 0))],
        out_shape=[jax.ShapeDtypeStruct((m, ATT_GW), BF16), jax.ShapeDtypeStruct((m, ATT_HEAD_DIM), F32)],
        compiler_params=_cparams(("parallel", "parallel"), 32),
        name=f"dilated_attention_{gi}",
    )(src, src, src, src, src, src, src, bias)
    if dil > 1:
        unperm = lambda a: a.reshape(dil, nview, a.shape[-1]).transpose(1, 0, 2).reshape(m, a.shape[-1])
        o, lse = unperm(o), unperm(lse)
    return o, lse


def _att_combine_kernel(o0, l0, o1, l1, o2, l2, out_ref):
    a, b, c = l0[...], l1[...], l2[...]
    mx = jnp.maximum(jnp.maximum(a, b), c)
    ea, eb, ec = jnp.exp(a - mx), jnp.exp(b - mx), jnp.exp(c - mx)
    inv = 1.0 / (ea + eb + ec)
    wa, wb, wc = ea * inv, eb * inv, ec * inv
    tr = out_ref.shape[0]
    for h in range(ATT_HPG):
        sl = slice(h * ATT_HEAD_DIM, (h + 1) * ATT_HEAD_DIM)
        bc = lambda w: jnp.broadcast_to(w[:, h:h + 1], (tr, ATT_HEAD_DIM))
        out_ref[:, sl] = (bc(wa) * o0[:, sl].astype(F32) + bc(wb) * o1[:, sl].astype(F32)
                          + bc(wc) * o2[:, sl].astype(F32)).astype(out_ref.dtype)


def _att_combine(outs, lses, tr=512):
    m, w = outs[0].shape
    ospec = pl.BlockSpec((tr, w), lambda i: (i, 0))
    lspec = pl.BlockSpec((tr, ATT_HEAD_DIM), lambda i: (i, 0))
    args = [x for pair in zip(outs, lses) for x in pair]
    return pl.pallas_call(
        _att_combine_kernel,
        grid=(m // tr,),
        in_specs=[ospec, lspec] * 3,
        out_specs=ospec,
        out_shape=jax.ShapeDtypeStruct((m, w), BF16),
        compiler_params=_cparams(("parallel",), 32),
        name="attention_combine",
    )(*args)


def _pack_w_in(w):
    splits = np.cumsum([SSM_INNER, SSM_XBC, 2 * SSM_HEADS, S5_WIDTH, 3 * 1536]).tolist()
    z, xbc, dt, u, qkv, gates = jnp.split(w, splits, axis=-1)
    ng = len(ATT_PATTERNS)
    qkv = qkv.reshape(w.shape[0], 3, ng, ATT_GW).transpose(0, 2, 1, 3).reshape(w.shape[0], 3 * ng * ATT_GW)
    main = jnp.concatenate([z, xbc, u, qkv, gates], axis=-1).astype(BF16)
    zpad = jnp.zeros((w.shape[0], 128 - SSM_HEADS), w.dtype)
    wdt = jnp.concatenate([dt[:, :SSM_HEADS], zpad, dt[:, SSM_HEADS:], zpad], axis=-1).astype(BF16)
    return main, wdt


def _pack_ffn(w_up, conv_w, conv_b, w_down):
    padc = lambda a: jnp.pad(a, [(0, 0)] * (a.ndim - 1) + [(0, D_FF_PAD - D_FF)])
    both = lambda a: jnp.concatenate([padc(a[..., :D_FF]), padc(a[..., D_FF:])], axis=-1)
    w_down_p = jnp.pad(w_down, ((0, D_FF_PAD - D_FF), (0, 0))).astype(BF16)
    return both(w_up).astype(BF16), both(conv_w).astype(F32), both(conv_b).astype(F32), w_down_p


def _trunk(x, seq_bounds, rel_bias, norm_mix, w_in, ssm_conv_w, ssm_conv_b, ssm_a_log, ssm_dt_bias, ssm_d,
           ssm_norm, ssm_w_out, s5_tables, s5_d, s5_w_glu, att_w_out, w_o, norm_ffn, w_up, ffn_conv_w,
           ffn_conv_b, w_down, final_norm):
    m = x.shape[0]
    depth = w_in.shape[0]
    tm = min(1024, m)
    tm2 = min(512, m)
    gate_blk = lambda b, tn: (COL_GATES + b * D_MODEL) // tn
    chunk_bounds = tuple((s // S5_CHUNK, l // S5_CHUNK) for s, l in seq_bounds)
    r_chunks = m // S5_CHUNK
    att_bias = [_att_bias(rel_bias, gi, _att_tile(seq_bounds, dil)) for gi, (_, dil) in enumerate(ATT_PATTERNS)]
    gated =lambda acc, gate: _sigmoid(gate.astype(F32)) * acc
    gated_add = lambda acc, gate, prev: prev + _sigmoid(gate.astype(F32)) * acc
    resid = lambda acc, res: res + acc

    for li in range(depth):
        w_main, w_dt = _pack_w_in(w_in[li])
        proj, dtraw = _in_proj(x, norm_mix[li].astype(F32), w_main, w_dt, ssm_conv_w[li].astype(F32),
                               ssm_conv_b[li].astype(F32), seq_bounds, tm)

        y_a = _ssd(proj, dtraw, ssm_dt_bias[li], ssm_a_log[li], ssm_d[li], ssm_norm[li].astype(F32), seq_bounds)
        merged = _matmul(y_a, ssm_w_out[li].astype(BF16), tm=tm, tn=512, out_dtype=F32, epilogue=gated,
                         extras=((proj, gate_blk(0, 512)),), name="ssm_out_proj")

        w5, ws5, wo5, lpa, lpb = (tbl[li] for tbl in s5_tables)
        u = proj[:, COL_U:COL_U + S5_WIDTH].reshape(r_chunks, S5_CHUNK, S5_GROUPS, S5_GROUP)
        u = u.transpose(2, 0, 1, 3).reshape(S5_GROUPS, r_chunks, S5_CW)
        d_tiled = jnp.tile(s5_d[li].astype(F32).reshape(S5_GROUPS, 1, S5_GROUP), (1, 1, S5_CHUNK))
        y_b = _s5(u, w5, ws5, wo5, lpa, lpb, d_tiled, chunk_bounds)
        y_b = y_b.reshape(S5_GROUPS, r_chunks, S5_CHUNK, S5_GROUP).transpose(1, 2, 0, 3).reshape(m, S5_WIDTH)
        merged = _glu_matmul(y_b, s5_w_glu[li].astype(BF16), proj, merged, tm=tm, tn=512,
                             gate_blk=gate_blk(1, 512))

        outs, lses = zip(*[_attention_group(proj, att_bias[gi], gi, seq_bounds)
                           for gi in range(len(ATT_PATTERNS))])
        comb = _att_combine(outs, lses, tr=tm2)
        merged = _matmul(comb, att_w_out[li].astype(BF16), tm=tm, tn=512, out_dtype=BF16, epilogue=gated_add,
                         extras=((proj, gate_blk(2, 512)), (merged, 0)), name="att_out_proj")

        x = _matmul(merged, w_o[li].astype(BF16), tm=tm, tn=512, out_dtype=F32, epilogue=resid,
                    extras=((x, 0),), name="mix_out_proj")

        w_up_p, cw_p, cb_p, w_down_p = _pack_ffn(w_up[li], ffn_conv_w[li], ffn_conv_b[li], w_down[li])
        act = _ffn_up(x, norm_ffn[li].astype(F32), w_up_p, cw_p, cb_p, seq_bounds, tm)
        x = _matmul(act, w_down_p, tm=tm, tn=512, out_dtype=F32, epilogue=resid, extras=((x, 0),),
                    vmem_mb=56, name="ffn_down")
    return _rmsnorm(x, final_norm, F32)


def kernel(x_prompt, x_sample, rel_bias, norm_mix, w_in, ssm_conv_w, ssm_conv_b, ssm_a_log, ssm_dt_bias, ssm_d, ssm_norm, ssm_w_out, s5_a_re, s5_a_im, s5_log_step, s5_b_re, s5_b_im, s5_c_re, s5_c_im, s5_d, s5_w_glu, att_w_out, w_o, norm_ffn, w_up, ffn_conv_w, ffn_conv_b, w_down, final_norm):
    d = x_prompt.shape[-1]
    seq_bounds = []
    row = 0
    for arr in (x_prompt, x_sample):
        for _ in range(arr.shape[0]):
            seq_bounds.append((row, arr.shape[1]))
            row += arr.shape[1]
    seq_bounds = tuple(seq_bounds)
    x = jnp.concatenate([x_prompt.reshape(-1, d), x_sample.reshape(-1, d)], axis=0)
    s5_tables = _s5_prep(s5_a_re, s5_a_im, s5_log_step, s5_b_re, s5_b_im, s5_c_re, s5_c_im)
    y = _trunk(x, seq_bounds, rel_bias, norm_mix, w_in, ssm_conv_w, ssm_conv_b, ssm_a_log, ssm_dt_bias, ssm_d,
               ssm_norm, ssm_w_out, s5_tables, s5_d, s5_w_glu, att_w_out, w_o, norm_ffn, w_up, ffn_conv_w,
               ffn_conv_b, w_down, final_norm)
    n_prompt = x_prompt.shape[0] * x_prompt.shape[1]
    return (y[:n_prompt].reshape(x_prompt.shape), y[n_prompt:].reshape(x_sample.shape))
```

```python
import functools
import math

import numpy as np
import jax
import jax.numpy as jnp
from jax import lax
from jax.experimental import pallas as pl
from jax.experimental.pallas import tpu as pltpu

F32 = jnp.float32
BF16 = jnp.bfloat16

D_MODEL = 2048
NORM_EPS = 1e-6
NEG_INF = -1e30

SSM_HEADDIM = 64
SSM_INNER = 1536
SSM_HEADS = 24
SSM_GROUPS = 4
SSM_HPG = SSM_HEADS // SSM_GROUPS
SSM_STATE = 128
SSM_CONV = 5
SSM_CHUNK = 128
SSM_XBC = 2560

S5_WIDTH = 1024
S5_GROUP = 16
S5_GROUPS = 64
S5_STATE = 64
S5_CHUNK = 32
S5_CW = S5_CHUNK * S5_GROUP
S5_LAGS = 128
S5_SCAN_STEPS = 8

ATT_HEAD_DIM = 128
ATT_HPG = 4
ATT_PATTERNS = ((128, 1), (512, 4), (2048, 16))
ATT_GW = ATT_HPG * ATT_HEAD_DIM
ATT_HALF = 64
REL_BUCKETS = 32
REL_MAX_DIST = 1024

D_FF = 5504
D_FF_PAD = 5632
FFN_CONV = 3

COL_Z = 0
COL_XBC = 1536
COL_U = 4096
COL_QKV = 5120
COL_GATES = 9728
IN_COLS = 15872
DT_COLS = 256

V7X_VMEM_BYTES = 64 * 1024 * 1024


def _cparams(sem, vmem_mb):
    return pltpu.CompilerParams(dimension_semantics=sem, vmem_limit_bytes=vmem_mb * 1024 * 1024)


def _sigmoid(x):
    return 1.0 / (1.0 + jnp.exp(-x))


def _silu(x):
    return x * _sigmoid(x)


def _split3(x):
    hi = x.astype(BF16)
    r1 = x - hi.astype(F32)
    mid = r1.astype(BF16)
    lo = (r1 - mid.astype(F32)).astype(BF16)
    return hi, mid, lo


def _dot_sel(x, sel):
    hi, mid, lo = _split3(x)
    d = lambda a: jnp.dot(a, sel, preferred_element_type=F32)
    return d(hi) + d(mid) + d(lo)


def _sel_dot(sel, x):
    hi, mid, lo = _split3(x)
    d = lambda a: jnp.dot(sel, a, preferred_element_type=F32)
    return d(hi) + d(mid) + d(lo)


def _dot_f32(a, b):
    ah, am, _ = _split3(a)
    bh, bm, _ = _split3(b)
    d = lambda x, y: jnp.dot(x, y, preferred_element_type=F32)
    return d(ah, bh) + d(ah, bm) + d(am, bh)


def _rmsnorm_kernel(x_ref, g_ref, o_ref):
    x = x_ref[...]
    ms = jnp.mean(x * x, axis=-1, keepdims=True)
    o_ref[...] = (x * lax.rsqrt(ms + NORM_EPS) * g_ref[...]).astype(o_ref.dtype)


def _rmsnorm(x, gain, out_dtype, tr=512):
    m, d = x.shape
    return pl.pallas_call(
        _rmsnorm_kernel,
        grid=(m // tr,),
        in_specs=[pl.BlockSpec((tr, d), lambda i: (i, 0)),
                  pl.BlockSpec((1, d), lambda i: (0, 0))],
        out_specs=pl.BlockSpec((tr, d), lambda i: (i, 0)),
        out_shape=jax.ShapeDtypeStruct((m, d), out_dtype),
        compiler_params=_cparams(("parallel",), 32),
        name="rmsnorm",
    )(x, gain.reshape(1, d))


def _mm_kernel(epilogue, n_extra, a_ref, b_ref, *rest):
    extras = rest[:n_extra]
    o_ref = rest[n_extra]
    acc = jnp.dot(a_ref[...], b_ref[...], preferred_element_type=F32)
    if epilogue is not None:
        acc = epilogue(acc, *[e[...] for e in extras])
    o_ref[...] = acc.astype(o_ref.dtype)


def _matmul(a, b, *, tm, tn, out_dtype, k=None, a_kblk=0, epilogue=None, extras=(), vmem_mb=48,
            name="matmul"):
    m = a.shape[0]
    kk, n = b.shape
    if k is None:
        k = a.shape[1]
    assert k == kk and m % tm == 0 and n % tn == 0
    in_specs = [pl.BlockSpec((tm, k), lambda i, j: (i, a_kblk)),
                pl.BlockSpec((k, tn), lambda i, j: (0, j))]
    args = [a, b]
    for arr, off in extras:
        in_specs.append(pl.BlockSpec((tm, tn), functools.partial(lambda i, j, o: (i, j + o), o=off)))
        args.append(arr)
    return pl.pallas_call(
        functools.partial(_mm_kernel, epilogue, len(extras)),
        grid=(m // tm, n // tn),
        in_specs=in_specs,
        out_specs=pl.BlockSpec((tm, tn), lambda i, j: (i, j)),
        out_shape=jax.ShapeDtypeStruct((m, n), out_dtype),
        compiler_params=_cparams(("parallel", "parallel"), vmem_mb),
        name=name,
    )(*args)


def _glu_kernel(a_ref, bv_ref, bg_ref, gate_ref, acc_ref, o_ref):
    a = a_ref[...].astype(BF16)
    val = jnp.dot(a, bv_ref[...], preferred_element_type=F32)
    gate = jnp.dot(a, bg_ref[...], preferred_element_type=F32)
    branch = val * _sigmoid(gate)
    o_ref[...] = acc_ref[...] + _sigmoid(gate_ref[...].astype(F32)) * branch


def _glu_matmul(a, w, proj, merged, *, tm, tn, gate_blk):
    m, k = a.shape
    n = w.shape[1] // 2
    nb = n // tn
    return pl.pallas_call(
        _glu_kernel,
        grid=(m // tm, nb),
        in_specs=[pl.BlockSpec((tm, k), lambda i, j: (i, 0)),
                  pl.BlockSpec((k, tn), lambda i, j: (0, j)),
                  pl.BlockSpec((k, tn), lambda i, j: (0, j + nb)),
                  pl.BlockSpec((tm, tn), lambda i, j: (i, j + gate_blk)),
                  pl.BlockSpec((tm, tn), lambda i, j: (i, j))],
        out_specs=pl.BlockSpec((tm, tn), lambda i, j: (i, j)),
        out_shape=jax.ShapeDtypeStruct((m, n), F32),
        compiler_params=_cparams(("parallel", "parallel"), 48),
        name="s5_glu_matmul",
    )(a, w, w, proj, merged)


HALO = 16


def _seq_flags(row0, nrows, seq_bounds):
    starts = [s for s, _ in seq_bounds]
    ends = [s + l for s, l in seq_bounds]
    is_start = functools.reduce(jnp.logical_or, [row0 == s for s in starts])
    is_end = functools.reduce(jnp.logical_or, [row0 + nrows == e for e in ends])
    return is_start, is_end


def _conv_taps(buf_ref, w_ref, b_ref, width, tr, cols=slice(None)):
    pad = width // 2
    acc = None
    for kk in range(width):
        term = buf_ref[pl.ds(HALO - pad + kk, tr), cols] * w_ref[kk:kk + 1, cols]
        acc = term if acc is None else acc + term
    return acc + b_ref[:, cols]


def _norm_rows(x, gain):
    ms = jnp.mean(x * x, axis=-1, keepdims=True)
    return (x * lax.rsqrt(ms + NORM_EPS) * gain).astype(BF16)


def _fill_normed(hn_ref, xp_ref, xm_ref, xn_ref, gain_ref, tm, seq_bounds):
    is_start, is_end = _seq_flags(pl.program_id(0) * tm, tm, seq_bounds)
    gain = gain_ref[...]
    hn_ref[0:HALO, :] = jnp.where(is_start, 0.0, _norm_rows(xp_ref[...], gain)).astype(BF16)
    hn_ref[HALO:HALO + tm, :] = _norm_rows(xm_ref[...], gain)
    hn_ref[HALO + tm:HALO + tm + HALO, :] = jnp.where(is_end, 0.0, _norm_rows(xn_ref[...], gain)).astype(BF16)


def _row_halo_specs(tm, d, nrows):
    hb = tm // HALO
    last = nrows // HALO - 1
    return [pl.BlockSpec((HALO, d), lambda i, j: (jnp.maximum(i * hb - 1, 0), 0)),
            pl.BlockSpec((tm, d), lambda i, j: (i, 0)),
            pl.BlockSpec((HALO, d), lambda i, j: (jnp.minimum((i + 1) * hb, last), 0))]


def _in_proj_kernel(tm, tn, seq_bounds, conv_lo, conv_hi, xp_ref, xm_ref, xn_ref, gain_ref, w_ref,
                    wdt_ref, cw_ref, cb_ref, o_ref, dt_ref, hn_ref, acc_ref):
    j = pl.program_id(1)

    @pl.when(j == 0)
    def _():
        _fill_normed(hn_ref, xp_ref, xm_ref, xn_ref, gain_ref, tm, seq_bounds)
        dt_ref[...] = jnp.dot(hn_ref[HALO:HALO + tm, :], wdt_ref[...], preferred_element_type=F32)

    is_conv = jnp.logical_and(j >= conv_lo, j < conv_hi)

    @pl.when(is_conv)
    def _():
        half = tn // 2
        for c in range(2):
            cols = slice(c * half, (c + 1) * half)
            acc_ref[:, cols] = jnp.dot(hn_ref[...], w_ref[:, cols], preferred_element_type=F32)
            o_ref[:, cols] = _silu(_conv_taps(acc_ref, cw_ref, cb_ref, SSM_CONV, tm, cols)).astype(o_ref.dtype)

    @pl.when(jnp.logical_not(is_conv))
    def _():
        o_ref[...] = jnp.dot(hn_ref[HALO:HALO + tm, :], w_ref[...],
                             preferred_element_type=F32).astype(o_ref.dtype)


def _in_proj(x, gain, w_main, w_dt, conv_w, conv_b, seq_bounds, tm, tn=512):
    m, d = x.shape
    n = w_main.shape[1]
    conv_lo, conv_hi = COL_XBC // tn, (COL_XBC + SSM_XBC) // tn
    cidx = lambda i, j: (0, jnp.clip(j - conv_lo, 0, conv_hi - conv_lo - 1))
    return pl.pallas_call(
        functools.partial(_in_proj_kernel, tm, tn, seq_bounds, conv_lo, conv_hi),
        grid=(m // tm, n // tn),
        in_specs=_row_halo_specs(tm, d, m) + [
            pl.BlockSpec((1, d), lambda i, j: (0, 0)),
            pl.BlockSpec((d, tn), lambda i, j: (0, j)),
            pl.BlockSpec((d, DT_COLS), lambda i, j: (0, 0)),
            pl.BlockSpec((SSM_CONV, tn), cidx),
            pl.BlockSpec((1, tn), cidx)],
        out_specs=[pl.BlockSpec((tm, tn), lambda i, j: (i, j)),
                   pl.BlockSpec((tm, DT_COLS), lambda i, j: (i, 0))],
        out_shape=[jax.ShapeDtypeStruct((m, n), BF16), jax.ShapeDtypeStruct((m, DT_COLS), F32)],
        scratch_shapes=[pltpu.VMEM((tm + 2 * HALO, d), BF16), pltpu.VMEM((tm + 2 * HALO, tn), F32)],
        compiler_params=_cparams(("parallel", "arbitrary"), 56),
        name="in_proj",
    )(x, x, x, gain.reshape(1, d), w_main, w_dt, conv_w, conv_b.reshape(1, -1))


def _ffn_up_kernel(tm, seq_bounds, xp_ref, xm_ref, xn_ref, gain_ref, wg_ref, wv_ref, cwg_ref, cbg_ref,
                   cwv_ref, cbv_ref, o_ref, hn_ref, sg_ref, sv_ref):
    @pl.when(pl.program_id(1) == 0)
    def _():
        _fill_normed(hn_ref, xp_ref, xm_ref, xn_ref, gain_ref, tm, seq_bounds)

    hn = hn_ref[...]
    sg_ref[...] = jnp.dot(hn, wg_ref[...], preferred_element_type=F32)
    sv_ref[...] = jnp.dot(hn, wv_ref[...], preferred_element_type=F32)
    gate = _conv_taps(sg_ref, cwg_ref, cbg_ref, FFN_CONV, tm)
    val = _conv_taps(sv_ref, cwv_ref, cbv_ref, FFN_CONV, tm)
    o_ref[...] = (_silu(gate) * val).astype(o_ref.dtype)


def _ffn_up(x, gain, w, conv_w, conv_b, seq_bounds, tm, tn=512):
    m, d = x.shape
    nb = D_FF_PAD // tn
    col = lambda rows, off: pl.BlockSpec((rows, tn), functools.partial(lambda i, j, o: (0, j + o), o=off))
    cb2 = conv_b.reshape(1, -1)
    return pl.pallas_call(
        functools.partial(_ffn_up_kernel, tm, seq_bounds),
        grid=(m // tm, nb),
        in_specs=_row_halo_specs(tm, d, m) + [
            pl.BlockSpec((1, d), lambda i, j: (0, 0)),
            col(d, 0), col(d, nb), col(FFN_CONV, 0), col(1, 0), col(FFN_CONV, nb), col(1, nb)],
        out_specs=pl.BlockSpec((tm, tn), lambda i, j: (i, j)),
        out_shape=jax.ShapeDtypeStruct((m, D_FF_PAD), BF16),
        scratch_shapes=[pltpu.VMEM((tm + 2 * HALO, d), BF16), pltpu.VMEM((tm + 2 * HALO, tn), F32),
                        pltpu.VMEM((tm + 2 * HALO, tn), F32)],
        compiler_params=_cparams(("parallel", "arbitrary"), 56),
        name="ffn_up",
    )(x, x, x, gain.reshape(1, d), w, w, conv_w, cb2, conv_w, cb2)


def _softplus(x):
    return jnp.maximum(x, 0.0) + jnp.log1p(jnp.exp(-jnp.abs(x)))


def _ssd_cumsums(dt_ref, bias_ref, alog_ref, d, row0=0):
    t = SSM_CHUNK
    dt = _softplus(dt_ref[row0:row0 + t, d * 128:(d + 1) * 128] + bias_ref[d])
    da = dt * (-jnp.exp(alog_ref[d]))
    row = lax.broadcasted_iota(jnp.int32, (t, t), 0)
    col = lax.broadcasted_iota(jnp.int32, (t, t), 1)
    tri = jnp.where((col <= row) if d == 0 else (col >= row), 1.0, 0.0).astype(BF16)
    return dt, _sel_dot(tri, da)


SSD_SCAN_CHUNKS = 2


def _ssd_state_kernel(nsteps, seq_bounds, xf_ref, bf_ref, dtf_ref, xb_ref, bb_ref, dtb_ref, bias_ref, alog_ref,
                      hf_ref, hb_ref, h_ref):
    t = SSM_CHUNK
    n = SSM_STATE
    p = SSM_HEADDIM
    kc = SSD_SCAN_CHUNKS
    i = pl.program_id(0)

    @pl.when(i == 0)
    def _():
        h_ref[...] = jnp.zeros(h_ref.shape, F32)

    for d, (x_ref, b_ref, dt_ref, out_ref) in enumerate(((xf_ref, bf_ref, dtf_ref, hf_ref),
                                                         (xb_ref, bb_ref, dtb_ref, hb_ref))):
        blk = i if d == 0 else nsteps - 1 - i
        contrib = []
        for c in range(kc):
            r0 = c * t
            dt, cs = _ssd_cumsums(dt_ref, bias_ref, alog_ref, d, r0)
            tot = cs[t - 1:t, :] if d == 0 else cs[0:1, :]
            w = jnp.exp(tot - cs) * dt
            etot = jnp.exp(tot)
            per_group = []
            for g in range(SSM_GROUPS):
                bg_t = b_ref[r0:r0 + t, g * n:(g + 1) * n].astype(F32).T.astype(BF16)
                xw, dec = [], []
                for hh in range(SSM_HPG):
                    h = g * SSM_HPG + hh
                    xh = x_ref[r0:r0 + t, h * p:(h + 1) * p].astype(F32)
                    xw.append((xh * jnp.broadcast_to(w[:, h:h + 1], (t, p))).astype(BF16))
                    dec.append(jnp.broadcast_to(etot[:, h:h + 1], (n, p)))
                st = jnp.dot(bg_t, jnp.concatenate(xw, axis=1), preferred_element_type=F32)
                per_group.append((st, jnp.concatenate(dec, axis=1)))
            contrib.append(per_group)
        for c in (range(kc) if d == 0 else reversed(range(kc))):
            is_start, is_end = _seq_flags((blk * kc + c) * t, t, seq_bounds)
            reset = is_start if d == 0 else is_end
            for g in range(SSM_GROUPS):
                st, dec = contrib[c][g]
                hin = jnp.where(reset, 0.0, h_ref[d, g])
                out_ref[c, g] = hin.astype(BF16)
                h_ref[d, g] = hin * dec + st


def _ssd_states(proj, dtraw, dt_bias, a_log, seq_bounds):
    m = proj.shape[0]
    t = SSM_CHUNK
    kc = SSD_SCAN_CHUNKS
    nchunks = m // t
    nsteps = nchunks // kc
    hw = SSM_HPG * SSM_HEADDIM
    bw = SSM_GROUPS * SSM_STATE
    fwd = lambda i: i
    bwd = lambda i: nsteps - 1 - i
    xspec = lambda f: pl.BlockSpec((kc * t, SSM_INNER), lambda i: (f(i), COL_XBC // SSM_INNER))
    bspec = lambda f: pl.BlockSpec((kc * t, bw), lambda i: (f(i), (COL_XBC + SSM_INNER) // bw))
    dspec = lambda f: pl.BlockSpec((kc * t, DT_COLS), lambda i: (f(i), 0))
    hspec = lambda f: pl.BlockSpec((kc, SSM_GROUPS, SSM_STATE, hw), lambda i: (f(i), 0, 0, 0))
    const = pl.BlockSpec((2, 1, 128), lambda i: (0, 0, 0))
    return pl.pallas_call(
        functools.partial(_ssd_state_kernel, nsteps, seq_bounds),
        grid=(nsteps,),
        in_specs=[xspec(fwd), bspec(fwd), dspec(fwd), xspec(bwd), bspec(bwd), dspec(bwd), const, const],
        out_specs=[hspec(fwd), hspec(bwd)],
        out_shape=[jax.ShapeDtypeStruct((nchunks, SSM_GROUPS, SSM_STATE, hw), BF16)] * 2,
        scratch_shapes=[pltpu.VMEM((2, SSM_GROUPS, SSM_STATE, hw), F32)],
        compiler_params=_cparams(("arbitrary",), 32),
        name="ssd_states",
    )(proj, proj, dtraw, proj, proj, dtraw, dt_bias, a_log)


def _ssd_out_kernel(x_ref, b_ref, c_ref, dt_ref, bias_ref, alog_ref, hf_ref, hb_ref, z_ref, dskip_ref,
                    gain_ref, o_ref, y_ref):
    t = SSM_CHUNK
    n = SSM_STATE
    p = SSM_HEADDIM
    dtf, csf = _ssd_cumsums(dt_ref, bias_ref, alog_ref, 0)
    dtb, csb = _ssd_cumsums(dt_ref, bias_ref, alog_ref, 1)
    csf_t, csb_t, dtf_t, dtb_t = csf.T, csb.T, dtf.T, dtb.T
    dts_t = dtf_t + dtb_t
    row = lax.broadcasted_iota(jnp.int32, (t, t), 0)
    col = lax.broadcasted_iota(jnp.int32, (t, t), 1)
    low = col <= row
    low_strict = col < row
    up_strict = col > row
    for g in range(SSM_GROUPS):
        bg = b_ref[:, g * n:(g + 1) * n]
        cg = c_ref[:, g * n:(g + 1) * n]
        cb = lax.dot_general(cg, bg, (((1,), (1,)), ((), ())), preferred_element_type=F32)
        cg_f = cg.astype(F32)
        for hh in range(SSM_HPG):
            h = g * SSM_HPG + hh
            colf = jnp.broadcast_to(csf[:, h:h + 1], (t, t))
            colb = jnp.broadcast_to(csb[:, h:h + 1], (t, t))
            seg = jnp.where(low, colf - jnp.broadcast_to(csf_t[h:h + 1, :], (t, t)),
                            colb - jnp.broadcast_to(csb_t[h:h + 1, :], (t, t)))
            coef = jnp.where(low_strict, jnp.broadcast_to(dtf_t[h:h + 1, :], (t, t)),
                             jnp.where(up_strict, jnp.broadcast_to(dtb_t[h:h + 1, :], (t, t)),
                                       jnp.broadcast_to(dts_t[h:h + 1, :], (t, t))))
            mh = jnp.exp(seg) * cb * coef
            lhs = jnp.concatenate([mh, cg_f * jnp.exp(colf), cg_f * jnp.exp(colb)], axis=1).astype(BF16)
            rhs = jnp.concatenate([x_ref[:, h * p:(h + 1) * p],
                                   hf_ref[0, g, :, hh * p:(hh + 1) * p],
                                   hb_ref[0, g, :, hh * p:(hh + 1) * p]], axis=0)
            y_ref[:, h * p:(h + 1) * p] = jnp.dot(lhs, rhs, preferred_element_type=F32)
    y = y_ref[...] + x_ref[...].astype(F32) * dskip_ref[...]
    y = y * _silu(z_ref[...].astype(F32))
    ms = jnp.mean(y * y, axis=-1, keepdims=True)
    o_ref[...] = (y * lax.rsqrt(ms + NORM_EPS) * gain_ref[...]).astype(o_ref.dtype)


def _ssd_out(proj, dtraw, dt_bias, a_log, h_f, h_b, d_skip, norm_g):
    m = proj.shape[0]
    t = SSM_CHUNK
    w = SSM_INNER
    hw = SSM_HPG * SSM_HEADDIM
    bw = SSM_GROUPS * SSM_STATE
    hspec = pl.BlockSpec((1, SSM_GROUPS, SSM_STATE, hw), lambda i: (i, 0, 0, 0))
    return pl.pallas_call(
        _ssd_out_kernel,
        grid=(m // t,),
        in_specs=[pl.BlockSpec((t, w), lambda i: (i, COL_XBC // w)),
                  pl.BlockSpec((t, bw), lambda i: (i, (COL_XBC + w) // bw)),
                  pl.BlockSpec((t, bw), lambda i: (i, (COL_XBC + w) // bw + 1)),
                  pl.BlockSpec((t, DT_COLS), lambda i: (i, 0)),
                  pl.BlockSpec((2, 1, 128), lambda i: (0, 0, 0)),
                  pl.BlockSpec((2, 1, 128), lambda i: (0, 0, 0)),
                  hspec, hspec,
                  pl.BlockSpec((t, w), lambda i: (i, COL_Z // w)),
                  pl.BlockSpec((1, w), lambda i: (0, 0)),
                  pl.BlockSpec((1, w), lambda i: (0, 0))],
        out_specs=pl.BlockSpec((t, w), lambda i: (i, 0)),
        out_shape=jax.ShapeDtypeStruct((m, w), BF16),
        scratch_shapes=[pltpu.VMEM((t, w), F32)],
        compiler_params=_cparams(("parallel",), 32),
        name="ssd_out",
    )(proj, proj, proj, dtraw, dt_bias, a_log, h_f, h_b, proj,
      jnp.repeat(d_skip.astype(F32), SSM_HEADDIM).reshape(1, w), norm_g.reshape(1, w))


def _ssd(proj, dtraw, dt_bias, a_log, d_skip, norm_g, seq_bounds):
    pad = lambda v: jnp.pad(v.astype(F32), ((0, 0), (0, 128 - SSM_HEADS))).reshape(2, 1, 128)
    dt_bias, a_log = pad(dt_bias), pad(a_log)
    h_f, h_b = _ssd_states(proj, dtraw, dt_bias, a_log, seq_bounds)
    return _ssd_out(proj, dtraw, dt_bias, a_log, h_f, h_b, d_skip, norm_g)


def _s5_selectors():
    t, c = S5_CHUNK, S5_GROUP
    lag = np.arange(S5_LAGS)[:, None]
    tok = (np.arange(S5_CW) // c)[None, :]
    e_exit_f = (lag == t - 1 - tok)
    e_exit_b = (lag == tok)
    e_in_f = (lag == tok + 1)
    e_in_b = (lag == t - tok)
    lagidx = (np.arange(2 * S5_CW) // c)[None, :]
    rel = lagidx - (t - 1)
    inb = lagidx <= 2 * t - 2
    e_k_f = (lag == rel) & (rel >= 0) & inb
    e_k_b = (lag == -rel) & (rel <= 0) & inb
    tile = (np.arange(c)[:, None] == (np.arange(2 * S5_CW) % c)[None, :])
    sel = np.stack([np.concatenate([a, b], axis=1) for a, b in
                    ((e_exit_f, e_in_f), (e_exit_b, e_in_b))])
    selk = np.stack([e_k_f, e_k_b])
    return (jnp.asarray(sel, BF16), jnp.asarray(selk, BF16), jnp.asarray(tile, BF16))


def _s5_prep_kernel(acol_ref, arow_ref, ls_ref, b_ref, bt_ref, ct_ref, sel_ref, selk_ref, tile_ref,
                    sign_ref, w_ref, ws_ref, wo_ref, lpa_ref, lpb_ref):
    t = S5_CHUNK
    cw = S5_CW
    p = S5_STATE
    tile = tile_ref[...]
    lagf = lax.broadcasted_iota(jnp.int32, (p, S5_LAGS), 1).astype(F32)
    kt = jnp.zeros((S5_GROUP, 2 * cw), F32)
    for d in range(2):
        step = jnp.exp(ls_ref[0, 0, d])
        are = acol_ref[0, 0, d, 0]
        aim = acol_ref[0, 0, d, 1]
        mag = are * step
        th = aim * step
        amp = jnp.exp(lagf * mag)
        pwr = amp * jnp.cos(lagf * th)
        pwi = amp * jnp.sin(lagf * th)
        lbr = jnp.exp(mag) * jnp.cos(th)
        lbi = jnp.exp(mag) * jnp.sin(th)
        den = are * are + aim * aim
        cfr = ((lbr - 1.0) * are + lbi * aim) / den
        cfi = (lbi * are - (lbr - 1.0) * aim) / den
        bre = b_ref[0, 0, 0]
        bim = b_ref[0, 0, 1]
        bbr = cfr * bre - cfi * bim
        bbi = cfr * bim + cfi * bre
        step_r = step
        are_r = arow_ref[0, 0, d, 0:1, 0:p]
        aim_r = arow_ref[0, 0, d, 1:2, 0:p]
        mag_r = are_r * step_r
        th_r = aim_r * step_r
        lbr_r = jnp.exp(mag_r) * jnp.cos(th_r)
        lbi_r = jnp.exp(mag_r) * jnp.sin(th_r)
        den_r = are_r * are_r + aim_r * aim_r
        cfr_r = ((lbr_r - 1.0) * are_r + lbi_r * aim_r) / den_r
        cfi_r = (lbi_r * are_r - (lbr_r - 1.0) * aim_r) / den_r
        btr = bt_ref[0, 0, 0]
        bti = bt_ref[0, 0, 1]
        bbr_t = cfr_r * btr - cfi_r * bti
        bbi_t = cfr_r * bti + cfi_r * btr
        ctr = ct_ref[0, 0, d, 0]
        cti = ct_ref[0, 0, d, 1]

        sel = sel_ref[d]
        er = _dot_sel(pwr, sel)
        ei = _dot_sel(pwi, sel)
        tb_r = _dot_sel(bbr, tile[:, 0:cw])
        tb_i = _dot_sel(bbi, tile[:, 0:cw])
        tc_r = _dot_sel(ctr, tile)
        tc_i = _dot_sel(cti, tile)
        ws_ref[0, 0, d * 2 * p:d * 2 * p + p, :] = (er[:, 0:cw] * tb_r - ei[:, 0:cw] * tb_i).astype(BF16)
        ws_ref[0, 0, d * 2 * p + p:(d + 1) * 2 * p, :] = (er[:, 0:cw] * tb_i + ei[:, 0:cw] * tb_r).astype(BF16)
        zr = er[:, cw:] * tc_r[:, 0:cw] - ei[:, cw:] * tc_i[:, 0:cw]
        zi = er[:, cw:] * tc_i[:, 0:cw] + ei[:, cw:] * tc_r[:, 0:cw]
        wo_ref[0, 0, d * 2 * p:d * 2 * p + p, :] = zr.astype(BF16)
        wo_ref[0, 0, d * 2 * p + p:(d + 1) * 2 * p, :] = (-zi).astype(BF16)
        selk = selk_ref[d]
        qr = _dot_sel(pwr, selk) * tc_r - _dot_sel(pwi, selk) * tc_i
        qi = _dot_sel(pwr, selk) * tc_i + _dot_sel(pwi, selk) * tc_r
        kt = kt + _dot_f32(bbr_t, qr) - _dot_f32(bbi_t, qi)
        are2 = arow_ref[0, 0, d, 0:1, :]
        aim2 = arow_ref[0, 0, d, 1:2, :]
        ampt = jnp.exp(are2 * step_r * float(t))
        zr2 = ampt * jnp.cos(aim2 * step_r * float(t))
        zi2 = ampt * jnp.sin(aim2 * step_r * float(t))
        sign = sign_ref[...]
        for k in range(S5_SCAN_STEPS):
            lpa_ref[0, 0, d, k:k + 1, :] = zr2
            lpb_ref[0, 0, d, k:k + 1, :] = sign * zi2
            zr2, zi2 = zr2 * zr2 - zi2 * zi2, 2.0 * zr2 * zi2
    for s in range(t):
        off = (t - 1 - s) * S5_GROUP
        w_ref[0, 0, s * S5_GROUP:(s + 1) * S5_GROUP, :] = kt[:, off:off + cw].astype(BF16)


def _s5_prep(a_re, a_im, log_step, b_re, b_im, c_re, c_im):
    depth = a_re.shape[0]
    g, p, c = S5_GROUPS, S5_STATE, S5_GROUP
    a = jnp.stack([a_re, a_im], axis=2).astype(F32)
    a = a.transpose(0, 3, 1, 2, 4)
    acol = a[..., None]
    arow = jnp.concatenate([a, a], axis=-1)
    ls = log_step.astype(F32).transpose(0, 2, 1).reshape(depth, g, 2, 1, 1)
    b = jnp.stack([b_re, b_im], axis=2).astype(F32)
    bt = b.transpose(0, 1, 2, 4, 3)
    ct = jnp.stack([c_re, c_im], axis=3).astype(F32)
    ct = ct.transpose(0, 2, 1, 3, 5, 4)
    sel, selk, tile = _s5_selectors()
    sign = jnp.concatenate([-jnp.ones((1, p), F32), jnp.ones((1, p), F32)], axis=1)
    full = lambda shp: pl.BlockSpec(shp, lambda l, j: (0,) * len(shp))
    per = lambda shp: pl.BlockSpec((1, 1) + shp, lambda l, j: (l, j) + (0,) * len(shp))
    cw = S5_CW
    return pl.pallas_call(
        _s5_prep_kernel,
        grid=(depth, g),
        in_specs=[per((2, 2, p, 1)), per((2, 2, 2 * p)), per((2, 1, 1)), per((2, p, c)), per((2, c, p)),
                  per((2, 2, p, c)), full((2, S5_LAGS, 2 * cw)), full((2, S5_LAGS, 2 * cw)),
                  full((c, 2 * cw)), full((1, 2 * p))],
        out_specs=[per((cw, cw)), per((4 * p, cw)), per((4 * p, cw)),
                   per((2, S5_SCAN_STEPS, 2 * p)), per((2, S5_SCAN_STEPS, 2 * p))],
        out_shape=[jax.ShapeDtypeStruct((depth, g, cw, cw), BF16),
                   jax.ShapeDtypeStruct((depth, g, 4 * p, cw), BF16),
                   jax.ShapeDtypeStruct((depth, g, 4 * p, cw), BF16),
                   jax.ShapeDtypeStruct((depth, g, 2, S5_SCAN_STEPS, 2 * p), F32),
                   jax.ShapeDtypeStruct((depth, g, 2, S5_SCAN_STEPS, 2 * p), F32)],
        compiler_params=_cparams(("parallel", "parallel"), 32),
        name="s5_prep",
    )(acol, arow, ls, b, bt, ct, sel, selk, tile, sign)


def _gelu_tanh(x):
    return 0.5 * x * (1.0 + jnp.tanh(math.sqrt(2.0 / math.pi) * (x + 0.044715 * (x * x * x))))


LANES = 128


BF16_ROWS = 16
S5_PERM_ROWS = BF16_ROWS * S5_CHUNK
S5_GPB = LANES // S5_GROUP


def _s5_kernel(chunk_bounds, u_ref, perm_ref, pick_ref, w_ref, ws_ref, wo_ref, lpa_ref, lpb_ref,
               d_ref, o_ref, cat_ref, ug_ref, yg_ref):
    t = S5_CHUNK
    g = pl.program_id(1)
    r = u_ref.shape[0] // t
    nblk = r // BF16_ROWS
    p2 = 2 * S5_STATE
    qw = S5_GPB * LANES

    @pl.when(g == 0)
    def _():
        def body(b, carry):
            rows = pl.ds(pl.multiple_of(b * S5_PERM_ROWS, S5_PERM_ROWS), S5_PERM_ROWS)
            blk = jnp.dot(perm_ref[0], u_ref[rows, :], preferred_element_type=F32).astype(BF16)
            crow = pl.ds(pl.multiple_of(b * BF16_ROWS, BF16_ROWS), BF16_ROWS)
            for s in range(t):
                cat_ref[crow, s * LANES:(s + 1) * LANES] = blk[s * BF16_ROWS:(s + 1) * BF16_ROWS, :]
            return carry
        lax.fori_loop(0, nblk, body, 0)
        for q in range(t // S5_GPB):
            picked = jnp.dot(cat_ref[:, q * qw:(q + 1) * qw], pick_ref[0],
                             preferred_element_type=F32).astype(BF16)
            for gg in range(S5_GPB):
                ug_ref[gg, :, q * LANES:(q + 1) * LANES] = picked[:, gg * LANES:(gg + 1) * LANES]

    u = ug_ref[g]
    uf = u.astype(F32)
    y = jnp.dot(u, w_ref[0, 0], preferred_element_type=F32)
    st = lax.dot_general(u, ws_ref[0, 0], (((1,), (1,)), ((), ())), preferred_element_type=F32)
    ridx = lax.broadcasted_iota(jnp.int32, (r, p2), 0)
    rloc = jnp.zeros((r, p2), jnp.int32)
    rlen = jnp.zeros((r, p2), jnp.int32)
    for s0, ln in chunk_bounds:
        inside = (ridx >= s0) & (ridx < s0 + ln)
        rloc = jnp.where(inside, ridx - s0, rloc)
        rlen = jnp.where(inside, ln, rlen)
    xin = []
    for d in range(2):
        x = st[:, d * p2:(d + 1) * p2]
        for k in range(S5_SCAN_STEPS):
            sh = 1 << k
            if d == 0:
                prev = jnp.where(rloc >= sh, pltpu.roll(x, sh, 0), 0.0)
            else:
                prev = jnp.where(rloc < rlen - sh, pltpu.roll(x, r - sh, 0), 0.0)
            x = (x + lpa_ref[0, 0, d, k:k + 1, :] * prev
                 + lpb_ref[0, 0, d, k:k + 1, :] * pltpu.roll(prev, S5_STATE, 1))
        if d == 0:
            xin.append(jnp.where(rloc >= 1, pltpu.roll(x, 1, 0), 0.0))
        else:
            xin.append(jnp.where(rloc < rlen - 1, pltpu.roll(x, r - 1, 0), 0.0))
    xin = jnp.concatenate(xin, axis=1).astype(BF16)
    y = y + jnp.dot(xin, wo_ref[0, 0], preferred_element_type=F32)
    y = _gelu_tanh(y + uf * d_ref[0])

    yg_ref[g] = y.astype(BF16)

    @pl.when(g == pl.num_programs(1) - 1)
    def _():
        for q in range(t // S5_GPB):
            lhs = jnp.concatenate([yg_ref[gg, :, q * LANES:(q + 1) * LANES] for gg in range(S5_GPB)], axis=1)
            cat_ref[:, q * qw:(q + 1) * qw] = jnp.dot(lhs, pick_ref[1],
                                                      preferred_element_type=F32).astype(BF16)

        def body(b, carry):
            crow = pl.ds(pl.multiple_of(b * BF16_ROWS, BF16_ROWS), BF16_ROWS)
            blk = jnp.concatenate([cat_ref[crow, s * LANES:(s + 1) * LANES] for s in range(t)], axis=0)
            rows = pl.ds(pl.multiple_of(b * S5_PERM_ROWS, S5_PERM_ROWS), S5_PERM_ROWS)
            o_ref[rows, :] = jnp.dot(perm_ref[1], blk, preferred_element_type=F32).astype(o_ref.dtype)
            return carry
        lax.fori_loop(0, nblk, body, 0)


def _s5_movers():
    dst = np.arange(S5_PERM_ROWS)[:, None]
    fwd = (dst % BF16_ROWS) * S5_CHUNK + dst // BF16_ROWS == np.arange(S5_PERM_ROWS)[None, :]
    src = np.arange(S5_GPB * LANES)
    s8, g, j = src // LANES, src % LANES // S5_GROUP, src % S5_GROUP
    pick = (g * LANES + s8 * S5_GROUP + j)[:, None] == src[None, :]
    return jnp.asarray(np.stack([fwd, fwd.T]), BF16), jnp.asarray(np.stack([pick, pick.T]), BF16)


def _s5(proj, tables, li, d_tiled, chunk_bounds):
    m = proj.shape[0]
    w, ws, wo, lpa, lpb = tables
    p = S5_STATE
    cw = S5_CW
    gpb = S5_GPB
    r = m // S5_CHUNK
    perm, pick = _s5_movers()
    per = lambda shp: pl.BlockSpec((1, 1) + shp, lambda b, j: (li, b * gpb + j) + (0,) * len(shp))
    return pl.pallas_call(
        functools.partial(_s5_kernel, chunk_bounds),
        grid=(S5_GROUPS // gpb, gpb),
        in_specs=[pl.BlockSpec((m, LANES), lambda b, j: (0, COL_U // LANES + b)),
                  pl.BlockSpec((2, S5_PERM_ROWS, S5_PERM_ROWS), lambda b, j: (0, 0, 0)),
                  pl.BlockSpec((2, gpb * LANES, gpb * LANES), lambda b, j: (0, 0, 0)),
                  per((cw, cw)), per((4 * p, cw)), per((4 * p, cw)),
                  per((2, S5_SCAN_STEPS, 2 * p)), per((2, S5_SCAN_STEPS, 2 * p)),
                  pl.BlockSpec((1, 1, cw), lambda b, j: (b * gpb + j, 0, 0))],
        out_specs=pl.BlockSpec((m, LANES), lambda b, j: (0, b)),
        out_shape=jax.ShapeDtypeStruct((m, S5_WIDTH), BF16),
        scratch_shapes=[pltpu.VMEM((r, S5_CHUNK * LANES), BF16), pltpu.VMEM((gpb, r, cw), BF16),
                        pltpu.VMEM((gpb, r, cw), BF16)],
        compiler_params=_cparams(("parallel", "arbitrary"), 48),
        name="s5_mix",
    )(proj, perm, pick, w, ws, wo, lpa, lpb, d_tiled)


def _t5_bucket(rel):
    half = REL_BUCKETS // 2
    exact = half // 2
    sign = (rel > 0).astype(np.int32) * half
    n = np.abs(rel)
    large = exact + (np.log(np.maximum(n, 1) / exact) / np.log(REL_MAX_DIST / exact)
                     * (half - exact)).astype(np.int32)
    large = np.minimum(large, half - 1)
    return sign + np.where(n < exact, n, large)


def _att_bias_kernel(gi, idx_ref, tbl_ref, o_ref):
    idx = idx_ref[...]
    for h in range(ATT_HPG):
        acc = jnp.full(idx.shape, NEG_INF, F32)
        for b in range(REL_BUCKETS):
            acc = jnp.where(idx == b, tbl_ref[b, gi * ATT_HPG + h], acc)
        o_ref[h] = acc


def _att_tile(seq_bounds, dil):
    return min(256, min(l for _, l in seq_bounds) // dil)


def _att_bias(rel_bias, gi, tq):
    dil = ATT_PATTERNS[gi][1]
    tk = tq + 2 * ATT_HALF
    rel = (np.arange(tk)[None, :] - ATT_HALF) - np.arange(tq)[:, None]
    idx = np.where(np.abs(rel) <= ATT_HALF, _t5_bucket(rel * dil), -1).astype(np.int32)
    return pl.pallas_call(
        functools.partial(_att_bias_kernel, gi),
        in_specs=[pl.BlockSpec(memory_space=pltpu.VMEM), pl.BlockSpec(memory_space=pltpu.SMEM)],
        out_specs=pl.BlockSpec(memory_space=pltpu.VMEM),
        out_shape=jax.ShapeDtypeStruct((ATT_HPG, tq, tk), F32),
        name=f"attention_bias_{gi}",
    )(jnp.asarray(idx), rel_bias.astype(F32))


def _attn_kernel(tq, blk_bounds, q_ref, kp_ref, km_ref, kn_ref, vp_ref, vm_ref, vn_ref, bias_ref,
                 o_ref, lse_ref):
    jb = pl.program_id(1)
    is_first, is_last = _seq_flags(jb, 1, blk_bounds)
    tk = tq + 2 * ATT_HALF
    colk = lax.broadcasted_iota(jnp.int32, (tq, tk), 1)
    valid = jnp.logical_and(jnp.logical_or(colk >= ATT_HALF, jnp.logical_not(is_first)),
                            jnp.logical_or(colk < tq + ATT_HALF, jnp.logical_not(is_last)))
    kcat = jnp.concatenate([kp_ref[...], km_ref[...], kn_ref[...]], axis=0)
    vcat = jnp.concatenate([vp_ref[...], vm_ref[...], vn_ref[...]], axis=0)
    scale = ATT_HEAD_DIM ** -0.5
    lane = lax.broadcasted_iota(jnp.int32, (tq, ATT_HEAD_DIM), 1)
    lse_tile = jnp.zeros((tq, ATT_HEAD_DIM), F32)
    for h in range(ATT_HPG):
        sl = slice(h * ATT_HEAD_DIM, (h + 1) * ATT_HEAD_DIM)
        s = lax.dot_general(q_ref[:, sl], kcat[:, sl], (((1,), (1,)), ((), ())),
                            preferred_element_type=F32)
        s = jnp.where(valid, s * scale + bias_ref[h], NEG_INF)
        mx = jnp.max(s, axis=-1, keepdims=True)
        pr = jnp.exp(s - mx)
        den = jnp.sum(pr, axis=-1, keepdims=True)
        o = jnp.dot(pr.astype(BF16), vcat[:, sl], preferred_element_type=F32)
        o_ref[:, sl] = (o / den).astype(o_ref.dtype)
        lse_tile = jnp.where(lane == h, mx + jnp.log(den), lse_tile)
    lse_ref[...] = lse_tile


def _attention_group(proj, bias, gi, seq_bounds):
    window, dil = ATT_PATTERNS[gi]
    assert window // (2 * dil) == ATT_HALF
    m = proj.shape[0]
    nview = m // dil
    tq = bias.shape[1]
    assert tq % ATT_HALF == 0 and all((s // dil) % tq == 0 and (l // dil) % tq == 0 for s, l in seq_bounds)
    blk_bounds = tuple((s // dil // tq, l // dil // tq) for s, l in seq_bounds)
    tk = tq + 2 * ATT_HALF
    col0 = COL_QKV + gi * 3 * ATT_GW
    if dil == 1:
        src, cb = proj, col0 // ATT_GW
    else:
        src = proj[:, col0:col0 + 3 * ATT_GW].reshape(nview, dil, 3 * ATT_GW)
        src, cb = src.transpose(1, 0, 2).reshape(m, 3 * ATT_GW), 0
    nb = nview // tq
    hb = tq // ATT_HALF
    lasth = m // ATT_HALF - 1

    def main(col):
        return pl.BlockSpec((tq, ATT_GW), lambda r, jb: (r * nb + jb, cb + col))

    def prev(col):
        return pl.BlockSpec((ATT_HALF, ATT_GW), lambda r, jb: (jnp.maximum((r * nb + jb) * hb - 1, 0), cb + col))

    def nxt(col):
        return pl.BlockSpec((ATT_HALF, ATT_GW), lambda r, jb: (jnp.minimum((r * nb + jb + 1) * hb, lasth), cb + col))

    o, lse = pl.pallas_call(
        functools.partial(_attn_kernel, tq, blk_bounds),
        grid=(dil, nb),
        in_specs=[main(0), prev(1), main(1), nxt(1), prev(2), main(2), nxt(2),
                  pl.BlockSpec((ATT_HPG, tq, tk), lambda r, jb: (0, 0, 0))],
        out_specs=[pl.BlockSpec((tq, ATT_GW), lambda r, jb: (r * nb + jb, 0)),
                   pl.BlockSpec((tq, ATT_HEAD_DIM), lambda r, jb: (r * nb + jb, 0))],
        out_shape=[jax.ShapeDtypeStruct((m, ATT_GW), BF16), jax.ShapeDtypeStruct((m, ATT_HEAD_DIM), F32)],
        compiler_params=_cparams(("parallel", "parallel"), 32),
        name=f"dilated_attention_{gi}",
    )(src, src, src, src, src, src, src, bias)
    if dil > 1:
        unperm = lambda a: a.reshape(dil, nview, a.shape[-1]).transpose(1, 0, 2).reshape(m, a.shape[-1])
        o, lse = unperm(o), unperm(lse)
    return o, lse


def _att_combine_kernel(o0, l0, o1, l1, o2, l2, out_ref):
    a, b, c = l0[...], l1[...], l2[...]
    mx = jnp.maximum(jnp.maximum(a, b), c)
    ea, eb, ec = jnp.exp(a - mx), jnp.exp(b - mx), jnp.exp(c - mx)
    inv = 1.0 / (ea + eb + ec)
    wa, wb, wc = ea * inv, eb * inv, ec * inv
    tr = out_ref.shape[0]
    for h in range(ATT_HPG):
        sl = slice(h * ATT_HEAD_DIM, (h + 1) * ATT_HEAD_DIM)
        bc = lambda w: jnp.broadcast_to(w[:, h:h + 1], (tr, ATT_HEAD_DIM))
        out_ref[:, sl] = (bc(wa) * o0[:, sl].astype(F32) + bc(wb) * o1[:, sl].astype(F32)
                          + bc(wc) * o2[:, sl].astype(F32)).astype(out_ref.dtype)


def _att_combine(outs, lses, tr=512):
    m, w = outs[0].shape
    ospec = pl.BlockSpec((tr, w), lambda i: (i, 0))
    lspec = pl.BlockSpec((tr, ATT_HEAD_DIM), lambda i: (i, 0))
    args = [x for pair in zip(outs, lses) for x in pair]
    return pl.pallas_call(
        _att_combine_kernel,
        grid=(m // tr,),
        in_specs=[ospec, lspec] * 3,
        out_specs=ospec,
        out_shape=jax.ShapeDtypeStruct((m, w), BF16),
        compiler_params=_cparams(("parallel",), 32),
        name="attention_combine",
    )(*args)


def _pack_w_in(w):
    splits = np.cumsum([SSM_INNER, SSM_XBC, 2 * SSM_HEADS, S5_WIDTH, 3 * 1536]).tolist()
    z, xbc, dt, u, qkv, gates = jnp.split(w, splits, axis=-1)
    ng = len(ATT_PATTERNS)
    qkv = qkv.reshape(w.shape[0], 3, ng, ATT_GW).transpose(0, 2, 1, 3).reshape(w.shape[0], 3 * ng * ATT_GW)
    main = jnp.concatenate([z, xbc, u, qkv, gates], axis=-1).astype(BF16)
    zpad = jnp.zeros((w.shape[0], 128 - SSM_HEADS), w.dtype)
    wdt = jnp.concatenate([dt[:, :SSM_HEADS], zpad, dt[:, SSM_HEADS:], zpad], axis=-1).astype(BF16)
    return main, wdt


def _pack_ffn(w_up, conv_w, conv_b, w_down):
    padc = lambda a: jnp.pad(a, [(0, 0)] * (a.ndim - 1) + [(0, D_FF_PAD - D_FF)])
    both = lambda a: jnp.concatenate([padc(a[..., :D_FF]), padc(a[..., D_FF:])], axis=-1)
    w_down_p = jnp.pad(w_down, ((0, D_FF_PAD - D_FF), (0, 0))).astype(BF16)
    return both(w_up).astype(BF16), both(conv_w).astype(F32), both(conv_b).astype(F32), w_down_p


def _trunk(x, seq_bounds, rel_bias, norm_mix, w_in, ssm_conv_w, ssm_conv_b, ssm_a_log, ssm_dt_bias, ssm_d,
           ssm_norm, ssm_w_out, s5_tables, s5_d, s5_w_glu, att_w_out, w_o, norm_ffn, w_up, ffn_conv_w,
           ffn_conv_b, w_down, final_norm):
    m = x.shape[0]
    depth = w_in.shape[0]
    tm = min(1024, m)
    tm2 = min(512, m)
    gate_blk = lambda b, tn: (COL_GATES + b * D_MODEL) // tn
    chunk_bounds = tuple((s // S5_CHUNK, l // S5_CHUNK) for s, l in seq_bounds)
    r_chunks = m // S5_CHUNK
    att_bias = [_att_bias(rel_bias, gi, _att_tile(seq_bounds, dil)) for gi, (_, dil) in enumerate(ATT_PATTERNS)]
    gated =lambda acc, gate: _sigmoid(gate.astype(F32)) * acc
    gated_add = lambda acc, gate, prev: prev + _sigmoid(gate.astype(F32)) * acc
    resid = lambda acc, res: res + acc

    for li in range(depth):
        w_main, w_dt = _pack_w_in(w_in[li])
        proj, dtraw = _in_proj(x, norm_mix[li].astype(F32), w_main, w_dt, ssm_conv_w[li].astype(F32),
                               ssm_conv_b[li].astype(F32), seq_bounds, tm)

        y_a = _ssd(proj, dtraw, ssm_dt_bias[li], ssm_a_log[li], ssm_d[li], ssm_norm[li].astype(F32), seq_bounds)
        merged = _matmul(y_a, ssm_w_out[li].astype(BF16), tm=tm, tn=512, out_dtype=F32, epilogue=gated,
                         extras=((proj, gate_blk(0, 512)),), name="ssm_out_proj")

        d_tiled = jnp.tile(s5_d[li].astype(F32).reshape(S5_GROUPS, 1, S5_GROUP), (1, 1, S5_CHUNK))
        y_b = _s5(proj, s5_tables, li, d_tiled, chunk_bounds)
        merged = _glu_matmul(y_b, s5_w_glu[li].astype(BF16), proj, merged, tm=tm, tn=512,
                             gate_blk=gate_blk(1, 512))

        outs, lses = zip(*[_attention_group(proj, att_bias[gi], gi, seq_bounds)
                           for gi in range(len(ATT_PATTERNS))])
        comb = _att_combine(outs, lses, tr=tm2)
        merged = _matmul(comb, att_w_out[li].astype(BF16), tm=tm, tn=512, out_dtype=BF16, epilogue=gated_add,
                         extras=((proj, gate_blk(2, 512)), (merged, 0)), name="att_out_proj")

        x = _matmul(merged, w_o[li].astype(BF16), tm=tm, tn=512, out_dtype=F32, epilogue=resid,
                    extras=((x, 0),), name="mix_out_proj")

        w_up_p, cw_p, cb_p, w_down_p = _pack_ffn(w_up[li], ffn_conv_w[li], ffn_conv_b[li], w_down[li])
        act = _ffn_up(x, norm_ffn[li].astype(F32), w_up_p, cw_p, cb_p, seq_bounds, tm)
        x = _matmul(act, w_down_p, tm=tm, tn=512, out_dtype=F32, epilogue=resid, extras=((x, 0),),
                    vmem_mb=56, name="ffn_down")
    return _rmsnorm(x, final_norm, F32)


def kernel(x_prompt, x_sample, rel_bias, norm_mix, w_in, ssm_conv_w, ssm_conv_b, ssm_a_log, ssm_dt_bias, ssm_d, ssm_norm, ssm_w_out, s5_a_re, s5_a_im, s5_log_step, s5_b_re, s5_b_im, s5_c_re, s5_c_im, s5_d, s5_w_glu, att_w_out, w_o, norm_ffn, w_up, ffn_conv_w, ffn_conv_b, w_down, final_norm):
    d = x_prompt.shape[-1]
    seq_bounds = []
    row = 0
    for arr in (x_prompt, x_sample):
        for _ in range(arr.shape[0]):
            seq_bounds.append((row, arr.shape[1]))
            row += arr.shape[1]
    seq_bounds = tuple(seq_bounds)
    x = jnp.concatenate([x_prompt.reshape(-1, d), x_sample.reshape(-1, d)], axis=0)
    s5_tables = _s5_prep(s5_a_re, s5_a_im, s5_log_step, s5_b_re, s5_b_im, s5_c_re, s5_c_im)
    y = _trunk(x, seq_bounds, rel_bias, norm_mix, w_in, ssm_conv_w, ssm_conv_b, ssm_a_log, ssm_dt_bias, ssm_d,
               ssm_norm, ssm_w_out, s5_tables, s5_d, s5_w_glu, att_w_out, w_o, norm_ffn, w_up, ffn_conv_w,
               ffn_conv_b, w_down, final_norm)
    n_prompt = x_prompt.shape[0] * x_prompt.shape[1]
    return (y[:n_prompt].reshape(x_prompt.shape), y[n_prompt:].reshape(x_sample.shape))
```

```python
import functools
import math

import numpy as np
import jax
import jax.numpy as jnp
from jax import lax
from jax.experimental import pallas as pl
from jax.experimental.pallas import tpu as pltpu

F32 = jnp.float32
BF16 = jnp.bfloat16

D_MODEL = 2048
NORM_EPS = 1e-6
NEG_INF = -1e30

SSM_HEADDIM = 64
SSM_INNER = 1536
SSM_HEADS = 24
SSM_GROUPS = 4
SSM_HPG = SSM_HEADS // SSM_GROUPS
SSM_STATE = 128
SSM_CONV = 5
SSM_CHUNK = 128
SSM_XBC = 2560

S5_WIDTH = 1024
S5_GROUP = 16
S5_GROUPS = 64
S5_STATE = 64
S5_CHUNK = 32
S5_CW = S5_CHUNK * S5_GROUP
S5_LAGS = 128
S5_SCAN_STEPS = 8

ATT_HEAD_DIM = 128
ATT_HPG = 4
ATT_PATTERNS = ((128, 1), (512, 4), (2048, 16))
ATT_GW = ATT_HPG * ATT_HEAD_DIM
ATT_HALF = 64
REL_BUCKETS = 32
REL_MAX_DIST = 1024

D_FF = 5504
D_FF_PAD = 5632
FFN_CONV = 3

COL_Z = 0
COL_XBC = 1536
COL_U = 4096
COL_QKV = 5120
COL_GATES = 9728
IN_COLS = 15872
DT_COLS = 256

V7X_VMEM_BYTES = 64 * 1024 * 1024


def _cparams(sem, vmem_mb):
    return pltpu.CompilerParams(dimension_semantics=sem, vmem_limit_bytes=vmem_mb * 1024 * 1024)


def _sigmoid(x):
    return 1.0 / (1.0 + jnp.exp(-x))


def _silu(x):
    return x * _sigmoid(x)


def _split3(x):
    hi = x.astype(BF16)
    r1 = x - hi.astype(F32)
    mid = r1.astype(BF16)
    lo = (r1 - mid.astype(F32)).astype(BF16)
    return hi, mid, lo


def _dot_sel(x, sel):
    hi, mid, lo = _split3(x)
    d = lambda a: jnp.dot(a, sel, preferred_element_type=F32)
    return d(hi) + d(mid) + d(lo)


def _sel_dot(sel, x):
    hi, mid, lo = _split3(x)
    d = lambda a: jnp.dot(sel, a, preferred_element_type=F32)
    return d(hi) + d(mid) + d(lo)


def _dot_f32(a, b):
    ah, am, _ = _split3(a)
    bh, bm, _ = _split3(b)
    d = lambda x, y: jnp.dot(x, y, preferred_element_type=F32)
    return d(ah, bh) + d(ah, bm) + d(am, bh)


def _rmsnorm_kernel(x_ref, g_ref, o_ref):
    x = x_ref[...]
    ms = jnp.mean(x * x, axis=-1, keepdims=True)
    o_ref[...] = (x * lax.rsqrt(ms + NORM_EPS) * g_ref[...]).astype(o_ref.dtype)


def _rmsnorm(x, gain, out_dtype, tr=512):
    m, d = x.shape
    return pl.pallas_call(
        _rmsnorm_kernel,
        grid=(m // tr,),
        in_specs=[pl.BlockSpec((tr, d), lambda i: (i, 0)),
                  pl.BlockSpec((1, d), lambda i: (0, 0))],
        out_specs=pl.BlockSpec((tr, d), lambda i: (i, 0)),
        out_shape=jax.ShapeDtypeStruct((m, d), out_dtype),
        compiler_params=_cparams(("parallel",), 32),
        name="rmsnorm",
    )(x, gain.reshape(1, d))


def _mm_kernel(epilogue, n_extra, a_ref, b_ref, *rest):
    extras = rest[:n_extra]
    o_ref = rest[n_extra]
    acc = jnp.dot(a_ref[...], b_ref[...], preferred_element_type=F32)
    if epilogue is not None:
        acc = epilogue(acc, *[e[...] for e in extras])
    o_ref[...] = acc.astype(o_ref.dtype)


def _layer_spec(w, layer, rows, tn, col_off=0):
    if w.ndim == 2:
        return pl.BlockSpec((rows, tn), lambda i, j: (0, j + col_off))
    return pl.BlockSpec((None, rows, tn), lambda i, j: (layer, 0, j + col_off))


def _matmul(a, b, *, tm, tn, out_dtype, k=None, a_kblk=0, epilogue=None, extras=(), vmem_mb=48,
            layer=None, name="matmul"):
    m = a.shape[0]
    kk, n = b.shape[-2:]
    if k is None:
        k = a.shape[1]
    assert k == kk and m % tm == 0 and n % tn == 0
    in_specs = [pl.BlockSpec((tm, k), lambda i, j: (i, a_kblk)), _layer_spec(b, layer, k, tn)]
    args = [a, b]
    for arr, off in extras:
        in_specs.append(pl.BlockSpec((tm, tn), functools.partial(lambda i, j, o: (i, j + o), o=off)))
        args.append(arr)
    return pl.pallas_call(
        functools.partial(_mm_kernel, epilogue, len(extras)),
        grid=(m // tm, n // tn),
        in_specs=in_specs,
        out_specs=pl.BlockSpec((tm, tn), lambda i, j: (i, j)),
        out_shape=jax.ShapeDtypeStruct((m, n), out_dtype),
        compiler_params=_cparams(("parallel", "parallel"), vmem_mb),
        name=name,
    )(*args)


def _glu_kernel(a_ref, bv_ref, bg_ref, gate_ref, acc_ref, o_ref):
    a = a_ref[...].astype(BF16)
    val = jnp.dot(a, bv_ref[...], preferred_element_type=F32)
    gate = jnp.dot(a, bg_ref[...], preferred_element_type=F32)
    branch = val * _sigmoid(gate)
    o_ref[...] = acc_ref[...] + _sigmoid(gate_ref[...].astype(F32)) * branch


def _glu_matmul(a, w, layer, proj, merged, *, tm, tn, gate_blk):
    m, k = a.shape
    n = w.shape[-1] // 2
    nb = n // tn
    return pl.pallas_call(
        _glu_kernel,
        grid=(m // tm, nb),
        in_specs=[pl.BlockSpec((tm, k), lambda i, j: (i, 0)),
                  _layer_spec(w, layer, k, tn), _layer_spec(w, layer, k, tn, nb),
                  pl.BlockSpec((tm, tn), lambda i, j: (i, j + gate_blk)),
                  pl.BlockSpec((tm, tn), lambda i, j: (i, j))],
        out_specs=pl.BlockSpec((tm, tn), lambda i, j: (i, j)),
        out_shape=jax.ShapeDtypeStruct((m, n), F32),
        compiler_params=_cparams(("parallel", "parallel"), 48),
        name="s5_glu_matmul",
    )(a, w, w, proj, merged)


HALO = 16


def _seq_flags(row0, nrows, seq_bounds):
    starts = [s for s, _ in seq_bounds]
    ends = [s + l for s, l in seq_bounds]
    is_start = functools.reduce(jnp.logical_or, [row0 == s for s in starts])
    is_end = functools.reduce(jnp.logical_or, [row0 + nrows == e for e in ends])
    return is_start, is_end


def _conv_taps(buf_ref, w_ref, b_ref, width, tr, cols=slice(None)):
    pad = width // 2
    acc = None
    for kk in range(width):
        term = buf_ref[pl.ds(HALO - pad + kk, tr), cols] * w_ref[kk:kk + 1, cols]
        acc = term if acc is None else acc + term
    return acc + b_ref[:, cols]


def _norm_rows(x, gain):
    ms = jnp.mean(x * x, axis=-1, keepdims=True)
    return (x * lax.rsqrt(ms + NORM_EPS) * gain).astype(BF16)


def _fill_normed(hn_ref, xp_ref, xm_ref, xn_ref, gain_ref, tm, seq_bounds):
    is_start, is_end = _seq_flags(pl.program_id(0) * tm, tm, seq_bounds)
    gain = gain_ref[...]
    hn_ref[0:HALO, :] = jnp.where(is_start, 0.0, _norm_rows(xp_ref[...], gain)).astype(BF16)
    hn_ref[HALO:HALO + tm, :] = _norm_rows(xm_ref[...], gain)
    hn_ref[HALO + tm:HALO + tm + HALO, :] = jnp.where(is_end, 0.0, _norm_rows(xn_ref[...], gain)).astype(BF16)


def _row_halo_specs(tm, d, nrows):
    hb = tm // HALO
    last = nrows // HALO - 1
    return [pl.BlockSpec((HALO, d), lambda i, j: (jnp.maximum(i * hb - 1, 0), 0)),
            pl.BlockSpec((tm, d), lambda i, j: (i, 0)),
            pl.BlockSpec((HALO, d), lambda i, j: (jnp.minimum((i + 1) * hb, last), 0))]


ATT_BLOCK = 256


def _att_dilated():
    return tuple((gi, dil) for gi, (_, dil) in enumerate(ATT_PATTERNS) if dil > 1)


def _att_row_perms():
    out = []
    for _, dil in _att_dilated():
        dst = np.arange(ATT_BLOCK)[:, None]
        per = ATT_BLOCK // dil
        fwd = (dst % per) * dil + dst // per == np.arange(ATT_BLOCK)[None, :]
        out.append(np.stack([fwd, fwd.T]))
    return jnp.asarray(np.stack(out), BF16)


def _in_proj_kernel(tm, tn, seq_bounds, conv_lo, conv_hi, dil_ranges, xp_ref, xm_ref, xn_ref, gain_ref, w_ref,
                    wdt_ref, cw_ref, cb_ref, perm_ref, o_ref, dt_ref, hn_ref, acc_ref):
    j = pl.program_id(1)

    @pl.when(j == 0)
    def _():
        _fill_normed(hn_ref, xp_ref, xm_ref, xn_ref, gain_ref, tm, seq_bounds)
        dt_ref[...] = jnp.dot(hn_ref[HALO:HALO + tm, :], wdt_ref[...], preferred_element_type=F32)

    is_conv = jnp.logical_and(j >= conv_lo, j < conv_hi)

    @pl.when(is_conv)
    def _():
        half = tn // 2
        for c in range(2):
            cols = slice(c * half, (c + 1) * half)
            acc_ref[:, cols] = jnp.dot(hn_ref[...], w_ref[:, cols], preferred_element_type=F32)
            o_ref[:, cols] = _silu(_conv_taps(acc_ref, cw_ref, cb_ref, SSM_CONV, tm, cols)).astype(o_ref.dtype)

    in_dil = [jnp.logical_and(j >= lo, j < hi) for lo, hi in dil_ranges]
    plain = jnp.logical_not(functools.reduce(jnp.logical_or, in_dil, is_conv))

    @pl.when(plain)
    def _():
        o_ref[...] = jnp.dot(hn_ref[HALO:HALO + tm, :], w_ref[...],
                             preferred_element_type=F32).astype(o_ref.dtype)

    for k, cond in enumerate(in_dil):
        @pl.when(cond)
        def _(k=k):
            acc = jnp.dot(hn_ref[HALO:HALO + tm, :], w_ref[...], preferred_element_type=F32).astype(BF16)
            for b in range(tm // ATT_BLOCK):
                rows = slice(b * ATT_BLOCK, (b + 1) * ATT_BLOCK)
                o_ref[rows, :] = jnp.dot(perm_ref[k, 0], acc[rows, :],
                                         preferred_element_type=F32).astype(o_ref.dtype)


def _in_proj(x, gain, w_main, w_dt, layer, conv_w, conv_b, seq_bounds, tm, tn=512):
    m, d = x.shape
    n = w_main.shape[-1]
    conv_lo, conv_hi = COL_XBC // tn, (COL_XBC + SSM_XBC) // tn
    cidx = lambda i, j: (0, jnp.clip(j - conv_lo, 0, conv_hi - conv_lo - 1))
    dil_ranges = tuple(((COL_QKV + gi * 3 * ATT_GW) // tn, (COL_QKV + (gi + 1) * 3 * ATT_GW) // tn)
                       for gi, _ in _att_dilated())
    perms = _att_row_perms()
    return pl.pallas_call(
        functools.partial(_in_proj_kernel, tm, tn, seq_bounds, conv_lo, conv_hi, dil_ranges),
        grid=(m // tm, n // tn),
        in_specs=_row_halo_specs(tm, d, m) + [
            pl.BlockSpec((1, d), lambda i, j: (0, 0)),
            _layer_spec(w_main, layer, d, tn),
            pl.BlockSpec((None, d, DT_COLS), lambda i, j: (layer, 0, 0)),
            pl.BlockSpec((SSM_CONV, tn), cidx),
            pl.BlockSpec((1, tn), cidx),
            pl.BlockSpec(perms.shape, lambda i, j: (0, 0, 0, 0))],
        out_specs=[pl.BlockSpec((tm, tn), lambda i, j: (i, j)),
                   pl.BlockSpec((tm, DT_COLS), lambda i, j: (i, 0))],
        out_shape=[jax.ShapeDtypeStruct((m, n), BF16), jax.ShapeDtypeStruct((m, DT_COLS), F32)],
        scratch_shapes=[pltpu.VMEM((tm + 2 * HALO, d), BF16), pltpu.VMEM((tm + 2 * HALO, tn), F32)],
        compiler_params=_cparams(("parallel", "arbitrary"), 56),
        name="in_proj",
    )(x, x, x, gain.reshape(1, d), w_main, w_dt, conv_w, conv_b.reshape(1, -1), perms)


def _ffn_up_kernel(tm, seq_bounds, xp_ref, xm_ref, xn_ref, gain_ref, wg_ref, wv_ref, cwg_ref, cbg_ref,
                   cwv_ref, cbv_ref, o_ref, hn_ref, sg_ref, sv_ref):
    @pl.when(pl.program_id(1) == 0)
    def _():
        _fill_normed(hn_ref, xp_ref, xm_ref, xn_ref, gain_ref, tm, seq_bounds)

    hn = hn_ref[...]
    sg_ref[...] = jnp.dot(hn, wg_ref[...], preferred_element_type=F32)
    sv_ref[...] = jnp.dot(hn, wv_ref[...], preferred_element_type=F32)
    gate = _conv_taps(sg_ref, cwg_ref, cbg_ref, FFN_CONV, tm)
    val = _conv_taps(sv_ref, cwv_ref, cbv_ref, FFN_CONV, tm)
    o_ref[...] = (_silu(gate) * val).astype(o_ref.dtype)


def _ffn_up(x, gain, w, layer, conv_w, conv_b, seq_bounds, tm, tn=512):
    m, d = x.shape
    nb = D_FF_PAD // tn
    col = lambda rows, off: pl.BlockSpec((rows, tn), functools.partial(lambda i, j, o: (0, j + o), o=off))
    cb2 = conv_b.reshape(1, -1)
    return pl.pallas_call(
        functools.partial(_ffn_up_kernel, tm, seq_bounds),
        grid=(m // tm, nb),
        in_specs=_row_halo_specs(tm, d, m) + [
            pl.BlockSpec((1, d), lambda i, j: (0, 0)),
            _layer_spec(w, layer, d, tn), _layer_spec(w, layer, d, tn, nb),
            col(FFN_CONV, 0), col(1, 0), col(FFN_CONV, nb), col(1, nb)],
        out_specs=pl.BlockSpec((tm, tn), lambda i, j: (i, j)),
        out_shape=jax.ShapeDtypeStruct((m, D_FF_PAD), BF16),
        scratch_shapes=[pltpu.VMEM((tm + 2 * HALO, d), BF16), pltpu.VMEM((tm + 2 * HALO, tn), F32),
                        pltpu.VMEM((tm + 2 * HALO, tn), F32)],
        compiler_params=_cparams(("parallel", "arbitrary"), 56),
        name="ffn_up",
    )(x, x, x, gain.reshape(1, d), w, w, conv_w, cb2, conv_w, cb2)


def _softplus(x):
    return jnp.maximum(x, 0.0) + jnp.log1p(jnp.exp(-jnp.abs(x)))


def _ssd_cumsums(dt_ref, bias_ref, alog_ref, d, row0=0):
    t = SSM_CHUNK
    dt = _softplus(dt_ref[row0:row0 + t, d * 128:(d + 1) * 128] + bias_ref[d])
    da = dt * (-jnp.exp(alog_ref[d]))
    row = lax.broadcasted_iota(jnp.int32, (t, t), 0)
    col = lax.broadcasted_iota(jnp.int32, (t, t), 1)
    tri = jnp.where((col <= row) if d == 0 else (col >= row), 1.0, 0.0).astype(BF16)
    return dt, _sel_dot(tri, da)


SSD_SCAN_CHUNKS = 2


def _ssd_state_kernel(nsteps, seq_bounds, xf_ref, bf_ref, dtf_ref, xb_ref, bb_ref, dtb_ref, bias_ref, alog_ref,
                      hf_ref, hb_ref, h_ref):
    t = SSM_CHUNK
    n = SSM_STATE
    p = SSM_HEADDIM
    kc = SSD_SCAN_CHUNKS
    i = pl.program_id(0)

    @pl.when(i == 0)
    def _():
        h_ref[...] = jnp.zeros(h_ref.shape, F32)

    for d, (x_ref, b_ref, dt_ref, out_ref) in enumerate(((xf_ref, bf_ref, dtf_ref, hf_ref),
                                                         (xb_ref, bb_ref, dtb_ref, hb_ref))):
        blk = i if d == 0 else nsteps - 1 - i
        contrib = []
        for c in range(kc):
            r0 = c * t
            dt, cs = _ssd_cumsums(dt_ref, bias_ref, alog_ref, d, r0)
            tot = cs[t - 1:t, :] if d == 0 else cs[0:1, :]
            w = jnp.exp(tot - cs) * dt
            etot = jnp.exp(tot)
            per_group = []
            for g in range(SSM_GROUPS):
                bg_t = b_ref[r0:r0 + t, g * n:(g + 1) * n].astype(F32).T.astype(BF16)
                xw, dec = [], []
                for hh in range(SSM_HPG):
                    h = g * SSM_HPG + hh
                    xh = x_ref[r0:r0 + t, h * p:(h + 1) * p].astype(F32)
                    xw.append((xh * jnp.broadcast_to(w[:, h:h + 1], (t, p))).astype(BF16))
                    dec.append(jnp.broadcast_to(etot[:, h:h + 1], (n, p)))
                st = jnp.dot(bg_t, jnp.concatenate(xw, axis=1), preferred_element_type=F32)
                per_group.append((st, jnp.concatenate(dec, axis=1)))
            contrib.append(per_group)
        for c in (range(kc) if d == 0 else reversed(range(kc))):
            is_start, is_end = _seq_flags((blk * kc + c) * t, t, seq_bounds)
            reset = is_start if d == 0 else is_end
            for g in range(SSM_GROUPS):
                st, dec = contrib[c][g]
                hin = jnp.where(reset, 0.0, h_ref[d, g])
                out_ref[c, g] = hin.astype(BF16)
                h_ref[d, g] = hin * dec + st


def _ssd_states(proj, dtraw, dt_bias, a_log, seq_bounds):
    m = proj.shape[0]
    t = SSM_CHUNK
    kc = SSD_SCAN_CHUNKS
    nchunks = m // t
    nsteps = nchunks // kc
    hw = SSM_HPG * SSM_HEADDIM
    bw = SSM_GROUPS * SSM_STATE
    fwd = lambda i: i
    bwd = lambda i: nsteps - 1 - i
    xspec = lambda f: pl.BlockSpec((kc * t, SSM_INNER), lambda i: (f(i), COL_XBC // SSM_INNER))
    bspec = lambda f: pl.BlockSpec((kc * t, bw), lambda i: (f(i), (COL_XBC + SSM_INNER) // bw))
    dspec = lambda f: pl.BlockSpec((kc * t, DT_COLS), lambda i: (f(i), 0))
    hspec = lambda f: pl.BlockSpec((kc, SSM_GROUPS, SSM_STATE, hw), lambda i: (f(i), 0, 0, 0))
    const = pl.BlockSpec((2, 1, 128), lambda i: (0, 0, 0))
    return pl.pallas_call(
        functools.partial(_ssd_state_kernel, nsteps, seq_bounds),
        grid=(nsteps,),
        in_specs=[xspec(fwd), bspec(fwd), dspec(fwd), xspec(bwd), bspec(bwd), dspec(bwd), const, const],
        out_specs=[hspec(fwd), hspec(bwd)],
        out_shape=[jax.ShapeDtypeStruct((nchunks, SSM_GROUPS, SSM_STATE, hw), BF16)] * 2,
        scratch_shapes=[pltpu.VMEM((2, SSM_GROUPS, SSM_STATE, hw), F32)],
        compiler_params=_cparams(("arbitrary",), 32),
        name="ssd_states",
    )(proj, proj, dtraw, proj, proj, dtraw, dt_bias, a_log)


def _ssd_out_kernel(x_ref, b_ref, c_ref, dt_ref, bias_ref, alog_ref, hf_ref, hb_ref, z_ref, dskip_ref,
                    gain_ref, o_ref, y_ref):
    t = SSM_CHUNK
    n = SSM_STATE
    p = SSM_HEADDIM
    dtf, csf = _ssd_cumsums(dt_ref, bias_ref, alog_ref, 0)
    dtb, csb = _ssd_cumsums(dt_ref, bias_ref, alog_ref, 1)
    csf_t, csb_t, dtf_t, dtb_t = csf.T, csb.T, dtf.T, dtb.T
    dts_t = dtf_t + dtb_t
    row = lax.broadcasted_iota(jnp.int32, (t, t), 0)
    col = lax.broadcasted_iota(jnp.int32, (t, t), 1)
    low = col <= row
    low_strict = col < row
    up_strict = col > row
    for g in range(SSM_GROUPS):
        bg = b_ref[:, g * n:(g + 1) * n]
        cg = c_ref[:, g * n:(g + 1) * n]
        cb = lax.dot_general(cg, bg, (((1,), (1,)), ((), ())), preferred_element_type=F32)
        cg_f = cg.astype(F32)
        for hh in range(SSM_HPG):
            h = g * SSM_HPG + hh
            colf = jnp.broadcast_to(csf[:, h:h + 1], (t, t))
            colb = jnp.broadcast_to(csb[:, h:h + 1], (t, t))
            seg = jnp.where(low, colf - jnp.broadcast_to(csf_t[h:h + 1, :], (t, t)),
                            colb - jnp.broadcast_to(csb_t[h:h + 1, :], (t, t)))
            coef = jnp.where(low_strict, jnp.broadcast_to(dtf_t[h:h + 1, :], (t, t)),
                             jnp.where(up_strict, jnp.broadcast_to(dtb_t[h:h + 1, :], (t, t)),
                                       jnp.broadcast_to(dts_t[h:h + 1, :], (t, t))))
            mh = jnp.exp(seg) * cb * coef
            lhs = jnp.concatenate([mh, cg_f * jnp.exp(colf), cg_f * jnp.exp(colb)], axis=1).astype(BF16)
            rhs = jnp.concatenate([x_ref[:, h * p:(h + 1) * p],
                                   hf_ref[0, g, :, hh * p:(hh + 1) * p],
                                   hb_ref[0, g, :, hh * p:(hh + 1) * p]], axis=0)
            y_ref[:, h * p:(h + 1) * p] = jnp.dot(lhs, rhs, preferred_element_type=F32)
    y = y_ref[...] + x_ref[...].astype(F32) * dskip_ref[...]
    y = y * _silu(z_ref[...].astype(F32))
    ms = jnp.mean(y * y, axis=-1, keepdims=True)
    o_ref[...] = (y * lax.rsqrt(ms + NORM_EPS) * gain_ref[...]).astype(o_ref.dtype)


def _ssd_out(proj, dtraw, dt_bias, a_log, h_f, h_b, d_skip, norm_g):
    m = proj.shape[0]
    t = SSM_CHUNK
    w = SSM_INNER
    hw = SSM_HPG * SSM_HEADDIM
    bw = SSM_GROUPS * SSM_STATE
    hspec = pl.BlockSpec((1, SSM_GROUPS, SSM_STATE, hw), lambda i: (i, 0, 0, 0))
    return pl.pallas_call(
        _ssd_out_kernel,
        grid=(m // t,),
        in_specs=[pl.BlockSpec((t, w), lambda i: (i, COL_XBC // w)),
                  pl.BlockSpec((t, bw), lambda i: (i, (COL_XBC + w) // bw)),
                  pl.BlockSpec((t, bw), lambda i: (i, (COL_XBC + w) // bw + 1)),
                  pl.BlockSpec((t, DT_COLS), lambda i: (i, 0)),
                  pl.BlockSpec((2, 1, 128), lambda i: (0, 0, 0)),
                  pl.BlockSpec((2, 1, 128), lambda i: (0, 0, 0)),
                  hspec, hspec,
                  pl.BlockSpec((t, w), lambda i: (i, COL_Z // w)),
                  pl.BlockSpec((1, w), lambda i: (0, 0)),
                  pl.BlockSpec((1, w), lambda i: (0, 0))],
        out_specs=pl.BlockSpec((t, w), lambda i: (i, 0)),
        out_shape=jax.ShapeDtypeStruct((m, w), BF16),
        scratch_shapes=[pltpu.VMEM((t, w), F32)],
        compiler_params=_cparams(("parallel",), 32),
        name="ssd_out",
    )(proj, proj, proj, dtraw, dt_bias, a_log, h_f, h_b, proj,
      jnp.repeat(d_skip.astype(F32), SSM_HEADDIM).reshape(1, w), norm_g.reshape(1, w))


def _ssd(proj, dtraw, dt_bias, a_log, d_skip, norm_g, seq_bounds):
    pad = lambda v: jnp.pad(v.astype(F32), ((0, 0), (0, 128 - SSM_HEADS))).reshape(2, 1, 128)
    dt_bias, a_log = pad(dt_bias), pad(a_log)
    h_f, h_b = _ssd_states(proj, dtraw, dt_bias, a_log, seq_bounds)
    return _ssd_out(proj, dtraw, dt_bias, a_log, h_f, h_b, d_skip, norm_g)


def _s5_selectors():
    t, c = S5_CHUNK, S5_GROUP
    lag = np.arange(S5_LAGS)[:, None]
    tok = (np.arange(S5_CW) // c)[None, :]
    e_exit_f = (lag == t - 1 - tok)
    e_exit_b = (lag == tok)
    e_in_f = (lag == tok + 1)
    e_in_b = (lag == t - tok)
    lagidx = (np.arange(2 * S5_CW) // c)[None, :]
    rel = lagidx - (t - 1)
    inb = lagidx <= 2 * t - 2
    e_k_f = (lag == rel) & (rel >= 0) & inb
    e_k_b = (lag == -rel) & (rel <= 0) & inb
    tile = (np.arange(c)[:, None] == (np.arange(2 * S5_CW) % c)[None, :])
    sel = np.stack([np.concatenate([a, b], axis=1) for a, b in
                    ((e_exit_f, e_in_f), (e_exit_b, e_in_b))])
    selk = np.stack([e_k_f, e_k_b])
    return (jnp.asarray(sel, BF16), jnp.asarray(selk, BF16), jnp.asarray(tile, BF16))


def _s5_prep_kernel(acol_ref, arow_ref, ls_ref, b_ref, bt_ref, ct_ref, sel_ref, selk_ref, tile_ref,
                    sign_ref, w_ref, ws_ref, wo_ref, lpa_ref, lpb_ref):
    t = S5_CHUNK
    cw = S5_CW
    p = S5_STATE
    tile = tile_ref[...]
    lagf = lax.broadcasted_iota(jnp.int32, (p, S5_LAGS), 1).astype(F32)
    kt = jnp.zeros((S5_GROUP, 2 * cw), F32)
    for d in range(2):
        step = jnp.exp(ls_ref[0, 0, d])
        are = acol_ref[0, 0, d, 0]
        aim = acol_ref[0, 0, d, 1]
        mag = are * step
        th = aim * step
        amp = jnp.exp(lagf * mag)
        pwr = amp * jnp.cos(lagf * th)
        pwi = amp * jnp.sin(lagf * th)
        lbr = jnp.exp(mag) * jnp.cos(th)
        lbi = jnp.exp(mag) * jnp.sin(th)
        den = are * are + aim * aim
        cfr = ((lbr - 1.0) * are + lbi * aim) / den
        cfi = (lbi * are - (lbr - 1.0) * aim) / den
        bre = b_ref[0, 0, 0]
        bim = b_ref[0, 0, 1]
        bbr = cfr * bre - cfi * bim
        bbi = cfr * bim + cfi * bre
        step_r = step
        are_r = arow_ref[0, 0, d, 0:1, 0:p]
        aim_r = arow_ref[0, 0, d, 1:2, 0:p]
        mag_r = are_r * step_r
        th_r = aim_r * step_r
        lbr_r = jnp.exp(mag_r) * jnp.cos(th_r)
        lbi_r = jnp.exp(mag_r) * jnp.sin(th_r)
        den_r = are_r * are_r + aim_r * aim_r
        cfr_r = ((lbr_r - 1.0) * are_r + lbi_r * aim_r) / den_r
        cfi_r = (lbi_r * are_r - (lbr_r - 1.0) * aim_r) / den_r
        btr = bt_ref[0, 0, 0]
        bti = bt_ref[0, 0, 1]
        bbr_t = cfr_r * btr - cfi_r * bti
        bbi_t = cfr_r * bti + cfi_r * btr
        ctr = ct_ref[0, 0, d, 0]
        cti = ct_ref[0, 0, d, 1]

        sel = sel_ref[d]
        er = _dot_sel(pwr, sel)
        ei = _dot_sel(pwi, sel)
        tb_r = _dot_sel(bbr, tile[:, 0:cw])
        tb_i = _dot_sel(bbi, tile[:, 0:cw])
        tc_r = _dot_sel(ctr, tile)
        tc_i = _dot_sel(cti, tile)
        ws_ref[0, 0, d * 2 * p:d * 2 * p + p, :] = (er[:, 0:cw] * tb_r - ei[:, 0:cw] * tb_i).astype(BF16)
        ws_ref[0, 0, d * 2 * p + p:(d + 1) * 2 * p, :] = (er[:, 0:cw] * tb_i + ei[:, 0:cw] * tb_r).astype(BF16)
        zr = er[:, cw:] * tc_r[:, 0:cw] - ei[:, cw:] * tc_i[:, 0:cw]
        zi = er[:, cw:] * tc_i[:, 0:cw] + ei[:, cw:] * tc_r[:, 0:cw]
        wo_ref[0, 0, d * 2 * p:d * 2 * p + p, :] = zr.astype(BF16)
        wo_ref[0, 0, d * 2 * p + p:(d + 1) * 2 * p, :] = (-zi).astype(BF16)
        selk = selk_ref[d]
        qr = _dot_sel(pwr, selk) * tc_r - _dot_sel(pwi, selk) * tc_i
        qi = _dot_sel(pwr, selk) * tc_i + _dot_sel(pwi, selk) * tc_r
        kt = kt + _dot_f32(bbr_t, qr) - _dot_f32(bbi_t, qi)
        are2 = arow_ref[0, 0, d, 0:1, :]
        aim2 = arow_ref[0, 0, d, 1:2, :]
        ampt = jnp.exp(are2 * step_r * float(t))
        zr2 = ampt * jnp.cos(aim2 * step_r * float(t))
        zi2 = ampt * jnp.sin(aim2 * step_r * float(t))
        sign = sign_ref[...]
        for k in range(S5_SCAN_STEPS):
            lpa_ref[0, 0, d, k:k + 1, :] = zr2
            lpb_ref[0, 0, d, k:k + 1, :] = sign * zi2
            zr2, zi2 = zr2 * zr2 - zi2 * zi2, 2.0 * zr2 * zi2
    for s in range(t):
        off = (t - 1 - s) * S5_GROUP
        w_ref[0, 0, s * S5_GROUP:(s + 1) * S5_GROUP, :] = kt[:, off:off + cw].astype(BF16)


def _s5_prep(a_re, a_im, log_step, b_re, b_im, c_re, c_im):
    depth = a_re.shape[0]
    g, p, c = S5_GROUPS, S5_STATE, S5_GROUP
    a = jnp.stack([a_re, a_im], axis=2).astype(F32)
    a = a.transpose(0, 3, 1, 2, 4)
    acol = a[..., None]
    arow = jnp.concatenate([a, a], axis=-1)
    ls = log_step.astype(F32).transpose(0, 2, 1).reshape(depth, g, 2, 1, 1)
    b = jnp.stack([b_re, b_im], axis=2).astype(F32)
    bt = b.transpose(0, 1, 2, 4, 3)
    ct = jnp.stack([c_re, c_im], axis=3).astype(F32)
    ct = ct.transpose(0, 2, 1, 3, 5, 4)
    sel, selk, tile = _s5_selectors()
    sign = jnp.concatenate([-jnp.ones((1, p), F32), jnp.ones((1, p), F32)], axis=1)
    full = lambda shp: pl.BlockSpec(shp, lambda l, j: (0,) * len(shp))
    per = lambda shp: pl.BlockSpec((1, 1) + shp, lambda l, j: (l, j) + (0,) * len(shp))
    cw = S5_CW
    return pl.pallas_call(
        _s5_prep_kernel,
        grid=(depth, g),
        in_specs=[per((2, 2, p, 1)), per((2, 2, 2 * p)), per((2, 1, 1)), per((2, p, c)), per((2, c, p)),
                  per((2, 2, p, c)), full((2, S5_LAGS, 2 * cw)), full((2, S5_LAGS, 2 * cw)),
                  full((c, 2 * cw)), full((1, 2 * p))],
        out_specs=[per((cw, cw)), per((4 * p, cw)), per((4 * p, cw)),
                   per((2, S5_SCAN_STEPS, 2 * p)), per((2, S5_SCAN_STEPS, 2 * p))],
        out_shape=[jax.ShapeDtypeStruct((depth, g, cw, cw), BF16),
                   jax.ShapeDtypeStruct((depth, g, 4 * p, cw), BF16),
                   jax.ShapeDtypeStruct((depth, g, 4 * p, cw), BF16),
                   jax.ShapeDtypeStruct((depth, g, 2, S5_SCAN_STEPS, 2 * p), F32),
                   jax.ShapeDtypeStruct((depth, g, 2, S5_SCAN_STEPS, 2 * p), F32)],
        compiler_params=_cparams(("parallel", "parallel"), 32),
        name="s5_prep",
    )(acol, arow, ls, b, bt, ct, sel, selk, tile, sign)


def _gelu_tanh(x):
    return 0.5 * x * (1.0 + jnp.tanh(math.sqrt(2.0 / math.pi) * (x + 0.044715 * (x * x * x))))


LANES = 128


BF16_ROWS = 16
S5_PERM_ROWS = BF16_ROWS * S5_CHUNK
S5_GPB = LANES // S5_GROUP


def _s5_kernel(chunk_bounds, u_ref, perm_ref, pick_ref, w_ref, ws_ref, wo_ref, lpa_ref, lpb_ref,
               d_ref, o_ref, cat_ref, ug_ref, yg_ref):
    t = S5_CHUNK
    g = pl.program_id(1)
    r = u_ref.shape[0] // t
    nblk = r // BF16_ROWS
    p2 = 2 * S5_STATE
    qw = S5_GPB * LANES

    @pl.when(g == 0)
    def _():
        def body(b, carry):
            rows = pl.ds(pl.multiple_of(b * S5_PERM_ROWS, S5_PERM_ROWS), S5_PERM_ROWS)
            blk = jnp.dot(perm_ref[0], u_ref[rows, :], preferred_element_type=F32).astype(BF16)
            crow = pl.ds(pl.multiple_of(b * BF16_ROWS, BF16_ROWS), BF16_ROWS)
            for s in range(t):
                cat_ref[crow, s * LANES:(s + 1) * LANES] = blk[s * BF16_ROWS:(s + 1) * BF16_ROWS, :]
            return carry
        lax.fori_loop(0, nblk, body, 0)
        for q in range(t // S5_GPB):
            picked = jnp.dot(cat_ref[:, q * qw:(q + 1) * qw], pick_ref[0],
                             preferred_element_type=F32).astype(BF16)
            for gg in range(S5_GPB):
                ug_ref[gg, :, q * LANES:(q + 1) * LANES] = picked[:, gg * LANES:(gg + 1) * LANES]

    u = ug_ref[g]
    uf = u.astype(F32)
    y = jnp.dot(u, w_ref[0, 0], preferred_element_type=F32)
    st = lax.dot_general(u, ws_ref[0, 0], (((1,), (1,)), ((), ())), preferred_element_type=F32)
    ridx = lax.broadcasted_iota(jnp.int32, (r, p2), 0)
    rloc = jnp.zeros((r, p2), jnp.int32)
    rlen = jnp.zeros((r, p2), jnp.int32)
    for s0, ln in chunk_bounds:
        inside = (ridx >= s0) & (ridx < s0 + ln)
        rloc = jnp.where(inside, ridx - s0, rloc)
        rlen = jnp.where(inside, ln, rlen)
    xin = []
    for d in range(2):
        x = st[:, d * p2:(d + 1) * p2]
        for k in range(S5_SCAN_STEPS):
            sh = 1 << k
            if d == 0:
                prev = jnp.where(rloc >= sh, pltpu.roll(x, sh, 0), 0.0)
            else:
                prev = jnp.where(rloc < rlen - sh, pltpu.roll(x, r - sh, 0), 0.0)
            x = (x + lpa_ref[0, 0, d, k:k + 1, :] * prev
                 + lpb_ref[0, 0, d, k:k + 1, :] * pltpu.roll(prev, S5_STATE, 1))
        if d == 0:
            xin.append(jnp.where(rloc >= 1, pltpu.roll(x, 1, 0), 0.0))
        else:
            xin.append(jnp.where(rloc < rlen - 1, pltpu.roll(x, r - 1, 0), 0.0))
    xin = jnp.concatenate(xin, axis=1).astype(BF16)
    y = y + jnp.dot(xin, wo_ref[0, 0], preferred_element_type=F32)
    y = _gelu_tanh(y + uf * d_ref[0])

    yg_ref[g] = y.astype(BF16)

    @pl.when(g == pl.num_programs(1) - 1)
    def _():
        for q in range(t // S5_GPB):
            lhs = jnp.concatenate([yg_ref[gg, :, q * LANES:(q + 1) * LANES] for gg in range(S5_GPB)], axis=1)
            cat_ref[:, q * qw:(q + 1) * qw] = jnp.dot(lhs, pick_ref[1],
                                                      preferred_element_type=F32).astype(BF16)

        def body(b, carry):
            crow = pl.ds(pl.multiple_of(b * BF16_ROWS, BF16_ROWS), BF16_ROWS)
            blk = jnp.concatenate([cat_ref[crow, s * LANES:(s + 1) * LANES] for s in range(t)], axis=0)
            rows = pl.ds(pl.multiple_of(b * S5_PERM_ROWS, S5_PERM_ROWS), S5_PERM_ROWS)
            o_ref[rows, :] = jnp.dot(perm_ref[1], blk, preferred_element_type=F32).astype(o_ref.dtype)
            return carry
        lax.fori_loop(0, nblk, body, 0)


def _s5_movers():
    dst = np.arange(S5_PERM_ROWS)[:, None]
    fwd = (dst % BF16_ROWS) * S5_CHUNK + dst // BF16_ROWS == np.arange(S5_PERM_ROWS)[None, :]
    src = np.arange(S5_GPB * LANES)
    s8, g, j = src // LANES, src % LANES // S5_GROUP, src % S5_GROUP
    pick = (g * LANES + s8 * S5_GROUP + j)[:, None] == src[None, :]
    return jnp.asarray(np.stack([fwd, fwd.T]), BF16), jnp.asarray(np.stack([pick, pick.T]), BF16)


def _s5(proj, tables, li, d_tiled, chunk_bounds):
    m = proj.shape[0]
    w, ws, wo, lpa, lpb = tables
    p = S5_STATE
    cw = S5_CW
    gpb = S5_GPB
    r = m // S5_CHUNK
    perm, pick = _s5_movers()
    per = lambda shp: pl.BlockSpec((1, 1) + shp, lambda b, j: (li, b * gpb + j) + (0,) * len(shp))
    return pl.pallas_call(
        functools.partial(_s5_kernel, chunk_bounds),
        grid=(S5_GROUPS // gpb, gpb),
        in_specs=[pl.BlockSpec((m, LANES), lambda b, j: (0, COL_U // LANES + b)),
                  pl.BlockSpec((2, S5_PERM_ROWS, S5_PERM_ROWS), lambda b, j: (0, 0, 0)),
                  pl.BlockSpec((2, gpb * LANES, gpb * LANES), lambda b, j: (0, 0, 0)),
                  per((cw, cw)), per((4 * p, cw)), per((4 * p, cw)),
                  per((2, S5_SCAN_STEPS, 2 * p)), per((2, S5_SCAN_STEPS, 2 * p)),
                  pl.BlockSpec((1, 1, cw), lambda b, j: (b * gpb + j, 0, 0))],
        out_specs=pl.BlockSpec((m, LANES), lambda b, j: (0, b)),
        out_shape=jax.ShapeDtypeStruct((m, S5_WIDTH), BF16),
        scratch_shapes=[pltpu.VMEM((r, S5_CHUNK * LANES), BF16), pltpu.VMEM((gpb, r, cw), BF16),
                        pltpu.VMEM((gpb, r, cw), BF16)],
        compiler_params=_cparams(("parallel", "arbitrary"), 48),
        name="s5_mix",
    )(proj, perm, pick, w, ws, wo, lpa, lpb, d_tiled)


def _t5_bucket(rel):
    half = REL_BUCKETS // 2
    exact = half // 2
    sign = (rel > 0).astype(np.int32) * half
    n = np.abs(rel)
    large = exact + (np.log(np.maximum(n, 1) / exact) / np.log(REL_MAX_DIST / exact)
                     * (half - exact)).astype(np.int32)
    large = np.minimum(large, half - 1)
    return sign + np.where(n < exact, n, large)


def _att_bias_kernel(gi, idx_ref, tbl_ref, o_ref):
    idx = idx_ref[...]
    for h in range(ATT_HPG):
        acc = jnp.full(idx.shape, NEG_INF, F32)
        for b in range(REL_BUCKETS):
            acc = jnp.where(idx == b, tbl_ref[b, gi * ATT_HPG + h], acc)
        o_ref[h] = acc


def _att_tile(seq_bounds, dil):
    return min(256, min(l for _, l in seq_bounds) // dil)


def _att_bias(rel_bias, gi, tq):
    dil = ATT_PATTERNS[gi][1]
    tk = tq + 2 * ATT_HALF
    rel = (np.arange(tk)[None, :] - ATT_HALF) - np.arange(tq)[:, None]
    idx = np.where(np.abs(rel) <= ATT_HALF, _t5_bucket(rel * dil), -1).astype(np.int32)
    return pl.pallas_call(
        functools.partial(_att_bias_kernel, gi),
        in_specs=[pl.BlockSpec(memory_space=pltpu.VMEM), pl.BlockSpec(memory_space=pltpu.SMEM)],
        out_specs=pl.BlockSpec(memory_space=pltpu.VMEM),
        out_shape=jax.ShapeDtypeStruct((ATT_HPG, tq, tk), F32),
        name=f"attention_bias_{gi}",
    )(jnp.asarray(idx), rel_bias.astype(F32))


def _attn_kernel(tq, blk_bounds, q_ref, kp_ref, km_ref, kn_ref, vp_ref, vm_ref, vn_ref, bias_ref,
                 o_ref, lse_ref):
    jb = pl.program_id(1)
    is_first, is_last = _seq_flags(jb, 1, blk_bounds)
    tk = tq + 2 * ATT_HALF
    flat = lambda ref: ref[...].reshape(-1, ref.shape[-1])
    colk = lax.broadcasted_iota(jnp.int32, (tq, tk), 1)
    valid = jnp.logical_and(jnp.logical_or(colk >= ATT_HALF, jnp.logical_not(is_first)),
                            jnp.logical_or(colk < tq + ATT_HALF, jnp.logical_not(is_last)))
    q = flat(q_ref)
    kcat = jnp.concatenate([flat(kp_ref), flat(km_ref), flat(kn_ref)], axis=0)
    vcat = jnp.concatenate([flat(vp_ref), flat(vm_ref), flat(vn_ref)], axis=0)
    scale = ATT_HEAD_DIM ** -0.5
    lane = lax.broadcasted_iota(jnp.int32, (tq, ATT_HEAD_DIM), 1)
    lse_tile = jnp.zeros((tq, ATT_HEAD_DIM), F32)
    outs = []
    for h in range(ATT_HPG):
        sl = slice(h * ATT_HEAD_DIM, (h + 1) * ATT_HEAD_DIM)
        s = lax.dot_general(q[:, sl], kcat[:, sl], (((1,), (1,)), ((), ())), preferred_element_type=F32)
        s = jnp.where(valid, s * scale + bias_ref[h], NEG_INF)
        mx = jnp.max(s, axis=-1, keepdims=True)
        pr = jnp.exp(s - mx)
        den = jnp.sum(pr, axis=-1, keepdims=True)
        o = jnp.dot(pr.astype(BF16), vcat[:, sl], preferred_element_type=F32)
        outs.append((o / den).astype(o_ref.dtype))
        lse_tile = jnp.where(lane == h, mx + jnp.log(den), lse_tile)
    o_ref[...] = jnp.concatenate(outs, axis=1).reshape(o_ref.shape)
    lse_ref[...] = lse_tile.reshape(lse_ref.shape)


def _attention_group(proj, bias, gi, seq_bounds):
    window, dil = ATT_PATTERNS[gi]
    assert window // (2 * dil) == ATT_HALF
    m = proj.shape[0]
    per = ATT_BLOCK // dil
    tq = bias.shape[1]
    nbq = tq // per
    hr = min(per, ATT_HALF)
    hb = ATT_HALF // hr
    nblocks = m // ATT_BLOCK
    assert tq % per == 0 and nbq % hb == 0 and per % hr == 0
    assert all((s // ATT_BLOCK) % nbq == 0 and (l // ATT_BLOCK) % nbq == 0 for s, l in seq_bounds)
    blk_bounds = tuple((s // ATT_BLOCK // nbq, l // ATT_BLOCK // nbq) for s, l in seq_bounds)
    tk = tq + 2 * ATT_HALF
    cb = (COL_QKV + gi * 3 * ATT_GW) // ATT_GW
    src = proj.reshape(nblocks, ATT_BLOCK, proj.shape[1])
    lasth = nblocks // hb - 1

    def main(col):
        return pl.BlockSpec((nbq, per, ATT_GW), lambda r, jb: (jb, r, cb + col))

    def prev(col):
        return pl.BlockSpec((hb, hr, ATT_GW),
                            lambda r, jb: (jnp.maximum(jb * (nbq // hb) - 1, 0), (r + 1) * (per // hr) - 1, cb + col))

    def nxt(col):
        return pl.BlockSpec((hb, hr, ATT_GW),
                            lambda r, jb: (jnp.minimum((jb + 1) * (nbq // hb), lasth), r * (per // hr), cb + col))

    o, lse = pl.pallas_call(
        functools.partial(_attn_kernel, tq, blk_bounds),
        grid=(dil, nblocks // nbq),
        in_specs=[main(0), prev(1), main(1), nxt(1), prev(2), main(2), nxt(2),
                  pl.BlockSpec((ATT_HPG, tq, tk), lambda r, jb: (0, 0, 0))],
        out_specs=[pl.BlockSpec((nbq, per, ATT_GW), lambda r, jb: (jb, r, 0)),
                   pl.BlockSpec((nbq, per, ATT_HEAD_DIM), lambda r, jb: (jb, r, 0))],
        out_shape=[jax.ShapeDtypeStruct((nblocks, ATT_BLOCK, ATT_GW), BF16),
                   jax.ShapeDtypeStruct((nblocks, ATT_BLOCK, ATT_HEAD_DIM), F32)],
        compiler_params=_cparams(("parallel", "parallel"), 32),
        name=f"dilated_attention_{gi}",
    )(src, src, src, src, src, src, src, bias)
    return o.reshape(m, ATT_GW), lse.reshape(m, ATT_HEAD_DIM)


def _att_combine_kernel(o0, l0, o1, l1, o2, l2, perm_ref, out_ref):
    tr = out_ref.shape[0]
    outs, lses = [o0[...]], [l0[...]]
    for k, (o_ref, l_ref) in enumerate(((o1, l1), (o2, l2))):
        inv = perm_ref[k, 1]
        ob, lb = [], []
        for b in range(tr // ATT_BLOCK):
            rows = slice(b * ATT_BLOCK, (b + 1) * ATT_BLOCK)
            ob.append(jnp.dot(inv, o_ref[rows, :], preferred_element_type=F32))
            lb.append(_sel_dot(inv, l_ref[rows, :]))
        outs.append(jnp.concatenate(ob, axis=0))
        lses.append(jnp.concatenate(lb, axis=0))
    a, b, c = lses
    mx = jnp.maximum(jnp.maximum(a, b), c)
    ea, eb, ec = jnp.exp(a - mx), jnp.exp(b - mx), jnp.exp(c - mx)
    inv = 1.0 / (ea + eb + ec)
    wa, wb, wc = ea * inv, eb * inv, ec * inv
    for h in range(ATT_HPG):
        sl = slice(h * ATT_HEAD_DIM, (h + 1) * ATT_HEAD_DIM)
        bc = lambda w: jnp.broadcast_to(w[:, h:h + 1], (tr, ATT_HEAD_DIM))
        out_ref[:, sl] = (bc(wa) * outs[0][:, sl].astype(F32) + bc(wb) * outs[1][:, sl].astype(F32)
                          + bc(wc) * outs[2][:, sl].astype(F32)).astype(out_ref.dtype)


def _att_combine(outs, lses, tr=512):
    assert [gi for gi, _ in _att_dilated()] == [1, 2]
    m, w = outs[0].shape
    ospec = pl.BlockSpec((tr, w), lambda i: (i, 0))
    lspec = pl.BlockSpec((tr, ATT_HEAD_DIM), lambda i: (i, 0))
    args = [x for pair in zip(outs, lses) for x in pair]
    perms = _att_row_perms()
    return pl.pallas_call(
        _att_combine_kernel,
        grid=(m // tr,),
        in_specs=[ospec, lspec] * 3 + [pl.BlockSpec(perms.shape, lambda i: (0, 0, 0, 0))],
        out_specs=ospec,
        out_shape=jax.ShapeDtypeStruct((m, w), BF16),
        compiler_params=_cparams(("parallel",), 32),
        name="attention_combine",
    )(*args, perms)


PACK_TILE = 512


def _pack_cols_kernel(branches, a_ref, b_ref, o_ref):
    j = pl.program_id(2)
    for lo, hi, off, valid in branches:
        @pl.when(jnp.logical_and(j >= lo, j < hi))
        def _(off=off, valid=valid):
            a = a_ref[0]
            val = a if off == 0 else jnp.concatenate([a, b_ref[0]], axis=1)[:, off:off + PACK_TILE]
            if valid < PACK_TILE:
                col = lax.broadcasted_iota(jnp.int32, val.shape, 1)
                val = jnp.where(col < valid, val, 0.0)
            o_ref[0] = val.astype(o_ref.dtype)


def _pack_cols(w, n_tiles, src_block, branches, tk=512):
    depth, k, n = w.shape
    last = -(-n // PACK_TILE) - 1
    return pl.pallas_call(
        functools.partial(_pack_cols_kernel, branches),
        grid=(depth, k // tk, n_tiles),
        in_specs=[pl.BlockSpec((1, tk, PACK_TILE), lambda l, i, j: (l, i, src_block(j))),
                  pl.BlockSpec((1, tk, PACK_TILE), lambda l, i, j: (l, i, jnp.minimum(src_block(j) + 1, last)))],
        out_specs=pl.BlockSpec((1, tk, PACK_TILE), lambda l, i, j: (l, i, j)),
        out_shape=jax.ShapeDtypeStruct((depth, k, n_tiles * PACK_TILE), BF16),
        compiler_params=_cparams(("parallel", "parallel", "parallel"), 32),
        name="pack_weight_columns",
    )(w, w)


def _pack_w_in(w_in):
    ng = len(ATT_PATTERNS)
    raw_dt = SSM_INNER + SSM_XBC
    shift = 2 * SSM_HEADS
    t_u = COL_U // PACK_TILE
    t_qkv = COL_QKV // PACK_TILE
    t_gates = COL_GATES // PACK_TILE
    assert raw_dt % PACK_TILE == 0 and COL_U == raw_dt

    def src_block(j):
        jj = jnp.clip(j - t_qkv, 0, 3 * ng - 1)
        return jnp.where(jnp.logical_and(j >= t_qkv, j < t_gates), t_qkv + (jj % 3) * ng + jj // 3, j)

    branches = ((0, t_u, 0, PACK_TILE), (t_u, IN_COLS // PACK_TILE, shift, PACK_TILE))
    main = _pack_cols(w_in, IN_COLS // PACK_TILE, src_block, branches)
    dt = w_in[:, :, raw_dt:raw_dt + shift]
    zpad = jnp.zeros(dt.shape[:2] + (128 - SSM_HEADS,), dt.dtype)
    wdt = jnp.concatenate([dt[..., :SSM_HEADS], zpad, dt[..., SSM_HEADS:], zpad], axis=-1).astype(BF16)
    return main, wdt


def _pack_ffn(w_up, conv_w, conv_b, w_down):
    nb = D_FF_PAD // PACK_TILE
    full, rem = divmod(D_FF, PACK_TILE)
    assert nb == full + 1 and rem > 0
    src_block = lambda j: jnp.where(j < nb, j, full + j - nb)
    branches = ((0, full, 0, PACK_TILE), (full, nb, 0, rem),
                (nb, nb + full, rem, PACK_TILE), (nb + full, 2 * nb, rem, rem))
    w_up_p = _pack_cols(w_up, 2 * nb, src_block, branches)
    padc = lambda a: jnp.pad(a, [(0, 0)] * (a.ndim - 1) + [(0, D_FF_PAD - D_FF)])
    both = lambda a: jnp.concatenate([padc(a[..., :D_FF]), padc(a[..., D_FF:])], axis=-1)
    w_down_p = jnp.pad(w_down, ((0, 0), (0, D_FF_PAD - D_FF), (0, 0))).astype(BF16)
    return w_up_p, both(conv_w).astype(F32), both(conv_b).astype(F32), w_down_p


def _trunk(x, seq_bounds, rel_bias, norm_mix, w_in, ssm_conv_w, ssm_conv_b, ssm_a_log, ssm_dt_bias, ssm_d,
           ssm_norm, ssm_w_out, s5_tables, s5_d, s5_w_glu, att_w_out, w_o, norm_ffn, w_up, ffn_conv_w,
           ffn_conv_b, w_down, final_norm):
    m = x.shape[0]
    depth = w_in.shape[0]
    tm = min(1024, m)
    tm2 = min(512, m)
    gate_blk = lambda b, tn: (COL_GATES + b * D_MODEL) // tn
    chunk_bounds = tuple((s // S5_CHUNK, l // S5_CHUNK) for s, l in seq_bounds)
    att_bias = [_att_bias(rel_bias, gi, _att_tile(seq_bounds, dil)) for gi, (_, dil) in enumerate(ATT_PATTERNS)]
    gated = lambda acc, gate: _sigmoid(gate.astype(F32)) * acc
    gated_add = lambda acc, gate, prev: prev + _sigmoid(gate.astype(F32)) * acc
    resid = lambda acc, res: res + acc

    w_main, w_dt = _pack_w_in(w_in)
    w_up_p, cw_p, cb_p, w_down_p = _pack_ffn(w_up, ffn_conv_w, ffn_conv_b, w_down)
    ssm_w_out, s5_w_glu, att_w_out, w_o = (w.astype(BF16) for w in (ssm_w_out, s5_w_glu, att_w_out, w_o))

    for li in range(depth):
        proj, dtraw = _in_proj(x, norm_mix[li].astype(F32), w_main, w_dt, li, ssm_conv_w[li].astype(F32),
                               ssm_conv_b[li].astype(F32), seq_bounds, tm)

        y_a = _ssd(proj, dtraw, ssm_dt_bias[li], ssm_a_log[li], ssm_d[li], ssm_norm[li].astype(F32), seq_bounds)
        merged = _matmul(y_a, ssm_w_out, layer=li, tm=tm, tn=512, out_dtype=F32, epilogue=gated,
                         extras=((proj, gate_blk(0, 512)),), name="ssm_out_proj")

        d_tiled = jnp.tile(s5_d[li].astype(F32).reshape(S5_GROUPS, 1, S5_GROUP), (1, 1, S5_CHUNK))
        y_b = _s5(proj, s5_tables, li, d_tiled, chunk_bounds)
        merged = _glu_matmul(y_b, s5_w_glu, li, proj, merged, tm=tm, tn=512, gate_blk=gate_blk(1, 512))

        outs, lses = zip(*[_attention_group(proj, att_bias[gi], gi, seq_bounds)
                           for gi in range(len(ATT_PATTERNS))])
        comb = _att_combine(outs, lses, tr=tm2)
        merged = _matmul(comb, att_w_out, layer=li, tm=tm, tn=512, out_dtype=BF16, epilogue=gated_add,
                         extras=((proj, gate_blk(2, 512)), (merged, 0)), name="att_out_proj")

        x = _matmul(merged, w_o, layer=li, tm=tm, tn=512, out_dtype=F32, epilogue=resid,
                    extras=((x, 0),), name="mix_out_proj")

        act = _ffn_up(x, norm_ffn[li].astype(F32), w_up_p, li, cw_p[li], cb_p[li], seq_bounds, tm)
        x = _matmul(act, w_down_p, layer=li, tm=tm, tn=512, out_dtype=F32, epilogue=resid, extras=((x, 0),),
                    vmem_mb=56, name="ffn_down")
    return _rmsnorm(x, final_norm, F32)


def kernel(x_prompt, x_sample, rel_bias, norm_mix, w_in, ssm_conv_w, ssm_conv_b, ssm_a_log, ssm_dt_bias, ssm_d, ssm_norm, ssm_w_out, s5_a_re, s5_a_im, s5_log_step, s5_b_re, s5_b_im, s5_c_re, s5_c_im, s5_d, s5_w_glu, att_w_out, w_o, norm_ffn, w_up, ffn_conv_w, ffn_conv_b, w_down, final_norm):
    d = x_prompt.shape[-1]
    seq_bounds = []
    row = 0
    for arr in (x_prompt, x_sample):
        for _ in range(arr.shape[0]):
            seq_bounds.append((row, arr.shape[1]))
            row += arr.shape[1]
    seq_bounds = tuple(seq_bounds)
    x = jnp.concatenate([x_prompt.reshape(-1, d), x_sample.reshape(-1, d)], axis=0)
    s5_tables = _s5_prep(s5_a_re, s5_a_im, s5_log_step, s5_b_re, s5_b_im, s5_c_re, s5_c_im)
    y = _trunk(x, seq_bounds, rel_bias, norm_mix, w_in, ssm_conv_w, ssm_conv_b, ssm_a_log, ssm_dt_bias, ssm_d,
               ssm_norm, ssm_w_out, s5_tables, s5_d, s5_w_glu, att_w_out, w_o, norm_ffn, w_up, ffn_conv_w,
               ffn_conv_b, w_down, final_norm)
    n_prompt = x_prompt.shape[0] * x_prompt.shape[1]
    return (y[:n_prompt].reshape(x_prompt.shape), y[n_prompt:].reshape(x_sample.shape))
```

```python
import functools
import math

import numpy as np
import jax
import jax.numpy as jnp
from jax import lax
from jax.experimental import pallas as pl
from jax.experimental.pallas import tpu as pltpu

F32 = jnp.float32
BF16 = jnp.bfloat16

D_MODEL = 2048
NORM_EPS = 1e-6
NEG_INF = -1e30

SSM_HEADDIM = 64
SSM_INNER = 1536
SSM_HEADS = 24
SSM_GROUPS = 4
SSM_HPG = SSM_HEADS // SSM_GROUPS
SSM_STATE = 128
SSM_CONV = 5
SSM_CHUNK = 128
SSM_XBC = 2560

S5_WIDTH = 1024
S5_GROUP = 16
S5_GROUPS = 64
S5_STATE = 64
S5_CHUNK = 32
S5_CW = S5_CHUNK * S5_GROUP
S5_LAGS = 128
S5_SCAN_STEPS = 8

ATT_HEAD_DIM = 128
ATT_HPG = 4
ATT_PATTERNS = ((128, 1), (512, 4), (2048, 16))
ATT_GW = ATT_HPG * ATT_HEAD_DIM
ATT_HALF = 64
REL_BUCKETS = 32
REL_MAX_DIST = 1024

D_FF = 5504
D_FF_PAD = 5632
FFN_CONV = 3

COL_Z = 0
COL_XBC = 1536
COL_U = 4096
COL_QKV = 5120
COL_GATES = 9728
IN_COLS = 15872
DT_COLS = 256

V7X_VMEM_BYTES = 64 * 1024 * 1024


def _cparams(sem, vmem_mb):
    return pltpu.CompilerParams(dimension_semantics=sem, vmem_limit_bytes=vmem_mb * 1024 * 1024)


def _sigmoid(x):
    return 1.0 / (1.0 + jnp.exp(-x))


def _silu(x):
    return x * _sigmoid(x)


def _split3(x):
    hi = x.astype(BF16)
    r1 = x - hi.astype(F32)
    mid = r1.astype(BF16)
    lo = (r1 - mid.astype(F32)).astype(BF16)
    return hi, mid, lo


def _dot_sel(x, sel):
    hi, mid, lo = _split3(x)
    d = lambda a: jnp.dot(a, sel, preferred_element_type=F32)
    return d(hi) + d(mid) + d(lo)


def _sel_dot(sel, x):
    hi, mid, lo = _split3(x)
    d = lambda a: jnp.dot(sel, a, preferred_element_type=F32)
    return d(hi) + d(mid) + d(lo)


def _dot_f32(a, b):
    ah, am, _ = _split3(a)
    bh, bm, _ = _split3(b)
    d = lambda x, y: jnp.dot(x, y, preferred_element_type=F32)
    return d(ah, bh) + d(ah, bm) + d(am, bh)


def _rmsnorm_kernel(x_ref, g_ref, o_ref):
    x = x_ref[...]
    ms = jnp.mean(x * x, axis=-1, keepdims=True)
    o_ref[...] = (x * lax.rsqrt(ms + NORM_EPS) * g_ref[...]).astype(o_ref.dtype)


def _final_norm_kernel(nb_first, x_ref, g_ref, first_ref, second_ref):
    x = x_ref[...]
    ms = jnp.mean(x * x, axis=-1, keepdims=True)
    y = x * lax.rsqrt(ms + NORM_EPS) * g_ref[...]
    i = pl.program_id(0)

    @pl.when(i < nb_first)
    def _():
        first_ref[...] = y

    @pl.when(i >= nb_first)
    def _():
        second_ref[...] = y


def _final_norm(x, gain, n_first, tr=512):
    m, d = x.shape
    assert n_first % tr == 0 and (m - n_first) % tr == 0 and 0 < n_first < m
    nb_first = n_first // tr
    return pl.pallas_call(
        functools.partial(_final_norm_kernel, nb_first),
        grid=(m // tr,),
        in_specs=[pl.BlockSpec((tr, d), lambda i: (i, 0)),
                  pl.BlockSpec((1, d), lambda i: (0, 0))],
        out_specs=[pl.BlockSpec((tr, d), lambda i: (jnp.minimum(i, nb_first - 1), 0)),
                   pl.BlockSpec((tr, d), lambda i: (jnp.maximum(i - nb_first, 0), 0))],
        out_shape=[jax.ShapeDtypeStruct((n_first, d), x.dtype), jax.ShapeDtypeStruct((m - n_first, d), x.dtype)],
        compiler_params=_cparams(("arbitrary",), 32),
        name="final_norm",
    )(x, gain.reshape(1, d))


def _mm_kernel(epilogue, n_extra, a_ref, b_ref, *rest):
    extras = rest[:n_extra]
    o_ref = rest[n_extra]
    acc = jnp.dot(a_ref[...], b_ref[...], preferred_element_type=F32)
    if epilogue is not None:
        acc = epilogue(acc, *[e[...] for e in extras])
    o_ref[...] = acc.astype(o_ref.dtype)


def _layer_spec(w, layer, rows, tn, col_off=0):
    if w.ndim == 2:
        return pl.BlockSpec((rows, tn), lambda i, j: (0, j + col_off))
    return pl.BlockSpec((None, rows, tn), lambda i, j: (layer, 0, j + col_off))


def _matmul(a, b, *, tm, tn, out_dtype, k=None, a_kblk=0, epilogue=None, extras=(), vmem_mb=48,
            layer=None, name="matmul"):
    m = a.shape[0]
    kk, n = b.shape[-2:]
    if k is None:
        k = a.shape[1]
    assert k == kk and m % tm == 0 and n % tn == 0
    in_specs = [pl.BlockSpec((tm, k), lambda i, j: (i, a_kblk)), _layer_spec(b, layer, k, tn)]
    args = [a, b]
    for arr, off in extras:
        in_specs.append(pl.BlockSpec((tm, tn), functools.partial(lambda i, j, o: (i, j + o), o=off)))
        args.append(arr)
    return pl.pallas_call(
        functools.partial(_mm_kernel, epilogue, len(extras)),
        grid=(m // tm, n // tn),
        in_specs=in_specs,
        out_specs=pl.BlockSpec((tm, tn), lambda i, j: (i, j)),
        out_shape=jax.ShapeDtypeStruct((m, n), out_dtype),
        compiler_params=_cparams(("parallel", "parallel"), vmem_mb),
        name=name,
    )(*args)


def _glu_kernel(a_ref, perm_ref, bv_ref, bg_ref, gate_ref, acc_ref, o_ref, a_scr):
    @pl.when(pl.program_id(1) == 0)
    def _():
        for b in range(a_ref.shape[0] // S5_PERM_ROWS):
            rows = slice(b * S5_PERM_ROWS, (b + 1) * S5_PERM_ROWS)
            a_scr[rows, :] = jnp.dot(perm_ref[1], a_ref[rows, :], preferred_element_type=F32).astype(BF16)

    a = a_scr[...]
    val = jnp.dot(a, bv_ref[...], preferred_element_type=F32)
    gate = jnp.dot(a, bg_ref[...], preferred_element_type=F32)
    branch = val * _sigmoid(gate)
    o_ref[...] = acc_ref[...] + _sigmoid(gate_ref[...].astype(F32)) * branch


def _glu_matmul(a, w, layer, proj, merged, *, tm, tn, gate_blk):
    m, k = a.shape
    n = w.shape[-1] // 2
    nb = n // tn
    perm, _ = _s5_movers()
    assert tm % S5_PERM_ROWS == 0
    return pl.pallas_call(
        _glu_kernel,
        grid=(m // tm, nb),
        in_specs=[pl.BlockSpec((tm, k), lambda i, j: (i, 0)),
                  pl.BlockSpec(perm.shape, lambda i, j: (0, 0, 0)),
                  _layer_spec(w, layer, k, tn), _layer_spec(w, layer, k, tn, nb),
                  pl.BlockSpec((tm, tn), lambda i, j: (i, j + gate_blk)),
                  pl.BlockSpec((tm, tn), lambda i, j: (i, j))],
        out_specs=pl.BlockSpec((tm, tn), lambda i, j: (i, j)),
        out_shape=jax.ShapeDtypeStruct((m, n), F32),
        scratch_shapes=[pltpu.VMEM((tm, k), BF16)],
        compiler_params=_cparams(("parallel", "arbitrary"), 48),
        name="s5_glu_matmul",
    )(a, perm, w, w, proj, merged)


HALO = 16


def _seq_flags(row0, nrows, seq_bounds):
    starts = [s for s, _ in seq_bounds]
    ends = [s + l for s, l in seq_bounds]
    is_start = functools.reduce(jnp.logical_or, [row0 == s for s in starts])
    is_end = functools.reduce(jnp.logical_or, [row0 + nrows == e for e in ends])
    return is_start, is_end


def _conv_taps(buf_ref, w_ref, b_ref, width, tr, cols=slice(None)):
    pad = width // 2
    acc = None
    for kk in range(width):
        term = buf_ref[pl.ds(HALO - pad + kk, tr), cols] * w_ref[kk:kk + 1, cols]
        acc = term if acc is None else acc + term
    return acc + b_ref[:, cols]


def _norm_rows(x, gain):
    ms = jnp.mean(x * x, axis=-1, keepdims=True)
    return (x * lax.rsqrt(ms + NORM_EPS) * gain).astype(BF16)


def _fill_normed(hn_ref, xp_ref, xm_ref, xn_ref, gain_ref, tm, seq_bounds):
    is_start, is_end = _seq_flags(pl.program_id(0) * tm, tm, seq_bounds)
    gain = gain_ref[...]
    hn_ref[0:HALO, :] = jnp.where(is_start, 0.0, _norm_rows(xp_ref[...], gain)).astype(BF16)
    hn_ref[HALO:HALO + tm, :] = _norm_rows(xm_ref[...], gain)
    hn_ref[HALO + tm:HALO + tm + HALO, :] = jnp.where(is_end, 0.0, _norm_rows(xn_ref[...], gain)).astype(BF16)


def _row_halo_specs(tm, d, nrows):
    hb = tm // HALO
    last = nrows // HALO - 1
    return [pl.BlockSpec((HALO, d), lambda i, j: (jnp.maximum(i * hb - 1, 0), 0)),
            pl.BlockSpec((tm, d), lambda i, j: (i, 0)),
            pl.BlockSpec((HALO, d), lambda i, j: (jnp.minimum((i + 1) * hb, last), 0))]


ATT_BLOCK = 256


def _att_dilated():
    return tuple((gi, dil) for gi, (_, dil) in enumerate(ATT_PATTERNS) if dil > 1)


def _att_row_perms():
    out = []
    for _, dil in _att_dilated():
        dst = np.arange(ATT_BLOCK)[:, None]
        per = ATT_BLOCK // dil
        fwd = (dst % per) * dil + dst // per == np.arange(ATT_BLOCK)[None, :]
        out.append(np.stack([fwd, fwd.T]))
    return jnp.asarray(np.stack(out), BF16)


def _in_proj_kernel(tm, tn, seq_bounds, conv_lo, conv_hi, dil_ranges, u_range, xp_ref, xm_ref, xn_ref, gain_ref,
                    w_ref, wdt_ref, cw_ref, cb_ref, perm_ref, uperm_ref, o_ref, dt_ref, hn_ref, acc_ref):
    j = pl.program_id(1)

    @pl.when(j == 0)
    def _():
        _fill_normed(hn_ref, xp_ref, xm_ref, xn_ref, gain_ref, tm, seq_bounds)
        dt_ref[...] = jnp.dot(hn_ref[HALO:HALO + tm, :], wdt_ref[...], preferred_element_type=F32)

    is_conv = jnp.logical_and(j >= conv_lo, j < conv_hi)

    @pl.when(is_conv)
    def _():
        half = tn // 2
        for c in range(2):
            cols = slice(c * half, (c + 1) * half)
            acc_ref[:, cols] = jnp.dot(hn_ref[...], w_ref[:, cols], preferred_element_type=F32)
            o_ref[:, cols] = _silu(_conv_taps(acc_ref, cw_ref, cb_ref, SSM_CONV, tm, cols)).astype(o_ref.dtype)

    regroup = [((lo, hi), ATT_BLOCK, functools.partial(lambda k: perm_ref[k, 0], k))
               for k, (lo, hi) in enumerate(dil_ranges)]
    regroup.append((u_range, S5_PERM_ROWS, lambda: uperm_ref[0]))
    conds = [jnp.logical_and(j >= lo, j < hi) for (lo, hi), _, _ in regroup]
    plain = jnp.logical_not(functools.reduce(jnp.logical_or, conds, is_conv))

    @pl.when(plain)
    def _():
        o_ref[...] = jnp.dot(hn_ref[HALO:HALO + tm, :], w_ref[...],
                             preferred_element_type=F32).astype(o_ref.dtype)

    for cond, (_, nrows, get_perm) in zip(conds, regroup):
        @pl.when(cond)
        def _(nrows=nrows, get_perm=get_perm):
            acc = jnp.dot(hn_ref[HALO:HALO + tm, :], w_ref[...], preferred_element_type=F32).astype(BF16)
            for b in range(tm // nrows):
                rows = slice(b * nrows, (b + 1) * nrows)
                o_ref[rows, :] = jnp.dot(get_perm(), acc[rows, :],
                                         preferred_element_type=F32).astype(o_ref.dtype)


def _in_proj(x, gain, w_main, w_dt, layer, conv_w, conv_b, seq_bounds, tm, tn=512):
    m, d = x.shape
    n = w_main.shape[-1]
    conv_lo, conv_hi = COL_XBC // tn, (COL_XBC + SSM_XBC) // tn
    cidx = lambda i, j: (0, jnp.clip(j - conv_lo, 0, conv_hi - conv_lo - 1))
    dil_ranges = tuple(((COL_QKV + gi * 3 * ATT_GW) // tn, (COL_QKV + (gi + 1) * 3 * ATT_GW) // tn)
                       for gi, _ in _att_dilated())
    u_range = (COL_U // tn, (COL_U + S5_WIDTH) // tn)
    perms = _att_row_perms()
    uperm, _ = _s5_movers()
    assert tm % S5_PERM_ROWS == 0 and tm % ATT_BLOCK == 0
    return pl.pallas_call(
        functools.partial(_in_proj_kernel, tm, tn, seq_bounds, conv_lo, conv_hi, dil_ranges, u_range),
        grid=(m // tm, n // tn),
        in_specs=_row_halo_specs(tm, d, m) + [
            pl.BlockSpec((1, d), lambda i, j: (0, 0)),
            _layer_spec(w_main, layer, d, tn),
            pl.BlockSpec((None, d, DT_COLS), lambda i, j: (layer, 0, 0)),
            pl.BlockSpec((SSM_CONV, tn), cidx),
            pl.BlockSpec((1, tn), cidx),
            pl.BlockSpec(perms.shape, lambda i, j: (0, 0, 0, 0)),
            pl.BlockSpec(uperm.shape, lambda i, j: (0, 0, 0))],
        out_specs=[pl.BlockSpec((tm, tn), lambda i, j: (i, j)),
                   pl.BlockSpec((tm, DT_COLS), lambda i, j: (i, 0))],
        out_shape=[jax.ShapeDtypeStruct((m, n), BF16), jax.ShapeDtypeStruct((m, DT_COLS), F32)],
        scratch_shapes=[pltpu.VMEM((tm + 2 * HALO, d), BF16), pltpu.VMEM((tm + 2 * HALO, tn), F32)],
        compiler_params=_cparams(("parallel", "arbitrary"), 56),
        name="in_proj",
    )(x, x, x, gain.reshape(1, d), w_main, w_dt, conv_w, conv_b.reshape(1, -1), perms, uperm)


def _ffn_up_kernel(tm, seq_bounds, xp_ref, xm_ref, xn_ref, gain_ref, wg_ref, wv_ref, cwg_ref, cbg_ref,
                   cwv_ref, cbv_ref, o_ref, hn_ref, sg_ref, sv_ref):
    @pl.when(pl.program_id(1) == 0)
    def _():
        _fill_normed(hn_ref, xp_ref, xm_ref, xn_ref, gain_ref, tm, seq_bounds)

    hn = hn_ref[...]
    sg_ref[...] = jnp.dot(hn, wg_ref[...], preferred_element_type=F32)
    sv_ref[...] = jnp.dot(hn, wv_ref[...], preferred_element_type=F32)
    gate = _conv_taps(sg_ref, cwg_ref, cbg_ref, FFN_CONV, tm)
    val = _conv_taps(sv_ref, cwv_ref, cbv_ref, FFN_CONV, tm)
    o_ref[...] = (_silu(gate) * val).astype(o_ref.dtype)


def _ffn_up(x, gain, w, layer, conv_w, conv_b, seq_bounds, tm, tn=512):
    m, d = x.shape
    nb = D_FF_PAD // tn
    col = lambda rows, off: pl.BlockSpec((rows, tn), functools.partial(lambda i, j, o: (0, j + o), o=off))
    cb2 = conv_b.reshape(1, -1)
    return pl.pallas_call(
        functools.partial(_ffn_up_kernel, tm, seq_bounds),
        grid=(m // tm, nb),
        in_specs=_row_halo_specs(tm, d, m) + [
            pl.BlockSpec((1, d), lambda i, j: (0, 0)),
            _layer_spec(w, layer, d, tn), _layer_spec(w, layer, d, tn, nb),
            col(FFN_CONV, 0), col(1, 0), col(FFN_CONV, nb), col(1, nb)],
        out_specs=pl.BlockSpec((tm, tn), lambda i, j: (i, j)),
        out_shape=jax.ShapeDtypeStruct((m, D_FF_PAD), BF16),
        scratch_shapes=[pltpu.VMEM((tm + 2 * HALO, d), BF16), pltpu.VMEM((tm + 2 * HALO, tn), F32),
                        pltpu.VMEM((tm + 2 * HALO, tn), F32)],
        compiler_params=_cparams(("parallel", "arbitrary"), 56),
        name="ffn_up",
    )(x, x, x, gain.reshape(1, d), w, w, conv_w, cb2, conv_w, cb2)


def _softplus(x):
    return jnp.maximum(x, 0.0) + jnp.log1p(jnp.exp(-jnp.abs(x)))


def _ssd_cumsums(dt_ref, bias_ref, alog_ref, d, row0=0):
    t = SSM_CHUNK
    dt = _softplus(dt_ref[row0:row0 + t, d * 128:(d + 1) * 128] + bias_ref[d])
    da = dt * (-jnp.exp(alog_ref[d]))
    row = lax.broadcasted_iota(jnp.int32, (t, t), 0)
    col = lax.broadcasted_iota(jnp.int32, (t, t), 1)
    tri = jnp.where((col <= row) if d == 0 else (col >= row), 1.0, 0.0).astype(BF16)
    return dt, _sel_dot(tri, da)


SSD_SCAN_CHUNKS = 2


def _ssd_state_kernel(nsteps, seq_bounds, xf_ref, bf_ref, dtf_ref, xb_ref, bb_ref, dtb_ref, bias_ref, alog_ref,
                      hf_ref, hb_ref, h_ref):
    t = SSM_CHUNK
    n = SSM_STATE
    p = SSM_HEADDIM
    kc = SSD_SCAN_CHUNKS
    i = pl.program_id(0)

    @pl.when(i == 0)
    def _():
        h_ref[...] = jnp.zeros(h_ref.shape, F32)

    for d, (x_ref, b_ref, dt_ref, out_ref) in enumerate(((xf_ref, bf_ref, dtf_ref, hf_ref),
                                                         (xb_ref, bb_ref, dtb_ref, hb_ref))):
        blk = i if d == 0 else nsteps - 1 - i
        contrib = []
        for c in range(kc):
            r0 = c * t
            dt, cs = _ssd_cumsums(dt_ref, bias_ref, alog_ref, d, r0)
            tot = cs[t - 1:t, :] if d == 0 else cs[0:1, :]
            w = jnp.exp(tot - cs) * dt
            etot = jnp.exp(tot)
            per_group = []
            for g in range(SSM_GROUPS):
                bg_t = b_ref[r0:r0 + t, g * n:(g + 1) * n].astype(F32).T.astype(BF16)
                xw, dec = [], []
                for hh in range(SSM_HPG):
                    h = g * SSM_HPG + hh
                    xh = x_ref[r0:r0 + t, h * p:(h + 1) * p].astype(F32)
                    xw.append((xh * jnp.broadcast_to(w[:, h:h + 1], (t, p))).astype(BF16))
                    dec.append(jnp.broadcast_to(etot[:, h:h + 1], (n, p)))
                st = jnp.dot(bg_t, jnp.concatenate(xw, axis=1), preferred_element_type=F32)
                per_group.append((st, jnp.concatenate(dec, axis=1)))
            contrib.append(per_group)
        for c in (range(kc) if d == 0 else reversed(range(kc))):
            is_start, is_end = _seq_flags((blk * kc + c) * t, t, seq_bounds)
            reset = is_start if d == 0 else is_end
            for g in range(SSM_GROUPS):
                st, dec = contrib[c][g]
                hin = jnp.where(reset, 0.0, h_ref[d, g])
                out_ref[c, g] = hin.astype(BF16)
                h_ref[d, g] = hin * dec + st


def _ssd_states(proj, dtraw, dt_bias, a_log, seq_bounds):
    m = proj.shape[0]
    t = SSM_CHUNK
    kc = SSD_SCAN_CHUNKS
    nchunks = m // t
    nsteps = nchunks // kc
    hw = SSM_HPG * SSM_HEADDIM
    bw = SSM_GROUPS * SSM_STATE
    fwd = lambda i: i
    bwd = lambda i: nsteps - 1 - i
    xspec = lambda f: pl.BlockSpec((kc * t, SSM_INNER), lambda i: (f(i), COL_XBC // SSM_INNER))
    bspec = lambda f: pl.BlockSpec((kc * t, bw), lambda i: (f(i), (COL_XBC + SSM_INNER) // bw))
    dspec = lambda f: pl.BlockSpec((kc * t, DT_COLS), lambda i: (f(i), 0))
    hspec = lambda f: pl.BlockSpec((kc, SSM_GROUPS, SSM_STATE, hw), lambda i: (f(i), 0, 0, 0))
    const = pl.BlockSpec((2, 1, 128), lambda i: (0, 0, 0))
    return pl.pallas_call(
        functools.partial(_ssd_state_kernel, nsteps, seq_bounds),
        grid=(nsteps,),
        in_specs=[xspec(fwd), bspec(fwd), dspec(fwd), xspec(bwd), bspec(bwd), dspec(bwd), const, const],
        out_specs=[hspec(fwd), hspec(bwd)],
        out_shape=[jax.ShapeDtypeStruct((nchunks, SSM_GROUPS, SSM_STATE, hw), BF16)] * 2,
        scratch_shapes=[pltpu.VMEM((2, SSM_GROUPS, SSM_STATE, hw), F32)],
        compiler_params=_cparams(("arbitrary",), 32),
        name="ssd_states",
    )(proj, proj, dtraw, proj, proj, dtraw, dt_bias, a_log)


def _ssd_out_kernel(x_ref, b_ref, c_ref, dt_ref, bias_ref, alog_ref, hf_ref, hb_ref, z_ref, dskip_ref,
                    gain_ref, o_ref, y_ref):
    t = SSM_CHUNK
    n = SSM_STATE
    p = SSM_HEADDIM
    dtf, csf = _ssd_cumsums(dt_ref, bias_ref, alog_ref, 0)
    dtb, csb = _ssd_cumsums(dt_ref, bias_ref, alog_ref, 1)
    csf_t, csb_t, dtf_t, dtb_t = csf.T, csb.T, dtf.T, dtb.T
    dts_t = dtf_t + dtb_t
    row = lax.broadcasted_iota(jnp.int32, (t, t), 0)
    col = lax.broadcasted_iota(jnp.int32, (t, t), 1)
    low = col <= row
    low_strict = col < row
    up_strict = col > row
    for g in range(SSM_GROUPS):
        bg = b_ref[:, g * n:(g + 1) * n]
        cg = c_ref[:, g * n:(g + 1) * n]
        cb = lax.dot_general(cg, bg, (((1,), (1,)), ((), ())), preferred_element_type=F32)
        cg_f = cg.astype(F32)
        for hh in range(SSM_HPG):
            h = g * SSM_HPG + hh
            colf = jnp.broadcast_to(csf[:, h:h + 1], (t, t))
            colb = jnp.broadcast_to(csb[:, h:h + 1], (t, t))
            seg = jnp.where(low, colf - jnp.broadcast_to(csf_t[h:h + 1, :], (t, t)),
                            colb - jnp.broadcast_to(csb_t[h:h + 1, :], (t, t)))
            coef = jnp.where(low_strict, jnp.broadcast_to(dtf_t[h:h + 1, :], (t, t)),
                             jnp.where(up_strict, jnp.broadcast_to(dtb_t[h:h + 1, :], (t, t)),
                                       jnp.broadcast_to(dts_t[h:h + 1, :], (t, t))))
            mh = jnp.exp(seg) * cb * coef
            lhs = jnp.concatenate([mh, cg_f * jnp.exp(colf), cg_f * jnp.exp(colb)], axis=1).astype(BF16)
            rhs = jnp.concatenate([x_ref[:, h * p:(h + 1) * p],
                                   hf_ref[0, g, :, hh * p:(hh + 1) * p],
                                   hb_ref[0, g, :, hh * p:(hh + 1) * p]], axis=0)
            y_ref[:, h * p:(h + 1) * p] = jnp.dot(lhs, rhs, preferred_element_type=F32)
    y = y_ref[...] + x_ref[...].astype(F32) * dskip_ref[...]
    y = y * _silu(z_ref[...].astype(F32))
    ms = jnp.mean(y * y, axis=-1, keepdims=True)
    o_ref[...] = (y * lax.rsqrt(ms + NORM_EPS) * gain_ref[...]).astype(o_ref.dtype)


def _ssd_out(proj, dtraw, dt_bias, a_log, h_f, h_b, d_skip, norm_g):
    m = proj.shape[0]
    t = SSM_CHUNK
    w = SSM_INNER
    hw = SSM_HPG * SSM_HEADDIM
    bw = SSM_GROUPS * SSM_STATE
    hspec = pl.BlockSpec((1, SSM_GROUPS, SSM_STATE, hw), lambda i: (i, 0, 0, 0))
    return pl.pallas_call(
        _ssd_out_kernel,
        grid=(m // t,),
        in_specs=[pl.BlockSpec((t, w), lambda i: (i, COL_XBC // w)),
                  pl.BlockSpec((t, bw), lambda i: (i, (COL_XBC + w) // bw)),
                  pl.BlockSpec((t, bw), lambda i: (i, (COL_XBC + w) // bw + 1)),
                  pl.BlockSpec((t, DT_COLS), lambda i: (i, 0)),
                  pl.BlockSpec((2, 1, 128), lambda i: (0, 0, 0)),
                  pl.BlockSpec((2, 1, 128), lambda i: (0, 0, 0)),
                  hspec, hspec,
                  pl.BlockSpec((t, w), lambda i: (i, COL_Z // w)),
                  pl.BlockSpec((1, w), lambda i: (0, 0)),
                  pl.BlockSpec((1, w), lambda i: (0, 0))],
        out_specs=pl.BlockSpec((t, w), lambda i: (i, 0)),
        out_shape=jax.ShapeDtypeStruct((m, w), BF16),
        scratch_shapes=[pltpu.VMEM((t, w), F32)],
        compiler_params=_cparams(("parallel",), 32),
        name="ssd_out",
    )(proj, proj, proj, dtraw, dt_bias, a_log, h_f, h_b, proj,
      jnp.repeat(d_skip.astype(F32), SSM_HEADDIM).reshape(1, w), norm_g.reshape(1, w))


def _ssd(proj, dtraw, dt_bias, a_log, d_skip, norm_g, seq_bounds):
    pad = lambda v: jnp.pad(v.astype(F32), ((0, 0), (0, 128 - SSM_HEADS))).reshape(2, 1, 128)
    dt_bias, a_log = pad(dt_bias), pad(a_log)
    h_f, h_b = _ssd_states(proj, dtraw, dt_bias, a_log, seq_bounds)
    return _ssd_out(proj, dtraw, dt_bias, a_log, h_f, h_b, d_skip, norm_g)


def _s5_selectors():
    t, c = S5_CHUNK, S5_GROUP
    lag = np.arange(S5_LAGS)[:, None]
    tok = (np.arange(S5_CW) // c)[None, :]
    e_exit_f = (lag == t - 1 - tok)
    e_exit_b = (lag == tok)
    e_in_f = (lag == tok + 1)
    e_in_b = (lag == t - tok)
    lagidx = (np.arange(2 * S5_CW) // c)[None, :]
    rel = lagidx - (t - 1)
    inb = lagidx <= 2 * t - 2
    e_k_f = (lag == rel) & (rel >= 0) & inb
    e_k_b = (lag == -rel) & (rel <= 0) & inb
    tile = (np.arange(c)[:, None] == (np.arange(2 * S5_CW) % c)[None, :])
    sel = np.stack([np.concatenate([a, b], axis=1) for a, b in
                    ((e_exit_f, e_in_f), (e_exit_b, e_in_b))])
    selk = np.stack([e_k_f, e_k_b])
    return (jnp.asarray(sel, BF16), jnp.asarray(selk, BF16), jnp.asarray(tile, BF16))


def _s5_prep_kernel(acol_ref, arow_ref, ls_ref, b_ref, bt_ref, ct_ref, sel_ref, selk_ref, tile_ref,
                    sign_ref, w_ref, ws_ref, wo_ref, lpa_ref, lpb_ref):
    t = S5_CHUNK
    cw = S5_CW
    p = S5_STATE
    tile = tile_ref[...]
    lagf = lax.broadcasted_iota(jnp.int32, (p, S5_LAGS), 1).astype(F32)
    kt = jnp.zeros((S5_GROUP, 2 * cw), F32)
    for d in range(2):
        step = jnp.exp(ls_ref[0, 0, d])
        are = acol_ref[0, 0, d, 0]
        aim = acol_ref[0, 0, d, 1]
        mag = are * step
        th = aim * step
        amp = jnp.exp(lagf * mag)
        pwr = amp * jnp.cos(lagf * th)
        pwi = amp * jnp.sin(lagf * th)
        lbr = jnp.exp(mag) * jnp.cos(th)
        lbi = jnp.exp(mag) * jnp.sin(th)
        den = are * are + aim * aim
        cfr = ((lbr - 1.0) * are + lbi * aim) / den
        cfi = (lbi * are - (lbr - 1.0) * aim) / den
        bre = b_ref[0, 0, 0]
        bim = b_ref[0, 0, 1]
        bbr = cfr * bre - cfi * bim
        bbi = cfr * bim + cfi * bre
        step_r = step
        are_r = arow_ref[0, 0, d, 0:1, 0:p]
        aim_r = arow_ref[0, 0, d, 1:2, 0:p]
        mag_r = are_r * step_r
        th_r = aim_r * step_r
        lbr_r = jnp.exp(mag_r) * jnp.cos(th_r)
        lbi_r = jnp.exp(mag_r) * jnp.sin(th_r)
        den_r = are_r * are_r + aim_r * aim_r
        cfr_r = ((lbr_r - 1.0) * are_r + lbi_r * aim_r) / den_r
        cfi_r = (lbi_r * are_r - (lbr_r - 1.0) * aim_r) / den_r
        btr = bt_ref[0, 0, 0]
        bti = bt_ref[0, 0, 1]
        bbr_t = cfr_r * btr - cfi_r * bti
        bbi_t = cfr_r * bti + cfi_r * btr
        ctr = ct_ref[0, 0, d, 0]
        cti = ct_ref[0, 0, d, 1]

        sel = sel_ref[d]
        er = _dot_sel(pwr, sel)
        ei = _dot_sel(pwi, sel)
        tb_r = _dot_sel(bbr, tile[:, 0:cw])
        tb_i = _dot_sel(bbi, tile[:, 0:cw])
        tc_r = _dot_sel(ctr, tile)
        tc_i = _dot_sel(cti, tile)
        ws_ref[0, 0, d * 2 * p:d * 2 * p + p, :] = (er[:, 0:cw] * tb_r - ei[:, 0:cw] * tb_i).astype(BF16)
        ws_ref[0, 0, d * 2 * p + p:(d + 1) * 2 * p, :] = (er[:, 0:cw] * tb_i + ei[:, 0:cw] * tb_r).astype(BF16)
        zr = er[:, cw:] * tc_r[:, 0:cw] - ei[:, cw:] * tc_i[:, 0:cw]
        zi = er[:, cw:] * tc_i[:, 0:cw] + ei[:, cw:] * tc_r[:, 0:cw]
        wo_ref[0, 0, d * 2 * p:d * 2 * p + p, :] = zr.astype(BF16)
        wo_ref[0, 0, d * 2 * p + p:(d + 1) * 2 * p, :] = (-zi).astype(BF16)
        selk = selk_ref[d]
        qr = _dot_sel(pwr, selk) * tc_r - _dot_sel(pwi, selk) * tc_i
        qi = _dot_sel(pwr, selk) * tc_i + _dot_sel(pwi, selk) * tc_r
        kt = kt + _dot_f32(bbr_t, qr) - _dot_f32(bbi_t, qi)
        are2 = arow_ref[0, 0, d, 0:1, :]
        aim2 = arow_ref[0, 0, d, 1:2, :]
        ampt = jnp.exp(are2 * step_r * float(t))
        zr2 = ampt * jnp.cos(aim2 * step_r * float(t))
        zi2 = ampt * jnp.sin(aim2 * step_r * float(t))
        sign = sign_ref[...]
        for k in range(S5_SCAN_STEPS):
            lpa_ref[0, 0, d, k:k + 1, :] = zr2
            lpb_ref[0, 0, d, k:k + 1, :] = sign * zi2
            zr2, zi2 = zr2 * zr2 - zi2 * zi2, 2.0 * zr2 * zi2
    for s in range(t):
        off = (t - 1 - s) * S5_GROUP
        w_ref[0, 0, s * S5_GROUP:(s + 1) * S5_GROUP, :] = kt[:, off:off + cw].astype(BF16)


def _s5_prep(a_re, a_im, log_step, b_re, b_im, c_re, c_im):
    depth = a_re.shape[0]
    g, p, c = S5_GROUPS, S5_STATE, S5_GROUP
    a = jnp.stack([a_re, a_im], axis=2).astype(F32)
    a = a.transpose(0, 3, 1, 2, 4)
    acol = a[..., None]
    arow = jnp.concatenate([a, a], axis=-1)
    ls = log_step.astype(F32).transpose(0, 2, 1).reshape(depth, g, 2, 1, 1)
    b = jnp.stack([b_re, b_im], axis=2).astype(F32)
    bt = b.transpose(0, 1, 2, 4, 3)
    ct = jnp.stack([c_re, c_im], axis=3).astype(F32)
    ct = ct.transpose(0, 2, 1, 3, 5, 4)
    sel, selk, tile = _s5_selectors()
    sign = jnp.concatenate([-jnp.ones((1, p), F32), jnp.ones((1, p), F32)], axis=1)
    full = lambda shp: pl.BlockSpec(shp, lambda l, j: (0,) * len(shp))
    per = lambda shp: pl.BlockSpec((1, 1) + shp, lambda l, j: (l, j) + (0,) * len(shp))
    cw = S5_CW
    return pl.pallas_call(
        _s5_prep_kernel,
        grid=(depth, g),
        in_specs=[per((2, 2, p, 1)), per((2, 2, 2 * p)), per((2, 1, 1)), per((2, p, c)), per((2, c, p)),
                  per((2, 2, p, c)), full((2, S5_LAGS, 2 * cw)), full((2, S5_LAGS, 2 * cw)),
                  full((c, 2 * cw)), full((1, 2 * p))],
        out_specs=[per((cw, cw)), per((4 * p, cw)), per((4 * p, cw)),
                   per((2, S5_SCAN_STEPS, 2 * p)), per((2, S5_SCAN_STEPS, 2 * p))],
        out_shape=[jax.ShapeDtypeStruct((depth, g, cw, cw), BF16),
                   jax.ShapeDtypeStruct((depth, g, 4 * p, cw), BF16),
                   jax.ShapeDtypeStruct((depth, g, 4 * p, cw), BF16),
                   jax.ShapeDtypeStruct((depth, g, 2, S5_SCAN_STEPS, 2 * p), F32),
                   jax.ShapeDtypeStruct((depth, g, 2, S5_SCAN_STEPS, 2 * p), F32)],
        compiler_params=_cparams(("parallel", "parallel"), 32),
        name="s5_prep",
    )(acol, arow, ls, b, bt, ct, sel, selk, tile, sign)


def _gelu_tanh(x):
    return 0.5 * x * (1.0 + jnp.tanh(math.sqrt(2.0 / math.pi) * (x + 0.044715 * (x * x * x))))


LANES = 128


BF16_ROWS = 16
S5_PERM_ROWS = BF16_ROWS * S5_CHUNK
S5_GPB = LANES // S5_GROUP


def _s5_kernel(chunk_bounds, u_ref, pick_ref, w_ref, ws_ref, wo_ref, lpa_ref, lpb_ref,
               d_ref, o_ref, cat_ref, ug_ref, yg_ref):
    t = S5_CHUNK
    g = pl.program_id(1)
    r = u_ref.shape[0] // t
    nblk = r // BF16_ROWS
    p2 = 2 * S5_STATE
    qw = S5_GPB * LANES

    @pl.when(g == 0)
    def _():
        def body(b, carry):
            crow = pl.ds(pl.multiple_of(b * BF16_ROWS, BF16_ROWS), BF16_ROWS)
            for s in range(t):
                rows = pl.ds(pl.multiple_of(b * S5_PERM_ROWS + s * BF16_ROWS, BF16_ROWS), BF16_ROWS)
                cat_ref[crow, s * LANES:(s + 1) * LANES] = u_ref[rows, :]
            return carry
        lax.fori_loop(0, nblk, body, 0)
        for q in range(t // S5_GPB):
            picked = jnp.dot(cat_ref[:, q * qw:(q + 1) * qw], pick_ref[0],
                             preferred_element_type=F32).astype(BF16)
            for gg in range(S5_GPB):
                ug_ref[gg, :, q * LANES:(q + 1) * LANES] = picked[:, gg * LANES:(gg + 1) * LANES]

    u = ug_ref[g]
    uf = u.astype(F32)
    y = jnp.dot(u, w_ref[0, 0], preferred_element_type=F32)
    st = lax.dot_general(u, ws_ref[0, 0], (((1,), (1,)), ((), ())), preferred_element_type=F32)
    ridx = lax.broadcasted_iota(jnp.int32, (r, p2), 0)
    rloc = jnp.zeros((r, p2), jnp.int32)
    rlen = jnp.zeros((r, p2), jnp.int32)
    for s0, ln in chunk_bounds:
        inside = (ridx >= s0) & (ridx < s0 + ln)
        rloc = jnp.where(inside, ridx - s0, rloc)
        rlen = jnp.where(inside, ln, rlen)
    xin = []
    for d in range(2):
        x = st[:, d * p2:(d + 1) * p2]
        for k in range(S5_SCAN_STEPS):
            sh = 1 << k
            if d == 0:
                prev = jnp.where(rloc >= sh, pltpu.roll(x, sh, 0), 0.0)
            else:
                prev = jnp.where(rloc < rlen - sh, pltpu.roll(x, r - sh, 0), 0.0)
            x = (x + lpa_ref[0, 0, d, k:k + 1, :] * prev
                 + lpb_ref[0, 0, d, k:k + 1, :] * pltpu.roll(prev, S5_STATE, 1))
        if d == 0:
            xin.append(jnp.where(rloc >= 1, pltpu.roll(x, 1, 0), 0.0))
        else:
            xin.append(jnp.where(rloc < rlen - 1, pltpu.roll(x, r - 1, 0), 0.0))
    xin = jnp.concatenate(xin, axis=1).astype(BF16)
    y = y + jnp.dot(xin, wo_ref[0, 0], preferred_element_type=F32)
    y = _gelu_tanh(y + uf * d_ref[0])

    yg_ref[g] = y.astype(BF16)

    @pl.when(g == pl.num_programs(1) - 1)
    def _():
        for q in range(t // S5_GPB):
            lhs = jnp.concatenate([yg_ref[gg, :, q * LANES:(q + 1) * LANES] for gg in range(S5_GPB)], axis=1)
            cat_ref[:, q * qw:(q + 1) * qw] = jnp.dot(lhs, pick_ref[1],
                                                      preferred_element_type=F32).astype(BF16)

        def body(b, carry):
            crow = pl.ds(pl.multiple_of(b * BF16_ROWS, BF16_ROWS), BF16_ROWS)
            for s in range(t):
                rows = pl.ds(pl.multiple_of(b * S5_PERM_ROWS + s * BF16_ROWS, BF16_ROWS), BF16_ROWS)
                o_ref[rows, :] = cat_ref[crow, s * LANES:(s + 1) * LANES]
            return carry
        lax.fori_loop(0, nblk, body, 0)


def _s5_movers():
    dst = np.arange(S5_PERM_ROWS)[:, None]
    fwd = (dst % BF16_ROWS) * S5_CHUNK + dst // BF16_ROWS == np.arange(S5_PERM_ROWS)[None, :]
    src = np.arange(S5_GPB * LANES)
    s8, g, j = src // LANES, src % LANES // S5_GROUP, src % S5_GROUP
    pick = (g * LANES + s8 * S5_GROUP + j)[:, None] == src[None, :]
    return jnp.asarray(np.stack([fwd, fwd.T]), BF16), jnp.asarray(np.stack([pick, pick.T]), BF16)


def _s5(proj, tables, li, d_tiled, chunk_bounds):
    m = proj.shape[0]
    w, ws, wo, lpa, lpb = tables
    p = S5_STATE
    cw = S5_CW
    gpb = S5_GPB
    r = m // S5_CHUNK
    _, pick = _s5_movers()
    per = lambda shp: pl.BlockSpec((1, 1) + shp, lambda b, j: (li, b * gpb + j) + (0,) * len(shp))
    return pl.pallas_call(
        functools.partial(_s5_kernel, chunk_bounds),
        grid=(S5_GROUPS // gpb, gpb),
        in_specs=[pl.BlockSpec((m, LANES), lambda b, j: (0, COL_U // LANES + b)),
                  pl.BlockSpec((2, gpb * LANES, gpb * LANES), lambda b, j: (0, 0, 0)),
                  per((cw, cw)), per((4 * p, cw)), per((4 * p, cw)),
                  per((2, S5_SCAN_STEPS, 2 * p)), per((2, S5_SCAN_STEPS, 2 * p)),
                  pl.BlockSpec((1, 1, cw), lambda b, j: (b * gpb + j, 0, 0))],
        out_specs=pl.BlockSpec((m, LANES), lambda b, j: (0, b)),
        out_shape=jax.ShapeDtypeStruct((m, S5_WIDTH), BF16),
        scratch_shapes=[pltpu.VMEM((r, S5_CHUNK * LANES), BF16), pltpu.VMEM((gpb, r, cw), BF16),
                        pltpu.VMEM((gpb, r, cw), BF16)],
        compiler_params=_cparams(("parallel", "arbitrary"), 48),
        name="s5_mix",
    )(proj, pick, w, ws, wo, lpa, lpb, d_tiled)


def _t5_bucket(rel):
    half = REL_BUCKETS // 2
    exact = half // 2
    sign = (rel > 0).astype(np.int32) * half
    n = np.abs(rel)
    large = exact + (np.log(np.maximum(n, 1) / exact) / np.log(REL_MAX_DIST / exact)
                     * (half - exact)).astype(np.int32)
    large = np.minimum(large, half - 1)
    return sign + np.where(n < exact, n, large)


def _att_bias_kernel(gi, idx_ref, tbl_ref, o_ref):
    idx = idx_ref[...]
    for h in range(ATT_HPG):
        acc = jnp.full(idx.shape, NEG_INF, F32)
        for b in range(REL_BUCKETS):
            acc = jnp.where(idx == b, tbl_ref[b, gi * ATT_HPG + h], acc)
        o_ref[h] = acc


def _att_tile(seq_bounds, dil):
    return min(256, min(l for _, l in seq_bounds) // dil)


def _att_bias(rel_bias, gi, tq):
    dil = ATT_PATTERNS[gi][1]
    tk = tq + 2 * ATT_HALF
    rel = (np.arange(tk)[None, :] - ATT_HALF) - np.arange(tq)[:, None]
    idx = np.where(np.abs(rel) <= ATT_HALF, _t5_bucket(rel * dil), -1).astype(np.int32)
    return pl.pallas_call(
        functools.partial(_att_bias_kernel, gi),
        in_specs=[pl.BlockSpec(memory_space=pltpu.VMEM), pl.BlockSpec(memory_space=pltpu.SMEM)],
        out_specs=pl.BlockSpec(memory_space=pltpu.VMEM),
        out_shape=jax.ShapeDtypeStruct((ATT_HPG, tq, tk), F32),
        name=f"attention_bias_{gi}",
    )(jnp.asarray(idx), rel_bias.astype(F32))


def _attn_kernel(tq, blk_bounds, q_ref, kp_ref, km_ref, kn_ref, vp_ref, vm_ref, vn_ref, bias_ref,
                 o_ref, lse_ref):
    jb = pl.program_id(1)
    is_first, is_last = _seq_flags(jb, 1, blk_bounds)
    tk = tq + 2 * ATT_HALF
    flat = lambda ref: ref[...].reshape(-1, ref.shape[-1])
    colk = lax.broadcasted_iota(jnp.int32, (tq, tk), 1)
    valid = jnp.logical_and(jnp.logical_or(colk >= ATT_HALF, jnp.logical_not(is_first)),
                            jnp.logical_or(colk < tq + ATT_HALF, jnp.logical_not(is_last)))
    q = flat(q_ref)
    kcat = jnp.concatenate([flat(kp_ref), flat(km_ref), flat(kn_ref)], axis=0)
    vcat = jnp.concatenate([flat(vp_ref), flat(vm_ref), flat(vn_ref)], axis=0)
    scale = ATT_HEAD_DIM ** -0.5
    lane = lax.broadcasted_iota(jnp.int32, (tq, ATT_HEAD_DIM), 1)
    lse_tile = jnp.zeros((tq, ATT_HEAD_DIM), F32)
    outs = []
    for h in range(ATT_HPG):
        sl = slice(h * ATT_HEAD_DIM, (h + 1) * ATT_HEAD_DIM)
        s = lax.dot_general(q[:, sl], kcat[:, sl], (((1,), (1,)), ((), ())), preferred_element_type=F32)
        s = jnp.where(valid, s * scale + bias_ref[h], NEG_INF)
        mx = jnp.max(s, axis=-1, keepdims=True)
        pr = jnp.exp(s - mx)
        den = jnp.sum(pr, axis=-1, keepdims=True)
        o = jnp.dot(pr.astype(BF16), vcat[:, sl], preferred_element_type=F32)
        outs.append((o / den).astype(o_ref.dtype))
        lse_tile = jnp.where(lane == h, mx + jnp.log(den), lse_tile)
    o_ref[...] = jnp.concatenate(outs, axis=1).reshape(o_ref.shape)
    lse_ref[...] = lse_tile.reshape(lse_ref.shape)


def _attention_group(proj, bias, gi, seq_bounds):
    window, dil = ATT_PATTERNS[gi]
    assert window // (2 * dil) == ATT_HALF
    m = proj.shape[0]
    per = ATT_BLOCK // dil
    tq = bias.shape[1]
    nbq = tq // per
    hr = min(per, ATT_HALF)
    hb = ATT_HALF // hr
    nblocks = m // ATT_BLOCK
    assert tq % per == 0 and nbq % hb == 0 and per % hr == 0
    assert all((s // ATT_BLOCK) % nbq == 0 and (l // ATT_BLOCK) % nbq == 0 for s, l in seq_bounds)
    blk_bounds = tuple((s // ATT_BLOCK // nbq, l // ATT_BLOCK // nbq) for s, l in seq_bounds)
    tk = tq + 2 * ATT_HALF
    cb = (COL_QKV + gi * 3 * ATT_GW) // ATT_GW
    src = proj.reshape(nblocks, ATT_BLOCK, proj.shape[1])
    lasth = nblocks // hb - 1

    def main(col):
        return pl.BlockSpec((nbq, per, ATT_GW), lambda r, jb: (jb, r, cb + col))

    def prev(col):
        return pl.BlockSpec((hb, hr, ATT_GW),
                            lambda r, jb: (jnp.maximum(jb * (nbq // hb) - 1, 0), (r + 1) * (per // hr) - 1, cb + col))

    def nxt(col):
        return pl.BlockSpec((hb, hr, ATT_GW),
                            lambda r, jb: (jnp.minimum((jb + 1) * (nbq // hb), lasth), r * (per // hr), cb + col))

    o, lse = pl.pallas_call(
        functools.partial(_attn_kernel, tq, blk_bounds),
        grid=(dil, nblocks // nbq),
        in_specs=[main(0), prev(1), main(1), nxt(1), prev(2), main(2), nxt(2),
                  pl.BlockSpec((ATT_HPG, tq, tk), lambda r, jb: (0, 0, 0))],
        out_specs=[pl.BlockSpec((nbq, per, ATT_GW), lambda r, jb: (jb, r, 0)),
                   pl.BlockSpec((nbq, per, ATT_HEAD_DIM), lambda r, jb: (jb, r, 0))],
        out_shape=[jax.ShapeDtypeStruct((nblocks, ATT_BLOCK, ATT_GW), BF16),
                   jax.ShapeDtypeStruct((nblocks, ATT_BLOCK, ATT_HEAD_DIM), F32)],
        compiler_params=_cparams(("parallel", "parallel"), 32),
        name=f"dilated_attention_{gi}",
    )(src, src, src, src, src, src, src, bias)
    return o.reshape(m, ATT_GW), lse.reshape(m, ATT_HEAD_DIM)


def _att_combine_kernel(o0, l0, o1, l1, o2, l2, perm_ref, out_ref):
    tr = out_ref.shape[0]
    outs, lses = [o0[...]], [l0[...]]
    for k, (o_ref, l_ref) in enumerate(((o1, l1), (o2, l2))):
        inv = perm_ref[k, 1]
        ob, lb = [], []
        for b in range(tr // ATT_BLOCK):
            rows = slice(b * ATT_BLOCK, (b + 1) * ATT_BLOCK)
            ob.append(jnp.dot(inv, o_ref[rows, :], preferred_element_type=F32))
            lb.append(_sel_dot(inv, l_ref[rows, :]))
        outs.append(jnp.concatenate(ob, axis=0))
        lses.append(jnp.concatenate(lb, axis=0))
    a, b, c = lses
    mx = jnp.maximum(jnp.maximum(a, b), c)
    ea, eb, ec = jnp.exp(a - mx), jnp.exp(b - mx), jnp.exp(c - mx)
    inv = 1.0 / (ea + eb + ec)
    wa, wb, wc = ea * inv, eb * inv, ec * inv
    for h in range(ATT_HPG):
        sl = slice(h * ATT_HEAD_DIM, (h + 1) * ATT_HEAD_DIM)
        bc = lambda w: jnp.broadcast_to(w[:, h:h + 1], (tr, ATT_HEAD_DIM))
        out_ref[:, sl] = (bc(wa) * outs[0][:, sl].astype(F32) + bc(wb) * outs[1][:, sl].astype(F32)
                          + bc(wc) * outs[2][:, sl].astype(F32)).astype(out_ref.dtype)


def _att_combine(outs, lses, tr=512):
    assert [gi for gi, _ in _att_dilated()] == [1, 2]
    m, w = outs[0].shape
    ospec = pl.BlockSpec((tr, w), lambda i: (i, 0))
    lspec = pl.BlockSpec((tr, ATT_HEAD_DIM), lambda i: (i, 0))
    args = [x for pair in zip(outs, lses) for x in pair]
    perms = _att_row_perms()
    return pl.pallas_call(
        _att_combine_kernel,
        grid=(m // tr,),
        in_specs=[ospec, lspec] * 3 + [pl.BlockSpec(perms.shape, lambda i: (0, 0, 0, 0))],
        out_specs=ospec,
        out_shape=jax.ShapeDtypeStruct((m, w), BF16),
        compiler_params=_cparams(("parallel",), 32),
        name="attention_combine",
    )(*args, perms)


PACK_TILE = 512


def _pack_cols_kernel(branches, a_ref, b_ref, o_ref):
    j = pl.program_id(2)
    for lo, hi, off, valid in branches:
        @pl.when(jnp.logical_and(j >= lo, j < hi))
        def _(off=off, valid=valid):
            a = a_ref[0]
            val = a if off == 0 else jnp.concatenate([a, b_ref[0]], axis=1)[:, off:off + PACK_TILE]
            if valid < PACK_TILE:
                col = lax.broadcasted_iota(jnp.int32, val.shape, 1)
                val = jnp.where(col < valid, val, 0.0)
            o_ref[0] = val.astype(o_ref.dtype)


def _pack_cols(w, n_tiles, src_block, branches, tk=512):
    depth, k, n = w.shape
    last = -(-n // PACK_TILE) - 1
    return pl.pallas_call(
        functools.partial(_pack_cols_kernel, branches),
        grid=(depth, k // tk, n_tiles),
        in_specs=[pl.BlockSpec((1, tk, PACK_TILE), lambda l, i, j: (l, i, src_block(j))),
                  pl.BlockSpec((1, tk, PACK_TILE), lambda l, i, j: (l, i, jnp.minimum(src_block(j) + 1, last)))],
        out_specs=pl.BlockSpec((1, tk, PACK_TILE), lambda l, i, j: (l, i, j)),
        out_shape=jax.ShapeDtypeStruct((depth, k, n_tiles * PACK_TILE), BF16),
        compiler_params=_cparams(("parallel", "parallel", "parallel"), 32),
        name="pack_weight_columns",
    )(w, w)


def _pack_w_in_kernel(t_u, shift, a_ref, b_ref, o_ref, dt_ref):
    j = pl.program_id(2)

    @pl.when(j < t_u)
    def _():
        o_ref[...] = a_ref[...].T.astype(o_ref.dtype)

    @pl.when(j >= t_u)
    def _():
        rows = jnp.concatenate([a_ref[...], b_ref[...]], axis=0)[shift:shift + PACK_TILE, :]
        o_ref[...] = rows.T.astype(o_ref.dtype)

    @pl.when(j == t_u)
    def _():
        t = a_ref[0:LANES, :].T
        dt_ref[:, 0:LANES] = t.astype(dt_ref.dtype)
        dt_ref[:, LANES:2 * LANES] = pltpu.roll(t, LANES - SSM_HEADS, 1).astype(dt_ref.dtype)


def _pack_w_in(w_in):
    ng = len(ATT_PATTERNS)
    raw_dt = SSM_INNER + SSM_XBC
    shift = 2 * SSM_HEADS
    t_u = COL_U // PACK_TILE
    t_qkv = COL_QKV // PACK_TILE
    t_gates = COL_GATES // PACK_TILE
    assert raw_dt % PACK_TILE == 0 and COL_U == raw_dt

    def src_block(j):
        jj = jnp.clip(j - t_qkv, 0, 3 * ng - 1)
        return jnp.where(jnp.logical_and(j >= t_qkv, j < t_gates), t_qkv + (jj % 3) * ng + jj // 3, j)

    depth, d, n = w_in.shape
    tk = min(1024, d)
    last = -(-n // LANES) - 1
    sub = PACK_TILE // LANES
    w_t = jnp.swapaxes(w_in, 1, 2)
    return pl.pallas_call(
        functools.partial(_pack_w_in_kernel, t_u, shift),
        grid=(depth, d // tk, IN_COLS // PACK_TILE),
        in_specs=[pl.BlockSpec((None, PACK_TILE, tk), lambda l, i, j: (l, src_block(j), i)),
                  pl.BlockSpec((None, LANES, tk), lambda l, i, j: (l, jnp.minimum((src_block(j) + 1) * sub, last), i))],
        out_specs=[pl.BlockSpec((None, tk, PACK_TILE), lambda l, i, j: (l, i, j)),
                   pl.BlockSpec((None, tk, DT_COLS), lambda l, i, j: (l, i, 0))],
        out_shape=[jax.ShapeDtypeStruct((depth, d, IN_COLS), BF16), jax.ShapeDtypeStruct((depth, d, DT_COLS), BF16)],
        compiler_params=_cparams(("parallel", "parallel", "arbitrary"), 32),
        name="pack_w_in",
    )(w_t, w_t)


def _pack_ffn(w_up, conv_w, conv_b, w_down):
    nb = D_FF_PAD // PACK_TILE
    full, rem = divmod(D_FF, PACK_TILE)
    assert nb == full + 1 and rem > 0
    src_block = lambda j: jnp.where(j < nb, j, full + j - nb)
    branches = ((0, full, 0, PACK_TILE), (full, nb, 0, rem),
                (nb, nb + full, rem, PACK_TILE), (nb + full, 2 * nb, rem, rem))
    w_up_p = _pack_cols(w_up, 2 * nb, src_block, branches)
    padc = lambda a: jnp.pad(a, [(0, 0)] * (a.ndim - 1) + [(0, D_FF_PAD - D_FF)])
    both = lambda a: jnp.concatenate([padc(a[..., :D_FF]), padc(a[..., D_FF:])], axis=-1)
    w_down_p = jnp.pad(w_down, ((0, 0), (0, D_FF_PAD - D_FF), (0, 0))).astype(BF16)
    return w_up_p, both(conv_w).astype(F32), both(conv_b).astype(F32), w_down_p


def _trunk(x, seq_bounds, rel_bias, norm_mix, w_in, ssm_conv_w, ssm_conv_b, ssm_a_log, ssm_dt_bias, ssm_d,
           ssm_norm, ssm_w_out, s5_tables, s5_d, s5_w_glu, att_w_out, w_o, norm_ffn, w_up, ffn_conv_w,
           ffn_conv_b, w_down, final_norm):
    m = x.shape[0]
    depth = w_in.shape[0]
    tm = min(1024, m)
    tm2 = min(512, m)
    gate_blk = lambda b, tn: (COL_GATES + b * D_MODEL) // tn
    chunk_bounds = tuple((s // S5_CHUNK, l // S5_CHUNK) for s, l in seq_bounds)
    att_bias = [_att_bias(rel_bias, gi, _att_tile(seq_bounds, dil)) for gi, (_, dil) in enumerate(ATT_PATTERNS)]
    gated = lambda acc, gate: _sigmoid(gate.astype(F32)) * acc
    gated_add = lambda acc, gate, prev: prev + _sigmoid(gate.astype(F32)) * acc
    resid = lambda acc, res: res + acc

    w_main, w_dt = _pack_w_in(w_in)
    w_up_p, cw_p, cb_p, w_down_p = _pack_ffn(w_up, ffn_conv_w, ffn_conv_b, w_down)
    ssm_w_out, s5_w_glu, att_w_out, w_o = (w.astype(BF16) for w in (ssm_w_out, s5_w_glu, att_w_out, w_o))

    for li in range(depth):
        proj, dtraw = _in_proj(x, norm_mix[li].astype(F32), w_main, w_dt, li, ssm_conv_w[li].astype(F32),
                               ssm_conv_b[li].astype(F32), seq_bounds, tm)

        y_a = _ssd(proj, dtraw, ssm_dt_bias[li], ssm_a_log[li], ssm_d[li], ssm_norm[li].astype(F32), seq_bounds)
        merged = _matmul(y_a, ssm_w_out, layer=li, tm=tm, tn=512, out_dtype=F32, epilogue=gated,
                         extras=((proj, gate_blk(0, 512)),), name="ssm_out_proj")

        d_tiled = jnp.tile(s5_d[li].astype(F32).reshape(S5_GROUPS, 1, S5_GROUP), (1, 1, S5_CHUNK))
        y_b = _s5(proj, s5_tables, li, d_tiled, chunk_bounds)
        merged = _glu_matmul(y_b, s5_w_glu, li, proj, merged, tm=tm, tn=512, gate_blk=gate_blk(1, 512))

        outs, lses = zip(*[_attention_group(proj, att_bias[gi], gi, seq_bounds)
                           for gi in range(len(ATT_PATTERNS))])
        comb = _att_combine(outs, lses, tr=tm2)
        merged = _matmul(comb, att_w_out, layer=li, tm=tm, tn=512, out_dtype=BF16, epilogue=gated_add,
                         extras=((proj, gate_blk(2, 512)), (merged, 0)), name="att_out_proj")

        x = _matmul(merged, w_o, layer=li, tm=tm, tn=512, out_dtype=F32, epilogue=resid,
                    extras=((x, 0),), name="mix_out_proj")

        act = _ffn_up(x, norm_ffn[li].astype(F32), w_up_p, li, cw_p[li], cb_p[li], seq_bounds, tm)
        x = _matmul(act, w_down_p, layer=li, tm=tm, tn=512, out_dtype=F32, epilogue=resid, extras=((x, 0),),
                    vmem_mb=56, name="ffn_down")
    return x


def kernel(x_prompt, x_sample, rel_bias, norm_mix, w_in, ssm_conv_w, ssm_conv_b, ssm_a_log, ssm_dt_bias, ssm_d, ssm_norm, ssm_w_out, s5_a_re, s5_a_im, s5_log_step, s5_b_re, s5_b_im, s5_c_re, s5_c_im, s5_d, s5_w_glu, att_w_out, w_o, norm_ffn, w_up, ffn_conv_w, ffn_conv_b, w_down, final_norm):
    d = x_prompt.shape[-1]
    seq_bounds = []
    row = 0
    for arr in (x_prompt, x_sample):
        for _ in range(arr.shape[0]):
            seq_bounds.append((row, arr.shape[1]))
            row += arr.shape[1]
    seq_bounds = tuple(seq_bounds)
    x = jnp.concatenate([x_prompt.reshape(-1, d), x_sample.reshape(-1, d)], axis=0)
    s5_tables = _s5_prep(s5_a_re, s5_a_im, s5_log_step, s5_b_re, s5_b_im, s5_c_re, s5_c_im)
    y = _trunk(x, seq_bounds, rel_bias, norm_mix, w_in, ssm_conv_w, ssm_conv_b, ssm_a_log, ssm_dt_bias, ssm_d,
               ssm_norm, ssm_w_out, s5_tables, s5_d, s5_w_glu, att_w_out, w_o, norm_ffn, w_up, ffn_conv_w,
               ffn_conv_b, w_down, final_norm)
    n_prompt = x_prompt.shape[0] * x_prompt.shape[1]
    y_prompt, y_sample = _final_norm(y, final_norm.astype(F32), n_prompt, tr=min(512, n_prompt))
    return (y_prompt.reshape(x_prompt.shape), y_sample.reshape(x_sample.shape))
```

```python
import functools
import math

import numpy as np
import jax
import jax.numpy as jnp
from jax import lax
from jax.experimental import pallas as pl
from jax.experimental.pallas import tpu as pltpu

F32 = jnp.float32
BF16 = jnp.bfloat16

D_MODEL = 2048
NORM_EPS = 1e-6
NEG_INF = -1e30

SSM_HEADDIM = 64
SSM_INNER = 1536
SSM_HEADS = 24
SSM_GROUPS = 4
SSM_HPG = SSM_HEADS // SSM_GROUPS
SSM_STATE = 128
SSM_CONV = 5
SSM_CHUNK = 128
SSM_XBC = 2560

S5_WIDTH = 1024
S5_GROUP = 16
S5_GROUPS = 64
S5_STATE = 64
S5_CHUNK = 32
S5_CW = S5_CHUNK * S5_GROUP
S5_LAGS = 128
S5_SCAN_STEPS = 8

ATT_HEAD_DIM = 128
ATT_HPG = 4
ATT_PATTERNS = ((128, 1), (512, 4), (2048, 16))
ATT_GW = ATT_HPG * ATT_HEAD_DIM
ATT_HALF = 64
REL_BUCKETS = 32
REL_MAX_DIST = 1024

D_FF = 5504
D_FF_PAD = 5632
FFN_CONV = 3

COL_Z = 0
COL_XBC = 1536
COL_U = 4096
COL_QKV = 5120
COL_GATES = 9728
IN_COLS = 15872
DT_COLS = 256

V7X_VMEM_BYTES = 64 * 1024 * 1024


def _cparams(sem, vmem_mb):
    return pltpu.CompilerParams(dimension_semantics=sem, vmem_limit_bytes=vmem_mb * 1024 * 1024)


def _sigmoid(x):
    return 1.0 / (1.0 + jnp.exp(-x))


def _silu(x):
    return x * _sigmoid(x)


def _split3(x):
    hi = x.astype(BF16)
    r1 = x - hi.astype(F32)
    mid = r1.astype(BF16)
    lo = (r1 - mid.astype(F32)).astype(BF16)
    return hi, mid, lo


def _dot_sel(x, sel):
    hi, mid, lo = _split3(x)
    d = lambda a: jnp.dot(a, sel, preferred_element_type=F32)
    return d(hi) + d(mid) + d(lo)


def _sel_dot(sel, x):
    hi, mid, lo = _split3(x)
    d = lambda a: jnp.dot(sel, a, preferred_element_type=F32)
    return d(hi) + d(mid) + d(lo)


def _dot_f32(a, b):
    ah, am, _ = _split3(a)
    bh, bm, _ = _split3(b)
    d = lambda x, y: jnp.dot(x, y, preferred_element_type=F32)
    return d(ah, bh) + d(ah, bm) + d(am, bh)


def _final_norm_kernel(nb_first, x_ref, g_ref, first_ref, second_ref):
    x = x_ref[...]
    ms = jnp.mean(x * x, axis=-1, keepdims=True)
    y = x * lax.rsqrt(ms + NORM_EPS) * g_ref[...]
    i = pl.program_id(0)

    @pl.when(i < nb_first)
    def _():
        first_ref[...] = y

    @pl.when(i >= nb_first)
    def _():
        second_ref[...] = y


def _final_norm(x, gain, n_first, tr=512):
    m, d = x.shape
    assert n_first % tr == 0 and (m - n_first) % tr == 0 and 0 < n_first < m
    nb_first = n_first // tr
    return pl.pallas_call(
        functools.partial(_final_norm_kernel, nb_first),
        grid=(m // tr,),
        in_specs=[pl.BlockSpec((tr, d), lambda i: (i, 0)),
                  pl.BlockSpec((1, d), lambda i: (0, 0))],
        out_specs=[pl.BlockSpec((tr, d), lambda i: (jnp.minimum(i, nb_first - 1), 0)),
                   pl.BlockSpec((tr, d), lambda i: (jnp.maximum(i - nb_first, 0), 0))],
        out_shape=[jax.ShapeDtypeStruct((n_first, d), x.dtype), jax.ShapeDtypeStruct((m - n_first, d), x.dtype)],
        compiler_params=_cparams(("arbitrary",), 32),
        name="final_norm",
    )(x, gain.reshape(1, d))


def _mm_kernel(epilogue, n_extra, a_ref, b_ref, *rest):
    extras = rest[:n_extra]
    o_ref = rest[n_extra]
    acc = jnp.dot(a_ref[...], b_ref[...], preferred_element_type=F32)
    if epilogue is not None:
        acc = epilogue(acc, *[e[...] for e in extras])
    o_ref[...] = acc.astype(o_ref.dtype)


def _layer_spec(w, layer, rows, tn, col_off=0):
    if w.ndim == 2:
        return pl.BlockSpec((rows, tn), lambda i, j: (0, j + col_off))
    return pl.BlockSpec((None, rows, tn), lambda i, j: (layer, 0, j + col_off))


def _matmul(a, b, *, tm, tn, out_dtype, k=None, a_kblk=0, epilogue=None, extras=(), vmem_mb=48,
            layer=None, name="matmul"):
    m = a.shape[0]
    kk, n = b.shape[-2:]
    if k is None:
        k = a.shape[1]
    assert k == kk and m % tm == 0 and n % tn == 0
    in_specs = [pl.BlockSpec((tm, k), lambda i, j: (i, a_kblk)), _layer_spec(b, layer, k, tn)]
    args = [a, b]
    for arr, off in extras:
        in_specs.append(pl.BlockSpec((tm, tn), functools.partial(lambda i, j, o: (i, j + o), o=off)))
        args.append(arr)
    return pl.pallas_call(
        functools.partial(_mm_kernel, epilogue, len(extras)),
        grid=(m // tm, n // tn),
        in_specs=in_specs,
        out_specs=pl.BlockSpec((tm, tn), lambda i, j: (i, j)),
        out_shape=jax.ShapeDtypeStruct((m, n), out_dtype),
        compiler_params=_cparams(("parallel", "parallel"), vmem_mb),
        name=name,
    )(*args)


def _merge_kernel(ya_ref, yb_ref, yc_ref, perm_ref, wa_ref, wv_ref, wg_ref, wc_ref, g0_ref, g1_ref, g2_ref,
                  o_ref, yb_scr):
    @pl.when(pl.program_id(1) == 0)
    def _():
        for b in range(yb_ref.shape[0] // S5_PERM_ROWS):
            rows = slice(b * S5_PERM_ROWS, (b + 1) * S5_PERM_ROWS)
            yb_scr[rows, :] = jnp.dot(perm_ref[1], yb_ref[rows, :], preferred_element_type=F32).astype(BF16)

    dot = lambda a, b: jnp.dot(a, b, preferred_element_type=F32)
    sig = lambda ref: _sigmoid(ref[...].astype(F32))
    yb = yb_scr[...]
    branch_a = dot(ya_ref[...], wa_ref[...])
    branch_b = dot(yb, wv_ref[...]) * _sigmoid(dot(yb, wg_ref[...]))
    branch_c = dot(yc_ref[...], wc_ref[...])
    o_ref[...] = (sig(g0_ref) * branch_a + sig(g1_ref) * branch_b + sig(g2_ref) * branch_c).astype(o_ref.dtype)


def _merge_branches(ya, yb, yc, w_a, w_glu, w_c, layer, proj, *, tm, tn):
    m = ya.shape[0]
    n = w_a.shape[-1]
    nb = n // tn
    perm, _ = _s5_movers()
    assert tm % S5_PERM_ROWS == 0
    rows = lambda a: pl.BlockSpec((tm, a.shape[1]), lambda i, j: (i, 0))
    gate = lambda b: pl.BlockSpec((tm, tn), lambda i, j: (i, j + (COL_GATES + b * n) // tn))
    return pl.pallas_call(
        _merge_kernel,
        grid=(m // tm, nb),
        in_specs=[rows(ya), rows(yb), rows(yc), pl.BlockSpec(perm.shape, lambda i, j: (0, 0, 0)),
                  _layer_spec(w_a, layer, ya.shape[1], tn),
                  _layer_spec(w_glu, layer, yb.shape[1], tn), _layer_spec(w_glu, layer, yb.shape[1], tn, nb),
                  _layer_spec(w_c, layer, yc.shape[1], tn),
                  gate(0), gate(1), gate(2)],
        out_specs=pl.BlockSpec((tm, tn), lambda i, j: (i, j)),
        out_shape=jax.ShapeDtypeStruct((m, n), BF16),
        scratch_shapes=[pltpu.VMEM((tm, yb.shape[1]), BF16)],
        compiler_params=_cparams(("parallel", "arbitrary"), 48),
        name="merge_branches",
    )(ya, yb, yc, perm, w_a, w_glu, w_glu, w_c, proj, proj, proj)


HALO = 16


def _seq_flags(row0, nrows, seq_bounds):
    starts = [s for s, _ in seq_bounds]
    ends = [s + l for s, l in seq_bounds]
    is_start = functools.reduce(jnp.logical_or, [row0 == s for s in starts])
    is_end = functools.reduce(jnp.logical_or, [row0 + nrows == e for e in ends])
    return is_start, is_end


def _conv_taps(buf_ref, w_ref, b_ref, width, tr, cols=slice(None)):
    pad = width // 2
    acc = None
    for kk in range(width):
        term = buf_ref[pl.ds(HALO - pad + kk, tr), cols] * w_ref[kk:kk + 1, cols]
        acc = term if acc is None else acc + term
    return acc + b_ref[:, cols]


def _norm_rows(x, gain):
    ms = jnp.mean(x * x, axis=-1, keepdims=True)
    return (x * lax.rsqrt(ms + NORM_EPS) * gain).astype(BF16)


def _fill_normed(hn_ref, xp_ref, xm_ref, xn_ref, gain_ref, tm, seq_bounds):
    is_start, is_end = _seq_flags(pl.program_id(0) * tm, tm, seq_bounds)
    gain = gain_ref[...]
    hn_ref[0:HALO, :] = jnp.where(is_start, 0.0, _norm_rows(xp_ref[...], gain)).astype(BF16)
    hn_ref[HALO:HALO + tm, :] = _norm_rows(xm_ref[...], gain)
    hn_ref[HALO + tm:HALO + tm + HALO, :] = jnp.where(is_end, 0.0, _norm_rows(xn_ref[...], gain)).astype(BF16)


def _row_halo_specs(tm, d, nrows):
    hb = tm // HALO
    last = nrows // HALO - 1
    return [pl.BlockSpec((HALO, d), lambda i, j: (jnp.maximum(i * hb - 1, 0), 0)),
            pl.BlockSpec((tm, d), lambda i, j: (i, 0)),
            pl.BlockSpec((HALO, d), lambda i, j: (jnp.minimum((i + 1) * hb, last), 0))]


ATT_BLOCK = 256


def _att_dilated():
    return tuple((gi, dil) for gi, (_, dil) in enumerate(ATT_PATTERNS) if dil > 1)


def _att_row_perms():
    out = []
    for _, dil in _att_dilated():
        dst = np.arange(ATT_BLOCK)[:, None]
        per = ATT_BLOCK // dil
        fwd = (dst % per) * dil + dst // per == np.arange(ATT_BLOCK)[None, :]
        out.append(np.stack([fwd, fwd.T]))
    return jnp.asarray(np.stack(out), BF16)


def _in_proj_kernel(tm, tn, seq_bounds, conv_lo, conv_hi, dil_ranges, u_range, xp_ref, xm_ref, xn_ref, gain_ref,
                    w_ref, wdt_ref, cw_ref, cb_ref, perm_ref, uperm_ref, o_ref, dt_ref, hn_ref, acc_ref):
    j = pl.program_id(1)

    @pl.when(j == 0)
    def _():
        _fill_normed(hn_ref, xp_ref, xm_ref, xn_ref, gain_ref, tm, seq_bounds)
        dt_ref[...] = jnp.dot(hn_ref[HALO:HALO + tm, :], wdt_ref[...], preferred_element_type=F32)

    is_conv = jnp.logical_and(j >= conv_lo, j < conv_hi)

    @pl.when(is_conv)
    def _():
        half = tn // 2
        for c in range(2):
            cols = slice(c * half, (c + 1) * half)
            acc_ref[:, cols] = jnp.dot(hn_ref[...], w_ref[:, cols], preferred_element_type=F32)
            o_ref[:, cols] = _silu(_conv_taps(acc_ref, cw_ref, cb_ref, SSM_CONV, tm, cols)).astype(o_ref.dtype)

    regroup = [((lo, hi), ATT_BLOCK, functools.partial(lambda k: perm_ref[k, 0], k))
               for k, (lo, hi) in enumerate(dil_ranges)]
    regroup.append((u_range, S5_PERM_ROWS, lambda: uperm_ref[0]))
    conds = [jnp.logical_and(j >= lo, j < hi) for (lo, hi), _, _ in regroup]
    plain = jnp.logical_not(functools.reduce(jnp.logical_or, conds, is_conv))

    @pl.when(plain)
    def _():
        o_ref[...] = jnp.dot(hn_ref[HALO:HALO + tm, :], w_ref[...],
                             preferred_element_type=F32).astype(o_ref.dtype)

    for cond, (_, nrows, get_perm) in zip(conds, regroup):
        @pl.when(cond)
        def _(nrows=nrows, get_perm=get_perm):
            acc = jnp.dot(hn_ref[HALO:HALO + tm, :], w_ref[...], preferred_element_type=F32).astype(BF16)
            for b in range(tm // nrows):
                rows = slice(b * nrows, (b + 1) * nrows)
                o_ref[rows, :] = jnp.dot(get_perm(), acc[rows, :],
                                         preferred_element_type=F32).astype(o_ref.dtype)


def _in_proj(x, gain, w_main, w_dt, layer, conv_w, conv_b, seq_bounds, tm, tn=512):
    m, d = x.shape
    n = w_main.shape[-1]
    conv_lo, conv_hi = COL_XBC // tn, (COL_XBC + SSM_XBC) // tn
    cidx = lambda i, j: (0, jnp.clip(j - conv_lo, 0, conv_hi - conv_lo - 1))
    dil_ranges = tuple(((COL_QKV + gi * 3 * ATT_GW) // tn, (COL_QKV + (gi + 1) * 3 * ATT_GW) // tn)
                       for gi, _ in _att_dilated())
    u_range = (COL_U // tn, (COL_U + S5_WIDTH) // tn)
    perms = _att_row_perms()
    uperm, _ = _s5_movers()
    assert tm % S5_PERM_ROWS == 0 and tm % ATT_BLOCK == 0
    return pl.pallas_call(
        functools.partial(_in_proj_kernel, tm, tn, seq_bounds, conv_lo, conv_hi, dil_ranges, u_range),
        grid=(m // tm, n // tn),
        in_specs=_row_halo_specs(tm, d, m) + [
            pl.BlockSpec((1, d), lambda i, j: (0, 0)),
            _layer_spec(w_main, layer, d, tn),
            pl.BlockSpec((None, d, DT_COLS), lambda i, j: (layer, 0, 0)),
            pl.BlockSpec((SSM_CONV, tn), cidx),
            pl.BlockSpec((1, tn), cidx),
            pl.BlockSpec(perms.shape, lambda i, j: (0, 0, 0, 0)),
            pl.BlockSpec(uperm.shape, lambda i, j: (0, 0, 0))],
        out_specs=[pl.BlockSpec((tm, tn), lambda i, j: (i, j)),
                   pl.BlockSpec((tm, DT_COLS), lambda i, j: (i, 0))],
        out_shape=[jax.ShapeDtypeStruct((m, n), BF16), jax.ShapeDtypeStruct((m, DT_COLS), F32)],
        scratch_shapes=[pltpu.VMEM((tm + 2 * HALO, d), BF16), pltpu.VMEM((tm + 2 * HALO, tn), F32)],
        compiler_params=_cparams(("parallel", "arbitrary"), 56),
        name="in_proj",
    )(x, x, x, gain.reshape(1, d), w_main, w_dt, conv_w, conv_b.reshape(1, -1), perms, uperm)


def _ffn_up_kernel(tm, seq_bounds, xp_ref, xm_ref, xn_ref, gain_ref, wg_ref, wv_ref, cwg_ref, cbg_ref,
                   cwv_ref, cbv_ref, o_ref, hn_ref, sg_ref, sv_ref):
    @pl.when(pl.program_id(1) == 0)
    def _():
        _fill_normed(hn_ref, xp_ref, xm_ref, xn_ref, gain_ref, tm, seq_bounds)

    hn = hn_ref[...]
    sg_ref[...] = jnp.dot(hn, wg_ref[...], preferred_element_type=F32)
    sv_ref[...] = jnp.dot(hn, wv_ref[...], preferred_element_type=F32)
    gate = _conv_taps(sg_ref, cwg_ref, cbg_ref, FFN_CONV, tm)
    val = _conv_taps(sv_ref, cwv_ref, cbv_ref, FFN_CONV, tm)
    o_ref[...] = (_silu(gate) * val).astype(o_ref.dtype)


def _ffn_up(x, gain, w, layer, conv_w, conv_b, seq_bounds, tm, tn=512):
    m, d = x.shape
    nb = D_FF_PAD // tn
    col = lambda rows, off: pl.BlockSpec((rows, tn), functools.partial(lambda i, j, o: (0, j + o), o=off))
    cb2 = conv_b.reshape(1, -1)
    return pl.pallas_call(
        functools.partial(_ffn_up_kernel, tm, seq_bounds),
        grid=(m // tm, nb),
        in_specs=_row_halo_specs(tm, d, m) + [
            pl.BlockSpec((1, d), lambda i, j: (0, 0)),
            _layer_spec(w, layer, d, tn), _layer_spec(w, layer, d, tn, nb),
            col(FFN_CONV, 0), col(1, 0), col(FFN_CONV, nb), col(1, nb)],
        out_specs=pl.BlockSpec((tm, tn), lambda i, j: (i, j)),
        out_shape=jax.ShapeDtypeStruct((m, D_FF_PAD), BF16),
        scratch_shapes=[pltpu.VMEM((tm + 2 * HALO, d), BF16), pltpu.VMEM((tm + 2 * HALO, tn), F32),
                        pltpu.VMEM((tm + 2 * HALO, tn), F32)],
        compiler_params=_cparams(("parallel", "arbitrary"), 56),
        name="ffn_up",
    )(x, x, x, gain.reshape(1, d), w, w, conv_w, cb2, conv_w, cb2)


def _softplus(x):
    return jnp.maximum(x, 0.0) + jnp.log1p(jnp.exp(-jnp.abs(x)))


def _ssd_cumsums(dt_ref, bias_ref, alog_ref, d, row0=0):
    t = SSM_CHUNK
    dt = _softplus(dt_ref[row0:row0 + t, d * 128:(d + 1) * 128] + bias_ref[d])
    da = dt * (-jnp.exp(alog_ref[d]) * math.log2(math.e))
    row = lax.broadcasted_iota(jnp.int32, (t, t), 0)
    col = lax.broadcasted_iota(jnp.int32, (t, t), 1)
    tri = jnp.where((col <= row) if d == 0 else (col >= row), 1.0, 0.0).astype(BF16)
    return dt, _sel_dot(tri, da)


SSD_SCAN_CHUNKS = 2


def _ssd_state_kernel(nsteps, seq_bounds, xf_ref, bf_ref, dtf_ref, xb_ref, bb_ref, dtb_ref, bias_ref, alog_ref,
                      hf_ref, hb_ref, h_ref):
    t = SSM_CHUNK
    n = SSM_STATE
    p = SSM_HEADDIM
    kc = SSD_SCAN_CHUNKS
    i = pl.program_id(0)

    @pl.when(i == 0)
    def _():
        h_ref[...] = jnp.zeros(h_ref.shape, F32)

    for d, (x_ref, b_ref, dt_ref, out_ref) in enumerate(((xf_ref, bf_ref, dtf_ref, hf_ref),
                                                         (xb_ref, bb_ref, dtb_ref, hb_ref))):
        blk = i if d == 0 else nsteps - 1 - i
        contrib = []
        for c in range(kc):
            r0 = c * t
            dt, cs = _ssd_cumsums(dt_ref, bias_ref, alog_ref, d, r0)
            tot = cs[t - 1:t, :] if d == 0 else cs[0:1, :]
            w = jnp.exp2(tot - cs) * dt
            etot = jnp.exp2(tot)
            per_group = []
            for g in range(SSM_GROUPS):
                bg_t = b_ref[r0:r0 + t, g * n:(g + 1) * n].astype(F32).T.astype(BF16)
                xw, dec = [], []
                for hh in range(SSM_HPG):
                    h = g * SSM_HPG + hh
                    xh = x_ref[r0:r0 + t, h * p:(h + 1) * p].astype(F32)
                    xw.append((xh * jnp.broadcast_to(w[:, h:h + 1], (t, p))).astype(BF16))
                    dec.append(jnp.broadcast_to(etot[:, h:h + 1], (n, p)))
                st = jnp.dot(bg_t, jnp.concatenate(xw, axis=1), preferred_element_type=F32)
                per_group.append((st, jnp.concatenate(dec, axis=1)))
            contrib.append(per_group)
        for c in (range(kc) if d == 0 else reversed(range(kc))):
            is_start, is_end = _seq_flags((blk * kc + c) * t, t, seq_bounds)
            reset = is_start if d == 0 else is_end
            for g in range(SSM_GROUPS):
                st, dec = contrib[c][g]
                hin = jnp.where(reset, 0.0, h_ref[d, g])
                out_ref[c, g] = hin.astype(BF16)
                h_ref[d, g] = hin * dec + st


def _ssd_states(proj, dtraw, dt_bias, a_log, seq_bounds):
    m = proj.shape[0]
    t = SSM_CHUNK
    kc = SSD_SCAN_CHUNKS
    nchunks = m // t
    nsteps = nchunks // kc
    hw = SSM_HPG * SSM_HEADDIM
    bw = SSM_GROUPS * SSM_STATE
    fwd = lambda i: i
    bwd = lambda i: nsteps - 1 - i
    xspec = lambda f: pl.BlockSpec((kc * t, SSM_INNER), lambda i: (f(i), COL_XBC // SSM_INNER))
    bspec = lambda f: pl.BlockSpec((kc * t, bw), lambda i: (f(i), (COL_XBC + SSM_INNER) // bw))
    dspec = lambda f: pl.BlockSpec((kc * t, DT_COLS), lambda i: (f(i), 0))
    hspec = lambda f: pl.BlockSpec((kc, SSM_GROUPS, SSM_STATE, hw), lambda i: (f(i), 0, 0, 0))
    const = pl.BlockSpec((2, 1, 128), lambda i: (0, 0, 0))
    return pl.pallas_call(
        functools.partial(_ssd_state_kernel, nsteps, seq_bounds),
        grid=(nsteps,),
        in_specs=[xspec(fwd), bspec(fwd), dspec(fwd), xspec(bwd), bspec(bwd), dspec(bwd), const, const],
        out_specs=[hspec(fwd), hspec(bwd)],
        out_shape=[jax.ShapeDtypeStruct((nchunks, SSM_GROUPS, SSM_STATE, hw), BF16)] * 2,
        scratch_shapes=[pltpu.VMEM((2, SSM_GROUPS, SSM_STATE, hw), F32)],
        compiler_params=_cparams(("arbitrary",), 32),
        name="ssd_states",
    )(proj, proj, dtraw, proj, proj, dtraw, dt_bias, a_log)


def _ssd_out_kernel(x_ref, b_ref, c_ref, dt_ref, bias_ref, alog_ref, hf_ref, hb_ref, z_ref, dskip_ref,
                    gain_ref, o_ref, y_ref):
    t = SSM_CHUNK
    n = SSM_STATE
    p = SSM_HEADDIM
    dtf, csf = _ssd_cumsums(dt_ref, bias_ref, alog_ref, 0)
    dtb, csb = _ssd_cumsums(dt_ref, bias_ref, alog_ref, 1)
    csf_t, csb_t, dtf_t, dtb_t = csf.T, csb.T, dtf.T, dtb.T
    dts_t = dtf_t + dtb_t
    row = lax.broadcasted_iota(jnp.int32, (t, t), 0)
    col = lax.broadcasted_iota(jnp.int32, (t, t), 1)
    low = col <= row
    low_strict = col < row
    up_strict = col > row
    for g in range(SSM_GROUPS):
        bg = b_ref[:, g * n:(g + 1) * n]
        cg = c_ref[:, g * n:(g + 1) * n]
        cb = lax.dot_general(cg, bg, (((1,), (1,)), ((), ())), preferred_element_type=F32)
        cg_f = cg.astype(F32)
        for hh in range(SSM_HPG):
            h = g * SSM_HPG + hh
            colf = jnp.broadcast_to(csf[:, h:h + 1], (t, t))
            colb = jnp.broadcast_to(csb[:, h:h + 1], (t, t))
            seg = jnp.where(low, colf - jnp.broadcast_to(csf_t[h:h + 1, :], (t, t)),
                            colb - jnp.broadcast_to(csb_t[h:h + 1, :], (t, t)))
            coef = jnp.where(low_strict, jnp.broadcast_to(dtf_t[h:h + 1, :], (t, t)),
                             jnp.where(up_strict, jnp.broadcast_to(dtb_t[h:h + 1, :], (t, t)),
                                       jnp.broadcast_to(dts_t[h:h + 1, :], (t, t))))
            mh = jnp.exp2(seg) * cb * coef
            lhs = jnp.concatenate([mh, cg_f * jnp.exp2(colf), cg_f * jnp.exp2(colb)], axis=1).astype(BF16)
            rhs = jnp.concatenate([x_ref[:, h * p:(h + 1) * p],
                                   hf_ref[0, g, :, hh * p:(hh + 1) * p],
                                   hb_ref[0, g, :, hh * p:(hh + 1) * p]], axis=0)
            y_ref[:, h * p:(h + 1) * p] = jnp.dot(lhs, rhs, preferred_element_type=F32)
    y = y_ref[...] + x_ref[...].astype(F32) * dskip_ref[...]
    y = y * _silu(z_ref[...].astype(F32))
    ms = jnp.mean(y * y, axis=-1, keepdims=True)
    o_ref[...] = (y * lax.rsqrt(ms + NORM_EPS) * gain_ref[...]).astype(o_ref.dtype)


def _ssd_out(proj, dtraw, dt_bias, a_log, h_f, h_b, d_skip, norm_g):
    m = proj.shape[0]
    t = SSM_CHUNK
    w = SSM_INNER
    hw = SSM_HPG * SSM_HEADDIM
    bw = SSM_GROUPS * SSM_STATE
    hspec = pl.BlockSpec((1, SSM_GROUPS, SSM_STATE, hw), lambda i: (i, 0, 0, 0))
    return pl.pallas_call(
        _ssd_out_kernel,
        grid=(m // t,),
        in_specs=[pl.BlockSpec((t, w), lambda i: (i, COL_XBC // w)),
                  pl.BlockSpec((t, bw), lambda i: (i, (COL_XBC + w) // bw)),
                  pl.BlockSpec((t, bw), lambda i: (i, (COL_XBC + w) // bw + 1)),
                  pl.BlockSpec((t, DT_COLS), lambda i: (i, 0)),
                  pl.BlockSpec((2, 1, 128), lambda i: (0, 0, 0)),
                  pl.BlockSpec((2, 1, 128), lambda i: (0, 0, 0)),
                  hspec, hspec,
                  pl.BlockSpec((t, w), lambda i: (i, COL_Z // w)),
                  pl.BlockSpec((1, w), lambda i: (0, 0)),
                  pl.BlockSpec((1, w), lambda i: (0, 0))],
        out_specs=pl.BlockSpec((t, w), lambda i: (i, 0)),
        out_shape=jax.ShapeDtypeStruct((m, w), BF16),
        scratch_shapes=[pltpu.VMEM((t, w), F32)],
        compiler_params=_cparams(("parallel",), 32),
        name="ssd_out",
    )(proj, proj, proj, dtraw, dt_bias, a_log, h_f, h_b, proj,
      jnp.repeat(d_skip.astype(F32), SSM_HEADDIM).reshape(1, w), norm_g.reshape(1, w))


def _ssd(proj, dtraw, dt_bias, a_log, d_skip, norm_g, seq_bounds):
    pad = lambda v: jnp.pad(v.astype(F32), ((0, 0), (0, 128 - SSM_HEADS))).reshape(2, 1, 128)
    dt_bias, a_log = pad(dt_bias), pad(a_log)
    h_f, h_b = _ssd_states(proj, dtraw, dt_bias, a_log, seq_bounds)
    return _ssd_out(proj, dtraw, dt_bias, a_log, h_f, h_b, d_skip, norm_g)


def _s5_selectors():
    t, c = S5_CHUNK, S5_GROUP
    lag = np.arange(S5_LAGS)[:, None]
    tok = (np.arange(S5_CW) // c)[None, :]
    e_exit_f = (lag == t - 1 - tok)
    e_exit_b = (lag == tok)
    e_in_f = (lag == tok + 1)
    e_in_b = (lag == t - tok)
    lagidx = (np.arange(2 * S5_CW) // c)[None, :]
    rel = lagidx - (t - 1)
    inb = lagidx <= 2 * t - 2
    e_k_f = (lag == rel) & (rel >= 0) & inb
    e_k_b = (lag == -rel) & (rel <= 0) & inb
    tile = (np.arange(c)[:, None] == (np.arange(2 * S5_CW) % c)[None, :])
    sel = np.stack([np.concatenate([a, b], axis=1) for a, b in
                    ((e_exit_f, e_in_f), (e_exit_b, e_in_b))])
    selk = np.stack([e_k_f, e_k_b])
    return (jnp.asarray(sel, BF16), jnp.asarray(selk, BF16), jnp.asarray(tile, BF16))


def _s5_prep_kernel(acol_ref, arow_ref, ls_ref, b_ref, bt_ref, ct_ref, sel_ref, selk_ref, tile_ref,
                    sign_ref, w_ref, ws_ref, wo_ref, lpa_ref, lpb_ref):
    t = S5_CHUNK
    cw = S5_CW
    p = S5_STATE
    tile = tile_ref[...]
    lagf = lax.broadcasted_iota(jnp.int32, (p, S5_LAGS), 1).astype(F32)
    kt = jnp.zeros((S5_GROUP, 2 * cw), F32)
    for d in range(2):
        step = jnp.exp(ls_ref[0, 0, d])
        are = acol_ref[0, 0, d, 0]
        aim = acol_ref[0, 0, d, 1]
        mag = are * step
        th = aim * step
        amp = jnp.exp(lagf * mag)
        pwr = amp * jnp.cos(lagf * th)
        pwi = amp * jnp.sin(lagf * th)
        lbr = jnp.exp(mag) * jnp.cos(th)
        lbi = jnp.exp(mag) * jnp.sin(th)
        den = are * are + aim * aim
        cfr = ((lbr - 1.0) * are + lbi * aim) / den
        cfi = (lbi * are - (lbr - 1.0) * aim) / den
        bre = b_ref[0, 0, 0]
        bim = b_ref[0, 0, 1]
        bbr = cfr * bre - cfi * bim
        bbi = cfr * bim + cfi * bre
        step_r = step
        are_r = arow_ref[0, 0, d, 0:1, 0:p]
        aim_r = arow_ref[0, 0, d, 1:2, 0:p]
        mag_r = are_r * step_r
        th_r = aim_r * step_r
        lbr_r = jnp.exp(mag_r) * jnp.cos(th_r)
        lbi_r = jnp.exp(mag_r) * jnp.sin(th_r)
        den_r = are_r * are_r + aim_r * aim_r
        cfr_r = ((lbr_r - 1.0) * are_r + lbi_r * aim_r) / den_r
        cfi_r = (lbi_r * are_r - (lbr_r - 1.0) * aim_r) / den_r
        btr = bt_ref[0, 0, 0]
        bti = bt_ref[0, 0, 1]
        bbr_t = cfr_r * btr - cfi_r * bti
        bbi_t = cfr_r * bti + cfi_r * btr
        ctr = ct_ref[0, 0, d, 0]
        cti = ct_ref[0, 0, d, 1]

        sel = sel_ref[d]
        er = _dot_sel(pwr, sel)
        ei = _dot_sel(pwi, sel)
        tb_r = _dot_sel(bbr, tile[:, 0:cw])
        tb_i = _dot_sel(bbi, tile[:, 0:cw])
        tc_r = _dot_sel(ctr, tile)
        tc_i = _dot_sel(cti, tile)
        ws_ref[0, 0, d * 2 * p:d * 2 * p + p, :] = (er[:, 0:cw] * tb_r - ei[:, 0:cw] * tb_i).astype(BF16)
        ws_ref[0, 0, d * 2 * p + p:(d + 1) * 2 * p, :] = (er[:, 0:cw] * tb_i + ei[:, 0:cw] * tb_r).astype(BF16)
        zr = er[:, cw:] * tc_r[:, 0:cw] - ei[:, cw:] * tc_i[:, 0:cw]
        zi = er[:, cw:] * tc_i[:, 0:cw] + ei[:, cw:] * tc_r[:, 0:cw]
        wo_ref[0, 0, d * 2 * p:d * 2 * p + p, :] = zr.astype(BF16)
        wo_ref[0, 0, d * 2 * p + p:(d + 1) * 2 * p, :] = (-zi).astype(BF16)
        selk = selk_ref[d]
        qr = _dot_sel(pwr, selk) * tc_r - _dot_sel(pwi, selk) * tc_i
        qi = _dot_sel(pwr, selk) * tc_i + _dot_sel(pwi, selk) * tc_r
        kt = kt + _dot_f32(bbr_t, qr) - _dot_f32(bbi_t, qi)
        are2 = arow_ref[0, 0, d, 0:1, :]
        aim2 = arow_ref[0, 0, d, 1:2, :]
        ampt = jnp.exp(are2 * step_r * float(t))
        zr2 = ampt * jnp.cos(aim2 * step_r * float(t))
        zi2 = ampt * jnp.sin(aim2 * step_r * float(t))
        sign = sign_ref[...]
        for k in range(S5_SCAN_STEPS):
            lpa_ref[0, 0, d, k:k + 1, :] = zr2
            lpb_ref[0, 0, d, k:k + 1, :] = sign * zi2
            zr2, zi2 = zr2 * zr2 - zi2 * zi2, 2.0 * zr2 * zi2
    for s in range(t):
        off = (t - 1 - s) * S5_GROUP
        w_ref[0, 0, s * S5_GROUP:(s + 1) * S5_GROUP, :] = kt[:, off:off + cw].astype(BF16)


def _s5_prep(a_re, a_im, log_step, b_re, b_im, c_re, c_im):
    depth = a_re.shape[0]
    g, p, c = S5_GROUPS, S5_STATE, S5_GROUP
    a = jnp.stack([a_re, a_im], axis=2).astype(F32)
    a = a.transpose(0, 3, 1, 2, 4)
    acol = a[..., None]
    arow = jnp.concatenate([a, a], axis=-1)
    ls = log_step.astype(F32).transpose(0, 2, 1).reshape(depth, g, 2, 1, 1)
    b = jnp.stack([b_re, b_im], axis=2).astype(F32)
    bt = b.transpose(0, 1, 2, 4, 3)
    ct = jnp.stack([c_re, c_im], axis=3).astype(F32)
    ct = ct.transpose(0, 2, 1, 3, 5, 4)
    sel, selk, tile = _s5_selectors()
    sign = jnp.concatenate([-jnp.ones((1, p), F32), jnp.ones((1, p), F32)], axis=1)
    full = lambda shp: pl.BlockSpec(shp, lambda l, j: (0,) * len(shp))
    per = lambda shp: pl.BlockSpec((1, 1) + shp, lambda l, j: (l, j) + (0,) * len(shp))
    cw = S5_CW
    return pl.pallas_call(
        _s5_prep_kernel,
        grid=(depth, g),
        in_specs=[per((2, 2, p, 1)), per((2, 2, 2 * p)), per((2, 1, 1)), per((2, p, c)), per((2, c, p)),
                  per((2, 2, p, c)), full((2, S5_LAGS, 2 * cw)), full((2, S5_LAGS, 2 * cw)),
                  full((c, 2 * cw)), full((1, 2 * p))],
        out_specs=[per((cw, cw)), per((4 * p, cw)), per((4 * p, cw)),
                   per((2, S5_SCAN_STEPS, 2 * p)), per((2, S5_SCAN_STEPS, 2 * p))],
        out_shape=[jax.ShapeDtypeStruct((depth, g, cw, cw), BF16),
                   jax.ShapeDtypeStruct((depth, g, 4 * p, cw), BF16),
                   jax.ShapeDtypeStruct((depth, g, 4 * p, cw), BF16),
                   jax.ShapeDtypeStruct((depth, g, 2, S5_SCAN_STEPS, 2 * p), F32),
                   jax.ShapeDtypeStruct((depth, g, 2, S5_SCAN_STEPS, 2 * p), F32)],
        compiler_params=_cparams(("parallel", "parallel"), 32),
        name="s5_prep",
    )(acol, arow, ls, b, bt, ct, sel, selk, tile, sign)


def _gelu_tanh(x):
    return 0.5 * x * (1.0 + jnp.tanh(math.sqrt(2.0 / math.pi) * (x + 0.044715 * (x * x * x))))


LANES = 128


BF16_ROWS = 16
S5_PERM_ROWS = BF16_ROWS * S5_CHUNK
S5_GPB = LANES // S5_GROUP


def _s5_kernel(chunk_bounds, u_ref, pick_ref, w_ref, ws_ref, wo_ref, lpa_ref, lpb_ref,
               d_ref, o_ref, cat_ref, ug_ref, yg_ref):
    t = S5_CHUNK
    g = pl.program_id(1)
    r = u_ref.shape[0] // t
    nblk = r // BF16_ROWS
    p2 = 2 * S5_STATE
    qw = S5_GPB * LANES

    @pl.when(g == 0)
    def _():
        def body(b, carry):
            crow = pl.ds(pl.multiple_of(b * BF16_ROWS, BF16_ROWS), BF16_ROWS)
            for s in range(t):
                rows = pl.ds(pl.multiple_of(b * S5_PERM_ROWS + s * BF16_ROWS, BF16_ROWS), BF16_ROWS)
                cat_ref[crow, s * LANES:(s + 1) * LANES] = u_ref[rows, :]
            return carry
        lax.fori_loop(0, nblk, body, 0)
        for q in range(t // S5_GPB):
            picked = jnp.dot(cat_ref[:, q * qw:(q + 1) * qw], pick_ref[0],
                             preferred_element_type=F32).astype(BF16)
            for gg in range(S5_GPB):
                ug_ref[gg, :, q * LANES:(q + 1) * LANES] = picked[:, gg * LANES:(gg + 1) * LANES]

    u = ug_ref[g]
    uf = u.astype(F32)
    y = jnp.dot(u, w_ref[0, 0], preferred_element_type=F32)
    st = lax.dot_general(u, ws_ref[0, 0], (((1,), (1,)), ((), ())), preferred_element_type=F32)
    ridx = lax.broadcasted_iota(jnp.int32, (r, p2), 0)
    rloc = jnp.zeros((r, p2), jnp.int32)
    rlen = jnp.zeros((r, p2), jnp.int32)
    for s0, ln in chunk_bounds:
        inside = (ridx >= s0) & (ridx < s0 + ln)
        rloc = jnp.where(inside, ridx - s0, rloc)
        rlen = jnp.where(inside, ln, rlen)
    xin = []
    for d in range(2):
        x = st[:, d * p2:(d + 1) * p2]
        for k in range(S5_SCAN_STEPS):
            sh = 1 << k
            if d == 0:
                prev = jnp.where(rloc >= sh, pltpu.roll(x, sh, 0), 0.0)
            else:
                prev = jnp.where(rloc < rlen - sh, pltpu.roll(x, r - sh, 0), 0.0)
            x = (x + lpa_ref[0, 0, d, k:k + 1, :] * prev
                 + lpb_ref[0, 0, d, k:k + 1, :] * pltpu.roll(prev, S5_STATE, 1))
        if d == 0:
            xin.append(jnp.where(rloc >= 1, pltpu.roll(x, 1, 0), 0.0))
        else:
            xin.append(jnp.where(rloc < rlen - 1, pltpu.roll(x, r - 1, 0), 0.0))
    xin = jnp.concatenate(xin, axis=1).astype(BF16)
    y = y + jnp.dot(xin, wo_ref[0, 0], preferred_element_type=F32)
    y = _gelu_tanh(y + uf * d_ref[0])

    yg_ref[g] = y.astype(BF16)

    @pl.when(g == pl.num_programs(1) - 1)
    def _():
        for q in range(t // S5_GPB):
            lhs = jnp.concatenate([yg_ref[gg, :, q * LANES:(q + 1) * LANES] for gg in range(S5_GPB)], axis=1)
            cat_ref[:, q * qw:(q + 1) * qw] = jnp.dot(lhs, pick_ref[1],
                                                      preferred_element_type=F32).astype(BF16)

        def body(b, carry):
            crow = pl.ds(pl.multiple_of(b * BF16_ROWS, BF16_ROWS), BF16_ROWS)
            for s in range(t):
                rows = pl.ds(pl.multiple_of(b * S5_PERM_ROWS + s * BF16_ROWS, BF16_ROWS), BF16_ROWS)
                o_ref[rows, :] = cat_ref[crow, s * LANES:(s + 1) * LANES]
            return carry
        lax.fori_loop(0, nblk, body, 0)


def _s5_movers():
    dst = np.arange(S5_PERM_ROWS)[:, None]
    fwd = (dst % BF16_ROWS) * S5_CHUNK + dst // BF16_ROWS == np.arange(S5_PERM_ROWS)[None, :]
    src = np.arange(S5_GPB * LANES)
    s8, g, j = src // LANES, src % LANES // S5_GROUP, src % S5_GROUP
    pick = (g * LANES + s8 * S5_GROUP + j)[:, None] == src[None, :]
    return jnp.asarray(np.stack([fwd, fwd.T]), BF16), jnp.asarray(np.stack([pick, pick.T]), BF16)


def _s5(proj, tables, li, d_tiled, chunk_bounds):
    m = proj.shape[0]
    w, ws, wo, lpa, lpb = tables
    p = S5_STATE
    cw = S5_CW
    gpb = S5_GPB
    r = m // S5_CHUNK
    _, pick = _s5_movers()
    per = lambda shp: pl.BlockSpec((1, 1) + shp, lambda b, j: (li, b * gpb + j) + (0,) * len(shp))
    return pl.pallas_call(
        functools.partial(_s5_kernel, chunk_bounds),
        grid=(S5_GROUPS // gpb, gpb),
        in_specs=[pl.BlockSpec((m, LANES), lambda b, j: (0, COL_U // LANES + b)),
                  pl.BlockSpec((2, gpb * LANES, gpb * LANES), lambda b, j: (0, 0, 0)),
                  per((cw, cw)), per((4 * p, cw)), per((4 * p, cw)),
                  per((2, S5_SCAN_STEPS, 2 * p)), per((2, S5_SCAN_STEPS, 2 * p)),
                  pl.BlockSpec((1, 1, cw), lambda b, j: (b * gpb + j, 0, 0))],
        out_specs=pl.BlockSpec((m, LANES), lambda b, j: (0, b)),
        out_shape=jax.ShapeDtypeStruct((m, S5_WIDTH), BF16),
        scratch_shapes=[pltpu.VMEM((r, S5_CHUNK * LANES), BF16), pltpu.VMEM((gpb, r, cw), BF16),
                        pltpu.VMEM((gpb, r, cw), BF16)],
        compiler_params=_cparams(("parallel", "arbitrary"), 48),
        name="s5_mix",
    )(proj, pick, w, ws, wo, lpa, lpb, d_tiled)


def _t5_bucket(rel):
    half = REL_BUCKETS // 2
    exact = half // 2
    sign = (rel > 0).astype(np.int32) * half
    n = np.abs(rel)
    large = exact + (np.log(np.maximum(n, 1) / exact) / np.log(REL_MAX_DIST / exact)
                     * (half - exact)).astype(np.int32)
    large = np.minimum(large, half - 1)
    return sign + np.where(n < exact, n, large)


def _att_bias_kernel(gi, idx_ref, tbl_ref, o_ref):
    idx = idx_ref[...]
    for h in range(ATT_HPG):
        acc = jnp.full(idx.shape, NEG_INF, F32)
        for b in range(REL_BUCKETS):
            acc = jnp.where(idx == b, tbl_ref[b, gi * ATT_HPG + h], acc)
        o_ref[h] = acc


ATT_TQ = 128


def _att_tile(seq_bounds, dil):
    return min(ATT_TQ, min(l for _, l in seq_bounds) // dil)


def _att_bias(rel_bias, gi, tq):
    dil = ATT_PATTERNS[gi][1]
    tk = tq + 2 * ATT_HALF
    rel = (np.arange(tk)[None, :] - ATT_HALF) - np.arange(tq)[:, None]
    idx = np.where(np.abs(rel) <= ATT_HALF, _t5_bucket(rel * dil), -1).astype(np.int32)
    return pl.pallas_call(
        functools.partial(_att_bias_kernel, gi),
        in_specs=[pl.BlockSpec(memory_space=pltpu.VMEM), pl.BlockSpec(memory_space=pltpu.SMEM)],
        out_specs=pl.BlockSpec(memory_space=pltpu.VMEM),
        out_shape=jax.ShapeDtypeStruct((ATT_HPG, tq, tk), F32),
        name=f"attention_bias_{gi}",
    )(jnp.asarray(idx), rel_bias.astype(F32))


def _attn_kernel(tq, blk_bounds, q_ref, kp_ref, km_ref, kn_ref, vp_ref, vm_ref, vn_ref, bias_ref,
                 o_ref, lse_ref):
    jb = pl.program_id(1)
    is_first, is_last = _seq_flags(jb, 1, blk_bounds)
    tk = tq + 2 * ATT_HALF
    flat = lambda ref: ref[...].reshape(-1, ref.shape[-1])
    colk = lax.broadcasted_iota(jnp.int32, (tq, tk), 1)
    valid = jnp.logical_and(jnp.logical_or(colk >= ATT_HALF, jnp.logical_not(is_first)),
                            jnp.logical_or(colk < tq + ATT_HALF, jnp.logical_not(is_last)))
    q = flat(q_ref)
    kcat = jnp.concatenate([flat(kp_ref), flat(km_ref), flat(kn_ref)], axis=0)
    vcat = jnp.concatenate([flat(vp_ref), flat(vm_ref), flat(vn_ref)], axis=0)
    scale = ATT_HEAD_DIM ** -0.5
    lane = lax.broadcasted_iota(jnp.int32, (tq, ATT_HEAD_DIM), 1)
    lse_tile = jnp.zeros((tq, ATT_HEAD_DIM), F32)
    outs = []
    for h in range(ATT_HPG):
        sl = slice(h * ATT_HEAD_DIM, (h + 1) * ATT_HEAD_DIM)
        s = lax.dot_general(q[:, sl], kcat[:, sl], (((1,), (1,)), ((), ())), preferred_element_type=F32)
        s = jnp.where(valid, s * scale + bias_ref[h], NEG_INF)
        mx = jnp.max(s, axis=-1, keepdims=True)
        pr = jnp.exp(s - mx)
        den = jnp.sum(pr, axis=-1, keepdims=True)
        o = jnp.dot(pr.astype(BF16), vcat[:, sl], preferred_element_type=F32)
        outs.append((o / den).astype(o_ref.dtype))
        lse_tile = jnp.where(lane == h, mx + jnp.log(den), lse_tile)
    o_ref[...] = jnp.concatenate(outs, axis=1).reshape(o_ref.shape)
    lse_ref[...] = lse_tile.reshape(lse_ref.shape)


def _attention_group(proj, bias, gi, seq_bounds):
    window, dil = ATT_PATTERNS[gi]
    assert window // (2 * dil) == ATT_HALF
    m = proj.shape[0]
    tq = bias.shape[1]
    blk = ATT_BLOCK if dil > 1 else tq
    per = blk // dil
    nbq = tq // per
    hr = min(per, ATT_HALF)
    hb = ATT_HALF // hr
    nblocks = m // blk
    assert tq % per == 0 and nbq % hb == 0 and per % hr == 0
    assert all((s // blk) % nbq == 0 and (l // blk) % nbq == 0 for s, l in seq_bounds)
    blk_bounds = tuple((s // blk // nbq, l // blk // nbq) for s, l in seq_bounds)
    tk = tq + 2 * ATT_HALF
    cb = (COL_QKV + gi * 3 * ATT_GW) // ATT_GW
    src = proj.reshape(nblocks, blk, proj.shape[1])
    lasth = nblocks // hb - 1

    def main(col):
        return pl.BlockSpec((nbq, per, ATT_GW), lambda r, jb: (jb, r, cb + col))

    def prev(col):
        return pl.BlockSpec((hb, hr, ATT_GW),
                            lambda r, jb: (jnp.maximum(jb * (nbq // hb) - 1, 0), (r + 1) * (per // hr) - 1, cb + col))

    def nxt(col):
        return pl.BlockSpec((hb, hr, ATT_GW),
                            lambda r, jb: (jnp.minimum((jb + 1) * (nbq // hb), lasth), r * (per // hr), cb + col))

    o, lse = pl.pallas_call(
        functools.partial(_attn_kernel, tq, blk_bounds),
        grid=(dil, nblocks // nbq),
        in_specs=[main(0), prev(1), main(1), nxt(1), prev(2), main(2), nxt(2),
                  pl.BlockSpec((ATT_HPG, tq, tk), lambda r, jb: (0, 0, 0))],
        out_specs=[pl.BlockSpec((nbq, per, ATT_GW), lambda r, jb: (jb, r, 0)),
                   pl.BlockSpec((nbq, per, ATT_HEAD_DIM), lambda r, jb: (jb, r, 0))],
        out_shape=[jax.ShapeDtypeStruct((nblocks, blk, ATT_GW), BF16),
                   jax.ShapeDtypeStruct((nblocks, blk, ATT_HEAD_DIM), F32)],
        compiler_params=_cparams(("parallel", "parallel"), 32),
        name=f"dilated_attention_{gi}",
    )(src, src, src, src, src, src, src, bias)
    return o.reshape(m, ATT_GW), lse.reshape(m, ATT_HEAD_DIM)


def _att_combine_kernel(o0, l0, o1, l1, o2, l2, perm_ref, out_ref):
    tr = out_ref.shape[0]
    outs, lses = [o0[...]], [l0[...]]
    for k, (o_ref, l_ref) in enumerate(((o1, l1), (o2, l2))):
        inv = perm_ref[k, 1]
        ob, lb = [], []
        for b in range(tr // ATT_BLOCK):
            rows = slice(b * ATT_BLOCK, (b + 1) * ATT_BLOCK)
            ob.append(jnp.dot(inv, o_ref[rows, :], preferred_element_type=F32))
            lb.append(_sel_dot(inv, l_ref[rows, :]))
        outs.append(jnp.concatenate(ob, axis=0))
        lses.append(jnp.concatenate(lb, axis=0))
    a, b, c = lses
    mx = jnp.maximum(jnp.maximum(a, b), c)
    ea, eb, ec = jnp.exp(a - mx), jnp.exp(b - mx), jnp.exp(c - mx)
    inv = 1.0 / (ea + eb + ec)
    wa, wb, wc = ea * inv, eb * inv, ec * inv
    for h in range(ATT_HPG):
        sl = slice(h * ATT_HEAD_DIM, (h + 1) * ATT_HEAD_DIM)
        bc = lambda w: jnp.broadcast_to(w[:, h:h + 1], (tr, ATT_HEAD_DIM))
        out_ref[:, sl] = (bc(wa) * outs[0][:, sl].astype(F32) + bc(wb) * outs[1][:, sl].astype(F32)
                          + bc(wc) * outs[2][:, sl].astype(F32)).astype(out_ref.dtype)


def _att_combine(outs, lses, tr=512):
    assert [gi for gi, _ in _att_dilated()] == [1, 2]
    m, w = outs[0].shape
    ospec = pl.BlockSpec((tr, w), lambda i: (i, 0))
    lspec = pl.BlockSpec((tr, ATT_HEAD_DIM), lambda i: (i, 0))
    args = [x for pair in zip(outs, lses) for x in pair]
    perms = _att_row_perms()
    return pl.pallas_call(
        _att_combine_kernel,
        grid=(m // tr,),
        in_specs=[ospec, lspec] * 3 + [pl.BlockSpec(perms.shape, lambda i: (0, 0, 0, 0))],
        out_specs=ospec,
        out_shape=jax.ShapeDtypeStruct((m, w), BF16),
        compiler_params=_cparams(("parallel",), 32),
        name="attention_combine",
    )(*args, perms)


PACK_TILE = 512


def _pack_cols_kernel(short_tiles, *refs):
    parts, o_ref = refs[:-1], refs[-1]
    j = pl.program_id(2)
    valid = PACK_TILE
    for lo, hi, v in short_tiles:
        valid = jnp.where(jnp.logical_and(j >= lo, j < hi), v, valid)
    val = jnp.concatenate([p[0] for p in parts], axis=1)
    col = lax.broadcasted_iota(jnp.int32, val.shape, 1)
    o_ref[0] = jnp.where(col < valid, val, 0.0).astype(o_ref.dtype)


def _pack_cols(w, n_tiles, src_lane_block, short_tiles, tk=512):
    depth, k, n = w.shape
    last = -(-n // LANES) - 1
    part = lambda q: pl.BlockSpec((1, tk, LANES), lambda l, i, j: (l, i, jnp.minimum(src_lane_block(j, q), last)))
    nparts = PACK_TILE // LANES
    return pl.pallas_call(
        functools.partial(_pack_cols_kernel, short_tiles),
        grid=(depth, k // tk, n_tiles),
        in_specs=[part(q) for q in range(nparts)],
        out_specs=pl.BlockSpec((1, tk, PACK_TILE), lambda l, i, j: (l, i, j)),
        out_shape=jax.ShapeDtypeStruct((depth, k, n_tiles * PACK_TILE), BF16),
        compiler_params=_cparams(("parallel", "parallel", "parallel"), 32),
        name="pack_weight_columns",
    )(*([w] * nparts))


def _pack_w_in_kernel(t_u, shift, a_ref, b_ref, o_ref, dt_ref):
    j = pl.program_id(2)

    @pl.when(j < t_u)
    def _():
        o_ref[...] = a_ref[...].T.astype(o_ref.dtype)

    @pl.when(j >= t_u)
    def _():
        rows = jnp.concatenate([a_ref[...], b_ref[...]], axis=0)[shift:shift + PACK_TILE, :]
        o_ref[...] = rows.T.astype(o_ref.dtype)

    @pl.when(j == t_u)
    def _():
        t = a_ref[0:LANES, :].T
        dt_ref[:, 0:LANES] = t.astype(dt_ref.dtype)
        dt_ref[:, LANES:2 * LANES] = pltpu.roll(t, LANES - SSM_HEADS, 1).astype(dt_ref.dtype)


def _pack_w_in(w_in):
    ng = len(ATT_PATTERNS)
    raw_dt = SSM_INNER + SSM_XBC
    shift = 2 * SSM_HEADS
    t_u = COL_U // PACK_TILE
    t_qkv = COL_QKV // PACK_TILE
    t_gates = COL_GATES // PACK_TILE
    assert raw_dt % PACK_TILE == 0 and COL_U == raw_dt

    def src_block(j):
        jj = jnp.clip(j - t_qkv, 0, 3 * ng - 1)
        return jnp.where(jnp.logical_and(j >= t_qkv, j < t_gates), t_qkv + (jj % 3) * ng + jj // 3, j)

    depth, d, n = w_in.shape
    tk = min(1024, d)
    last = -(-n // LANES) - 1
    sub = PACK_TILE // LANES
    w_t = jnp.swapaxes(w_in, 1, 2)
    return pl.pallas_call(
        functools.partial(_pack_w_in_kernel, t_u, shift),
        grid=(depth, d // tk, IN_COLS // PACK_TILE),
        in_specs=[pl.BlockSpec((None, PACK_TILE, tk), lambda l, i, j: (l, src_block(j), i)),
                  pl.BlockSpec((None, LANES, tk), lambda l, i, j: (l, jnp.minimum((src_block(j) + 1) * sub, last), i))],
        out_specs=[pl.BlockSpec((None, tk, PACK_TILE), lambda l, i, j: (l, i, j)),
                   pl.BlockSpec((None, tk, DT_COLS), lambda l, i, j: (l, i, 0))],
        out_shape=[jax.ShapeDtypeStruct((depth, d, IN_COLS), BF16), jax.ShapeDtypeStruct((depth, d, DT_COLS), BF16)],
        compiler_params=_cparams(("parallel", "parallel", "arbitrary"), 32),
        name="pack_w_in",
    )(w_t, w_t)


def _pack_ffn(w_up, conv_w, conv_b, w_down):
    nb = D_FF_PAD // PACK_TILE
    full, rem = divmod(D_FF, PACK_TILE)
    nparts = PACK_TILE // LANES
    assert nb == full + 1 and rem > 0 and D_FF % LANES == 0
    src = lambda j, q: jnp.where(j < nb, j * nparts + q, D_FF // LANES + (j - nb) * nparts + q)
    w_up_p = _pack_cols(w_up, 2 * nb, src, ((full, nb, rem), (nb + full, 2 * nb, rem)))
    padc = lambda a: jnp.pad(a, [(0, 0)] * (a.ndim - 1) + [(0, D_FF_PAD - D_FF)])
    both = lambda a: jnp.concatenate([padc(a[..., :D_FF]), padc(a[..., D_FF:])], axis=-1)
    w_down_p = jnp.pad(w_down, ((0, 0), (0, D_FF_PAD - D_FF), (0, 0))).astype(BF16)
    return w_up_p, both(conv_w).astype(F32), both(conv_b).astype(F32), w_down_p


def _trunk(x, seq_bounds, rel_bias, norm_mix, w_in, ssm_conv_w, ssm_conv_b, ssm_a_log, ssm_dt_bias, ssm_d,
           ssm_norm, ssm_w_out, s5_tables, s5_d, s5_w_glu, att_w_out, w_o, norm_ffn, w_up, ffn_conv_w,
           ffn_conv_b, w_down, final_norm):
    m = x.shape[0]
    depth = w_in.shape[0]
    tm = min(1024, m)
    tm2 = min(512, m)
    gate_blk = lambda b, tn: (COL_GATES + b * D_MODEL) // tn
    chunk_bounds = tuple((s // S5_CHUNK, l // S5_CHUNK) for s, l in seq_bounds)
    att_bias = [_att_bias(rel_bias, gi, _att_tile(seq_bounds, dil)) for gi, (_, dil) in enumerate(ATT_PATTERNS)]
    gated = lambda acc, gate: _sigmoid(gate.astype(F32)) * acc
    gated_add = lambda acc, gate, prev: prev + _sigmoid(gate.astype(F32)) * acc
    resid = lambda acc, res: res + acc

    w_main, w_dt = _pack_w_in(w_in)
    w_up_p, cw_p, cb_p, w_down_p = _pack_ffn(w_up, ffn_conv_w, ffn_conv_b, w_down)
    ssm_w_out, s5_w_glu, att_w_out, w_o = (w.astype(BF16) for w in (ssm_w_out, s5_w_glu, att_w_out, w_o))

    for li in range(depth):
        proj, dtraw = _in_proj(x, norm_mix[li].astype(F32), w_main, w_dt, li, ssm_conv_w[li].astype(F32),
                               ssm_conv_b[li].astype(F32), seq_bounds, tm)

        y_a = _ssd(proj, dtraw, ssm_dt_bias[li], ssm_a_log[li], ssm_d[li], ssm_norm[li].astype(F32), seq_bounds)

        d_tiled = jnp.tile(s5_d[li].astype(F32).reshape(S5_GROUPS, 1, S5_GROUP), (1, 1, S5_CHUNK))
        y_b = _s5(proj, s5_tables, li, d_tiled, chunk_bounds)

        outs, lses = zip(*[_attention_group(proj, att_bias[gi], gi, seq_bounds)
                           for gi in range(len(ATT_PATTERNS))])
        y_c = _att_combine(outs, lses, tr=tm2)

        merged = _merge_branches(y_a, y_b, y_c, ssm_w_out, s5_w_glu, att_w_out, li, proj, tm=tm, tn=512)
        x = _matmul(merged, w_o, layer=li, tm=tm, tn=512, out_dtype=F32, epilogue=resid,
                    extras=((x, 0),), name="mix_out_proj")

        act = _ffn_up(x, norm_ffn[li].astype(F32), w_up_p, li, cw_p[li], cb_p[li], seq_bounds, tm)
        x = _matmul(act, w_down_p, layer=li, tm=tm, tn=512, out_dtype=F32, epilogue=resid, extras=((x, 0),),
                    vmem_mb=56, name="ffn_down")
    return x


def kernel(x_prompt, x_sample, rel_bias, norm_mix, w_in, ssm_conv_w, ssm_conv_b, ssm_a_log, ssm_dt_bias, ssm_d, ssm_norm, ssm_w_out, s5_a_re, s5_a_im, s5_log_step, s5_b_re, s5_b_im, s5_c_re, s5_c_im, s5_d, s5_w_glu, att_w_out, w_o, norm_ffn, w_up, ffn_conv_w, ffn_conv_b, w_down, final_norm):
    d = x_prompt.shape[-1]
    seq_bounds = []
    row = 0
    for arr in (x_prompt, x_sample):
        for _ in range(arr.shape[0]):
            seq_bounds.append((row, arr.shape[1]))
            row += arr.shape[1]
    seq_bounds = tuple(seq_bounds)
    x = jnp.concatenate([x_prompt.reshape(-1, d), x_sample.reshape(-1, d)], axis=0)
    s5_tables = _s5_prep(s5_a_re, s5_a_im, s5_log_step, s5_b_re, s5_b_im, s5_c_re, s5_c_im)
    y = _trunk(x, seq_bounds, rel_bias, norm_mix, w_in, ssm_conv_w, ssm_conv_b, ssm_a_log, ssm_dt_bias, ssm_d,
               ssm_norm, ssm_w_out, s5_tables, s5_d, s5_w_glu, att_w_out, w_o, norm_ffn, w_up, ffn_conv_w,
               ffn_conv_b, w_down, final_norm)
    n_prompt = x_prompt.shape[0] * x_prompt.shape[1]
    y_prompt, y_sample = _final_norm(y, final_norm.astype(F32), n_prompt, tr=min(512, n_prompt))
    return (y_prompt.reshape(x_prompt.shape), y_sample.reshape(x_sample.shape))
```

```python
import functools
import math

import numpy as np
import jax
import jax.numpy as jnp
from jax import lax
from jax.experimental import pallas as pl
from jax.experimental.pallas import tpu as pltpu

F32 = jnp.float32
BF16 = jnp.bfloat16

D_MODEL = 2048
NORM_EPS = 1e-6
NEG_INF = -1e30

SSM_HEADDIM = 64
SSM_INNER = 1536
SSM_HEADS = 24
SSM_GROUPS = 4
SSM_HPG = SSM_HEADS // SSM_GROUPS
SSM_STATE = 128
SSM_CONV = 5
SSM_CHUNK = 128
SSM_XBC = 2560

S5_WIDTH = 1024
S5_GROUP = 16
S5_GROUPS = 64
S5_STATE = 64
S5_CHUNK = 32
S5_CW = S5_CHUNK * S5_GROUP
S5_LAGS = 128
S5_SCAN_STEPS = 8

ATT_HEAD_DIM = 128
ATT_HPG = 4
ATT_PATTERNS = ((128, 1), (512, 4), (2048, 16))
ATT_GW = ATT_HPG * ATT_HEAD_DIM
ATT_HALF = 64
REL_BUCKETS = 32
REL_MAX_DIST = 1024

D_FF = 5504
D_FF_PAD = 5632
FFN_CONV = 3

COL_Z = 0
COL_XBC = 1536
COL_U = 4096
COL_QKV = 5120
COL_GATES = 9728
IN_COLS = 15872
DT_COLS = 256

V7X_VMEM_BYTES = 64 * 1024 * 1024


def _cparams(sem, vmem_mb):
    assert vmem_mb * 1024 * 1024 <= V7X_VMEM_BYTES
    return pltpu.CompilerParams(dimension_semantics=sem, vmem_limit_bytes=vmem_mb * 1024 * 1024)


def _sigmoid(x):
    return 1.0 / (1.0 + jnp.exp(-x))


def _silu(x):
    return x * _sigmoid(x)


def _split3(x):
    hi = x.astype(BF16)
    r1 = x - hi.astype(F32)
    mid = r1.astype(BF16)
    lo = (r1 - mid.astype(F32)).astype(BF16)
    return hi, mid, lo


def _dot_sel(x, sel):
    hi, mid, _ = _split3(x)
    d = lambda a: jnp.dot(a, sel, preferred_element_type=F32)
    return d(hi) + d(mid)


def _sel_dot(sel, x):
    hi, mid, lo = _split3(x)
    d = lambda a: jnp.dot(sel, a, preferred_element_type=F32)
    return d(hi) + d(mid) + d(lo)


def _dot_f32(a, b):
    ah, am, _ = _split3(a)
    bh, bm, _ = _split3(b)
    d = lambda x, y: jnp.dot(x, y, preferred_element_type=F32)
    return d(ah, bh) + d(ah, bm) + d(am, bh)


def _final_norm_kernel(nb_first, x_ref, g_ref, first_ref, second_ref):
    x = x_ref[...]
    ms = jnp.mean(x * x, axis=-1, keepdims=True)
    y = x * lax.rsqrt(ms + NORM_EPS) * g_ref[...]
    i = pl.program_id(0)

    @pl.when(i < nb_first)
    def _():
        first_ref[...] = y

    @pl.when(i >= nb_first)
    def _():
        second_ref[...] = y


def _final_norm(x, gain, n_first, tr=512):
    m, d = x.shape
    assert n_first % tr == 0 and (m - n_first) % tr == 0 and 0 < n_first < m
    nb_first = n_first // tr
    return pl.pallas_call(
        functools.partial(_final_norm_kernel, nb_first),
        grid=(m // tr,),
        in_specs=[pl.BlockSpec((tr, d), lambda i: (i, 0)),
                  pl.BlockSpec((1, d), lambda i: (0, 0))],
        out_specs=[pl.BlockSpec((tr, d), lambda i: (jnp.minimum(i, nb_first - 1), 0)),
                   pl.BlockSpec((tr, d), lambda i: (jnp.maximum(i - nb_first, 0), 0))],
        out_shape=[jax.ShapeDtypeStruct((n_first, d), x.dtype), jax.ShapeDtypeStruct((m - n_first, d), x.dtype)],
        compiler_params=_cparams(("arbitrary",), 32),
        name="final_norm",
    )(x, gain.reshape(1, d))


def _mm_kernel(epilogue, n_extra, a_ref, b_ref, *rest):
    extras = rest[:n_extra]
    o_ref = rest[n_extra]
    acc = jnp.dot(a_ref[...], b_ref[...], preferred_element_type=F32)
    if epilogue is not None:
        acc = epilogue(acc, *[e[...] for e in extras])
    o_ref[...] = acc.astype(o_ref.dtype)


def _layer_spec(w, layer, rows, tn, col_off=0):
    if w.ndim == 2:
        return pl.BlockSpec((rows, tn), lambda i, j: (0, j + col_off))
    return pl.BlockSpec((None, rows, tn), lambda i, j: (layer, 0, j + col_off))


def _matmul(a, b, *, tm, tn, out_dtype, k=None, a_kblk=0, epilogue=None, extras=(), vmem_mb=48,
            layer=None, name="matmul"):
    m = a.shape[0]
    kk, n = b.shape[-2:]
    if k is None:
        k = a.shape[1]
    assert k == kk and m % tm == 0 and n % tn == 0
    in_specs = [pl.BlockSpec((tm, k), lambda i, j: (i, a_kblk)), _layer_spec(b, layer, k, tn)]
    args = [a, b]
    for arr, off in extras:
        in_specs.append(pl.BlockSpec((tm, tn), functools.partial(lambda i, j, o: (i, j + o), o=off)))
        args.append(arr)
    return pl.pallas_call(
        functools.partial(_mm_kernel, epilogue, len(extras)),
        grid=(m // tm, n // tn),
        in_specs=in_specs,
        out_specs=pl.BlockSpec((tm, tn), lambda i, j: (i, j)),
        out_shape=jax.ShapeDtypeStruct((m, n), out_dtype),
        compiler_params=_cparams(("parallel", "parallel"), vmem_mb),
        name=name,
    )(*args)


def _merge_kernel(ya_ref, yb_ref, yc_ref, perm_ref, wa_ref, wv_ref, wg_ref, wc_ref, g0_ref, g1_ref, g2_ref,
                  o_ref, yb_scr):
    @pl.when(pl.program_id(1) == 0)
    def _():
        for b in range(yb_ref.shape[0] // S5_PERM_ROWS):
            rows = slice(b * S5_PERM_ROWS, (b + 1) * S5_PERM_ROWS)
            yb_scr[rows, :] = jnp.dot(perm_ref[1], yb_ref[rows, :], preferred_element_type=F32).astype(BF16)

    dot = lambda a, b: jnp.dot(a, b, preferred_element_type=F32)
    sig = lambda ref: _sigmoid(ref[...].astype(F32))
    yb = yb_scr[...]
    branch_a = dot(ya_ref[...], wa_ref[...])
    branch_b = dot(yb, wv_ref[...]) * _sigmoid(dot(yb, wg_ref[...]))
    branch_c = dot(yc_ref[...], wc_ref[...])
    o_ref[...] = (sig(g0_ref) * branch_a + sig(g1_ref) * branch_b + sig(g2_ref) * branch_c).astype(o_ref.dtype)


def _merge_branches(ya, yb, yc, w_a, w_glu, w_c, layer, proj, *, tm, tn):
    m = ya.shape[0]
    n = w_a.shape[-1]
    nb = n // tn
    perm, _ = _s5_movers()
    assert tm % S5_PERM_ROWS == 0
    rows = lambda a: pl.BlockSpec((tm, a.shape[1]), lambda i, j: (i, 0))
    gate = lambda b: pl.BlockSpec((tm, tn), lambda i, j: (i, j + (COL_GATES + b * n) // tn))
    return pl.pallas_call(
        _merge_kernel,
        grid=(m // tm, nb),
        in_specs=[rows(ya), rows(yb), rows(yc), pl.BlockSpec(perm.shape, lambda i, j: (0, 0, 0)),
                  _layer_spec(w_a, layer, ya.shape[1], tn),
                  _layer_spec(w_glu, layer, yb.shape[1], tn), _layer_spec(w_glu, layer, yb.shape[1], tn, nb),
                  _layer_spec(w_c, layer, yc.shape[1], tn),
                  gate(0), gate(1), gate(2)],
        out_specs=pl.BlockSpec((tm, tn), lambda i, j: (i, j)),
        out_shape=jax.ShapeDtypeStruct((m, n), BF16),
        scratch_shapes=[pltpu.VMEM((tm, yb.shape[1]), BF16)],
        compiler_params=_cparams(("parallel", "arbitrary"), 48),
        name="merge_branches",
    )(ya, yb, yc, perm, w_a, w_glu, w_glu, w_c, proj, proj, proj)


HALO = 16


def _seq_flags(row0, nrows, seq_bounds):
    starts = [s for s, _ in seq_bounds]
    ends = [s + l for s, l in seq_bounds]
    is_start = functools.reduce(jnp.logical_or, [row0 == s for s in starts])
    is_end = functools.reduce(jnp.logical_or, [row0 + nrows == e for e in ends])
    return is_start, is_end


def _conv_taps(buf_ref, w_ref, b_ref, width, tr, cols=slice(None)):
    pad = width // 2
    acc = None
    for kk in range(width):
        term = buf_ref[pl.ds(HALO - pad + kk, tr), :] * w_ref[kk:kk + 1, cols]
        acc = term if acc is None else acc + term
    return acc + b_ref[:, cols]


def _norm_rows(x, gain):
    ms = jnp.mean(x * x, axis=-1, keepdims=True)
    return (x * lax.rsqrt(ms + NORM_EPS) * gain).astype(BF16)


def _fill_normed(hn_ref, xp_ref, xm_ref, xn_ref, gain_ref, tm, seq_bounds):
    is_start, is_end = _seq_flags(pl.program_id(0) * tm, tm, seq_bounds)
    gain = gain_ref[...]
    hn_ref[0:HALO, :] = jnp.where(is_start, 0.0, _norm_rows(xp_ref[...], gain)).astype(BF16)
    hn_ref[HALO:HALO + tm, :] = _norm_rows(xm_ref[...], gain)
    hn_ref[HALO + tm:HALO + tm + HALO, :] = jnp.where(is_end, 0.0, _norm_rows(xn_ref[...], gain)).astype(BF16)


def _row_halo_specs(tm, d, nrows):
    hb = tm // HALO
    last = nrows // HALO - 1
    return [pl.BlockSpec((HALO, d), lambda i, j: (jnp.maximum(i * hb - 1, 0), 0)),
            pl.BlockSpec((tm, d), lambda i, j: (i, 0)),
            pl.BlockSpec((HALO, d), lambda i, j: (jnp.minimum((i + 1) * hb, last), 0))]


ATT_BLOCK = 256


def _att_dilated():
    return tuple((gi, dil) for gi, (_, dil) in enumerate(ATT_PATTERNS) if dil > 1)


def _att_row_perms():
    out = []
    for _, dil in _att_dilated():
        dst = np.arange(ATT_BLOCK)[:, None]
        per = ATT_BLOCK // dil
        fwd = (dst % per) * dil + dst // per == np.arange(ATT_BLOCK)[None, :]
        out.append(np.stack([fwd, fwd.T]))
    return jnp.asarray(np.stack(out), BF16)


def _in_proj_kernel(tm, tn, seq_bounds, conv_lo, conv_hi, dil_ranges, u_range, xp_ref, xm_ref, xn_ref, gain_ref,
                    w_ref, wdt_ref, cw_ref, cb_ref, perm_ref, uperm_ref, o_ref, dt_ref, hn_ref, *acc_refs):
    j = pl.program_id(1)

    @pl.when(j == 0)
    def _():
        _fill_normed(hn_ref, xp_ref, xm_ref, xn_ref, gain_ref, tm, seq_bounds)
        dt_ref[...] = jnp.dot(hn_ref[HALO:HALO + tm, :], wdt_ref[...], preferred_element_type=F32)

    is_conv = jnp.logical_and(j >= conv_lo, j < conv_hi)

    @pl.when(is_conv)
    def _():
        part = tn // len(acc_refs)
        for c, acc_ref in enumerate(acc_refs):
            cols = slice(c * part, (c + 1) * part)
            acc_ref[...] = jnp.dot(hn_ref[...], w_ref[:, cols], preferred_element_type=F32)
            o_ref[:, cols] = _silu(_conv_taps(acc_ref, cw_ref, cb_ref, SSM_CONV, tm, cols)).astype(o_ref.dtype)

    regroup = [((lo, hi), ATT_BLOCK, functools.partial(lambda k: perm_ref[k, 0], k))
               for k, (lo, hi) in enumerate(dil_ranges)]
    regroup.append((u_range, S5_PERM_ROWS, lambda: uperm_ref[0]))
    conds = [jnp.logical_and(j >= lo, j < hi) for (lo, hi), _, _ in regroup]
    plain = jnp.logical_not(functools.reduce(jnp.logical_or, conds, is_conv))

    @pl.when(plain)
    def _():
        o_ref[...] = jnp.dot(hn_ref[HALO:HALO + tm, :], w_ref[...],
                             preferred_element_type=F32).astype(o_ref.dtype)

    for cond, (_, nrows, get_perm) in zip(conds, regroup):
        @pl.when(cond)
        def _(nrows=nrows, get_perm=get_perm):
            acc = jnp.dot(hn_ref[HALO:HALO + tm, :], w_ref[...], preferred_element_type=F32).astype(BF16)
            for b in range(tm // nrows):
                rows = slice(b * nrows, (b + 1) * nrows)
                o_ref[rows, :] = jnp.dot(get_perm(), acc[rows, :],
                                         preferred_element_type=F32).astype(o_ref.dtype)


def _in_proj(x, gain, w_main, w_dt, layer, conv_w, conv_b, seq_bounds, tm, tn=512):
    m, d = x.shape
    n = w_main.shape[-1]
    conv_lo, conv_hi = COL_XBC // tn, (COL_XBC + SSM_XBC) // tn
    cidx = lambda i, j: (0, jnp.clip(j - conv_lo, 0, conv_hi - conv_lo - 1))
    dil_ranges = tuple(((COL_QKV + gi * 3 * ATT_GW) // tn, (COL_QKV + (gi + 1) * 3 * ATT_GW) // tn)
                       for gi, _ in _att_dilated())
    u_range = (COL_U // tn, (COL_U + S5_WIDTH) // tn)
    perms = _att_row_perms()
    uperm, _ = _s5_movers()
    assert tm % S5_PERM_ROWS == 0 and tm % ATT_BLOCK == 0
    return pl.pallas_call(
        functools.partial(_in_proj_kernel, tm, tn, seq_bounds, conv_lo, conv_hi, dil_ranges, u_range),
        grid=(m // tm, n // tn),
        in_specs=_row_halo_specs(tm, d, m) + [
            pl.BlockSpec((1, d), lambda i, j: (0, 0)),
            _layer_spec(w_main, layer, d, tn),
            pl.BlockSpec((None, d, DT_COLS), lambda i, j: (layer, 0, 0)),
            pl.BlockSpec((SSM_CONV, tn), cidx),
            pl.BlockSpec((1, tn), cidx),
            pl.BlockSpec(perms.shape, lambda i, j: (0, 0, 0, 0)),
            pl.BlockSpec(uperm.shape, lambda i, j: (0, 0, 0))],
        out_specs=[pl.BlockSpec((tm, tn), lambda i, j: (i, j)),
                   pl.BlockSpec((tm, DT_COLS), lambda i, j: (i, 0))],
        out_shape=[jax.ShapeDtypeStruct((m, n), BF16), jax.ShapeDtypeStruct((m, DT_COLS), F32)],
        scratch_shapes=[pltpu.VMEM((tm + 2 * HALO, d), BF16), pltpu.VMEM((tm + 2 * HALO, tn // 2), F32),
                        pltpu.VMEM((tm + 2 * HALO, tn // 2), F32)],
        compiler_params=_cparams(("parallel", "arbitrary"), 56),
        name="in_proj",
    )(x, x, x, gain.reshape(1, d), w_main, w_dt, conv_w, conv_b.reshape(1, -1), perms, uperm)


def _ffn_up_kernel(tm, seq_bounds, xp_ref, xm_ref, xn_ref, gain_ref, wg_ref, wv_ref, cwg_ref, cbg_ref,
                   cwv_ref, cbv_ref, o_ref, hn_ref, sg_ref, sv_ref):
    @pl.when(pl.program_id(1) == 0)
    def _():
        _fill_normed(hn_ref, xp_ref, xm_ref, xn_ref, gain_ref, tm, seq_bounds)

    hn = hn_ref[...]
    sg_ref[...] = jnp.dot(hn, wg_ref[...], preferred_element_type=F32)
    sv_ref[...] = jnp.dot(hn, wv_ref[...], preferred_element_type=F32)
    gate = _conv_taps(sg_ref, cwg_ref, cbg_ref, FFN_CONV, tm)
    val = _conv_taps(sv_ref, cwv_ref, cbv_ref, FFN_CONV, tm)
    o_ref[...] = (_silu(gate) * val).astype(o_ref.dtype)


def _ffn_up(x, gain, w, layer, conv_w, conv_b, seq_bounds, tm, tn=512):
    m, d = x.shape
    nb = D_FF_PAD // tn
    col = lambda rows, off: pl.BlockSpec((rows, tn), functools.partial(lambda i, j, o: (0, j + o), o=off))
    cb2 = conv_b.reshape(1, -1)
    return pl.pallas_call(
        functools.partial(_ffn_up_kernel, tm, seq_bounds),
        grid=(m // tm, nb),
        in_specs=_row_halo_specs(tm, d, m) + [
            pl.BlockSpec((1, d), lambda i, j: (0, 0)),
            _layer_spec(w, layer, d, tn), _layer_spec(w, layer, d, tn, nb),
            col(FFN_CONV, 0), col(1, 0), col(FFN_CONV, nb), col(1, nb)],
        out_specs=pl.BlockSpec((tm, tn), lambda i, j: (i, j)),
        out_shape=jax.ShapeDtypeStruct((m, D_FF_PAD), BF16),
        scratch_shapes=[pltpu.VMEM((tm + 2 * HALO, d), BF16), pltpu.VMEM((tm + 2 * HALO, tn), F32),
                        pltpu.VMEM((tm + 2 * HALO, tn), F32)],
        compiler_params=_cparams(("parallel", "arbitrary"), 56),
        name="ffn_up",
    )(x, x, x, gain.reshape(1, d), w, w, conv_w, cb2, conv_w, cb2)


def _softplus(x):
    return jnp.maximum(x, 0.0) + jnp.log1p(jnp.exp(-jnp.abs(x)))


def _ssd_cumsums(dt_ref, bias_ref, alog_ref, d, row0=0):
    t = SSM_CHUNK
    dt = _softplus(dt_ref[row0:row0 + t, d * 128:(d + 1) * 128] + bias_ref[d])
    da = dt * (-jnp.exp(alog_ref[d]) * math.log2(math.e))
    row = lax.broadcasted_iota(jnp.int32, (t, t), 0)
    col = lax.broadcasted_iota(jnp.int32, (t, t), 1)
    tri = jnp.where((col <= row) if d == 0 else (col >= row), 1.0, 0.0).astype(BF16)
    return dt, _sel_dot(tri, da)


SSD_SCAN_CHUNKS = 2


def _ssd_state_kernel(nsteps, seq_bounds, xf_ref, bf_ref, dtf_ref, xb_ref, bb_ref, dtb_ref, bias_ref, alog_ref,
                      hf_ref, hb_ref, h_ref):
    t = SSM_CHUNK
    n = SSM_STATE
    p = SSM_HEADDIM
    kc = SSD_SCAN_CHUNKS
    i = pl.program_id(0)

    @pl.when(i == 0)
    def _():
        h_ref[...] = jnp.zeros(h_ref.shape, F32)

    for d, (x_ref, b_ref, dt_ref, out_ref) in enumerate(((xf_ref, bf_ref, dtf_ref, hf_ref),
                                                         (xb_ref, bb_ref, dtb_ref, hb_ref))):
        blk = i if d == 0 else nsteps - 1 - i
        contrib = []
        for c in range(kc):
            r0 = c * t
            dt, cs = _ssd_cumsums(dt_ref, bias_ref, alog_ref, d, r0)
            tot = cs[t - 1:t, :] if d == 0 else cs[0:1, :]
            w = jnp.exp2(tot - cs) * dt
            etot = jnp.exp2(tot)
            per_group = []
            for g in range(SSM_GROUPS):
                bg = b_ref[r0:r0 + t, g * n:(g + 1) * n]
                xw, dec = [], []
                for hh in range(SSM_HPG):
                    h = g * SSM_HPG + hh
                    xh = x_ref[r0:r0 + t, h * p:(h + 1) * p].astype(F32)
                    xw.append((xh * jnp.broadcast_to(w[:, h:h + 1], (t, p))).astype(BF16))
                    dec.append(jnp.broadcast_to(etot[:, h:h + 1], (n, p)))
                st = lax.dot_general(bg, jnp.concatenate(xw, axis=1), (((0,), (0,)), ((), ())),
                                     preferred_element_type=F32)
                per_group.append((st, jnp.concatenate(dec, axis=1)))
            contrib.append(per_group)
        for c in (range(kc) if d == 0 else reversed(range(kc))):
            is_start, is_end = _seq_flags((blk * kc + c) * t, t, seq_bounds)
            reset = is_start if d == 0 else is_end
            for g in range(SSM_GROUPS):
                st, dec = contrib[c][g]
                hin = jnp.where(reset, 0.0, h_ref[d, g])
                out_ref[c, g] = hin.astype(BF16)
                h_ref[d, g] = hin * dec + st


def _ssd_states(proj, dtraw, dt_bias, a_log, seq_bounds):
    m = proj.shape[0]
    t = SSM_CHUNK
    kc = SSD_SCAN_CHUNKS
    nchunks = m // t
    nsteps = nchunks // kc
    hw = SSM_HPG * SSM_HEADDIM
    bw = SSM_GROUPS * SSM_STATE
    fwd = lambda i: i
    bwd = lambda i: nsteps - 1 - i
    xspec = lambda f: pl.BlockSpec((kc * t, SSM_INNER), lambda i: (f(i), COL_XBC // SSM_INNER))
    bspec = lambda f: pl.BlockSpec((kc * t, bw), lambda i: (f(i), (COL_XBC + SSM_INNER) // bw))
    dspec = lambda f: pl.BlockSpec((kc * t, DT_COLS), lambda i: (f(i), 0))
    hspec = lambda f: pl.BlockSpec((kc, SSM_GROUPS, SSM_STATE, hw), lambda i: (f(i), 0, 0, 0))
    const = pl.BlockSpec((2, 1, 128), lambda i: (0, 0, 0))
    return pl.pallas_call(
        functools.partial(_ssd_state_kernel, nsteps, seq_bounds),
        grid=(nsteps,),
        in_specs=[xspec(fwd), bspec(fwd), dspec(fwd), xspec(bwd), bspec(bwd), dspec(bwd), const, const],
        out_specs=[hspec(fwd), hspec(bwd)],
        out_shape=[jax.ShapeDtypeStruct((nchunks, SSM_GROUPS, SSM_STATE, hw), BF16)] * 2,
        scratch_shapes=[pltpu.VMEM((2, SSM_GROUPS, SSM_STATE, hw), F32)],
        compiler_params=_cparams(("arbitrary",), 32),
        name="ssd_states",
    )(proj, proj, dtraw, proj, proj, dtraw, dt_bias, a_log)


def _ssd_out_kernel(x_ref, b_ref, c_ref, dt_ref, bias_ref, alog_ref, hf_ref, hb_ref, z_ref, dskip_ref,
                    gain_ref, o_ref, y_ref):
    t = SSM_CHUNK
    n = SSM_STATE
    p = SSM_HEADDIM
    dtf, csf = _ssd_cumsums(dt_ref, bias_ref, alog_ref, 0)
    dtb, csb = _ssd_cumsums(dt_ref, bias_ref, alog_ref, 1)
    csf_t, csb_t, dtf_t, dtb_t = csf.T, csb.T, dtf.T, dtb.T
    dts_t = dtf_t + dtb_t
    row = lax.broadcasted_iota(jnp.int32, (t, t), 0)
    col = lax.broadcasted_iota(jnp.int32, (t, t), 1)
    low = col <= row
    low_strict = col < row
    up_strict = col > row
    for g in range(SSM_GROUPS):
        bg = b_ref[:, g * n:(g + 1) * n]
        cg = c_ref[:, g * n:(g + 1) * n]
        cb = lax.dot_general(cg, bg, (((1,), (1,)), ((), ())), preferred_element_type=F32)
        cg_f = cg.astype(F32)
        for hh in range(SSM_HPG):
            h = g * SSM_HPG + hh
            colf = jnp.broadcast_to(csf[:, h:h + 1], (t, t))
            colb = jnp.broadcast_to(csb[:, h:h + 1], (t, t))
            seg = jnp.where(low, colf - jnp.broadcast_to(csf_t[h:h + 1, :], (t, t)),
                            colb - jnp.broadcast_to(csb_t[h:h + 1, :], (t, t)))
            coef = jnp.where(low_strict, jnp.broadcast_to(dtf_t[h:h + 1, :], (t, t)),
                             jnp.where(up_strict, jnp.broadcast_to(dtb_t[h:h + 1, :], (t, t)),
                                       jnp.broadcast_to(dts_t[h:h + 1, :], (t, t))))
            mh = jnp.exp2(seg) * cb * coef
            lhs = jnp.concatenate([mh, cg_f * jnp.exp2(colf), cg_f * jnp.exp2(colb)], axis=1).astype(BF16)
            rhs = jnp.concatenate([x_ref[:, h * p:(h + 1) * p],
                                   hf_ref[0, g, :, hh * p:(hh + 1) * p],
                                   hb_ref[0, g, :, hh * p:(hh + 1) * p]], axis=0)
            y_ref[:, h * p:(h + 1) * p] = jnp.dot(lhs, rhs, preferred_element_type=F32)
    y = y_ref[...] + x_ref[...].astype(F32) * dskip_ref[...]
    y = y * _silu(z_ref[...].astype(F32))
    ms = jnp.mean(y * y, axis=-1, keepdims=True)
    o_ref[...] = (y * lax.rsqrt(ms + NORM_EPS) * gain_ref[...]).astype(o_ref.dtype)


def _ssd_out(proj, dtraw, dt_bias, a_log, h_f, h_b, d_skip, norm_g):
    m = proj.shape[0]
    t = SSM_CHUNK
    w = SSM_INNER
    hw = SSM_HPG * SSM_HEADDIM
    bw = SSM_GROUPS * SSM_STATE
    hspec = pl.BlockSpec((1, SSM_GROUPS, SSM_STATE, hw), lambda i: (i, 0, 0, 0))
    return pl.pallas_call(
        _ssd_out_kernel,
        grid=(m // t,),
        in_specs=[pl.BlockSpec((t, w), lambda i: (i, COL_XBC // w)),
                  pl.BlockSpec((t, bw), lambda i: (i, (COL_XBC + w) // bw)),
                  pl.BlockSpec((t, bw), lambda i: (i, (COL_XBC + w) // bw + 1)),
                  pl.BlockSpec((t, DT_COLS), lambda i: (i, 0)),
                  pl.BlockSpec((2, 1, 128), lambda i: (0, 0, 0)),
                  pl.BlockSpec((2, 1, 128), lambda i: (0, 0, 0)),
                  hspec, hspec,
                  pl.BlockSpec((t, w), lambda i: (i, COL_Z // w)),
                  pl.BlockSpec((1, w), lambda i: (0, 0)),
                  pl.BlockSpec((1, w), lambda i: (0, 0))],
        out_specs=pl.BlockSpec((t, w), lambda i: (i, 0)),
        out_shape=jax.ShapeDtypeStruct((m, w), BF16),
        scratch_shapes=[pltpu.VMEM((t, w), F32)],
        compiler_params=_cparams(("parallel",), 32),
        name="ssd_out",
    )(proj, proj, proj, dtraw, dt_bias, a_log, h_f, h_b, proj,
      jnp.repeat(d_skip.astype(F32), SSM_HEADDIM).reshape(1, w), norm_g.reshape(1, w))


def _ssd(proj, dtraw, dt_bias, a_log, d_skip, norm_g, seq_bounds):
    pad = lambda v: jnp.pad(v.astype(F32), ((0, 0), (0, 128 - SSM_HEADS))).reshape(2, 1, 128)
    dt_bias, a_log = pad(dt_bias), pad(a_log)
    h_f, h_b = _ssd_states(proj, dtraw, dt_bias, a_log, seq_bounds)
    return _ssd_out(proj, dtraw, dt_bias, a_log, h_f, h_b, d_skip, norm_g)


def _s5_selectors():
    t, c = S5_CHUNK, S5_GROUP
    lag = np.arange(S5_LAGS)[:, None]
    tok = (np.arange(S5_CW) // c)[None, :]
    e_exit_f = (lag == t - 1 - tok)
    e_exit_b = (lag == tok)
    e_in_f = (lag == tok + 1)
    e_in_b = (lag == t - tok)
    lagidx = (np.arange(2 * S5_CW) // c)[None, :]
    rel = lagidx - (t - 1)
    inb = lagidx <= 2 * t - 2
    e_k_f = (lag == rel) & (rel >= 0) & inb
    e_k_b = (lag == -rel) & (rel <= 0) & inb
    tile = (np.arange(c)[:, None] == (np.arange(2 * S5_CW) % c)[None, :])
    sel = np.stack([np.concatenate([a, b], axis=1) for a, b in
                    ((e_exit_f, e_in_f), (e_exit_b, e_in_b))])
    selk = np.stack([e_k_f, e_k_b])
    return (jnp.asarray(sel, BF16), jnp.asarray(selk, BF16), jnp.asarray(tile, BF16))


def _s5_prep_kernel(acol_ref, arow_ref, ls_ref, b_ref, bt_ref, ct_ref, sel_ref, selk_ref, tile_ref,
                    sign_ref, w_ref, ws_ref, wo_ref, lpa_ref, lpb_ref):
    t = S5_CHUNK
    cw = S5_CW
    p = S5_STATE
    tile = tile_ref[...]
    lagf = lax.broadcasted_iota(jnp.int32, (p, S5_LAGS), 1).astype(F32)
    kt = jnp.zeros((S5_GROUP, 2 * cw), F32)
    for d in range(2):
        step = jnp.exp(ls_ref[0, 0, d])
        are = acol_ref[0, 0, d, 0]
        aim = acol_ref[0, 0, d, 1]
        mag = are * step
        th = aim * step
        amp = jnp.exp(lagf * mag)
        pwr = amp * jnp.cos(lagf * th)
        pwi = amp * jnp.sin(lagf * th)
        lbr = jnp.exp(mag) * jnp.cos(th)
        lbi = jnp.exp(mag) * jnp.sin(th)
        den = are * are + aim * aim
        cfr = ((lbr - 1.0) * are + lbi * aim) / den
        cfi = (lbi * are - (lbr - 1.0) * aim) / den
        bre = b_ref[0, 0, 0]
        bim = b_ref[0, 0, 1]
        bbr = cfr * bre - cfi * bim
        bbi = cfr * bim + cfi * bre
        step_r = step
        are_r = arow_ref[0, 0, d, 0:1, 0:p]
        aim_r = arow_ref[0, 0, d, 1:2, 0:p]
        mag_r = are_r * step_r
        th_r = aim_r * step_r
        lbr_r = jnp.exp(mag_r) * jnp.cos(th_r)
        lbi_r = jnp.exp(mag_r) * jnp.sin(th_r)
        den_r = are_r * are_r + aim_r * aim_r
        cfr_r = ((lbr_r - 1.0) * are_r + lbi_r * aim_r) / den_r
        cfi_r = (lbi_r * are_r - (lbr_r - 1.0) * aim_r) / den_r
        btr = bt_ref[0, 0, 0]
        bti = bt_ref[0, 0, 1]
        bbr_t = cfr_r * btr - cfi_r * bti
        bbi_t = cfr_r * bti + cfi_r * btr
        ctr = ct_ref[0, 0, d, 0]
        cti = ct_ref[0, 0, d, 1]

        sel = sel_ref[d]
        er = _dot_sel(pwr, sel)
        ei = _dot_sel(pwi, sel)
        tb_r = _dot_sel(bbr, tile[:, 0:cw])
        tb_i = _dot_sel(bbi, tile[:, 0:cw])
        tc_r = _dot_sel(ctr, tile)
        tc_i = _dot_sel(cti, tile)
        ws_ref[0, 0, d * 2 * p:d * 2 * p + p, :] = (er[:, 0:cw] * tb_r - ei[:, 0:cw] * tb_i).astype(BF16)
        ws_ref[0, 0, d * 2 * p + p:(d + 1) * 2 * p, :] = (er[:, 0:cw] * tb_i + ei[:, 0:cw] * tb_r).astype(BF16)
        zr = er[:, cw:] * tc_r[:, 0:cw] - ei[:, cw:] * tc_i[:, 0:cw]
        zi = er[:, cw:] * tc_i[:, 0:cw] + ei[:, cw:] * tc_r[:, 0:cw]
        wo_ref[0, 0, d * 2 * p:d * 2 * p + p, :] = zr.astype(BF16)
        wo_ref[0, 0, d * 2 * p + p:(d + 1) * 2 * p, :] = (-zi).astype(BF16)
        selk = selk_ref[d]
        kr, ki = _dot_sel(pwr, selk), _dot_sel(pwi, selk)
        qr = kr * tc_r - ki * tc_i
        qi = kr * tc_i + ki * tc_r
        kt = kt + _dot_f32(bbr_t, qr) - _dot_f32(bbi_t, qi)
        are2 = arow_ref[0, 0, d, 0:1, :]
        aim2 = arow_ref[0, 0, d, 1:2, :]
        ampt = jnp.exp(are2 * step_r * float(t))
        zr2 = ampt * jnp.cos(aim2 * step_r * float(t))
        zi2 = ampt * jnp.sin(aim2 * step_r * float(t))
        sign = sign_ref[...]
        for k in range(S5_SCAN_STEPS):
            lpa_ref[0, 0, d, k:k + 1, :] = zr2
            lpb_ref[0, 0, d, k:k + 1, :] = sign * zi2
            zr2, zi2 = zr2 * zr2 - zi2 * zi2, 2.0 * zr2 * zi2
    for s in range(t):
        off = (t - 1 - s) * S5_GROUP
        w_ref[0, 0, s * S5_GROUP:(s + 1) * S5_GROUP, :] = kt[:, off:off + cw].astype(BF16)


def _s5_prep(a_re, a_im, log_step, b_re, b_im, c_re, c_im):
    depth = a_re.shape[0]
    g, p, c = S5_GROUPS, S5_STATE, S5_GROUP
    a = jnp.stack([a_re, a_im], axis=2).astype(F32)
    a = a.transpose(0, 3, 1, 2, 4)
    acol = a[..., None]
    arow = jnp.concatenate([a, a], axis=-1)
    ls = log_step.astype(F32).transpose(0, 2, 1).reshape(depth, g, 2, 1, 1)
    b = jnp.stack([b_re, b_im], axis=2).astype(F32)
    bt = b.transpose(0, 1, 2, 4, 3)
    ct = jnp.stack([c_re, c_im], axis=3).astype(F32)
    ct = ct.transpose(0, 2, 1, 3, 5, 4)
    sel, selk, tile = _s5_selectors()
    sign = jnp.concatenate([-jnp.ones((1, p), F32), jnp.ones((1, p), F32)], axis=1)
    full = lambda shp: pl.BlockSpec(shp, lambda l, j: (0,) * len(shp))
    per = lambda shp: pl.BlockSpec((1, 1) + shp, lambda l, j: (l, j) + (0,) * len(shp))
    cw = S5_CW
    return pl.pallas_call(
        _s5_prep_kernel,
        grid=(depth, g),
        in_specs=[per((2, 2, p, 1)), per((2, 2, 2 * p)), per((2, 1, 1)), per((2, p, c)), per((2, c, p)),
                  per((2, 2, p, c)), full((2, S5_LAGS, 2 * cw)), full((2, S5_LAGS, 2 * cw)),
                  full((c, 2 * cw)), full((1, 2 * p))],
        out_specs=[per((cw, cw)), per((4 * p, cw)), per((4 * p, cw)),
                   per((2, S5_SCAN_STEPS, 2 * p)), per((2, S5_SCAN_STEPS, 2 * p))],
        out_shape=[jax.ShapeDtypeStruct((depth, g, cw, cw), BF16),
                   jax.ShapeDtypeStruct((depth, g, 4 * p, cw), BF16),
                   jax.ShapeDtypeStruct((depth, g, 4 * p, cw), BF16),
                   jax.ShapeDtypeStruct((depth, g, 2, S5_SCAN_STEPS, 2 * p), F32),
                   jax.ShapeDtypeStruct((depth, g, 2, S5_SCAN_STEPS, 2 * p), F32)],
        compiler_params=_cparams(("parallel", "parallel"), 32),
        name="s5_prep",
    )(acol, arow, ls, b, bt, ct, sel, selk, tile, sign)


def _gelu_tanh(x):
    return 0.5 * x * (1.0 + jnp.tanh(math.sqrt(2.0 / math.pi) * (x + 0.044715 * (x * x * x))))


LANES = 128


BF16_ROWS = 16
S5_PERM_ROWS = BF16_ROWS * S5_CHUNK
S5_GPB = LANES // S5_GROUP


def _s5_kernel(chunk_bounds, u_ref, pick_ref, w_ref, ws_ref, wo_ref, lpa_ref, lpb_ref,
               d_ref, o_ref, cat_ref, ug_ref, yg_ref):
    t = S5_CHUNK
    g = pl.program_id(1)
    r = u_ref.shape[0] // t
    nblk = r // BF16_ROWS
    p2 = 2 * S5_STATE
    qw = S5_GPB * LANES

    @pl.when(g == 0)
    def _():
        def body(b, carry):
            crow = pl.ds(pl.multiple_of(b * BF16_ROWS, BF16_ROWS), BF16_ROWS)
            for s in range(t):
                rows = pl.ds(pl.multiple_of(b * S5_PERM_ROWS + s * BF16_ROWS, BF16_ROWS), BF16_ROWS)
                cat_ref[crow, s * LANES:(s + 1) * LANES] = u_ref[rows, :]
            return carry
        lax.fori_loop(0, nblk, body, 0)
        for q in range(t // S5_GPB):
            picked = jnp.dot(cat_ref[:, q * qw:(q + 1) * qw], pick_ref[0],
                             preferred_element_type=F32).astype(BF16)
            for gg in range(S5_GPB):
                ug_ref[gg, :, q * LANES:(q + 1) * LANES] = picked[:, gg * LANES:(gg + 1) * LANES]

    u = ug_ref[g]
    uf = u.astype(F32)
    y = jnp.dot(u, w_ref[0, 0], preferred_element_type=F32)
    st = lax.dot_general(u, ws_ref[0, 0], (((1,), (1,)), ((), ())), preferred_element_type=F32)
    ridx = lax.broadcasted_iota(jnp.int32, (r, p2), 0)
    rloc = jnp.zeros((r, p2), jnp.int32)
    rlen = jnp.zeros((r, p2), jnp.int32)
    for s0, ln in chunk_bounds:
        inside = (ridx >= s0) & (ridx < s0 + ln)
        rloc = jnp.where(inside, ridx - s0, rloc)
        rlen = jnp.where(inside, ln, rlen)
    xin = []
    for d in range(2):
        x = st[:, d * p2:(d + 1) * p2]
        for k in range(S5_SCAN_STEPS):
            sh = 1 << k
            if d == 0:
                prev = jnp.where(rloc >= sh, pltpu.roll(x, sh, 0), 0.0)
            else:
                prev = jnp.where(rloc < rlen - sh, pltpu.roll(x, r - sh, 0), 0.0)
            x = (x + lpa_ref[0, 0, d, k:k + 1, :] * prev
                 + lpb_ref[0, 0, d, k:k + 1, :] * pltpu.roll(prev, S5_STATE, 1))
        if d == 0:
            xin.append(jnp.where(rloc >= 1, pltpu.roll(x, 1, 0), 0.0))
        else:
            xin.append(jnp.where(rloc < rlen - 1, pltpu.roll(x, r - 1, 0), 0.0))
    xin = jnp.concatenate(xin, axis=1).astype(BF16)
    y = y + jnp.dot(xin, wo_ref[0, 0], preferred_element_type=F32)
    y = _gelu_tanh(y + uf * d_ref[0])

    yg_ref[g] = y.astype(BF16)

    @pl.when(g == pl.num_programs(1) - 1)
    def _():
        for q in range(t // S5_GPB):
            lhs = jnp.concatenate([yg_ref[gg, :, q * LANES:(q + 1) * LANES] for gg in range(S5_GPB)], axis=1)
            cat_ref[:, q * qw:(q + 1) * qw] = jnp.dot(lhs, pick_ref[1],
                                                      preferred_element_type=F32).astype(BF16)

        def body(b, carry):
            crow = pl.ds(pl.multiple_of(b * BF16_ROWS, BF16_ROWS), BF16_ROWS)
            for s in range(t):
                rows = pl.ds(pl.multiple_of(b * S5_PERM_ROWS + s * BF16_ROWS, BF16_ROWS), BF16_ROWS)
                o_ref[rows, :] = cat_ref[crow, s * LANES:(s + 1) * LANES]
            return carry
        lax.fori_loop(0, nblk, body, 0)


def _s5_movers():
    dst = np.arange(S5_PERM_ROWS)[:, None]
    fwd = (dst % BF16_ROWS) * S5_CHUNK + dst // BF16_ROWS == np.arange(S5_PERM_ROWS)[None, :]
    src = np.arange(S5_GPB * LANES)
    s8, g, j = src // LANES, src % LANES // S5_GROUP, src % S5_GROUP
    pick = (g * LANES + s8 * S5_GROUP + j)[:, None] == src[None, :]
    return jnp.asarray(np.stack([fwd, fwd.T]), BF16), jnp.asarray(np.stack([pick, pick.T]), BF16)


def _s5(proj, tables, li, d_tiled, chunk_bounds):
    m = proj.shape[0]
    w, ws, wo, lpa, lpb = tables
    p = S5_STATE
    cw = S5_CW
    gpb = S5_GPB
    r = m // S5_CHUNK
    _, pick = _s5_movers()
    per = lambda shp: pl.BlockSpec((1, 1) + shp, lambda b, j: (li, b * gpb + j) + (0,) * len(shp))
    return pl.pallas_call(
        functools.partial(_s5_kernel, chunk_bounds),
        grid=(S5_GROUPS // gpb, gpb),
        in_specs=[pl.BlockSpec((m, LANES), lambda b, j: (0, COL_U // LANES + b)),
                  pl.BlockSpec((2, gpb * LANES, gpb * LANES), lambda b, j: (0, 0, 0)),
                  per((cw, cw)), per((4 * p, cw)), per((4 * p, cw)),
                  per((2, S5_SCAN_STEPS, 2 * p)), per((2, S5_SCAN_STEPS, 2 * p)),
                  pl.BlockSpec((1, 1, cw), lambda b, j: (b * gpb + j, 0, 0))],
        out_specs=pl.BlockSpec((m, LANES), lambda b, j: (0, b)),
        out_shape=jax.ShapeDtypeStruct((m, S5_WIDTH), BF16),
        scratch_shapes=[pltpu.VMEM((r, S5_CHUNK * LANES), BF16), pltpu.VMEM((gpb, r, cw), BF16),
                        pltpu.VMEM((gpb, r, cw), BF16)],
        compiler_params=_cparams(("parallel", "arbitrary"), 48),
        name="s5_mix",
    )(proj, pick, w, ws, wo, lpa, lpb, d_tiled)


def _t5_bucket(rel):
    half = REL_BUCKETS // 2
    exact = half // 2
    sign = (rel > 0).astype(np.int32) * half
    n = np.abs(rel)
    large = exact + (np.log(np.maximum(n, 1) / exact) / np.log(REL_MAX_DIST / exact)
                     * (half - exact)).astype(np.int32)
    large = np.minimum(large, half - 1)
    return sign + np.where(n < exact, n, large)


def _att_bias_kernel(gi, idx_ref, tbl_ref, o_ref):
    idx = idx_ref[...]
    for h in range(ATT_HPG):
        acc = jnp.full(idx.shape, NEG_INF, F32)
        for b in range(REL_BUCKETS):
            acc = jnp.where(idx == b, tbl_ref[b, gi * ATT_HPG + h], acc)
        o_ref[h] = acc


ATT_TQ = 256


def _att_tile(seq_bounds, dil):
    return min(ATT_TQ, min(l for _, l in seq_bounds) // dil)


def _att_bias(rel_bias, gi, tq):
    dil = ATT_PATTERNS[gi][1]
    tk = tq + 2 * ATT_HALF
    rel = (np.arange(tk)[None, :] - ATT_HALF) - np.arange(tq)[:, None]
    idx = np.where(np.abs(rel) <= ATT_HALF, _t5_bucket(rel * dil), -1).astype(np.int32)
    return pl.pallas_call(
        functools.partial(_att_bias_kernel, gi),
        in_specs=[pl.BlockSpec(memory_space=pltpu.VMEM), pl.BlockSpec(memory_space=pltpu.SMEM)],
        out_specs=pl.BlockSpec(memory_space=pltpu.VMEM),
        out_shape=jax.ShapeDtypeStruct((ATT_HPG, tq, tk), F32),
        name=f"attention_bias_{gi}",
    )(jnp.asarray(idx), rel_bias.astype(F32))


def _attn_kernel(tq, blk_bounds, q_ref, kp_ref, km_ref, kn_ref, vp_ref, vm_ref, vn_ref, bias_ref,
                 o_ref, lse_ref):
    jb = pl.program_id(1)
    is_first, is_last = _seq_flags(jb, 1, blk_bounds)
    tk = tq + 2 * ATT_HALF
    flat = lambda ref: ref[...].reshape(-1, ref.shape[-1])
    colk = lax.broadcasted_iota(jnp.int32, (tq, tk), 1)
    valid = jnp.logical_and(jnp.logical_or(colk >= ATT_HALF, jnp.logical_not(is_first)),
                            jnp.logical_or(colk < tq + ATT_HALF, jnp.logical_not(is_last)))
    q = flat(q_ref)
    kcat = jnp.concatenate([flat(kp_ref), flat(km_ref), flat(kn_ref)], axis=0)
    vcat = jnp.concatenate([flat(vp_ref), flat(vm_ref), flat(vn_ref)], axis=0)
    scale = ATT_HEAD_DIM ** -0.5
    lane = lax.broadcasted_iota(jnp.int32, (tq, ATT_HEAD_DIM), 1)
    lse_tile = jnp.zeros((tq, ATT_HEAD_DIM), F32)
    outs = []
    for h in range(ATT_HPG):
        sl = slice(h * ATT_HEAD_DIM, (h + 1) * ATT_HEAD_DIM)
        s = lax.dot_general(q[:, sl], kcat[:, sl], (((1,), (1,)), ((), ())), preferred_element_type=F32)
        s = jnp.where(valid, s * scale + bias_ref[h], NEG_INF)
        mx = jnp.max(s, axis=-1, keepdims=True)
        pr = jnp.exp(s - mx)
        den = jnp.sum(pr, axis=-1, keepdims=True)
        o = jnp.dot(pr.astype(BF16), vcat[:, sl], preferred_element_type=F32)
        outs.append((o / den).astype(o_ref.dtype))
        lse_tile = jnp.where(lane == h, mx + jnp.log(den), lse_tile)
    o_ref[...] = jnp.concatenate(outs, axis=1).reshape(o_ref.shape)
    lse_ref[...] = lse_tile.reshape(lse_ref.shape)


def _attention_group(proj, bias, gi, seq_bounds):
    window, dil = ATT_PATTERNS[gi]
    assert window // (2 * dil) == ATT_HALF
    m = proj.shape[0]
    tq = bias.shape[1]
    blk = ATT_BLOCK if dil > 1 else tq
    per = blk // dil
    nbq = tq // per
    hr = min(per, ATT_HALF)
    hb = ATT_HALF // hr
    nblocks = m // blk
    assert tq % per == 0 and nbq % hb == 0 and per % hr == 0
    assert all((s // blk) % nbq == 0 and (l // blk) % nbq == 0 for s, l in seq_bounds)
    blk_bounds = tuple((s // blk // nbq, l // blk // nbq) for s, l in seq_bounds)
    tk = tq + 2 * ATT_HALF
    cb = (COL_QKV + gi * 3 * ATT_GW) // ATT_GW
    src = proj.reshape(nblocks, blk, proj.shape[1])
    lasth = nblocks // hb - 1

    def main(col):
        return pl.BlockSpec((nbq, per, ATT_GW), lambda r, jb: (jb, r, cb + col))

    def prev(col):
        return pl.BlockSpec((hb, hr, ATT_GW),
                            lambda r, jb: (jnp.maximum(jb * (nbq // hb) - 1, 0), (r + 1) * (per // hr) - 1, cb + col))

    def nxt(col):
        return pl.BlockSpec((hb, hr, ATT_GW),
                            lambda r, jb: (jnp.minimum((jb + 1) * (nbq // hb), lasth), r * (per // hr), cb + col))

    o, lse = pl.pallas_call(
        functools.partial(_attn_kernel, tq, blk_bounds),
        grid=(dil, nblocks // nbq),
        in_specs=[main(0), prev(1), main(1), nxt(1), prev(2), main(2), nxt(2),
                  pl.BlockSpec((ATT_HPG, tq, tk), lambda r, jb: (0, 0, 0))],
        out_specs=[pl.BlockSpec((nbq, per, ATT_GW), lambda r, jb: (jb, r, 0)),
                   pl.BlockSpec((nbq, per, ATT_HEAD_DIM), lambda r, jb: (jb, r, 0))],
        out_shape=[jax.ShapeDtypeStruct((nblocks, blk, ATT_GW), BF16),
                   jax.ShapeDtypeStruct((nblocks, blk, ATT_HEAD_DIM), F32)],
        compiler_params=_cparams(("parallel", "parallel"), 32),
        name=f"dilated_attention_{gi}",
    )(src, src, src, src, src, src, src, bias)
    return o.reshape(m, ATT_GW), lse.reshape(m, ATT_HEAD_DIM)


def _att_combine_kernel(o0, l0, o1, l1, o2, l2, perm_ref, out_ref):
    tr = out_ref.shape[0]
    outs, lses = [o0[...]], [l0[...]]
    for k, (o_ref, l_ref) in enumerate(((o1, l1), (o2, l2))):
        inv = perm_ref[k, 1]
        ob, lb = [], []
        for b in range(tr // ATT_BLOCK):
            rows = slice(b * ATT_BLOCK, (b + 1) * ATT_BLOCK)
            ob.append(jnp.dot(inv, o_ref[rows, :], preferred_element_type=F32))
            lb.append(_sel_dot(inv, l_ref[rows, :]))
        outs.append(jnp.concatenate(ob, axis=0))
        lses.append(jnp.concatenate(lb, axis=0))
    a, b, c = lses
    mx = jnp.maximum(jnp.maximum(a, b), c)
    ea, eb, ec = jnp.exp(a - mx), jnp.exp(b - mx), jnp.exp(c - mx)
    inv = 1.0 / (ea + eb + ec)
    wa, wb, wc = ea * inv, eb * inv, ec * inv
    for h in range(ATT_HPG):
        sl = slice(h * ATT_HEAD_DIM, (h + 1) * ATT_HEAD_DIM)
        bc = lambda w: jnp.broadcast_to(w[:, h:h + 1], (tr, ATT_HEAD_DIM))
        out_ref[:, sl] = (bc(wa) * outs[0][:, sl].astype(F32) + bc(wb) * outs[1][:, sl].astype(F32)
                          + bc(wc) * outs[2][:, sl].astype(F32)).astype(out_ref.dtype)


def _att_combine(outs, lses, tr=512):
    assert [gi for gi, _ in _att_dilated()] == [1, 2]
    m, w = outs[0].shape
    ospec = pl.BlockSpec((tr, w), lambda i: (i, 0))
    lspec = pl.BlockSpec((tr, ATT_HEAD_DIM), lambda i: (i, 0))
    args = [x for pair in zip(outs, lses) for x in pair]
    perms = _att_row_perms()
    return pl.pallas_call(
        _att_combine_kernel,
        grid=(m // tr,),
        in_specs=[ospec, lspec] * 3 + [pl.BlockSpec(perms.shape, lambda i: (0, 0, 0, 0))],
        out_specs=ospec,
        out_shape=jax.ShapeDtypeStruct((m, w), BF16),
        compiler_params=_cparams(("parallel",), 32),
        name="attention_combine",
    )(*args, perms)


PACK_TILE = 512


def _pack_cols_kernel(short_tiles, *refs):
    parts, o_ref = refs[:-1], refs[-1]
    j = pl.program_id(2)
    valid = PACK_TILE
    for lo, hi, v in short_tiles:
        valid = jnp.where(jnp.logical_and(j >= lo, j < hi), v, valid)
    val = jnp.concatenate([p[0] for p in parts], axis=1)
    col = lax.broadcasted_iota(jnp.int32, val.shape, 1)
    o_ref[0] = jnp.where(col < valid, val, 0.0).astype(o_ref.dtype)


def _pack_cols(w, n_tiles, src_lane_block, short_tiles, tk=512):
    depth, k, n = w.shape
    last = -(-n // LANES) - 1
    part = lambda q: pl.BlockSpec((1, tk, LANES), lambda l, i, j: (l, i, jnp.minimum(src_lane_block(j, q), last)))
    nparts = PACK_TILE // LANES
    return pl.pallas_call(
        functools.partial(_pack_cols_kernel, short_tiles),
        grid=(depth, k // tk, n_tiles),
        in_specs=[part(q) for q in range(nparts)],
        out_specs=pl.BlockSpec((1, tk, PACK_TILE), lambda l, i, j: (l, i, j)),
        out_shape=jax.ShapeDtypeStruct((depth, k, n_tiles * PACK_TILE), BF16),
        compiler_params=_cparams(("parallel", "parallel", "parallel"), 32),
        name="pack_weight_columns",
    )(*([w] * nparts))


def _pack_w_in_kernel(t_u, shift, a_ref, b_ref, o_ref, dt_ref):
    j = pl.program_id(2)

    @pl.when(j < t_u)
    def _():
        o_ref[...] = a_ref[...].T.astype(o_ref.dtype)

    @pl.when(j >= t_u)
    def _():
        rows = jnp.concatenate([a_ref[...], b_ref[...]], axis=0)[shift:shift + PACK_TILE, :]
        o_ref[...] = rows.T.astype(o_ref.dtype)

    @pl.when(j == t_u)
    def _():
        t = a_ref[0:LANES, :].T
        dt_ref[:, 0:LANES] = t.astype(dt_ref.dtype)
        dt_ref[:, LANES:2 * LANES] = pltpu.roll(t, LANES - SSM_HEADS, 1).astype(dt_ref.dtype)


def _pack_w_in(w_in):
    ng = len(ATT_PATTERNS)
    raw_dt = SSM_INNER + SSM_XBC
    shift = 2 * SSM_HEADS
    t_u = COL_U // PACK_TILE
    t_qkv = COL_QKV // PACK_TILE
    t_gates = COL_GATES // PACK_TILE
    assert raw_dt % PACK_TILE == 0 and COL_U == raw_dt

    def src_block(j):
        jj = jnp.clip(j - t_qkv, 0, 3 * ng - 1)
        return jnp.where(jnp.logical_and(j >= t_qkv, j < t_gates), t_qkv + (jj % 3) * ng + jj // 3, j)

    depth, d, n = w_in.shape
    tk = min(1024, d)
    last = -(-n // LANES) - 1
    sub = PACK_TILE // LANES
    w_t = jnp.swapaxes(w_in, 1, 2)
    return pl.pallas_call(
        functools.partial(_pack_w_in_kernel, t_u, shift),
        grid=(depth, d // tk, IN_COLS // PACK_TILE),
        in_specs=[pl.BlockSpec((None, PACK_TILE, tk), lambda l, i, j: (l, src_block(j), i)),
                  pl.BlockSpec((None, LANES, tk), lambda l, i, j: (l, jnp.minimum((src_block(j) + 1) * sub, last), i))],
        out_specs=[pl.BlockSpec((None, tk, PACK_TILE), lambda l, i, j: (l, i, j)),
                   pl.BlockSpec((None, tk, DT_COLS), lambda l, i, j: (l, i, 0))],
        out_shape=[jax.ShapeDtypeStruct((depth, d, IN_COLS), BF16), jax.ShapeDtypeStruct((depth, d, DT_COLS), BF16)],
        compiler_params=_cparams(("parallel", "parallel", "arbitrary"), 32),
        name="pack_w_in",
    )(w_t, w_t)


def _pack_ffn(w_up, conv_w, conv_b, w_down):
    nb = D_FF_PAD // PACK_TILE
    full, rem = divmod(D_FF, PACK_TILE)
    nparts = PACK_TILE // LANES
    assert nb == full + 1 and rem > 0 and D_FF % LANES == 0
    src = lambda j, q: jnp.where(j < nb, j * nparts + q, D_FF // LANES + (j - nb) * nparts + q)
    w_up_p = _pack_cols(w_up, 2 * nb, src, ((full, nb, rem), (nb + full, 2 * nb, rem)))
    padc = lambda a: jnp.pad(a, [(0, 0)] * (a.ndim - 1) + [(0, D_FF_PAD - D_FF)])
    both = lambda a: jnp.concatenate([padc(a[..., :D_FF]), padc(a[..., D_FF:])], axis=-1)
    w_down_p = jnp.pad(w_down, ((0, 0), (0, D_FF_PAD - D_FF), (0, 0))).astype(BF16)
    return w_up_p, both(conv_w).astype(F32), both(conv_b).astype(F32), w_down_p


def _trunk(x, seq_bounds, rel_bias, norm_mix, w_in, ssm_conv_w, ssm_conv_b, ssm_a_log, ssm_dt_bias, ssm_d,
           ssm_norm, ssm_w_out, s5_tables, s5_d, s5_w_glu, att_w_out, w_o, norm_ffn, w_up, ffn_conv_w,
           ffn_conv_b, w_down):
    m = x.shape[0]
    depth = w_in.shape[0]
    tm = min(1024, m)
    tm2 = min(512, m)
    chunk_bounds = tuple((s // S5_CHUNK, l // S5_CHUNK) for s, l in seq_bounds)
    att_bias = [_att_bias(rel_bias, gi, _att_tile(seq_bounds, dil)) for gi, (_, dil) in enumerate(ATT_PATTERNS)]
    resid = lambda acc, res: res + acc

    w_main, w_dt = _pack_w_in(w_in)
    w_up_p, cw_p, cb_p, w_down_p = _pack_ffn(w_up, ffn_conv_w, ffn_conv_b, w_down)
    ssm_w_out, s5_w_glu, att_w_out, w_o = (w.astype(BF16) for w in (ssm_w_out, s5_w_glu, att_w_out, w_o))

    for li in range(depth):
        proj, dtraw = _in_proj(x, norm_mix[li].astype(F32), w_main, w_dt, li, ssm_conv_w[li].astype(F32),
                               ssm_conv_b[li].astype(F32), seq_bounds, tm)

        y_a = _ssd(proj, dtraw, ssm_dt_bias[li], ssm_a_log[li], ssm_d[li], ssm_norm[li].astype(F32), seq_bounds)

        d_tiled = jnp.tile(s5_d[li].astype(F32).reshape(S5_GROUPS, 1, S5_GROUP), (1, 1, S5_CHUNK))
        y_b = _s5(proj, s5_tables, li, d_tiled, chunk_bounds)

        outs, lses = zip(*[_attention_group(proj, att_bias[gi], gi, seq_bounds)
                           for gi in range(len(ATT_PATTERNS))])
        y_c = _att_combine(outs, lses, tr=tm2)

        merged = _merge_branches(y_a, y_b, y_c, ssm_w_out, s5_w_glu, att_w_out, li, proj, tm=tm, tn=512)
        x = _matmul(merged, w_o, layer=li, tm=tm, tn=512, out_dtype=F32, epilogue=resid,
                    extras=((x, 0),), name="mix_out_proj")

        act = _ffn_up(x, norm_ffn[li].astype(F32), w_up_p, li, cw_p[li], cb_p[li], seq_bounds, tm)
        x = _matmul(act, w_down_p, layer=li, tm=tm, tn=512, out_dtype=F32, epilogue=resid, extras=((x, 0),),
                    vmem_mb=56, name="ffn_down")
    return x


def kernel(x_prompt, x_sample, rel_bias, norm_mix, w_in, ssm_conv_w, ssm_conv_b, ssm_a_log, ssm_dt_bias, ssm_d, ssm_norm, ssm_w_out, s5_a_re, s5_a_im, s5_log_step, s5_b_re, s5_b_im, s5_c_re, s5_c_im, s5_d, s5_w_glu, att_w_out, w_o, norm_ffn, w_up, ffn_conv_w, ffn_conv_b, w_down, final_norm):
    d = x_prompt.shape[-1]
    seq_bounds = []
    row = 0
    for arr in (x_prompt, x_sample):
        for _ in range(arr.shape[0]):
            seq_bounds.append((row, arr.shape[1]))
            row += arr.shape[1]
    seq_bounds = tuple(seq_bounds)
    x = jnp.concatenate([x_prompt.reshape(-1, d), x_sample.reshape(-1, d)], axis=0)
    s5_tables = _s5_prep(s5_a_re, s5_a_im, s5_log_step, s5_b_re, s5_b_im, s5_c_re, s5_c_im)
    y = _trunk(x, seq_bounds, rel_bias, norm_mix, w_in, ssm_conv_w, ssm_conv_b, ssm_a_log, ssm_dt_bias, ssm_d,
               ssm_norm, ssm_w_out, s5_tables, s5_d, s5_w_glu, att_w_out, w_o, norm_ffn, w_up, ffn_conv_w,
               ffn_conv_b, w_down)
    n_prompt = x_prompt.shape[0] * x_prompt.shape[1]
    y_prompt, y_sample = _final_norm(y, final_norm.astype(F32), n_prompt, tr=min(512, n_prompt))
    return (y_prompt.reshape(x_prompt.shape), y_sample.reshape(x_sample.shape))
```

```python
import functools
import math

import numpy as np
import jax
import jax.numpy as jnp
from jax import lax
from jax.experimental import pallas as pl
from jax.experimental.pallas import tpu as pltpu

F32 = jnp.float32
BF16 = jnp.bfloat16

D_MODEL = 2048
NORM_EPS = 1e-6
NEG_INF = -1e30

SSM_HEADDIM = 64
SSM_INNER = 1536
SSM_HEADS = 24
SSM_GROUPS = 4
SSM_HPG = SSM_HEADS // SSM_GROUPS
SSM_STATE = 128
SSM_CONV = 5
SSM_CHUNK = 128
SSM_XBC = 2560

S5_WIDTH = 1024
S5_GROUP = 16
S5_GROUPS = 64
S5_STATE = 64
S5_CHUNK = 32
S5_CW = S5_CHUNK * S5_GROUP
S5_LAGS = 128
S5_SCAN_STEPS = 8

ATT_HEAD_DIM = 128
ATT_HPG = 4
ATT_PATTERNS = ((128, 1), (512, 4), (2048, 16))
ATT_GW = ATT_HPG * ATT_HEAD_DIM
ATT_HALF = 64
REL_BUCKETS = 32
REL_MAX_DIST = 1024

D_FF = 5504
D_FF_PAD = 5632
FFN_CONV = 3

COL_Z = 0
COL_XBC = 1536
COL_U = 4096
COL_QKV = 5120
COL_GATES = 9728
IN_COLS = 15872
DT_COLS = 256

V7X_VMEM_BYTES = 64 * 1024 * 1024


def _cparams(sem, vmem_mb):
    assert vmem_mb * 1024 * 1024 <= V7X_VMEM_BYTES
    return pltpu.CompilerParams(dimension_semantics=sem, vmem_limit_bytes=vmem_mb * 1024 * 1024)


def _sigmoid(x):
    return 1.0 / (1.0 + jnp.exp(-x))


def _silu(x):
    return x * _sigmoid(x)


def _split3(x):
    hi = x.astype(BF16)
    r1 = x - hi.astype(F32)
    mid = r1.astype(BF16)
    lo = (r1 - mid.astype(F32)).astype(BF16)
    return hi, mid, lo


def _dot_sel(x, sel):
    hi, mid, _ = _split3(x)
    d = lambda a: jnp.dot(a, sel, preferred_element_type=F32)
    return d(hi) + d(mid)


def _sel_dot(sel, x):
    hi, mid, lo = _split3(x)
    d = lambda a: jnp.dot(sel, a, preferred_element_type=F32)
    return d(hi) + d(mid) + d(lo)


def _dot_f32(a, b):
    ah, am, _ = _split3(a)
    bh, bm, _ = _split3(b)
    d = lambda x, y: jnp.dot(x, y, preferred_element_type=F32)
    return d(ah, bh) + d(ah, bm) + d(am, bh)


def _final_norm_kernel(nb_first, x_ref, g_ref, first_ref, second_ref):
    x = x_ref[...]
    ms = jnp.mean(x * x, axis=-1, keepdims=True)
    y = x * lax.rsqrt(ms + NORM_EPS) * g_ref[...]
    i = pl.program_id(0)

    @pl.when(i < nb_first)
    def _():
        first_ref[...] = y

    @pl.when(i >= nb_first)
    def _():
        second_ref[...] = y


def _final_norm(x, gain, n_first, tr=512):
    m, d = x.shape
    assert n_first % tr == 0 and (m - n_first) % tr == 0 and 0 < n_first < m
    nb_first = n_first // tr
    return pl.pallas_call(
        functools.partial(_final_norm_kernel, nb_first),
        grid=(m // tr,),
        in_specs=[pl.BlockSpec((tr, d), lambda i: (i, 0)),
                  pl.BlockSpec((1, d), lambda i: (0, 0))],
        out_specs=[pl.BlockSpec((tr, d), lambda i: (jnp.minimum(i, nb_first - 1), 0)),
                   pl.BlockSpec((tr, d), lambda i: (jnp.maximum(i - nb_first, 0), 0))],
        out_shape=[jax.ShapeDtypeStruct((n_first, d), x.dtype), jax.ShapeDtypeStruct((m - n_first, d), x.dtype)],
        compiler_params=_cparams(("arbitrary",), 32),
        name="final_norm",
    )(x, gain.reshape(1, d))


def _mm_kernel(epilogue, n_extra, a_ref, b_ref, *rest):
    extras = rest[:n_extra]
    o_ref = rest[n_extra]
    acc = jnp.dot(a_ref[...], b_ref[...], preferred_element_type=F32)
    if epilogue is not None:
        acc = epilogue(acc, *[e[...] for e in extras])
    o_ref[...] = acc.astype(o_ref.dtype)


def _layer_spec(w, layer, rows, tn, col_off=0):
    if w.ndim == 2:
        return pl.BlockSpec((rows, tn), lambda i, j: (0, j + col_off))
    return pl.BlockSpec((None, rows, tn), lambda i, j: (layer, 0, j + col_off))


def _matmul(a, b, *, tm, tn, out_dtype, k=None, a_kblk=0, epilogue=None, extras=(), vmem_mb=48,
            layer=None, name="matmul"):
    m = a.shape[0]
    kk, n = b.shape[-2:]
    if k is None:
        k = a.shape[1]
    assert k == kk and m % tm == 0 and n % tn == 0
    in_specs = [pl.BlockSpec((tm, k), lambda i, j: (i, a_kblk)), _layer_spec(b, layer, k, tn)]
    args = [a, b]
    for arr, off in extras:
        in_specs.append(pl.BlockSpec((tm, tn), functools.partial(lambda i, j, o: (i, j + o), o=off)))
        args.append(arr)
    return pl.pallas_call(
        functools.partial(_mm_kernel, epilogue, len(extras)),
        grid=(m // tm, n // tn),
        in_specs=in_specs,
        out_specs=pl.BlockSpec((tm, tn), lambda i, j: (i, j)),
        out_shape=jax.ShapeDtypeStruct((m, n), out_dtype),
        compiler_params=_cparams(("parallel", "parallel"), vmem_mb),
        name=name,
    )(*args)


def _merge_kernel(ya_ref, yb_ref, yc_ref, perm_ref, wa_ref, wv_ref, wg_ref, wc_ref, g0_ref, g1_ref, g2_ref,
                  o_ref, yb_scr):
    @pl.when(pl.program_id(1) == 0)
    def _():
        for b in range(yb_ref.shape[0] // S5_PERM_ROWS):
            rows = slice(b * S5_PERM_ROWS, (b + 1) * S5_PERM_ROWS)
            yb_scr[rows, :] = jnp.dot(perm_ref[1], yb_ref[rows, :], preferred_element_type=F32).astype(BF16)

    dot = lambda a, b: jnp.dot(a, b, preferred_element_type=F32)
    sig = lambda ref: _sigmoid(ref[...].astype(F32))
    yb = yb_scr[...]
    branch_a = dot(ya_ref[...], wa_ref[...])
    branch_b = dot(yb, wv_ref[...]) * _sigmoid(dot(yb, wg_ref[...]))
    branch_c = dot(yc_ref[...], wc_ref[...])
    o_ref[...] = (sig(g0_ref) * branch_a + sig(g1_ref) * branch_b + sig(g2_ref) * branch_c).astype(o_ref.dtype)


def _merge_branches(ya, yb, yc, w_a, w_glu, w_c, layer, proj, *, tm, tn):
    m = ya.shape[0]
    n = w_a.shape[-1]
    nb = n // tn
    perm, _ = _s5_movers()
    assert tm % S5_PERM_ROWS == 0
    rows = lambda a: pl.BlockSpec((tm, a.shape[1]), lambda i, j: (i, 0))
    gate = lambda b: pl.BlockSpec((tm, tn), lambda i, j: (i, j + (COL_GATES + b * n) // tn))
    return pl.pallas_call(
        _merge_kernel,
        grid=(m // tm, nb),
        in_specs=[rows(ya), rows(yb), rows(yc), pl.BlockSpec(perm.shape, lambda i, j: (0, 0, 0)),
                  _layer_spec(w_a, layer, ya.shape[1], tn),
                  _layer_spec(w_glu, layer, yb.shape[1], tn), _layer_spec(w_glu, layer, yb.shape[1], tn, nb),
                  _layer_spec(w_c, layer, yc.shape[1], tn),
                  gate(0), gate(1), gate(2)],
        out_specs=pl.BlockSpec((tm, tn), lambda i, j: (i, j)),
        out_shape=jax.ShapeDtypeStruct((m, n), BF16),
        scratch_shapes=[pltpu.VMEM((tm, yb.shape[1]), BF16)],
        compiler_params=_cparams(("parallel", "arbitrary"), 48),
        name="merge_branches",
    )(ya, yb, yc, perm, w_a, w_glu, w_glu, w_c, proj, proj, proj)


HALO = 16


def _seq_flags(row0, nrows, seq_bounds):
    starts = [s for s, _ in seq_bounds]
    ends = [s + l for s, l in seq_bounds]
    is_start = functools.reduce(jnp.logical_or, [row0 == s for s in starts])
    is_end = functools.reduce(jnp.logical_or, [row0 + nrows == e for e in ends])
    return is_start, is_end


def _conv_taps(buf_ref, w_ref, b_ref, width, tr, cols=slice(None)):
    pad = width // 2
    acc = None
    for kk in range(width):
        term = buf_ref[pl.ds(HALO - pad + kk, tr), :] * w_ref[kk:kk + 1, cols]
        acc = term if acc is None else acc + term
    return acc + b_ref[:, cols]


def _norm_rows(x, gain):
    ms = jnp.mean(x * x, axis=-1, keepdims=True)
    return (x * lax.rsqrt(ms + NORM_EPS) * gain).astype(BF16)


def _fill_normed(hn_ref, xp_ref, xm_ref, xn_ref, gain_ref, tm, seq_bounds):
    is_start, is_end = _seq_flags(pl.program_id(0) * tm, tm, seq_bounds)
    gain = gain_ref[...]
    hn_ref[0:HALO, :] = jnp.where(is_start, 0.0, _norm_rows(xp_ref[...], gain)).astype(BF16)
    hn_ref[HALO:HALO + tm, :] = _norm_rows(xm_ref[...], gain)
    hn_ref[HALO + tm:HALO + tm + HALO, :] = jnp.where(is_end, 0.0, _norm_rows(xn_ref[...], gain)).astype(BF16)


def _row_halo_specs(tm, d, nrows):
    hb = tm // HALO
    last = nrows // HALO - 1
    return [pl.BlockSpec((HALO, d), lambda i, j: (jnp.maximum(i * hb - 1, 0), 0)),
            pl.BlockSpec((tm, d), lambda i, j: (i, 0)),
            pl.BlockSpec((HALO, d), lambda i, j: (jnp.minimum((i + 1) * hb, last), 0))]


ATT_BLOCK = 256


def _att_dilated():
    return tuple((gi, dil) for gi, (_, dil) in enumerate(ATT_PATTERNS) if dil > 1)


def _att_row_perms():
    out = []
    for _, dil in _att_dilated():
        dst = np.arange(ATT_BLOCK)[:, None]
        per = ATT_BLOCK // dil
        fwd = (dst % per) * dil + dst // per == np.arange(ATT_BLOCK)[None, :]
        out.append(np.stack([fwd, fwd.T]))
    return jnp.asarray(np.stack(out), BF16)


def _in_proj_kernel(tm, tn, seq_bounds, conv_lo, conv_hi, dil_ranges, u_range, xp_ref, xm_ref, xn_ref, gain_ref,
                    w_ref, wdt_ref, cw_ref, cb_ref, perm_ref, uperm_ref, o_ref, dt_ref, hn_ref, *acc_refs):
    j = pl.program_id(1)

    @pl.when(j == 0)
    def _():
        _fill_normed(hn_ref, xp_ref, xm_ref, xn_ref, gain_ref, tm, seq_bounds)
        dt_ref[...] = jnp.dot(hn_ref[HALO:HALO + tm, :], wdt_ref[...], preferred_element_type=F32)

    is_conv = jnp.logical_and(j >= conv_lo, j < conv_hi)

    @pl.when(is_conv)
    def _():
        part = tn // len(acc_refs)
        for c, acc_ref in enumerate(acc_refs):
            cols = slice(c * part, (c + 1) * part)
            acc_ref[...] = jnp.dot(hn_ref[...], w_ref[:, cols], preferred_element_type=F32)
            o_ref[:, cols] = _silu(_conv_taps(acc_ref, cw_ref, cb_ref, SSM_CONV, tm, cols)).astype(o_ref.dtype)

    regroup = [((lo, hi), ATT_BLOCK, functools.partial(lambda k: perm_ref[k, 0], k))
               for k, (lo, hi) in enumerate(dil_ranges)]
    regroup.append((u_range, S5_PERM_ROWS, lambda: uperm_ref[0]))
    conds = [jnp.logical_and(j >= lo, j < hi) for (lo, hi), _, _ in regroup]
    plain = jnp.logical_not(functools.reduce(jnp.logical_or, conds, is_conv))

    @pl.when(plain)
    def _():
        o_ref[...] = jnp.dot(hn_ref[HALO:HALO + tm, :], w_ref[...],
                             preferred_element_type=F32).astype(o_ref.dtype)

    for cond, (_, nrows, get_perm) in zip(conds, regroup):
        @pl.when(cond)
        def _(nrows=nrows, get_perm=get_perm):
            acc = jnp.dot(hn_ref[HALO:HALO + tm, :], w_ref[...], preferred_element_type=F32).astype(BF16)
            for b in range(tm // nrows):
                rows = slice(b * nrows, (b + 1) * nrows)
                o_ref[rows, :] = jnp.dot(get_perm(), acc[rows, :],
                                         preferred_element_type=F32).astype(o_ref.dtype)


def _in_proj(x, gain, w_main, w_dt, layer, conv_w, conv_b, seq_bounds, tm, tn=512):
    m, d = x.shape
    n = w_main.shape[-1]
    conv_lo, conv_hi = COL_XBC // tn, (COL_XBC + SSM_XBC) // tn
    cidx = lambda i, j: (0, jnp.clip(j - conv_lo, 0, conv_hi - conv_lo - 1))
    dil_ranges = tuple(((COL_QKV + gi * 3 * ATT_GW) // tn, (COL_QKV + (gi + 1) * 3 * ATT_GW) // tn)
                       for gi, _ in _att_dilated())
    u_range = (COL_U // tn, (COL_U + S5_WIDTH) // tn)
    perms = _att_row_perms()
    uperm, _ = _s5_movers()
    assert tm % S5_PERM_ROWS == 0 and tm % ATT_BLOCK == 0
    return pl.pallas_call(
        functools.partial(_in_proj_kernel, tm, tn, seq_bounds, conv_lo, conv_hi, dil_ranges, u_range),
        grid=(m // tm, n // tn),
        in_specs=_row_halo_specs(tm, d, m) + [
            pl.BlockSpec((1, d), lambda i, j: (0, 0)),
            _layer_spec(w_main, layer, d, tn),
            pl.BlockSpec((None, d, DT_COLS), lambda i, j: (layer, 0, 0)),
            pl.BlockSpec((SSM_CONV, tn), cidx),
            pl.BlockSpec((1, tn), cidx),
            pl.BlockSpec(perms.shape, lambda i, j: (0, 0, 0, 0)),
            pl.BlockSpec(uperm.shape, lambda i, j: (0, 0, 0))],
        out_specs=[pl.BlockSpec((tm, tn), lambda i, j: (i, j)),
                   pl.BlockSpec((tm, DT_COLS), lambda i, j: (i, 0))],
        out_shape=[jax.ShapeDtypeStruct((m, n), BF16), jax.ShapeDtypeStruct((m, DT_COLS), F32)],
        scratch_shapes=[pltpu.VMEM((tm + 2 * HALO, d), BF16), pltpu.VMEM((tm + 2 * HALO, tn // 2), F32),
                        pltpu.VMEM((tm + 2 * HALO, tn // 2), F32)],
        compiler_params=_cparams(("parallel", "arbitrary"), 56),
        name="in_proj",
    )(x, x, x, gain.reshape(1, d), w_main, w_dt, conv_w, conv_b.reshape(1, -1), perms, uperm)


def _ffn_up_kernel(tm, seq_bounds, xp_ref, xm_ref, xn_ref, gain_ref, wg_ref, wv_ref, cwg_ref, cbg_ref,
                   cwv_ref, cbv_ref, o_ref, hn_ref, sg_ref, sv_ref):
    @pl.when(pl.program_id(1) == 0)
    def _():
        _fill_normed(hn_ref, xp_ref, xm_ref, xn_ref, gain_ref, tm, seq_bounds)

    hn = hn_ref[...]
    sg_ref[...] = jnp.dot(hn, wg_ref[...], preferred_element_type=F32)
    sv_ref[...] = jnp.dot(hn, wv_ref[...], preferred_element_type=F32)
    gate = _conv_taps(sg_ref, cwg_ref, cbg_ref, FFN_CONV, tm)
    val = _conv_taps(sv_ref, cwv_ref, cbv_ref, FFN_CONV, tm)
    o_ref[...] = (_silu(gate) * val).astype(o_ref.dtype)


def _ffn_up(x, gain, w, layer, conv_w, conv_b, seq_bounds, tm, tn=512):
    m, d = x.shape
    nb = D_FF_PAD // tn
    col = lambda rows, off: pl.BlockSpec((rows, tn), functools.partial(lambda i, j, o: (0, j + o), o=off))
    cb2 = conv_b.reshape(1, -1)
    return pl.pallas_call(
        functools.partial(_ffn_up_kernel, tm, seq_bounds),
        grid=(m // tm, nb),
        in_specs=_row_halo_specs(tm, d, m) + [
            pl.BlockSpec((1, d), lambda i, j: (0, 0)),
            _layer_spec(w, layer, d, tn), _layer_spec(w, layer, d, tn, nb),
            col(FFN_CONV, 0), col(1, 0), col(FFN_CONV, nb), col(1, nb)],
        out_specs=pl.BlockSpec((tm, tn), lambda i, j: (i, j)),
        out_shape=jax.ShapeDtypeStruct((m, D_FF_PAD), BF16),
        scratch_shapes=[pltpu.VMEM((tm + 2 * HALO, d), BF16), pltpu.VMEM((tm + 2 * HALO, tn), F32),
                        pltpu.VMEM((tm + 2 * HALO, tn), F32)],
        compiler_params=_cparams(("parallel", "arbitrary"), 56),
        name="ffn_up",
    )(x, x, x, gain.reshape(1, d), w, w, conv_w, cb2, conv_w, cb2)


def _softplus(x):
    return jnp.maximum(x, 0.0) + jnp.log1p(jnp.exp(-jnp.abs(x)))


def _ssd_cumsums(dt_ref, bias_ref, alog_ref, d, row0=0):
    t = SSM_CHUNK
    dt = _softplus(dt_ref[row0:row0 + t, d * 128:(d + 1) * 128] + bias_ref[d])
    da = dt * (-jnp.exp(alog_ref[d]) * math.log2(math.e))
    row = lax.broadcasted_iota(jnp.int32, (t, t), 0)
    col = lax.broadcasted_iota(jnp.int32, (t, t), 1)
    tri = jnp.where((col <= row) if d == 0 else (col >= row), 1.0, 0.0).astype(BF16)
    return dt, _sel_dot(tri, da)


SSD_SCAN_CHUNKS = 2


def _ssd_state_kernel(nsteps, seq_bounds, xf_ref, bf_ref, dtf_ref, xb_ref, bb_ref, dtb_ref, bias_ref, alog_ref,
                      hf_ref, hb_ref, h_ref):
    t = SSM_CHUNK
    n = SSM_STATE
    p = SSM_HEADDIM
    kc = SSD_SCAN_CHUNKS
    i = pl.program_id(0)

    @pl.when(i == 0)
    def _():
        h_ref[...] = jnp.zeros(h_ref.shape, F32)

    for d, (x_ref, b_ref, dt_ref, out_ref) in enumerate(((xf_ref, bf_ref, dtf_ref, hf_ref),
                                                         (xb_ref, bb_ref, dtb_ref, hb_ref))):
        blk = i if d == 0 else nsteps - 1 - i
        contrib = []
        for c in range(kc):
            r0 = c * t
            dt, cs = _ssd_cumsums(dt_ref, bias_ref, alog_ref, d, r0)
            tot = cs[t - 1:t, :] if d == 0 else cs[0:1, :]
            w = jnp.exp2(tot - cs) * dt
            etot = jnp.exp2(tot)
            per_group = []
            for g in range(SSM_GROUPS):
                bg = b_ref[r0:r0 + t, g * n:(g + 1) * n]
                xw, dec = [], []
                for hh in range(SSM_HPG):
                    h = g * SSM_HPG + hh
                    xh = x_ref[r0:r0 + t, h * p:(h + 1) * p].astype(F32)
                    xw.append((xh * jnp.broadcast_to(w[:, h:h + 1], (t, p))).astype(BF16))
                    dec.append(jnp.broadcast_to(etot[:, h:h + 1], (n, p)))
                st = lax.dot_general(bg, jnp.concatenate(xw, axis=1), (((0,), (0,)), ((), ())),
                                     preferred_element_type=F32)
                per_group.append((st, jnp.concatenate(dec, axis=1)))
            contrib.append(per_group)
        for c in (range(kc) if d == 0 else reversed(range(kc))):
            is_start, is_end = _seq_flags((blk * kc + c) * t, t, seq_bounds)
            reset = is_start if d == 0 else is_end
            for g in range(SSM_GROUPS):
                st, dec = contrib[c][g]
                hin = jnp.where(reset, 0.0, h_ref[d, g])
                out_ref[c, g] = hin.astype(BF16)
                h_ref[d, g] = hin * dec + st


def _ssd_states(proj, dtraw, dt_bias, a_log, seq_bounds):
    m = proj.shape[0]
    t = SSM_CHUNK
    kc = SSD_SCAN_CHUNKS
    nchunks = m // t
    nsteps = nchunks // kc
    hw = SSM_HPG * SSM_HEADDIM
    bw = SSM_GROUPS * SSM_STATE
    fwd = lambda i: i
    bwd = lambda i: nsteps - 1 - i
    xspec = lambda f: pl.BlockSpec((kc * t, SSM_INNER), lambda i: (f(i), COL_XBC // SSM_INNER))
    bspec = lambda f: pl.BlockSpec((kc * t, bw), lambda i: (f(i), (COL_XBC + SSM_INNER) // bw))
    dspec = lambda f: pl.BlockSpec((kc * t, DT_COLS), lambda i: (f(i), 0))
    hspec = lambda f: pl.BlockSpec((kc, SSM_GROUPS, SSM_STATE, hw), lambda i: (f(i), 0, 0, 0))
    const = pl.BlockSpec((2, 1, 128), lambda i: (0, 0, 0))
    return pl.pallas_call(
        functools.partial(_ssd_state_kernel, nsteps, seq_bounds),
        grid=(nsteps,),
        in_specs=[xspec(fwd), bspec(fwd), dspec(fwd), xspec(bwd), bspec(bwd), dspec(bwd), const, const],
        out_specs=[hspec(fwd), hspec(bwd)],
        out_shape=[jax.ShapeDtypeStruct((nchunks, SSM_GROUPS, SSM_STATE, hw), BF16)] * 2,
        scratch_shapes=[pltpu.VMEM((2, SSM_GROUPS, SSM_STATE, hw), F32)],
        compiler_params=_cparams(("arbitrary",), 32),
        name="ssd_states",
    )(proj, proj, dtraw, proj, proj, dtraw, dt_bias, a_log)


def _ssd_out_kernel(x_ref, b_ref, c_ref, dt_ref, bias_ref, alog_ref, hf_ref, hb_ref, z_ref, dskip_ref,
                    gain_ref, o_ref, y_ref):
    t = SSM_CHUNK
    n = SSM_STATE
    p = SSM_HEADDIM
    dtf, csf = _ssd_cumsums(dt_ref, bias_ref, alog_ref, 0)
    dtb, csb = _ssd_cumsums(dt_ref, bias_ref, alog_ref, 1)
    csf_t, csb_t, dtf_t, dtb_t = csf.T, csb.T, dtf.T, dtb.T
    dts_t = dtf_t + dtb_t
    row = lax.broadcasted_iota(jnp.int32, (t, t), 0)
    col = lax.broadcasted_iota(jnp.int32, (t, t), 1)
    low = col <= row
    low_strict = col < row
    up_strict = col > row
    for g in range(SSM_GROUPS):
        bg = b_ref[:, g * n:(g + 1) * n]
        cg = c_ref[:, g * n:(g + 1) * n]
        cb = lax.dot_general(cg, bg, (((1,), (1,)), ((), ())), preferred_element_type=F32)
        cg_f = cg.astype(F32)
        for hh in range(SSM_HPG):
            h = g * SSM_HPG + hh
            colf = jnp.broadcast_to(csf[:, h:h + 1], (t, t))
            colb = jnp.broadcast_to(csb[:, h:h + 1], (t, t))
            seg = jnp.where(low, colf - jnp.broadcast_to(csf_t[h:h + 1, :], (t, t)),
                            colb - jnp.broadcast_to(csb_t[h:h + 1, :], (t, t)))
            coef = jnp.where(low_strict, jnp.broadcast_to(dtf_t[h:h + 1, :], (t, t)),
                             jnp.where(up_strict, jnp.broadcast_to(dtb_t[h:h + 1, :], (t, t)),
                                       jnp.broadcast_to(dts_t[h:h + 1, :], (t, t))))
            mh = jnp.exp2(seg) * cb * coef
            lhs = jnp.concatenate([mh, cg_f * jnp.exp2(colf), cg_f * jnp.exp2(colb)], axis=1).astype(BF16)
            rhs = jnp.concatenate([x_ref[:, h * p:(h + 1) * p],
                                   hf_ref[0, g, :, hh * p:(hh + 1) * p],
                                   hb_ref[0, g, :, hh * p:(hh + 1) * p]], axis=0)
            y_ref[:, h * p:(h + 1) * p] = jnp.dot(lhs, rhs, preferred_element_type=F32)
    y = y_ref[...] + x_ref[...].astype(F32) * dskip_ref[...]
    y = y * _silu(z_ref[...].astype(F32))
    ms = jnp.mean(y * y, axis=-1, keepdims=True)
    o_ref[...] = (y * lax.rsqrt(ms + NORM_EPS) * gain_ref[...]).astype(o_ref.dtype)


def _ssd_out(proj, dtraw, dt_bias, a_log, h_f, h_b, d_skip, norm_g):
    m = proj.shape[0]
    t = SSM_CHUNK
    w = SSM_INNER
    hw = SSM_HPG * SSM_HEADDIM
    bw = SSM_GROUPS * SSM_STATE
    hspec = pl.BlockSpec((1, SSM_GROUPS, SSM_STATE, hw), lambda i: (i, 0, 0, 0))
    return pl.pallas_call(
        _ssd_out_kernel,
        grid=(m // t,),
        in_specs=[pl.BlockSpec((t, w), lambda i: (i, COL_XBC // w)),
                  pl.BlockSpec((t, bw), lambda i: (i, (COL_XBC + w) // bw)),
                  pl.BlockSpec((t, bw), lambda i: (i, (COL_XBC + w) // bw + 1)),
                  pl.BlockSpec((t, DT_COLS), lambda i: (i, 0)),
                  pl.BlockSpec((2, 1, 128), lambda i: (0, 0, 0)),
                  pl.BlockSpec((2, 1, 128), lambda i: (0, 0, 0)),
                  hspec, hspec,
                  pl.BlockSpec((t, w), lambda i: (i, COL_Z // w)),
                  pl.BlockSpec((1, w), lambda i: (0, 0)),
                  pl.BlockSpec((1, w), lambda i: (0, 0))],
        out_specs=pl.BlockSpec((t, w), lambda i: (i, 0)),
        out_shape=jax.ShapeDtypeStruct((m, w), BF16),
        scratch_shapes=[pltpu.VMEM((t, w), F32)],
        compiler_params=_cparams(("parallel",), 32),
        name="ssd_out",
    )(proj, proj, proj, dtraw, dt_bias, a_log, h_f, h_b, proj,
      jnp.repeat(d_skip.astype(F32), SSM_HEADDIM).reshape(1, w), norm_g.reshape(1, w))


def _ssd(proj, dtraw, dt_bias, a_log, d_skip, norm_g, seq_bounds):
    pad = lambda v: jnp.pad(v.astype(F32), ((0, 0), (0, 128 - SSM_HEADS))).reshape(2, 1, 128)
    dt_bias, a_log = pad(dt_bias), pad(a_log)
    h_f, h_b = _ssd_states(proj, dtraw, dt_bias, a_log, seq_bounds)
    return _ssd_out(proj, dtraw, dt_bias, a_log, h_f, h_b, d_skip, norm_g)


def _s5_selectors():
    t, c = S5_CHUNK, S5_GROUP
    lag = np.arange(S5_LAGS)[:, None]
    tok = (np.arange(S5_CW) // c)[None, :]
    e_exit_f = (lag == t - 1 - tok)
    e_exit_b = (lag == tok)
    e_in_f = (lag == tok + 1)
    e_in_b = (lag == t - tok)
    lagidx = (np.arange(2 * S5_CW) // c)[None, :]
    rel = lagidx - (t - 1)
    inb = lagidx <= 2 * t - 2
    e_k_f = (lag == rel) & (rel >= 0) & inb
    e_k_b = (lag == -rel) & (rel <= 0) & inb
    tile = (np.arange(c)[:, None] == (np.arange(2 * S5_CW) % c)[None, :])
    sel = np.stack([np.concatenate([a, b], axis=1) for a, b in
                    ((e_exit_f, e_in_f), (e_exit_b, e_in_b))])
    selk = np.stack([e_k_f, e_k_b])
    return (jnp.asarray(sel, BF16), jnp.asarray(selk, BF16), jnp.asarray(tile, BF16))


def _s5_prep_kernel(acol_ref, arow_ref, ls_ref, b_ref, bt_ref, ct_ref, sel_ref, selk_ref, tile_ref,
                    sign_ref, w_ref, ws_ref, wo_ref, lpa_ref, lpb_ref):
    t = S5_CHUNK
    cw = S5_CW
    p = S5_STATE
    tile = tile_ref[...]
    lagf = lax.broadcasted_iota(jnp.int32, (p, S5_LAGS), 1).astype(F32)
    kt = jnp.zeros((S5_GROUP, 2 * cw), F32)
    for d in range(2):
        step = jnp.exp(ls_ref[0, 0, d])
        are = acol_ref[0, 0, d, 0]
        aim = acol_ref[0, 0, d, 1]
        mag = are * step
        th = aim * step
        amp = jnp.exp(lagf * mag)
        pwr = amp * jnp.cos(lagf * th)
        pwi = amp * jnp.sin(lagf * th)
        lbr = jnp.exp(mag) * jnp.cos(th)
        lbi = jnp.exp(mag) * jnp.sin(th)
        den = are * are + aim * aim
        cfr = ((lbr - 1.0) * are + lbi * aim) / den
        cfi = (lbi * are - (lbr - 1.0) * aim) / den
        bre = b_ref[0, 0, 0]
        bim = b_ref[0, 0, 1]
        bbr = cfr * bre - cfi * bim
        bbi = cfr * bim + cfi * bre
        step_r = step
        are_r = arow_ref[0, 0, d, 0:1, 0:p]
        aim_r = arow_ref[0, 0, d, 1:2, 0:p]
        mag_r = are_r * step_r
        th_r = aim_r * step_r
        lbr_r = jnp.exp(mag_r) * jnp.cos(th_r)
        lbi_r = jnp.exp(mag_r) * jnp.sin(th_r)
        den_r = are_r * are_r + aim_r * aim_r
        cfr_r = ((lbr_r - 1.0) * are_r + lbi_r * aim_r) / den_r
        cfi_r = (lbi_r * are_r - (lbr_r - 1.0) * aim_r) / den_r
        btr = bt_ref[0, 0, 0]
        bti = bt_ref[0, 0, 1]
        bbr_t = cfr_r * btr - cfi_r * bti
        bbi_t = cfr_r * bti + cfi_r * btr
        ctr = ct_ref[0, 0, d, 0]
        cti = ct_ref[0, 0, d, 1]

        sel = sel_ref[d]
        er = _dot_sel(pwr, sel)
        ei = _dot_sel(pwi, sel)
        tb_r = _dot_sel(bbr, tile[:, 0:cw])
        tb_i = _dot_sel(bbi, tile[:, 0:cw])
        tc_r = _dot_sel(ctr, tile)
        tc_i = _dot_sel(cti, tile)
        ws_ref[0, 0, d * 2 * p:d * 2 * p + p, :] = (er[:, 0:cw] * tb_r - ei[:, 0:cw] * tb_i).astype(BF16)
        ws_ref[0, 0, d * 2 * p + p:(d + 1) * 2 * p, :] = (er[:, 0:cw] * tb_i + ei[:, 0:cw] * tb_r).astype(BF16)
        zr = er[:, cw:] * tc_r[:, 0:cw] - ei[:, cw:] * tc_i[:, 0:cw]
        zi = er[:, cw:] * tc_i[:, 0:cw] + ei[:, cw:] * tc_r[:, 0:cw]
        wo_ref[0, 0, d * 2 * p:d * 2 * p + p, :] = zr.astype(BF16)
        wo_ref[0, 0, d * 2 * p + p:(d + 1) * 2 * p, :] = (-zi).astype(BF16)
        selk = selk_ref[d]
        kr, ki = _dot_sel(pwr, selk), _dot_sel(pwi, selk)
        qr = kr * tc_r - ki * tc_i
        qi = kr * tc_i + ki * tc_r
        kt = kt + _dot_f32(bbr_t, qr) - _dot_f32(bbi_t, qi)
        are2 = arow_ref[0, 0, d, 0:1, :]
        aim2 = arow_ref[0, 0, d, 1:2, :]
        ampt = jnp.exp(are2 * step_r * float(t))
        zr2 = ampt * jnp.cos(aim2 * step_r * float(t))
        zi2 = ampt * jnp.sin(aim2 * step_r * float(t))
        sign = sign_ref[...]
        for k in range(S5_SCAN_STEPS):
            lpa_ref[0, 0, d, k:k + 1, :] = zr2
            lpb_ref[0, 0, d, k:k + 1, :] = sign * zi2
            zr2, zi2 = zr2 * zr2 - zi2 * zi2, 2.0 * zr2 * zi2
    for s in range(t):
        off = (t - 1 - s) * S5_GROUP
        w_ref[0, 0, s * S5_GROUP:(s + 1) * S5_GROUP, :] = kt[:, off:off + cw].astype(BF16)


def _s5_prep(a_re, a_im, log_step, b_re, b_im, c_re, c_im):
    depth = a_re.shape[0]
    g, p, c = S5_GROUPS, S5_STATE, S5_GROUP
    a = jnp.stack([a_re, a_im], axis=2).astype(F32)
    a = a.transpose(0, 3, 1, 2, 4)
    acol = a[..., None]
    arow = jnp.concatenate([a, a], axis=-1)
    ls = log_step.astype(F32).transpose(0, 2, 1).reshape(depth, g, 2, 1, 1)
    b = jnp.stack([b_re, b_im], axis=2).astype(F32)
    bt = b.transpose(0, 1, 2, 4, 3)
    ct = jnp.stack([c_re, c_im], axis=3).astype(F32)
    ct = ct.transpose(0, 2, 1, 3, 5, 4)
    sel, selk, tile = _s5_selectors()
    sign = jnp.concatenate([-jnp.ones((1, p), F32), jnp.ones((1, p), F32)], axis=1)
    full = lambda shp: pl.BlockSpec(shp, lambda l, j: (0,) * len(shp))
    per = lambda shp: pl.BlockSpec((1, 1) + shp, lambda l, j: (l, j) + (0,) * len(shp))
    cw = S5_CW
    return pl.pallas_call(
        _s5_prep_kernel,
        grid=(depth, g),
        in_specs=[per((2, 2, p, 1)), per((2, 2, 2 * p)), per((2, 1, 1)), per((2, p, c)), per((2, c, p)),
                  per((2, 2, p, c)), full((2, S5_LAGS, 2 * cw)), full((2, S5_LAGS, 2 * cw)),
                  full((c, 2 * cw)), full((1, 2 * p))],
        out_specs=[per((cw, cw)), per((4 * p, cw)), per((4 * p, cw)),
                   per((2, S5_SCAN_STEPS, 2 * p)), per((2, S5_SCAN_STEPS, 2 * p))],
        out_shape=[jax.ShapeDtypeStruct((depth, g, cw, cw), BF16),
                   jax.ShapeDtypeStruct((depth, g, 4 * p, cw), BF16),
                   jax.ShapeDtypeStruct((depth, g, 4 * p, cw), BF16),
                   jax.ShapeDtypeStruct((depth, g, 2, S5_SCAN_STEPS, 2 * p), F32),
                   jax.ShapeDtypeStruct((depth, g, 2, S5_SCAN_STEPS, 2 * p), F32)],
        compiler_params=_cparams(("parallel", "parallel"), 32),
        name="s5_prep",
    )(acol, arow, ls, b, bt, ct, sel, selk, tile, sign)


def _gelu_tanh(x):
    return 0.5 * x * (1.0 + jnp.tanh(math.sqrt(2.0 / math.pi) * (x + 0.044715 * (x * x * x))))


LANES = 128


BF16_ROWS = 16
S5_PERM_ROWS = BF16_ROWS * S5_CHUNK
S5_GPB = LANES // S5_GROUP


def _s5_kernel(chunk_bounds, u_ref, pick_ref, w_ref, ws_ref, wo_ref, lpa_ref, lpb_ref,
               d_ref, o_ref, cat_ref, ug_ref, yg_ref):
    t = S5_CHUNK
    g = pl.program_id(1)
    r = u_ref.shape[0] // t
    nblk = r // BF16_ROWS
    p2 = 2 * S5_STATE
    qw = S5_GPB * LANES

    @pl.when(g == 0)
    def _():
        def body(b, carry):
            crow = pl.ds(pl.multiple_of(b * BF16_ROWS, BF16_ROWS), BF16_ROWS)
            for s in range(t):
                rows = pl.ds(pl.multiple_of(b * S5_PERM_ROWS + s * BF16_ROWS, BF16_ROWS), BF16_ROWS)
                cat_ref[crow, s * LANES:(s + 1) * LANES] = u_ref[rows, :]
            return carry
        lax.fori_loop(0, nblk, body, 0)
        for q in range(t // S5_GPB):
            picked = jnp.dot(cat_ref[:, q * qw:(q + 1) * qw], pick_ref[0],
                             preferred_element_type=F32).astype(BF16)
            for gg in range(S5_GPB):
                ug_ref[gg, :, q * LANES:(q + 1) * LANES] = picked[:, gg * LANES:(gg + 1) * LANES]

    u = ug_ref[g]
    uf = u.astype(F32)
    y = jnp.dot(u, w_ref[0, 0], preferred_element_type=F32)
    st = lax.dot_general(u, ws_ref[0, 0], (((1,), (1,)), ((), ())), preferred_element_type=F32)
    ridx = lax.broadcasted_iota(jnp.int32, (r, p2), 0)
    rloc = jnp.zeros((r, p2), jnp.int32)
    rlen = jnp.zeros((r, p2), jnp.int32)
    for s0, ln in chunk_bounds:
        inside = (ridx >= s0) & (ridx < s0 + ln)
        rloc = jnp.where(inside, ridx - s0, rloc)
        rlen = jnp.where(inside, ln, rlen)
    xin = []
    for d in range(2):
        x = st[:, d * p2:(d + 1) * p2]
        for k in range(S5_SCAN_STEPS):
            sh = 1 << k
            if d == 0:
                prev = jnp.where(rloc >= sh, pltpu.roll(x, sh, 0), 0.0)
            else:
                prev = jnp.where(rloc < rlen - sh, pltpu.roll(x, r - sh, 0), 0.0)
            x = (x + lpa_ref[0, 0, d, k:k + 1, :] * prev
                 + lpb_ref[0, 0, d, k:k + 1, :] * pltpu.roll(prev, S5_STATE, 1))
        if d == 0:
            xin.append(jnp.where(rloc >= 1, pltpu.roll(x, 1, 0), 0.0))
        else:
            xin.append(jnp.where(rloc < rlen - 1, pltpu.roll(x, r - 1, 0), 0.0))
    xin = jnp.concatenate(xin, axis=1).astype(BF16)
    y = y + jnp.dot(xin, wo_ref[0, 0], preferred_element_type=F32)
    y = _gelu_tanh(y + uf * d_ref[0])

    yg_ref[g] = y.astype(BF16)

    @pl.when(g == pl.num_programs(1) - 1)
    def _():
        for q in range(t // S5_GPB):
            lhs = jnp.concatenate([yg_ref[gg, :, q * LANES:(q + 1) * LANES] for gg in range(S5_GPB)], axis=1)
            cat_ref[:, q * qw:(q + 1) * qw] = jnp.dot(lhs, pick_ref[1],
                                                      preferred_element_type=F32).astype(BF16)

        def body(b, carry):
            crow = pl.ds(pl.multiple_of(b * BF16_ROWS, BF16_ROWS), BF16_ROWS)
            for s in range(t):
                rows = pl.ds(pl.multiple_of(b * S5_PERM_ROWS + s * BF16_ROWS, BF16_ROWS), BF16_ROWS)
                o_ref[rows, :] = cat_ref[crow, s * LANES:(s + 1) * LANES]
            return carry
        lax.fori_loop(0, nblk, body, 0)


def _s5_movers():
    dst = np.arange(S5_PERM_ROWS)[:, None]
    fwd = (dst % BF16_ROWS) * S5_CHUNK + dst // BF16_ROWS == np.arange(S5_PERM_ROWS)[None, :]
    src = np.arange(S5_GPB * LANES)
    s8, g, j = src // LANES, src % LANES // S5_GROUP, src % S5_GROUP
    pick = (g * LANES + s8 * S5_GROUP + j)[:, None] == src[None, :]
    return jnp.asarray(np.stack([fwd, fwd.T]), BF16), jnp.asarray(np.stack([pick, pick.T]), BF16)


def _s5(proj, tables, li, d_tiled, chunk_bounds):
    m = proj.shape[0]
    w, ws, wo, lpa, lpb = tables
    p = S5_STATE
    cw = S5_CW
    gpb = S5_GPB
    r = m // S5_CHUNK
    _, pick = _s5_movers()
    per = lambda shp: pl.BlockSpec((1, 1) + shp, lambda b, j: (li, b * gpb + j) + (0,) * len(shp))
    return pl.pallas_call(
        functools.partial(_s5_kernel, chunk_bounds),
        grid=(S5_GROUPS // gpb, gpb),
        in_specs=[pl.BlockSpec((m, LANES), lambda b, j: (0, COL_U // LANES + b)),
                  pl.BlockSpec((2, gpb * LANES, gpb * LANES), lambda b, j: (0, 0, 0)),
                  per((cw, cw)), per((4 * p, cw)), per((4 * p, cw)),
                  per((2, S5_SCAN_STEPS, 2 * p)), per((2, S5_SCAN_STEPS, 2 * p)),
                  pl.BlockSpec((1, 1, cw), lambda b, j: (b * gpb + j, 0, 0))],
        out_specs=pl.BlockSpec((m, LANES), lambda b, j: (0, b)),
        out_shape=jax.ShapeDtypeStruct((m, S5_WIDTH), BF16),
        scratch_shapes=[pltpu.VMEM((r, S5_CHUNK * LANES), BF16), pltpu.VMEM((gpb, r, cw), BF16),
                        pltpu.VMEM((gpb, r, cw), BF16)],
        compiler_params=_cparams(("parallel", "arbitrary"), 48),
        name="s5_mix",
    )(proj, pick, w, ws, wo, lpa, lpb, d_tiled)


def _t5_bucket(rel):
    half = REL_BUCKETS // 2
    exact = half // 2
    sign = (rel > 0).astype(np.int32) * half
    n = np.abs(rel)
    large = exact + (np.log(np.maximum(n, 1) / exact) / np.log(REL_MAX_DIST / exact)
                     * (half - exact)).astype(np.int32)
    large = np.minimum(large, half - 1)
    return sign + np.where(n < exact, n, large)


def _att_bias_kernel(gi, idx_ref, tbl_ref, o_ref):
    idx = idx_ref[...]
    for h in range(ATT_HPG):
        acc = jnp.full(idx.shape, NEG_INF, F32)
        for b in range(REL_BUCKETS):
            acc = jnp.where(idx == b, tbl_ref[b, gi * ATT_HPG + h], acc)
        o_ref[h] = acc


ATT_TQ = 256
ATT_QSUB = 128


def _att_tile(seq_bounds, dil):
    return min(ATT_TQ, min(l for _, l in seq_bounds) // dil)


def _att_bias(rel_bias, gi, tq):
    dil = ATT_PATTERNS[gi][1]
    tk = tq + 2 * ATT_HALF
    rel = (np.arange(tk)[None, :] - ATT_HALF) - np.arange(tq)[:, None]
    idx = np.where(np.abs(rel) <= ATT_HALF, _t5_bucket(rel * dil), -1).astype(np.int32)
    return pl.pallas_call(
        functools.partial(_att_bias_kernel, gi),
        in_specs=[pl.BlockSpec(memory_space=pltpu.VMEM), pl.BlockSpec(memory_space=pltpu.SMEM)],
        out_specs=pl.BlockSpec(memory_space=pltpu.VMEM),
        out_shape=jax.ShapeDtypeStruct((ATT_HPG, tq, tk), F32),
        name=f"attention_bias_{gi}",
    )(jnp.asarray(idx), rel_bias.astype(F32))


def _attn_kernel(tq, blk_bounds, q_ref, kp_ref, km_ref, kn_ref, vp_ref, vm_ref, vn_ref, bias_ref,
                 o_ref, lse_ref):
    jb = pl.program_id(1)
    is_first, is_last = _seq_flags(jb, 1, blk_bounds)
    flat = lambda ref: ref[...].reshape(-1, ref.shape[-1])
    q = flat(q_ref)
    kcat = jnp.concatenate([flat(kp_ref), flat(km_ref), flat(kn_ref)], axis=0)
    vcat = jnp.concatenate([flat(vp_ref), flat(vm_ref), flat(vn_ref)], axis=0)
    scale = ATT_HEAD_DIM ** -0.5
    qs = min(tq, ATT_QSUB)
    lane = lax.broadcasted_iota(jnp.int32, (qs, ATT_HEAD_DIM), 1)
    out_rows, lse_rows = [], []
    for a in range(0, tq, qs):
        keys = slice(a, a + qs + 2 * ATT_HALF)
        colk = a + lax.broadcasted_iota(jnp.int32, (qs, qs + 2 * ATT_HALF), 1)
        valid = jnp.logical_and(jnp.logical_or(colk >= ATT_HALF, jnp.logical_not(is_first)),
                                jnp.logical_or(colk < tq + ATT_HALF, jnp.logical_not(is_last)))
        lse_tile = jnp.zeros((qs, ATT_HEAD_DIM), F32)
        outs = []
        for h in range(ATT_HPG):
            sl = slice(h * ATT_HEAD_DIM, (h + 1) * ATT_HEAD_DIM)
            s = lax.dot_general(q[a:a + qs, sl], kcat[keys, sl], (((1,), (1,)), ((), ())),
                                preferred_element_type=F32)
            s = jnp.where(valid, s * scale + bias_ref[h, a:a + qs, keys], NEG_INF)
            mx = jnp.max(s, axis=-1, keepdims=True)
            pr = jnp.exp(s - mx)
            den = jnp.sum(pr, axis=-1, keepdims=True)
            o = jnp.dot(pr.astype(BF16), vcat[keys, sl], preferred_element_type=F32)
            outs.append((o / den).astype(o_ref.dtype))
            lse_tile = jnp.where(lane == h, mx + jnp.log(den), lse_tile)
        out_rows.append(jnp.concatenate(outs, axis=1))
        lse_rows.append(lse_tile)
    o_ref[...] = jnp.concatenate(out_rows, axis=0).reshape(o_ref.shape)
    lse_ref[...] = jnp.concatenate(lse_rows, axis=0).reshape(lse_ref.shape)


def _attention_group(proj, bias, gi, seq_bounds):
    window, dil = ATT_PATTERNS[gi]
    assert window // (2 * dil) == ATT_HALF
    m = proj.shape[0]
    tq = bias.shape[1]
    blk = ATT_BLOCK if dil > 1 else tq
    per = blk // dil
    nbq = tq // per
    hr = min(per, ATT_HALF)
    hb = ATT_HALF // hr
    nblocks = m // blk
    assert tq % per == 0 and nbq % hb == 0 and per % hr == 0
    assert all((s // blk) % nbq == 0 and (l // blk) % nbq == 0 for s, l in seq_bounds)
    blk_bounds = tuple((s // blk // nbq, l // blk // nbq) for s, l in seq_bounds)
    tk = tq + 2 * ATT_HALF
    cb = (COL_QKV + gi * 3 * ATT_GW) // ATT_GW
    src = proj.reshape(nblocks, blk, proj.shape[1])
    lasth = nblocks // hb - 1

    def main(col):
        return pl.BlockSpec((nbq, per, ATT_GW), lambda r, jb: (jb, r, cb + col))

    def prev(col):
        return pl.BlockSpec((hb, hr, ATT_GW),
                            lambda r, jb: (jnp.maximum(jb * (nbq // hb) - 1, 0), (r + 1) * (per // hr) - 1, cb + col))

    def nxt(col):
        return pl.BlockSpec((hb, hr, ATT_GW),
                            lambda r, jb: (jnp.minimum((jb + 1) * (nbq // hb), lasth), r * (per // hr), cb + col))

    o, lse = pl.pallas_call(
        functools.partial(_attn_kernel, tq, blk_bounds),
        grid=(dil, nblocks // nbq),
        in_specs=[main(0), prev(1), main(1), nxt(1), prev(2), main(2), nxt(2),
                  pl.BlockSpec((ATT_HPG, tq, tk), lambda r, jb: (0, 0, 0))],
        out_specs=[pl.BlockSpec((nbq, per, ATT_GW), lambda r, jb: (jb, r, 0)),
                   pl.BlockSpec((nbq, per, ATT_HEAD_DIM), lambda r, jb: (jb, r, 0))],
        out_shape=[jax.ShapeDtypeStruct((nblocks, blk, ATT_GW), BF16),
                   jax.ShapeDtypeStruct((nblocks, blk, ATT_HEAD_DIM), F32)],
        compiler_params=_cparams(("parallel", "parallel"), 32),
        name=f"dilated_attention_{gi}",
    )(src, src, src, src, src, src, src, bias)
    return o.reshape(m, ATT_GW), lse.reshape(m, ATT_HEAD_DIM)


def _att_combine_kernel(o0, l0, o1, l1, o2, l2, perm_ref, out_ref):
    tr = out_ref.shape[0]
    outs, lses = [o0[...]], [l0[...]]
    for k, (o_ref, l_ref) in enumerate(((o1, l1), (o2, l2))):
        inv = perm_ref[k, 1]
        ob, lb = [], []
        for b in range(tr // ATT_BLOCK):
            rows = slice(b * ATT_BLOCK, (b + 1) * ATT_BLOCK)
            ob.append(jnp.dot(inv, o_ref[rows, :], preferred_element_type=F32))
            lb.append(_sel_dot(inv, l_ref[rows, :]))
        outs.append(jnp.concatenate(ob, axis=0))
        lses.append(jnp.concatenate(lb, axis=0))
    a, b, c = lses
    mx = jnp.maximum(jnp.maximum(a, b), c)
    ea, eb, ec = jnp.exp(a - mx), jnp.exp(b - mx), jnp.exp(c - mx)
    inv = 1.0 / (ea + eb + ec)
    wa, wb, wc = ea * inv, eb * inv, ec * inv
    for h in range(ATT_HPG):
        sl = slice(h * ATT_HEAD_DIM, (h + 1) * ATT_HEAD_DIM)
        bc = lambda w: jnp.broadcast_to(w[:, h:h + 1], (tr, ATT_HEAD_DIM))
        out_ref[:, sl] = (bc(wa) * outs[0][:, sl].astype(F32) + bc(wb) * outs[1][:, sl].astype(F32)
                          + bc(wc) * outs[2][:, sl].astype(F32)).astype(out_ref.dtype)


def _att_combine(outs, lses, tr=512):
    assert [gi for gi, _ in _att_dilated()] == [1, 2]
    m, w = outs[0].shape
    ospec = pl.BlockSpec((tr, w), lambda i: (i, 0))
    lspec = pl.BlockSpec((tr, ATT_HEAD_DIM), lambda i: (i, 0))
    args = [x for pair in zip(outs, lses) for x in pair]
    perms = _att_row_perms()
    return pl.pallas_call(
        _att_combine_kernel,
        grid=(m // tr,),
        in_specs=[ospec, lspec] * 3 + [pl.BlockSpec(perms.shape, lambda i: (0, 0, 0, 0))],
        out_specs=ospec,
        out_shape=jax.ShapeDtypeStruct((m, w), BF16),
        compiler_params=_cparams(("parallel",), 32),
        name="attention_combine",
    )(*args, perms)


PACK_TILE = 512


def _pack_cols_kernel(short_tiles, *refs):
    parts, o_ref = refs[:-1], refs[-1]
    j = pl.program_id(2)
    valid = PACK_TILE
    for lo, hi, v in short_tiles:
        valid = jnp.where(jnp.logical_and(j >= lo, j < hi), v, valid)
    val = jnp.concatenate([p[0] for p in parts], axis=1)
    col = lax.broadcasted_iota(jnp.int32, val.shape, 1)
    o_ref[0] = jnp.where(col < valid, val, 0.0).astype(o_ref.dtype)


def _pack_cols(w, n_tiles, src_lane_block, short_tiles):
    depth, k, n = w.shape
    tk = min(2048, k)
    last = -(-n // LANES) - 1
    part = lambda q: pl.BlockSpec((1, tk, LANES), lambda l, i, j: (l, i, jnp.minimum(src_lane_block(j, q), last)))
    nparts = PACK_TILE // LANES
    return pl.pallas_call(
        functools.partial(_pack_cols_kernel, short_tiles),
        grid=(depth, k // tk, n_tiles),
        in_specs=[part(q) for q in range(nparts)],
        out_specs=pl.BlockSpec((1, tk, PACK_TILE), lambda l, i, j: (l, i, j)),
        out_shape=jax.ShapeDtypeStruct((depth, k, n_tiles * PACK_TILE), BF16),
        compiler_params=_cparams(("parallel", "parallel", "parallel"), 32),
        name="pack_weight_columns",
    )(*([w] * nparts))


def _pack_w_in_kernel(t_u, shift, a_ref, b_ref, o_ref, dt_ref):
    j = pl.program_id(2)

    @pl.when(j < t_u)
    def _():
        o_ref[...] = a_ref[...].T.astype(o_ref.dtype)

    @pl.when(j >= t_u)
    def _():
        rows = jnp.concatenate([a_ref[...], b_ref[...]], axis=0)[shift:shift + PACK_TILE, :]
        o_ref[...] = rows.T.astype(o_ref.dtype)

    @pl.when(j == t_u)
    def _():
        t = a_ref[0:LANES, :].T
        dt_ref[:, 0:LANES] = t.astype(dt_ref.dtype)
        dt_ref[:, LANES:2 * LANES] = pltpu.roll(t, LANES - SSM_HEADS, 1).astype(dt_ref.dtype)


def _pack_w_in(w_in):
    ng = len(ATT_PATTERNS)
    raw_dt = SSM_INNER + SSM_XBC
    shift = 2 * SSM_HEADS
    t_u = COL_U // PACK_TILE
    t_qkv = COL_QKV // PACK_TILE
    t_gates = COL_GATES // PACK_TILE
    assert raw_dt % PACK_TILE == 0 and COL_U == raw_dt

    def src_block(j):
        jj = jnp.clip(j - t_qkv, 0, 3 * ng - 1)
        return jnp.where(jnp.logical_and(j >= t_qkv, j < t_gates), t_qkv + (jj % 3) * ng + jj // 3, j)

    depth, d, n = w_in.shape
    tk = min(1024, d)
    last = -(-n // LANES) - 1
    sub = PACK_TILE // LANES
    w_t = jnp.swapaxes(w_in, 1, 2)
    return pl.pallas_call(
        functools.partial(_pack_w_in_kernel, t_u, shift),
        grid=(depth, d // tk, IN_COLS // PACK_TILE),
        in_specs=[pl.BlockSpec((None, PACK_TILE, tk), lambda l, i, j: (l, src_block(j), i)),
                  pl.BlockSpec((None, LANES, tk), lambda l, i, j: (l, jnp.minimum((src_block(j) + 1) * sub, last), i))],
        out_specs=[pl.BlockSpec((None, tk, PACK_TILE), lambda l, i, j: (l, i, j)),
                   pl.BlockSpec((None, tk, DT_COLS), lambda l, i, j: (l, i, 0))],
        out_shape=[jax.ShapeDtypeStruct((depth, d, IN_COLS), BF16), jax.ShapeDtypeStruct((depth, d, DT_COLS), BF16)],
        compiler_params=_cparams(("parallel", "parallel", "arbitrary"), 32),
        name="pack_w_in",
    )(w_t, w_t)


def _pack_ffn(w_up, conv_w, conv_b, w_down):
    nb = D_FF_PAD // PACK_TILE
    full, rem = divmod(D_FF, PACK_TILE)
    nparts = PACK_TILE // LANES
    assert nb == full + 1 and rem > 0 and D_FF % LANES == 0
    src = lambda j, q: jnp.where(j < nb, j * nparts + q, D_FF // LANES + (j - nb) * nparts + q)
    w_up_p = _pack_cols(w_up, 2 * nb, src, ((full, nb, rem), (nb + full, 2 * nb, rem)))
    padc = lambda a: jnp.pad(a, [(0, 0)] * (a.ndim - 1) + [(0, D_FF_PAD - D_FF)])
    both = lambda a: jnp.concatenate([padc(a[..., :D_FF]), padc(a[..., D_FF:])], axis=-1)
    w_down_p = jnp.pad(w_down, ((0, 0), (0, D_FF_PAD - D_FF), (0, 0))).astype(BF16)
    return w_up_p, both(conv_w).astype(F32), both(conv_b).astype(F32), w_down_p


def _trunk(x, seq_bounds, rel_bias, norm_mix, w_in, ssm_conv_w, ssm_conv_b, ssm_a_log, ssm_dt_bias, ssm_d,
           ssm_norm, ssm_w_out, s5_tables, s5_d, s5_w_glu, att_w_out, w_o, norm_ffn, w_up, ffn_conv_w,
           ffn_conv_b, w_down):
    m = x.shape[0]
    depth = w_in.shape[0]
    tm = min(1024, m)
    tm2 = min(512, m)
    chunk_bounds = tuple((s // S5_CHUNK, l // S5_CHUNK) for s, l in seq_bounds)
    att_bias = [_att_bias(rel_bias, gi, _att_tile(seq_bounds, dil)) for gi, (_, dil) in enumerate(ATT_PATTERNS)]
    resid = lambda acc, res: res + acc

    w_main, w_dt = _pack_w_in(w_in)
    w_up_p, cw_p, cb_p, w_down_p = _pack_ffn(w_up, ffn_conv_w, ffn_conv_b, w_down)
    ssm_w_out, s5_w_glu, att_w_out, w_o = (w.astype(BF16) for w in (ssm_w_out, s5_w_glu, att_w_out, w_o))

    for li in range(depth):
        proj, dtraw = _in_proj(x, norm_mix[li].astype(F32), w_main, w_dt, li, ssm_conv_w[li].astype(F32),
                               ssm_conv_b[li].astype(F32), seq_bounds, tm)

        y_a = _ssd(proj, dtraw, ssm_dt_bias[li], ssm_a_log[li], ssm_d[li], ssm_norm[li].astype(F32), seq_bounds)

        d_tiled = jnp.tile(s5_d[li].astype(F32).reshape(S5_GROUPS, 1, S5_GROUP), (1, 1, S5_CHUNK))
        y_b = _s5(proj, s5_tables, li, d_tiled, chunk_bounds)

        outs, lses = zip(*[_attention_group(proj, att_bias[gi], gi, seq_bounds)
                           for gi in range(len(ATT_PATTERNS))])
        y_c = _att_combine(outs, lses, tr=tm2)

        merged = _merge_branches(y_a, y_b, y_c, ssm_w_out, s5_w_glu, att_w_out, li, proj, tm=tm, tn=512)
        x = _matmul(merged, w_o, layer=li, tm=tm, tn=512, out_dtype=F32, epilogue=resid,
                    extras=((x, 0),), name="mix_out_proj")

        act = _ffn_up(x, norm_ffn[li].astype(F32), w_up_p, li, cw_p[li], cb_p[li], seq_bounds, tm)
        x = _matmul(act, w_down_p, layer=li, tm=tm, tn=512, out_dtype=F32, epilogue=resid, extras=((x, 0),),
                    vmem_mb=56, name="ffn_down")
    return x


def kernel(x_prompt, x_sample, rel_bias, norm_mix, w_in, ssm_conv_w, ssm_conv_b, ssm_a_log, ssm_dt_bias, ssm_d, ssm_norm, ssm_w_out, s5_a_re, s5_a_im, s5_log_step, s5_b_re, s5_b_im, s5_c_re, s5_c_im, s5_d, s5_w_glu, att_w_out, w_o, norm_ffn, w_up, ffn_conv_w, ffn_conv_b, w_down, final_norm):
    d = x_prompt.shape[-1]
    seq_bounds = []
    row = 0
    for arr in (x_prompt, x_sample):
        for _ in range(arr.shape[0]):
            seq_bounds.append((row, arr.shape[1]))
            row += arr.shape[1]
    seq_bounds = tuple(seq_bounds)
    x = jnp.concatenate([x_prompt.reshape(-1, d), x_sample.reshape(-1, d)], axis=0)
    s5_tables = _s5_prep(s5_a_re, s5_a_im, s5_log_step, s5_b_re, s5_b_im, s5_c_re, s5_c_im)
    y = _trunk(x, seq_bounds, rel_bias, norm_mix, w_in, ssm_conv_w, ssm_conv_b, ssm_a_log, ssm_dt_bias, ssm_d,
               ssm_norm, ssm_w_out, s5_tables, s5_d, s5_w_glu, att_w_out, w_o, norm_ffn, w_up, ffn_conv_w,
               ffn_conv_b, w_down)
    n_prompt = x_prompt.shape[0] * x_prompt.shape[1]
    y_prompt, y_sample = _final_norm(y, final_norm.astype(F32), n_prompt, tr=min(512, n_prompt))
    return (y_prompt.reshape(x_prompt.shape), y_sample.reshape(x_sample.shape))
```

```python
import functools
import math

import numpy as np
import jax
import jax.numpy as jnp
from jax import lax
from jax.experimental import pallas as pl
from jax.experimental.pallas import tpu as pltpu

F32 = jnp.float32
BF16 = jnp.bfloat16

D_MODEL = 2048
NORM_EPS = 1e-6
NEG_INF = -1e30

SSM_HEADDIM = 64
SSM_INNER = 1536
SSM_HEADS = 24
SSM_GROUPS = 4
SSM_HPG = SSM_HEADS // SSM_GROUPS
SSM_STATE = 128
SSM_CONV = 5
SSM_CHUNK = 128
SSM_XBC = 2560

S5_WIDTH = 1024
S5_GROUP = 16
S5_GROUPS = 64
S5_STATE = 64
S5_CHUNK = 32
S5_CW = S5_CHUNK * S5_GROUP
S5_LAGS = 128
S5_SCAN_STEPS = 8

ATT_HEAD_DIM = 128
ATT_HPG = 4
ATT_PATTERNS = ((128, 1), (512, 4), (2048, 16))
ATT_GW = ATT_HPG * ATT_HEAD_DIM
ATT_HALF = 64
REL_BUCKETS = 32
REL_MAX_DIST = 1024

D_FF = 5504
D_FF_PAD = 5632
FFN_CONV = 3

COL_Z = 0
COL_XBC = 1536
COL_U = 4096
COL_QKV = 5120
COL_GATES = 9728
IN_COLS = 15872
DT_COLS = 256

V7X_VMEM_BYTES = 64 * 1024 * 1024


def _cparams(sem, vmem_mb):
    assert vmem_mb * 1024 * 1024 <= V7X_VMEM_BYTES
    return pltpu.CompilerParams(dimension_semantics=sem, vmem_limit_bytes=vmem_mb * 1024 * 1024)


def _sigmoid(x):
    return 1.0 / (1.0 + jnp.exp(-x))


def _silu(x):
    return x * _sigmoid(x)


def _split3(x):
    hi = x.astype(BF16)
    r1 = x - hi.astype(F32)
    mid = r1.astype(BF16)
    lo = (r1 - mid.astype(F32)).astype(BF16)
    return hi, mid, lo


def _dot_sel(x, sel):
    hi, mid, _ = _split3(x)
    d = lambda a: jnp.dot(a, sel, preferred_element_type=F32)
    return d(hi) + d(mid)


def _sel_dot(sel, x):
    hi, mid, lo = _split3(x)
    d = lambda a: jnp.dot(sel, a, preferred_element_type=F32)
    return d(hi) + d(mid) + d(lo)


def _dot_f32(a, b):
    ah, am, _ = _split3(a)
    bh, bm, _ = _split3(b)
    d = lambda x, y: jnp.dot(x, y, preferred_element_type=F32)
    return d(ah, bh) + d(ah, bm) + d(am, bh)


def _final_norm_kernel(nb_first, x_ref, g_ref, first_ref, second_ref):
    x = x_ref[...]
    ms = jnp.mean(x * x, axis=-1, keepdims=True)
    y = x * lax.rsqrt(ms + NORM_EPS) * g_ref[...]
    i = pl.program_id(0)

    @pl.when(i < nb_first)
    def _():
        first_ref[...] = y

    @pl.when(i >= nb_first)
    def _():
        second_ref[...] = y


def _final_norm(x, gain, n_first, tr=512):
    m, d = x.shape
    assert n_first % tr == 0 and (m - n_first) % tr == 0 and 0 < n_first < m
    nb_first = n_first // tr
    return pl.pallas_call(
        functools.partial(_final_norm_kernel, nb_first),
        grid=(m // tr,),
        in_specs=[pl.BlockSpec((tr, d), lambda i: (i, 0)),
                  pl.BlockSpec((1, d), lambda i: (0, 0))],
        out_specs=[pl.BlockSpec((tr, d), lambda i: (jnp.minimum(i, nb_first - 1), 0)),
                   pl.BlockSpec((tr, d), lambda i: (jnp.maximum(i - nb_first, 0), 0))],
        out_shape=[jax.ShapeDtypeStruct((n_first, d), x.dtype), jax.ShapeDtypeStruct((m - n_first, d), x.dtype)],
        compiler_params=_cparams(("arbitrary",), 32),
        name="final_norm",
    )(x, gain.reshape(1, d))


def _mm_kernel(epilogue, n_extra, a_ref, b_ref, *rest):
    extras = rest[:n_extra]
    o_ref = rest[n_extra]
    acc = jnp.dot(a_ref[...], b_ref[...], preferred_element_type=F32)
    if epilogue is not None:
        acc = epilogue(acc, *[e[...] for e in extras])
    o_ref[...] = acc.astype(o_ref.dtype)


def _layer_spec(w, layer, rows, tn, col_off=0):
    if w.ndim == 2:
        return pl.BlockSpec((rows, tn), lambda i, j: (0, j + col_off))
    return pl.BlockSpec((None, rows, tn), lambda i, j: (layer, 0, j + col_off))


def _matmul(a, b, *, tm, tn, out_dtype, k=None, a_kblk=0, epilogue=None, extras=(), vmem_mb=48,
            layer=None, name="matmul"):
    m = a.shape[0]
    kk, n = b.shape[-2:]
    if k is None:
        k = a.shape[1]
    assert k == kk and m % tm == 0 and n % tn == 0
    in_specs = [pl.BlockSpec((tm, k), lambda i, j: (i, a_kblk)), _layer_spec(b, layer, k, tn)]
    args = [a, b]
    for arr, off in extras:
        in_specs.append(pl.BlockSpec((tm, tn), functools.partial(lambda i, j, o: (i, j + o), o=off)))
        args.append(arr)
    return pl.pallas_call(
        functools.partial(_mm_kernel, epilogue, len(extras)),
        grid=(m // tm, n // tn),
        in_specs=in_specs,
        out_specs=pl.BlockSpec((tm, tn), lambda i, j: (i, j)),
        out_shape=jax.ShapeDtypeStruct((m, n), out_dtype),
        compiler_params=_cparams(("parallel", "parallel"), vmem_mb),
        name=name,
    )(*args)


def _merge_kernel(ya_ref, yb_ref, yc_ref, perm_ref, wa_ref, wv_ref, wg_ref, wc_ref, g0_ref, g1_ref, g2_ref,
                  o_ref, yb_scr):
    @pl.when(pl.program_id(1) == 0)
    def _():
        for b in range(yb_ref.shape[0] // S5_PERM_ROWS):
            rows = slice(b * S5_PERM_ROWS, (b + 1) * S5_PERM_ROWS)
            yb_scr[rows, :] = jnp.dot(perm_ref[1], yb_ref[rows, :], preferred_element_type=F32).astype(BF16)

    dot = lambda a, b: jnp.dot(a, b, preferred_element_type=F32)
    sigmoid = lambda v: 0.5 * jnp.tanh(0.5 * v) + 0.5
    sig = lambda ref: sigmoid(ref[...].astype(F32))
    yb = yb_scr[...]
    branch_a = dot(ya_ref[...], wa_ref[...])
    branch_b = dot(yb, wv_ref[...]) * sigmoid(dot(yb, wg_ref[...]))
    branch_c = dot(yc_ref[...], wc_ref[...])
    o_ref[...] = (sig(g0_ref) * branch_a + sig(g1_ref) * branch_b + sig(g2_ref) * branch_c).astype(o_ref.dtype)


def _merge_branches(ya, yb, yc, w_a, w_glu, w_c, layer, proj, *, tm, tn):
    m = ya.shape[0]
    n = w_a.shape[-1]
    nb = n // tn
    perm, _ = _s5_movers()
    assert tm % S5_PERM_ROWS == 0
    rows = lambda a: pl.BlockSpec((tm, a.shape[1]), lambda i, j: (i, 0))
    gate = lambda b: pl.BlockSpec((tm, tn), lambda i, j: (i, j + (COL_GATES + b * n) // tn))
    return pl.pallas_call(
        _merge_kernel,
        grid=(m // tm, nb),
        in_specs=[rows(ya), rows(yb), rows(yc), pl.BlockSpec(perm.shape, lambda i, j: (0, 0, 0)),
                  _layer_spec(w_a, layer, ya.shape[1], tn),
                  _layer_spec(w_glu, layer, yb.shape[1], tn), _layer_spec(w_glu, layer, yb.shape[1], tn, nb),
                  _layer_spec(w_c, layer, yc.shape[1], tn),
                  gate(0), gate(1), gate(2)],
        out_specs=pl.BlockSpec((tm, tn), lambda i, j: (i, j)),
        out_shape=jax.ShapeDtypeStruct((m, n), BF16),
        scratch_shapes=[pltpu.VMEM((tm, yb.shape[1]), BF16)],
        compiler_params=_cparams(("parallel", "arbitrary"), 48),
        name="merge_branches",
    )(ya, yb, yc, perm, w_a, w_glu, w_glu, w_c, proj, proj, proj)


HALO = 16


def _seq_flags(row0, nrows, seq_bounds):
    starts = [s for s, _ in seq_bounds]
    ends = [s + l for s, l in seq_bounds]
    is_start = functools.reduce(jnp.logical_or, [row0 == s for s in starts])
    is_end = functools.reduce(jnp.logical_or, [row0 + nrows == e for e in ends])
    return is_start, is_end


def _conv_taps(buf_ref, w_ref, b_ref, width, tr, cols=slice(None)):
    pad = width // 2
    acc = None
    for kk in range(width):
        term = buf_ref[pl.ds(HALO - pad + kk, tr), :] * w_ref[kk:kk + 1, cols]
        acc = term if acc is None else acc + term
    return acc + b_ref[:, cols]


def _norm_rows(x, gain):
    ms = jnp.mean(x * x, axis=-1, keepdims=True)
    return (x * lax.rsqrt(ms + NORM_EPS) * gain).astype(BF16)


def _fill_normed(hn_ref, xp_ref, xm_ref, xn_ref, gain_ref, tm, seq_bounds):
    is_start, is_end = _seq_flags(pl.program_id(0) * tm, tm, seq_bounds)
    gain = gain_ref[...]
    hn_ref[0:HALO, :] = jnp.where(is_start, 0.0, _norm_rows(xp_ref[...], gain)).astype(BF16)
    hn_ref[HALO:HALO + tm, :] = _norm_rows(xm_ref[...], gain)
    hn_ref[HALO + tm:HALO + tm + HALO, :] = jnp.where(is_end, 0.0, _norm_rows(xn_ref[...], gain)).astype(BF16)


def _row_halo_specs(tm, d, nrows):
    hb = tm // HALO
    last = nrows // HALO - 1
    return [pl.BlockSpec((HALO, d), lambda i, j: (jnp.maximum(i * hb - 1, 0), 0)),
            pl.BlockSpec((tm, d), lambda i, j: (i, 0)),
            pl.BlockSpec((HALO, d), lambda i, j: (jnp.minimum((i + 1) * hb, last), 0))]


ATT_BLOCK = 256


def _att_dilated():
    return tuple((gi, dil) for gi, (_, dil) in enumerate(ATT_PATTERNS) if dil > 1)


def _att_row_perms():
    out = []
    for _, dil in _att_dilated():
        dst = np.arange(ATT_BLOCK)[:, None]
        per = ATT_BLOCK // dil
        fwd = (dst % per) * dil + dst // per == np.arange(ATT_BLOCK)[None, :]
        out.append(np.stack([fwd, fwd.T]))
    return jnp.asarray(np.stack(out), BF16)


def _in_proj_kernel(tm, tn, seq_bounds, conv_lo, conv_hi, dil_ranges, u_range, xp_ref, xm_ref, xn_ref, gain_ref,
                    w_ref, wdt_ref, cw_ref, cb_ref, perm_ref, uperm_ref, o_ref, dt_ref, hn_ref, *acc_refs):
    j = pl.program_id(1)

    @pl.when(j == 0)
    def _():
        _fill_normed(hn_ref, xp_ref, xm_ref, xn_ref, gain_ref, tm, seq_bounds)
        dt_ref[...] = jnp.dot(hn_ref[HALO:HALO + tm, :], wdt_ref[...], preferred_element_type=F32)

    is_conv = jnp.logical_and(j >= conv_lo, j < conv_hi)

    @pl.when(is_conv)
    def _():
        part = tn // len(acc_refs)
        for c, acc_ref in enumerate(acc_refs):
            cols = slice(c * part, (c + 1) * part)
            acc_ref[...] = jnp.dot(hn_ref[...], w_ref[:, cols], preferred_element_type=F32)
            o_ref[:, cols] = _silu(_conv_taps(acc_ref, cw_ref, cb_ref, SSM_CONV, tm, cols)).astype(o_ref.dtype)

    regroup = [((lo, hi), ATT_BLOCK, functools.partial(lambda k: perm_ref[k, 0], k))
               for k, (lo, hi) in enumerate(dil_ranges)]
    regroup.append((u_range, S5_PERM_ROWS, lambda: uperm_ref[0]))
    conds = [jnp.logical_and(j >= lo, j < hi) for (lo, hi), _, _ in regroup]
    plain = jnp.logical_not(functools.reduce(jnp.logical_or, conds, is_conv))

    @pl.when(plain)
    def _():
        o_ref[...] = jnp.dot(hn_ref[HALO:HALO + tm, :], w_ref[...],
                             preferred_element_type=F32).astype(o_ref.dtype)

    for cond, (_, nrows, get_perm) in zip(conds, regroup):
        @pl.when(cond)
        def _(nrows=nrows, get_perm=get_perm):
            acc = jnp.dot(hn_ref[HALO:HALO + tm, :], w_ref[...], preferred_element_type=F32).astype(BF16)
            for b in range(tm // nrows):
                rows = slice(b * nrows, (b + 1) * nrows)
                o_ref[rows, :] = jnp.dot(get_perm(), acc[rows, :],
                                         preferred_element_type=F32).astype(o_ref.dtype)


def _in_proj(x, gain, w_main, w_dt, layer, conv_w, conv_b, seq_bounds, tm, tn=512):
    m, d = x.shape
    n = w_main.shape[-1]
    conv_lo, conv_hi = COL_XBC // tn, (COL_XBC + SSM_XBC) // tn
    cidx = lambda i, j: (0, jnp.clip(j - conv_lo, 0, conv_hi - conv_lo - 1))
    dil_ranges = tuple(((COL_QKV + gi * 3 * ATT_GW) // tn, (COL_QKV + (gi + 1) * 3 * ATT_GW) // tn)
                       for gi, _ in _att_dilated())
    u_range = (COL_U // tn, (COL_U + S5_WIDTH) // tn)
    perms = _att_row_perms()
    uperm, _ = _s5_movers()
    assert tm % S5_PERM_ROWS == 0 and tm % ATT_BLOCK == 0
    return pl.pallas_call(
        functools.partial(_in_proj_kernel, tm, tn, seq_bounds, conv_lo, conv_hi, dil_ranges, u_range),
        grid=(m // tm, n // tn),
        in_specs=_row_halo_specs(tm, d, m) + [
            pl.BlockSpec((1, d), lambda i, j: (0, 0)),
            _layer_spec(w_main, layer, d, tn),
            pl.BlockSpec((None, d, DT_COLS), lambda i, j: (layer, 0, 0)),
            pl.BlockSpec((SSM_CONV, tn), cidx),
            pl.BlockSpec((1, tn), cidx),
            pl.BlockSpec(perms.shape, lambda i, j: (0, 0, 0, 0)),
            pl.BlockSpec(uperm.shape, lambda i, j: (0, 0, 0))],
        out_specs=[pl.BlockSpec((tm, tn), lambda i, j: (i, j)),
                   pl.BlockSpec((tm, DT_COLS), lambda i, j: (i, 0))],
        out_shape=[jax.ShapeDtypeStruct((m, n), BF16), jax.ShapeDtypeStruct((m, DT_COLS), F32)],
        scratch_shapes=[pltpu.VMEM((tm + 2 * HALO, d), BF16), pltpu.VMEM((tm + 2 * HALO, tn // 2), F32),
                        pltpu.VMEM((tm + 2 * HALO, tn // 2), F32)],
        compiler_params=_cparams(("parallel", "arbitrary"), 56),
        name="in_proj",
    )(x, x, x, gain.reshape(1, d), w_main, w_dt, conv_w, conv_b.reshape(1, -1), perms, uperm)


def _ffn_up_kernel(tm, seq_bounds, xp_ref, xm_ref, xn_ref, gain_ref, wg_ref, wv_ref, cwg_ref, cbg_ref,
                   cwv_ref, cbv_ref, o_ref, hn_ref, sg_ref, sv_ref):
    @pl.when(pl.program_id(1) == 0)
    def _():
        _fill_normed(hn_ref, xp_ref, xm_ref, xn_ref, gain_ref, tm, seq_bounds)

    hn = hn_ref[...]
    sg_ref[...] = jnp.dot(hn, wg_ref[...], preferred_element_type=F32)
    sv_ref[...] = jnp.dot(hn, wv_ref[...], preferred_element_type=F32)
    gate = _conv_taps(sg_ref, cwg_ref, cbg_ref, FFN_CONV, tm)
    val = _conv_taps(sv_ref, cwv_ref, cbv_ref, FFN_CONV, tm)
    o_ref[...] = (_silu(gate) * val).astype(o_ref.dtype)


def _ffn_up(x, gain, w, layer, conv_w, conv_b, seq_bounds, tm, tn=512):
    m, d = x.shape
    nb = D_FF_PAD // tn
    col = lambda rows, off: pl.BlockSpec((rows, tn), functools.partial(lambda i, j, o: (0, j + o), o=off))
    cb2 = conv_b.reshape(1, -1)
    return pl.pallas_call(
        functools.partial(_ffn_up_kernel, tm, seq_bounds),
        grid=(m // tm, nb),
        in_specs=_row_halo_specs(tm, d, m) + [
            pl.BlockSpec((1, d), lambda i, j: (0, 0)),
            _layer_spec(w, layer, d, tn), _layer_spec(w, layer, d, tn, nb),
            col(FFN_CONV, 0), col(1, 0), col(FFN_CONV, nb), col(1, nb)],
        out_specs=pl.BlockSpec((tm, tn), lambda i, j: (i, j)),
        out_shape=jax.ShapeDtypeStruct((m, D_FF_PAD), BF16),
        scratch_shapes=[pltpu.VMEM((tm + 2 * HALO, d), BF16), pltpu.VMEM((tm + 2 * HALO, tn), F32),
                        pltpu.VMEM((tm + 2 * HALO, tn), F32)],
        compiler_params=_cparams(("parallel", "arbitrary"), 56),
        name="ffn_up",
    )(x, x, x, gain.reshape(1, d), w, w, conv_w, cb2, conv_w, cb2)


def _softplus(x):
    return jnp.maximum(x, 0.0) + jnp.log1p(jnp.exp(-jnp.abs(x)))


def _ssd_cumsums(dt_ref, bias_ref, alog_ref, d, row0=0):
    t = SSM_CHUNK
    dt = _softplus(dt_ref[row0:row0 + t, d * 128:(d + 1) * 128] + bias_ref[d])
    da = dt * (-jnp.exp(alog_ref[d]) * math.log2(math.e))
    row = lax.broadcasted_iota(jnp.int32, (t, t), 0)
    col = lax.broadcasted_iota(jnp.int32, (t, t), 1)
    tri = jnp.where((col <= row) if d == 0 else (col >= row), 1.0, 0.0).astype(BF16)
    return dt, _sel_dot(tri, da)


SSD_SCAN_CHUNKS = 2


def _ssd_state_kernel(nsteps, seq_bounds, xf_ref, bf_ref, dtf_ref, xb_ref, bb_ref, dtb_ref, bias_ref, alog_ref,
                      hf_ref, hb_ref, h_ref):
    t = SSM_CHUNK
    n = SSM_STATE
    p = SSM_HEADDIM
    kc = SSD_SCAN_CHUNKS
    i = pl.program_id(0)

    @pl.when(i == 0)
    def _():
        h_ref[...] = jnp.zeros(h_ref.shape, F32)

    for d, (x_ref, b_ref, dt_ref, out_ref) in enumerate(((xf_ref, bf_ref, dtf_ref, hf_ref),
                                                         (xb_ref, bb_ref, dtb_ref, hb_ref))):
        blk = i if d == 0 else nsteps - 1 - i
        contrib = []
        for c in range(kc):
            r0 = c * t
            dt, cs = _ssd_cumsums(dt_ref, bias_ref, alog_ref, d, r0)
            tot = cs[t - 1:t, :] if d == 0 else cs[0:1, :]
            w = jnp.exp2(tot - cs) * dt
            etot = jnp.exp2(tot)
            per_group = []
            for g in range(SSM_GROUPS):
                bg = b_ref[r0:r0 + t, g * n:(g + 1) * n]
                xw, dec = [], []
                for hh in range(SSM_HPG):
                    h = g * SSM_HPG + hh
                    xh = x_ref[r0:r0 + t, h * p:(h + 1) * p].astype(F32)
                    xw.append((xh * jnp.broadcast_to(w[:, h:h + 1], (t, p))).astype(BF16))
                    dec.append(jnp.broadcast_to(etot[:, h:h + 1], (n, p)))
                st = lax.dot_general(bg, jnp.concatenate(xw, axis=1), (((0,), (0,)), ((), ())),
                                     preferred_element_type=F32)
                per_group.append((st, jnp.concatenate(dec, axis=1)))
            contrib.append(per_group)
        for c in (range(kc) if d == 0 else reversed(range(kc))):
            is_start, is_end = _seq_flags((blk * kc + c) * t, t, seq_bounds)
            reset = is_start if d == 0 else is_end
            for g in range(SSM_GROUPS):
                st, dec = contrib[c][g]
                hin = jnp.where(reset, 0.0, h_ref[d, g])
                out_ref[c, g] = hin.astype(BF16)
                h_ref[d, g] = hin * dec + st


def _ssd_states(proj, dtraw, dt_bias, a_log, seq_bounds):
    m = proj.shape[0]
    t = SSM_CHUNK
    kc = SSD_SCAN_CHUNKS
    nchunks = m // t
    nsteps = nchunks // kc
    hw = SSM_HPG * SSM_HEADDIM
    bw = SSM_GROUPS * SSM_STATE
    fwd = lambda i: i
    bwd = lambda i: nsteps - 1 - i
    xspec = lambda f: pl.BlockSpec((kc * t, SSM_INNER), lambda i: (f(i), COL_XBC // SSM_INNER))
    bspec = lambda f: pl.BlockSpec((kc * t, bw), lambda i: (f(i), (COL_XBC + SSM_INNER) // bw))
    dspec = lambda f: pl.BlockSpec((kc * t, DT_COLS), lambda i: (f(i), 0))
    hspec = lambda f: pl.BlockSpec((kc, SSM_GROUPS, SSM_STATE, hw), lambda i: (f(i), 0, 0, 0))
    const = pl.BlockSpec((2, 1, 128), lambda i: (0, 0, 0))
    return pl.pallas_call(
        functools.partial(_ssd_state_kernel, nsteps, seq_bounds),
        grid=(nsteps,),
        in_specs=[xspec(fwd), bspec(fwd), dspec(fwd), xspec(bwd), bspec(bwd), dspec(bwd), const, const],
        out_specs=[hspec(fwd), hspec(bwd)],
        out_shape=[jax.ShapeDtypeStruct((nchunks, SSM_GROUPS, SSM_STATE, hw), BF16)] * 2,
        scratch_shapes=[pltpu.VMEM((2, SSM_GROUPS, SSM_STATE, hw), F32)],
        compiler_params=_cparams(("arbitrary",), 32),
        name="ssd_states",
    )(proj, proj, dtraw, proj, proj, dtraw, dt_bias, a_log)


def _ssd_out_kernel(x_ref, b_ref, c_ref, dt_ref, bias_ref, alog_ref, hf_ref, hb_ref, z_ref, dskip_ref,
                    gain_ref, o_ref, y_ref):
    t = SSM_CHUNK
    n = SSM_STATE
    p = SSM_HEADDIM
    dtf, csf = _ssd_cumsums(dt_ref, bias_ref, alog_ref, 0)
    dtb, csb = _ssd_cumsums(dt_ref, bias_ref, alog_ref, 1)
    csf_t, csb_t, dtf_t, dtb_t = csf.T, csb.T, dtf.T, dtb.T
    dts_t = dtf_t + dtb_t
    row = lax.broadcasted_iota(jnp.int32, (t, t), 0)
    col = lax.broadcasted_iota(jnp.int32, (t, t), 1)
    low = col <= row
    low_strict = col < row
    up_strict = col > row
    for g in range(SSM_GROUPS):
        bg = b_ref[:, g * n:(g + 1) * n]
        cg = c_ref[:, g * n:(g + 1) * n]
        cb = lax.dot_general(cg, bg, (((1,), (1,)), ((), ())), preferred_element_type=F32)
        cg_f = cg.astype(F32)
        for hh in range(SSM_HPG):
            h = g * SSM_HPG + hh
            colf = jnp.broadcast_to(csf[:, h:h + 1], (t, t))
            colb = jnp.broadcast_to(csb[:, h:h + 1], (t, t))
            seg = jnp.where(low, colf - jnp.broadcast_to(csf_t[h:h + 1, :], (t, t)),
                            colb - jnp.broadcast_to(csb_t[h:h + 1, :], (t, t)))
            coef = jnp.where(low_strict, jnp.broadcast_to(dtf_t[h:h + 1, :], (t, t)),
                             jnp.where(up_strict, jnp.broadcast_to(dtb_t[h:h + 1, :], (t, t)),
                                       jnp.broadcast_to(dts_t[h:h + 1, :], (t, t))))
            mh = jnp.exp2(seg) * cb * coef
            lhs = jnp.concatenate([mh, cg_f * jnp.exp2(colf), cg_f * jnp.exp2(colb)], axis=1).astype(BF16)
            rhs = jnp.concatenate([x_ref[:, h * p:(h + 1) * p],
                                   hf_ref[0, g, :, hh * p:(hh + 1) * p],
                                   hb_ref[0, g, :, hh * p:(hh + 1) * p]], axis=0)
            y_ref[:, h * p:(h + 1) * p] = jnp.dot(lhs, rhs, preferred_element_type=F32)
    y = y_ref[...] + x_ref[...].astype(F32) * dskip_ref[...]
    y = y * _silu(z_ref[...].astype(F32))
    ms = jnp.mean(y * y, axis=-1, keepdims=True)
    o_ref[...] = (y * lax.rsqrt(ms + NORM_EPS) * gain_ref[...]).astype(o_ref.dtype)


def _ssd_out(proj, dtraw, dt_bias, a_log, h_f, h_b, d_skip, norm_g):
    m = proj.shape[0]
    t = SSM_CHUNK
    w = SSM_INNER
    hw = SSM_HPG * SSM_HEADDIM
    bw = SSM_GROUPS * SSM_STATE
    hspec = pl.BlockSpec((1, SSM_GROUPS, SSM_STATE, hw), lambda i: (i, 0, 0, 0))
    return pl.pallas_call(
        _ssd_out_kernel,
        grid=(m // t,),
        in_specs=[pl.BlockSpec((t, w), lambda i: (i, COL_XBC // w)),
                  pl.BlockSpec((t, bw), lambda i: (i, (COL_XBC + w) // bw)),
                  pl.BlockSpec((t, bw), lambda i: (i, (COL_XBC + w) // bw + 1)),
                  pl.BlockSpec((t, DT_COLS), lambda i: (i, 0)),
                  pl.BlockSpec((2, 1, 128), lambda i: (0, 0, 0)),
                  pl.BlockSpec((2, 1, 128), lambda i: (0, 0, 0)),
                  hspec, hspec,
                  pl.BlockSpec((t, w), lambda i: (i, COL_Z // w)),
                  pl.BlockSpec((1, w), lambda i: (0, 0)),
                  pl.BlockSpec((1, w), lambda i: (0, 0))],
        out_specs=pl.BlockSpec((t, w), lambda i: (i, 0)),
        out_shape=jax.ShapeDtypeStruct((m, w), BF16),
        scratch_shapes=[pltpu.VMEM((t, w), F32)],
        compiler_params=_cparams(("parallel",), 32),
        name="ssd_out",
    )(proj, proj, proj, dtraw, dt_bias, a_log, h_f, h_b, proj,
      jnp.repeat(d_skip.astype(F32), SSM_HEADDIM).reshape(1, w), norm_g.reshape(1, w))


def _ssd(proj, dtraw, dt_bias, a_log, d_skip, norm_g, seq_bounds):
    pad = lambda v: jnp.pad(v.astype(F32), ((0, 0), (0, 128 - SSM_HEADS))).reshape(2, 1, 128)
    dt_bias, a_log = pad(dt_bias), pad(a_log)
    h_f, h_b = _ssd_states(proj, dtraw, dt_bias, a_log, seq_bounds)
    return _ssd_out(proj, dtraw, dt_bias, a_log, h_f, h_b, d_skip, norm_g)


def _s5_selectors():
    t, c = S5_CHUNK, S5_GROUP
    lag = np.arange(S5_LAGS)[:, None]
    tok = (np.arange(S5_CW) // c)[None, :]
    e_exit_f = (lag == t - 1 - tok)
    e_exit_b = (lag == tok)
    e_in_f = (lag == tok + 1)
    e_in_b = (lag == t - tok)
    lagidx = (np.arange(2 * S5_CW) // c)[None, :]
    rel = lagidx - (t - 1)
    inb = lagidx <= 2 * t - 2
    e_k_f = (lag == rel) & (rel >= 0) & inb
    e_k_b = (lag == -rel) & (rel <= 0) & inb
    tile = (np.arange(c)[:, None] == (np.arange(2 * S5_CW) % c)[None, :])
    sel = np.stack([np.concatenate([a, b], axis=1) for a, b in
                    ((e_exit_f, e_in_f), (e_exit_b, e_in_b))])
    selk = np.stack([e_k_f, e_k_b])
    return (jnp.asarray(sel, BF16), jnp.asarray(selk, BF16), jnp.asarray(tile, BF16))


def _s5_prep_kernel(acol_ref, arow_ref, ls_ref, b_ref, bt_ref, ct_ref, sel_ref, selk_ref, tile_ref,
                    sign_ref, w_ref, ws_ref, wo_ref, lpa_ref, lpb_ref):
    t = S5_CHUNK
    cw = S5_CW
    p = S5_STATE
    tile = tile_ref[...]
    lagf = lax.broadcasted_iota(jnp.int32, (p, S5_LAGS), 1).astype(F32)
    kt = jnp.zeros((S5_GROUP, 2 * cw), F32)
    for d in range(2):
        step = jnp.exp(ls_ref[0, 0, d])
        are = acol_ref[0, 0, d, 0]
        aim = acol_ref[0, 0, d, 1]
        mag = are * step
        th = aim * step
        amp = jnp.exp(lagf * mag)
        pwr = amp * jnp.cos(lagf * th)
        pwi = amp * jnp.sin(lagf * th)
        lbr = jnp.exp(mag) * jnp.cos(th)
        lbi = jnp.exp(mag) * jnp.sin(th)
        den = are * are + aim * aim
        cfr = ((lbr - 1.0) * are + lbi * aim) / den
        cfi = (lbi * are - (lbr - 1.0) * aim) / den
        bre = b_ref[0, 0, 0]
        bim = b_ref[0, 0, 1]
        bbr = cfr * bre - cfi * bim
        bbi = cfr * bim + cfi * bre
        step_r = step
        are_r = arow_ref[0, 0, d, 0:1, 0:p]
        aim_r = arow_ref[0, 0, d, 1:2, 0:p]
        mag_r = are_r * step_r
        th_r = aim_r * step_r
        lbr_r = jnp.exp(mag_r) * jnp.cos(th_r)
        lbi_r = jnp.exp(mag_r) * jnp.sin(th_r)
        den_r = are_r * are_r + aim_r * aim_r
        cfr_r = ((lbr_r - 1.0) * are_r + lbi_r * aim_r) / den_r
        cfi_r = (lbi_r * are_r - (lbr_r - 1.0) * aim_r) / den_r
        btr = bt_ref[0, 0, 0]
        bti = bt_ref[0, 0, 1]
        bbr_t = cfr_r * btr - cfi_r * bti
        bbi_t = cfr_r * bti + cfi_r * btr
        ctr = ct_ref[0, 0, d, 0]
        cti = ct_ref[0, 0, d, 1]

        sel = sel_ref[d]
        er = _dot_sel(pwr, sel)
        ei = _dot_sel(pwi, sel)
        tb_r = _dot_sel(bbr, tile[:, 0:cw])
        tb_i = _dot_sel(bbi, tile[:, 0:cw])
        tc_r = _dot_sel(ctr, tile)
        tc_i = _dot_sel(cti, tile)
        ws_ref[0, 0, d * 2 * p:d * 2 * p + p, :] = (er[:, 0:cw] * tb_r - ei[:, 0:cw] * tb_i).astype(BF16)
        ws_ref[0, 0, d * 2 * p + p:(d + 1) * 2 * p, :] = (er[:, 0:cw] * tb_i + ei[:, 0:cw] * tb_r).astype(BF16)
        zr = er[:, cw:] * tc_r[:, 0:cw] - ei[:, cw:] * tc_i[:, 0:cw]
        zi = er[:, cw:] * tc_i[:, 0:cw] + ei[:, cw:] * tc_r[:, 0:cw]
        wo_ref[0, 0, d * 2 * p:d * 2 * p + p, :] = zr.astype(BF16)
        wo_ref[0, 0, d * 2 * p + p:(d + 1) * 2 * p, :] = (-zi).astype(BF16)
        selk = selk_ref[d]
        kr, ki = _dot_sel(pwr, selk), _dot_sel(pwi, selk)
        qr = kr * tc_r - ki * tc_i
        qi = kr * tc_i + ki * tc_r
        kt = kt + _dot_f32(bbr_t, qr) - _dot_f32(bbi_t, qi)
        are2 = arow_ref[0, 0, d, 0:1, :]
        aim2 = arow_ref[0, 0, d, 1:2, :]
        ampt = jnp.exp(are2 * step_r * float(t))
        zr2 = ampt * jnp.cos(aim2 * step_r * float(t))
        zi2 = ampt * jnp.sin(aim2 * step_r * float(t))
        sign = sign_ref[...]
        for k in range(S5_SCAN_STEPS):
            lpa_ref[0, 0, d, k:k + 1, :] = zr2
            lpb_ref[0, 0, d, k:k + 1, :] = sign * zi2
            zr2, zi2 = zr2 * zr2 - zi2 * zi2, 2.0 * zr2 * zi2
    for s in range(t):
        off = (t - 1 - s) * S5_GROUP
        w_ref[0, 0, s * S5_GROUP:(s + 1) * S5_GROUP, :] = kt[:, off:off + cw].astype(BF16)


def _s5_prep(a_re, a_im, log_step, b_re, b_im, c_re, c_im):
    depth = a_re.shape[0]
    g, p, c = S5_GROUPS, S5_STATE, S5_GROUP
    a = jnp.stack([a_re, a_im], axis=2).astype(F32)
    a = a.transpose(0, 3, 1, 2, 4)
    acol = a[..., None]
    arow = jnp.concatenate([a, a], axis=-1)
    ls = log_step.astype(F32).transpose(0, 2, 1).reshape(depth, g, 2, 1, 1)
    b = jnp.stack([b_re, b_im], axis=2).astype(F32)
    bt = b.transpose(0, 1, 2, 4, 3)
    ct = jnp.stack([c_re, c_im], axis=3).astype(F32)
    ct = ct.transpose(0, 2, 1, 3, 5, 4)
    sel, selk, tile = _s5_selectors()
    sign = jnp.concatenate([-jnp.ones((1, p), F32), jnp.ones((1, p), F32)], axis=1)
    full = lambda shp: pl.BlockSpec(shp, lambda l, j: (0,) * len(shp))
    per = lambda shp: pl.BlockSpec((1, 1) + shp, lambda l, j: (l, j) + (0,) * len(shp))
    cw = S5_CW
    return pl.pallas_call(
        _s5_prep_kernel,
        grid=(depth, g),
        in_specs=[per((2, 2, p, 1)), per((2, 2, 2 * p)), per((2, 1, 1)), per((2, p, c)), per((2, c, p)),
                  per((2, 2, p, c)), full((2, S5_LAGS, 2 * cw)), full((2, S5_LAGS, 2 * cw)),
                  full((c, 2 * cw)), full((1, 2 * p))],
        out_specs=[per((cw, cw)), per((4 * p, cw)), per((4 * p, cw)),
                   per((2, S5_SCAN_STEPS, 2 * p)), per((2, S5_SCAN_STEPS, 2 * p))],
        out_shape=[jax.ShapeDtypeStruct((depth, g, cw, cw), BF16),
                   jax.ShapeDtypeStruct((depth, g, 4 * p, cw), BF16),
                   jax.ShapeDtypeStruct((depth, g, 4 * p, cw), BF16),
                   jax.ShapeDtypeStruct((depth, g, 2, S5_SCAN_STEPS, 2 * p), F32),
                   jax.ShapeDtypeStruct((depth, g, 2, S5_SCAN_STEPS, 2 * p), F32)],
        compiler_params=_cparams(("parallel", "parallel"), 32),
        name="s5_prep",
    )(acol, arow, ls, b, bt, ct, sel, selk, tile, sign)


def _gelu_tanh(x):
    return 0.5 * x * (1.0 + jnp.tanh(math.sqrt(2.0 / math.pi) * (x + 0.044715 * (x * x * x))))


LANES = 128


BF16_ROWS = 16
S5_PERM_ROWS = BF16_ROWS * S5_CHUNK
S5_GPB = LANES // S5_GROUP


def _s5_kernel(chunk_bounds, u_ref, pick_ref, w_ref, ws_ref, wo_ref, lpa_ref, lpb_ref,
               d_ref, o_ref, cat_ref, ug_ref, yg_ref):
    t = S5_CHUNK
    g = pl.program_id(1)
    r = u_ref.shape[0] // t
    nblk = r // BF16_ROWS
    p2 = 2 * S5_STATE
    qw = S5_GPB * LANES

    @pl.when(g == 0)
    def _():
        def body(b, carry):
            crow = pl.ds(pl.multiple_of(b * BF16_ROWS, BF16_ROWS), BF16_ROWS)
            for s in range(t):
                rows = pl.ds(pl.multiple_of(b * S5_PERM_ROWS + s * BF16_ROWS, BF16_ROWS), BF16_ROWS)
                cat_ref[crow, s * LANES:(s + 1) * LANES] = u_ref[rows, :]
            return carry
        lax.fori_loop(0, nblk, body, 0)
        for q in range(t // S5_GPB):
            picked = jnp.dot(cat_ref[:, q * qw:(q + 1) * qw], pick_ref[0],
                             preferred_element_type=F32).astype(BF16)
            for gg in range(S5_GPB):
                ug_ref[gg, :, q * LANES:(q + 1) * LANES] = picked[:, gg * LANES:(gg + 1) * LANES]

    u = ug_ref[g]
    uf = u.astype(F32)
    y = jnp.dot(u, w_ref[0, 0], preferred_element_type=F32)
    st = lax.dot_general(u, ws_ref[0, 0], (((1,), (1,)), ((), ())), preferred_element_type=F32)
    ridx = lax.broadcasted_iota(jnp.int32, (r, p2), 0)
    rloc = jnp.zeros((r, p2), jnp.int32)
    rlen = jnp.zeros((r, p2), jnp.int32)
    for s0, ln in chunk_bounds:
        inside = (ridx >= s0) & (ridx < s0 + ln)
        rloc = jnp.where(inside, ridx - s0, rloc)
        rlen = jnp.where(inside, ln, rlen)
    xin = []
    for d in range(2):
        x = st[:, d * p2:(d + 1) * p2]
        for k in range(S5_SCAN_STEPS):
            sh = 1 << k
            if d == 0:
                prev = jnp.where(rloc >= sh, pltpu.roll(x, sh, 0), 0.0)
            else:
                prev = jnp.where(rloc < rlen - sh, pltpu.roll(x, r - sh, 0), 0.0)
            x = (x + lpa_ref[0, 0, d, k:k + 1, :] * prev
                 + lpb_ref[0, 0, d, k:k + 1, :] * pltpu.roll(prev, S5_STATE, 1))
        if d == 0:
            xin.append(jnp.where(rloc >= 1, pltpu.roll(x, 1, 0), 0.0))
        else:
            xin.append(jnp.where(rloc < rlen - 1, pltpu.roll(x, r - 1, 0), 0.0))
    xin = jnp.concatenate(xin, axis=1).astype(BF16)
    y = y + jnp.dot(xin, wo_ref[0, 0], preferred_element_type=F32)
    y = _gelu_tanh(y + uf * d_ref[0])

    yg_ref[g] = y.astype(BF16)

    @pl.when(g == pl.num_programs(1) - 1)
    def _():
        for q in range(t // S5_GPB):
            lhs = jnp.concatenate([yg_ref[gg, :, q * LANES:(q + 1) * LANES] for gg in range(S5_GPB)], axis=1)
            cat_ref[:, q * qw:(q + 1) * qw] = jnp.dot(lhs, pick_ref[1],
                                                      preferred_element_type=F32).astype(BF16)

        def body(b, carry):
            crow = pl.ds(pl.multiple_of(b * BF16_ROWS, BF16_ROWS), BF16_ROWS)
            for s in range(t):
                rows = pl.ds(pl.multiple_of(b * S5_PERM_ROWS + s * BF16_ROWS, BF16_ROWS), BF16_ROWS)
                o_ref[rows, :] = cat_ref[crow, s * LANES:(s + 1) * LANES]
            return carry
        lax.fori_loop(0, nblk, body, 0)


def _s5_movers():
    dst = np.arange(S5_PERM_ROWS)[:, None]
    fwd = (dst % BF16_ROWS) * S5_CHUNK + dst // BF16_ROWS == np.arange(S5_PERM_ROWS)[None, :]
    src = np.arange(S5_GPB * LANES)
    s8, g, j = src // LANES, src % LANES // S5_GROUP, src % S5_GROUP
    pick = (g * LANES + s8 * S5_GROUP + j)[:, None] == src[None, :]
    return jnp.asarray(np.stack([fwd, fwd.T]), BF16), jnp.asarray(np.stack([pick, pick.T]), BF16)


def _s5(proj, tables, li, d_tiled, chunk_bounds):
    m = proj.shape[0]
    w, ws, wo, lpa, lpb = tables
    p = S5_STATE
    cw = S5_CW
    gpb = S5_GPB
    r = m // S5_CHUNK
    _, pick = _s5_movers()
    per = lambda shp: pl.BlockSpec((1, 1) + shp, lambda b, j: (li, b * gpb + j) + (0,) * len(shp))
    return pl.pallas_call(
        functools.partial(_s5_kernel, chunk_bounds),
        grid=(S5_GROUPS // gpb, gpb),
        in_specs=[pl.BlockSpec((m, LANES), lambda b, j: (0, COL_U // LANES + b)),
                  pl.BlockSpec((2, gpb * LANES, gpb * LANES), lambda b, j: (0, 0, 0)),
                  per((cw, cw)), per((4 * p, cw)), per((4 * p, cw)),
                  per((2, S5_SCAN_STEPS, 2 * p)), per((2, S5_SCAN_STEPS, 2 * p)),
                  pl.BlockSpec((1, 1, cw), lambda b, j: (b * gpb + j, 0, 0))],
        out_specs=pl.BlockSpec((m, LANES), lambda b, j: (0, b)),
        out_shape=jax.ShapeDtypeStruct((m, S5_WIDTH), BF16),
        scratch_shapes=[pltpu.VMEM((r, S5_CHUNK * LANES), BF16), pltpu.VMEM((gpb, r, cw), BF16),
                        pltpu.VMEM((gpb, r, cw), BF16)],
        compiler_params=_cparams(("parallel", "arbitrary"), 48),
        name="s5_mix",
    )(proj, pick, w, ws, wo, lpa, lpb, d_tiled)


def _t5_bucket(rel):
    half = REL_BUCKETS // 2
    exact = half // 2
    sign = (rel > 0).astype(np.int32) * half
    n = np.abs(rel)
    large = exact + (np.log(np.maximum(n, 1) / exact) / np.log(REL_MAX_DIST / exact)
                     * (half - exact)).astype(np.int32)
    large = np.minimum(large, half - 1)
    return sign + np.where(n < exact, n, large)


def _att_bias_kernel(gi, idx_ref, tbl_ref, o_ref):
    idx = idx_ref[...]
    for h in range(ATT_HPG):
        acc = jnp.full(idx.shape, NEG_INF, F32)
        for b in range(REL_BUCKETS):
            acc = jnp.where(idx == b, tbl_ref[b, gi * ATT_HPG + h], acc)
        o_ref[h] = acc


ATT_TQ = 256
ATT_QSUB = 128


def _att_tile(seq_bounds, dil):
    return min(ATT_TQ, min(l for _, l in seq_bounds) // dil)


def _att_bias(rel_bias, gi, tq):
    dil = ATT_PATTERNS[gi][1]
    tk = tq + 2 * ATT_HALF
    rel = (np.arange(tk)[None, :] - ATT_HALF) - np.arange(tq)[:, None]
    idx = np.where(np.abs(rel) <= ATT_HALF, _t5_bucket(rel * dil), -1).astype(np.int32)
    return pl.pallas_call(
        functools.partial(_att_bias_kernel, gi),
        in_specs=[pl.BlockSpec(memory_space=pltpu.VMEM), pl.BlockSpec(memory_space=pltpu.SMEM)],
        out_specs=pl.BlockSpec(memory_space=pltpu.VMEM),
        out_shape=jax.ShapeDtypeStruct((ATT_HPG, tq, tk), F32),
        name=f"attention_bias_{gi}",
    )(jnp.asarray(idx), rel_bias.astype(F32))


def _attn_kernel(tq, blk_bounds, q_ref, kp_ref, km_ref, kn_ref, vp_ref, vm_ref, vn_ref, bias_ref,
                 o_ref, lse_ref):
    jb = pl.program_id(1)
    is_first, is_last = _seq_flags(jb, 1, blk_bounds)
    flat = lambda ref: ref[...].reshape(-1, ref.shape[-1])
    q = flat(q_ref)
    kcat = jnp.concatenate([flat(kp_ref), flat(km_ref), flat(kn_ref)], axis=0)
    vcat = jnp.concatenate([flat(vp_ref), flat(vm_ref), flat(vn_ref)], axis=0)
    scale = ATT_HEAD_DIM ** -0.5
    qs = min(tq, ATT_QSUB)
    lane = lax.broadcasted_iota(jnp.int32, (qs, ATT_HEAD_DIM), 1)
    out_rows, lse_rows = [], []
    for a in range(0, tq, qs):
        keys = slice(a, a + qs + 2 * ATT_HALF)
        colk = a + lax.broadcasted_iota(jnp.int32, (qs, qs + 2 * ATT_HALF), 1)
        valid = jnp.logical_and(jnp.logical_or(colk >= ATT_HALF, jnp.logical_not(is_first)),
                                jnp.logical_or(colk < tq + ATT_HALF, jnp.logical_not(is_last)))
        lse_tile = jnp.zeros((qs, ATT_HEAD_DIM), F32)
        outs = []
        for h in range(ATT_HPG):
            sl = slice(h * ATT_HEAD_DIM, (h + 1) * ATT_HEAD_DIM)
            s = lax.dot_general(q[a:a + qs, sl], kcat[keys, sl], (((1,), (1,)), ((), ())),
                                preferred_element_type=F32)
            s = jnp.where(valid, s * scale + bias_ref[h, a:a + qs, keys], NEG_INF)
            mx = jnp.max(s, axis=-1, keepdims=True)
            pr = jnp.exp(s - mx)
            den = jnp.sum(pr, axis=-1, keepdims=True)
            o = jnp.dot(pr.astype(BF16), vcat[keys, sl], preferred_element_type=F32)
            outs.append((o / den).astype(o_ref.dtype))
            lse_tile = jnp.where(lane == h, mx + jnp.log(den), lse_tile)
        out_rows.append(jnp.concatenate(outs, axis=1))
        lse_rows.append(lse_tile)
    o_ref[...] = jnp.concatenate(out_rows, axis=0).reshape(o_ref.shape)
    lse_ref[...] = jnp.concatenate(lse_rows, axis=0).reshape(lse_ref.shape)


def _attention_group(proj, bias, gi, seq_bounds):
    window, dil = ATT_PATTERNS[gi]
    assert window // (2 * dil) == ATT_HALF
    m = proj.shape[0]
    tq = bias.shape[1]
    blk = ATT_BLOCK if dil > 1 else tq
    per = blk // dil
    nbq = tq // per
    hr = min(per, ATT_HALF)
    hb = ATT_HALF // hr
    nblocks = m // blk
    assert tq % per == 0 and nbq % hb == 0 and per % hr == 0
    assert all((s // blk) % nbq == 0 and (l // blk) % nbq == 0 for s, l in seq_bounds)
    blk_bounds = tuple((s // blk // nbq, l // blk // nbq) for s, l in seq_bounds)
    tk = tq + 2 * ATT_HALF
    cb = (COL_QKV + gi * 3 * ATT_GW) // ATT_GW
    src = proj.reshape(nblocks, blk, proj.shape[1])
    lasth = nblocks // hb - 1

    def main(col):
        return pl.BlockSpec((nbq, per, ATT_GW), lambda r, jb: (jb, r, cb + col))

    def prev(col):
        return pl.BlockSpec((hb, hr, ATT_GW),
                            lambda r, jb: (jnp.maximum(jb * (nbq // hb) - 1, 0), (r + 1) * (per // hr) - 1, cb + col))

    def nxt(col):
        return pl.BlockSpec((hb, hr, ATT_GW),
                            lambda r, jb: (jnp.minimum((jb + 1) * (nbq // hb), lasth), r * (per // hr), cb + col))

    o, lse = pl.pallas_call(
        functools.partial(_attn_kernel, tq, blk_bounds),
        grid=(dil, nblocks // nbq),
        in_specs=[main(0), prev(1), main(1), nxt(1), prev(2), main(2), nxt(2),
                  pl.BlockSpec((ATT_HPG, tq, tk), lambda r, jb: (0, 0, 0))],
        out_specs=[pl.BlockSpec((nbq, per, ATT_GW), lambda r, jb: (jb, r, 0)),
                   pl.BlockSpec((nbq, per, ATT_HEAD_DIM), lambda r, jb: (jb, r, 0))],
        out_shape=[jax.ShapeDtypeStruct((nblocks, blk, ATT_GW), BF16),
                   jax.ShapeDtypeStruct((nblocks, blk, ATT_HEAD_DIM), F32)],
        compiler_params=_cparams(("parallel", "parallel"), 32),
        name=f"dilated_attention_{gi}",
    )(src, src, src, src, src, src, src, bias)
    return o.reshape(m, ATT_GW), lse.reshape(m, ATT_HEAD_DIM)


def _att_combine_kernel(o0, l0, o1, l1, o2, l2, perm_ref, out_ref):
    tr = out_ref.shape[0]
    outs, lses = [o0[...]], [l0[...]]
    for k, (o_ref, l_ref) in enumerate(((o1, l1), (o2, l2))):
        inv = perm_ref[k, 1]
        ob, lb = [], []
        for b in range(tr // ATT_BLOCK):
            rows = slice(b * ATT_BLOCK, (b + 1) * ATT_BLOCK)
            ob.append(jnp.dot(inv, o_ref[rows, :], preferred_element_type=F32))
            lb.append(_sel_dot(inv, l_ref[rows, :]))
        outs.append(jnp.concatenate(ob, axis=0))
        lses.append(jnp.concatenate(lb, axis=0))
    a, b, c = lses
    mx = jnp.maximum(jnp.maximum(a, b), c)
    ea, eb, ec = jnp.exp(a - mx), jnp.exp(b - mx), jnp.exp(c - mx)
    inv = 1.0 / (ea + eb + ec)
    wa, wb, wc = ea * inv, eb * inv, ec * inv
    for h in range(ATT_HPG):
        sl = slice(h * ATT_HEAD_DIM, (h + 1) * ATT_HEAD_DIM)
        bc = lambda w: jnp.broadcast_to(w[:, h:h + 1], (tr, ATT_HEAD_DIM))
        out_ref[:, sl] = (bc(wa) * outs[0][:, sl].astype(F32) + bc(wb) * outs[1][:, sl].astype(F32)
                          + bc(wc) * outs[2][:, sl].astype(F32)).astype(out_ref.dtype)


def _att_combine(outs, lses, tr=512):
    assert [gi for gi, _ in _att_dilated()] == [1, 2]
    m, w = outs[0].shape
    ospec = pl.BlockSpec((tr, w), lambda i: (i, 0))
    lspec = pl.BlockSpec((tr, ATT_HEAD_DIM), lambda i: (i, 0))
    args = [x for pair in zip(outs, lses) for x in pair]
    perms = _att_row_perms()
    return pl.pallas_call(
        _att_combine_kernel,
        grid=(m // tr,),
        in_specs=[ospec, lspec] * 3 + [pl.BlockSpec(perms.shape, lambda i: (0, 0, 0, 0))],
        out_specs=ospec,
        out_shape=jax.ShapeDtypeStruct((m, w), BF16),
        compiler_params=_cparams(("parallel",), 32),
        name="attention_combine",
    )(*args, perms)


PACK_TILE = 512


def _pack_cols_kernel(short_tiles, *refs):
    parts, o_ref = refs[:-1], refs[-1]
    j = pl.program_id(2)
    valid = PACK_TILE
    for lo, hi, v in short_tiles:
        valid = jnp.where(jnp.logical_and(j >= lo, j < hi), v, valid)
    val = jnp.concatenate([p[0] for p in parts], axis=1)
    col = lax.broadcasted_iota(jnp.int32, val.shape, 1)
    o_ref[0] = jnp.where(col < valid, val, 0.0).astype(o_ref.dtype)


def _pack_cols(w, n_tiles, src_lane_block, short_tiles):
    depth, k, n = w.shape
    tk = min(2048, k)
    last = -(-n // LANES) - 1
    part = lambda q: pl.BlockSpec((1, tk, LANES), lambda l, i, j: (l, i, jnp.minimum(src_lane_block(j, q), last)))
    nparts = PACK_TILE // LANES
    return pl.pallas_call(
        functools.partial(_pack_cols_kernel, short_tiles),
        grid=(depth, k // tk, n_tiles),
        in_specs=[part(q) for q in range(nparts)],
        out_specs=pl.BlockSpec((1, tk, PACK_TILE), lambda l, i, j: (l, i, j)),
        out_shape=jax.ShapeDtypeStruct((depth, k, n_tiles * PACK_TILE), BF16),
        compiler_params=_cparams(("parallel", "parallel", "parallel"), 32),
        name="pack_weight_columns",
    )(*([w] * nparts))


def _pack_w_in_kernel(t_u, shift, a_ref, b_ref, o_ref, dt_ref):
    j = pl.program_id(2)

    @pl.when(j < t_u)
    def _():
        o_ref[...] = a_ref[...].T.astype(o_ref.dtype)

    @pl.when(j >= t_u)
    def _():
        rows = jnp.concatenate([a_ref[...], b_ref[...]], axis=0)[shift:shift + PACK_TILE, :]
        o_ref[...] = rows.T.astype(o_ref.dtype)

    @pl.when(j == t_u)
    def _():
        t = a_ref[0:LANES, :].T
        dt_ref[:, 0:LANES] = t.astype(dt_ref.dtype)
        dt_ref[:, LANES:2 * LANES] = pltpu.roll(t, LANES - SSM_HEADS, 1).astype(dt_ref.dtype)


def _pack_w_in(w_in):
    ng = len(ATT_PATTERNS)
    raw_dt = SSM_INNER + SSM_XBC
    shift = 2 * SSM_HEADS
    t_u = COL_U // PACK_TILE
    t_qkv = COL_QKV // PACK_TILE
    t_gates = COL_GATES // PACK_TILE
    assert raw_dt % PACK_TILE == 0 and COL_U == raw_dt

    def src_block(j):
        jj = jnp.clip(j - t_qkv, 0, 3 * ng - 1)
        return jnp.where(jnp.logical_and(j >= t_qkv, j < t_gates), t_qkv + (jj % 3) * ng + jj // 3, j)

    depth, d, n = w_in.shape
    tk = min(2048, d)
    last = -(-n // LANES) - 1
    sub = PACK_TILE // LANES
    w_t = jnp.swapaxes(w_in, 1, 2)
    return pl.pallas_call(
        functools.partial(_pack_w_in_kernel, t_u, shift),
        grid=(depth, d // tk, IN_COLS // PACK_TILE),
        in_specs=[pl.BlockSpec((None, PACK_TILE, tk), lambda l, i, j: (l, src_block(j), i)),
                  pl.BlockSpec((None, LANES, tk), lambda l, i, j: (l, jnp.minimum((src_block(j) + 1) * sub, last), i))],
        out_specs=[pl.BlockSpec((None, tk, PACK_TILE), lambda l, i, j: (l, i, j)),
                   pl.BlockSpec((None, tk, DT_COLS), lambda l, i, j: (l, i, 0))],
        out_shape=[jax.ShapeDtypeStruct((depth, d, IN_COLS), BF16), jax.ShapeDtypeStruct((depth, d, DT_COLS), BF16)],
        compiler_params=_cparams(("parallel", "parallel", "arbitrary"), 32),
        name="pack_w_in",
    )(w_t, w_t)


def _pack_ffn(w_up, conv_w, conv_b, w_down):
    nb = D_FF_PAD // PACK_TILE
    full, rem = divmod(D_FF, PACK_TILE)
    nparts = PACK_TILE // LANES
    assert nb == full + 1 and rem > 0 and D_FF % LANES == 0
    src = lambda j, q: jnp.where(j < nb, j * nparts + q, D_FF // LANES + (j - nb) * nparts + q)
    w_up_p = _pack_cols(w_up, 2 * nb, src, ((full, nb, rem), (nb + full, 2 * nb, rem)))
    padc = lambda a: jnp.pad(a, [(0, 0)] * (a.ndim - 1) + [(0, D_FF_PAD - D_FF)])
    both = lambda a: jnp.concatenate([padc(a[..., :D_FF]), padc(a[..., D_FF:])], axis=-1)
    w_down_p = jnp.pad(w_down, ((0, 0), (0, D_FF_PAD - D_FF), (0, 0))).astype(BF16)
    return w_up_p, both(conv_w).astype(F32), both(conv_b).astype(F32), w_down_p


def _trunk(x, seq_bounds, rel_bias, norm_mix, w_in, ssm_conv_w, ssm_conv_b, ssm_a_log, ssm_dt_bias, ssm_d,
           ssm_norm, ssm_w_out, s5_tables, s5_d, s5_w_glu, att_w_out, w_o, norm_ffn, w_up, ffn_conv_w,
           ffn_conv_b, w_down):
    m = x.shape[0]
    depth = w_in.shape[0]
    tm = min(1024, m)
    tm2 = min(512, m)
    chunk_bounds = tuple((s // S5_CHUNK, l // S5_CHUNK) for s, l in seq_bounds)
    att_bias = [_att_bias(rel_bias, gi, _att_tile(seq_bounds, dil)) for gi, (_, dil) in enumerate(ATT_PATTERNS)]
    resid = lambda acc, res: res + acc

    w_main, w_dt = _pack_w_in(w_in)
    w_up_p, cw_p, cb_p, w_down_p = _pack_ffn(w_up, ffn_conv_w, ffn_conv_b, w_down)
    ssm_w_out, s5_w_glu, att_w_out, w_o = (w.astype(BF16) for w in (ssm_w_out, s5_w_glu, att_w_out, w_o))

    for li in range(depth):
        proj, dtraw = _in_proj(x, norm_mix[li].astype(F32), w_main, w_dt, li, ssm_conv_w[li].astype(F32),
                               ssm_conv_b[li].astype(F32), seq_bounds, tm)

        y_a = _ssd(proj, dtraw, ssm_dt_bias[li], ssm_a_log[li], ssm_d[li], ssm_norm[li].astype(F32), seq_bounds)

        d_tiled = jnp.tile(s5_d[li].astype(F32).reshape(S5_GROUPS, 1, S5_GROUP), (1, 1, S5_CHUNK))
        y_b = _s5(proj, s5_tables, li, d_tiled, chunk_bounds)

        outs, lses = zip(*[_attention_group(proj, att_bias[gi], gi, seq_bounds)
                           for gi in range(len(ATT_PATTERNS))])
        y_c = _att_combine(outs, lses, tr=tm2)

        merged = _merge_branches(y_a, y_b, y_c, ssm_w_out, s5_w_glu, att_w_out, li, proj, tm=tm, tn=512)
        x = _matmul(merged, w_o, layer=li, tm=tm, tn=512, out_dtype=F32, epilogue=resid,
                    extras=((x, 0),), name="mix_out_proj")

        act = _ffn_up(x, norm_ffn[li].astype(F32), w_up_p, li, cw_p[li], cb_p[li], seq_bounds, tm)
        x = _matmul(act, w_down_p, layer=li, tm=tm, tn=512, out_dtype=F32, epilogue=resid, extras=((x, 0),),
                    vmem_mb=56, name="ffn_down")
    return x


def kernel(x_prompt, x_sample, rel_bias, norm_mix, w_in, ssm_conv_w, ssm_conv_b, ssm_a_log, ssm_dt_bias, ssm_d, ssm_norm, ssm_w_out, s5_a_re, s5_a_im, s5_log_step, s5_b_re, s5_b_im, s5_c_re, s5_c_im, s5_d, s5_w_glu, att_w_out, w_o, norm_ffn, w_up, ffn_conv_w, ffn_conv_b, w_down, final_norm):
    d = x_prompt.shape[-1]
    seq_bounds = []
    row = 0
    for arr in (x_prompt, x_sample):
        for _ in range(arr.shape[0]):
            seq_bounds.append((row, arr.shape[1]))
            row += arr.shape[1]
    seq_bounds = tuple(seq_bounds)
    x = jnp.concatenate([x_prompt.reshape(-1, d), x_sample.reshape(-1, d)], axis=0)
    s5_tables = _s5_prep(s5_a_re, s5_a_im, s5_log_step, s5_b_re, s5_b_im, s5_c_re, s5_c_im)
    y = _trunk(x, seq_bounds, rel_bias, norm_mix, w_in, ssm_conv_w, ssm_conv_b, ssm_a_log, ssm_dt_bias, ssm_d,
               ssm_norm, ssm_w_out, s5_tables, s5_d, s5_w_glu, att_w_out, w_o, norm_ffn, w_up, ffn_conv_w,
               ffn_conv_b, w_down)
    n_prompt = x_prompt.shape[0] * x_prompt.shape[1]
    y_prompt, y_sample = _final_norm(y, final_norm.astype(F32), n_prompt, tr=min(512, n_prompt))
    return (y_prompt.reshape(x_prompt.shape), y_sample.reshape(x_sample.shape))
```

```python
import functools
import math

import numpy as np
import jax
import jax.numpy as jnp
from jax import lax
from jax.experimental import pallas as pl
from jax.experimental.pallas import tpu as pltpu

F32 = jnp.float32
BF16 = jnp.bfloat16

D_MODEL = 2048
NORM_EPS = 1e-6
NEG_INF = -1e30

SSM_HEADDIM = 64
SSM_INNER = 1536
SSM_HEADS = 24
SSM_GROUPS = 4
SSM_HPG = SSM_HEADS // SSM_GROUPS
SSM_STATE = 128
SSM_CONV = 5
SSM_CHUNK = 128
SSM_XBC = 2560

S5_WIDTH = 1024
S5_GROUP = 16
S5_GROUPS = 64
S5_STATE = 64
S5_CHUNK = 32
S5_CW = S5_CHUNK * S5_GROUP
S5_LAGS = 128
S5_SCAN_STEPS = 8

ATT_HEAD_DIM = 128
ATT_HPG = 4
ATT_PATTERNS = ((128, 1), (512, 4), (2048, 16))
ATT_GW = ATT_HPG * ATT_HEAD_DIM
ATT_HALF = 64
REL_BUCKETS = 32
REL_MAX_DIST = 1024

D_FF = 5504
D_FF_PAD = 5632
FFN_CONV = 3

COL_Z = 0
COL_XBC = 1536
COL_U = 4096
COL_QKV = 5120
COL_GATES = 9728
IN_COLS = 15872
DT_COLS = 256

V7X_VMEM_BYTES = 64 * 1024 * 1024


def _cparams(sem, vmem_mb):
    assert vmem_mb * 1024 * 1024 <= V7X_VMEM_BYTES
    return pltpu.CompilerParams(dimension_semantics=sem, vmem_limit_bytes=vmem_mb * 1024 * 1024)


def _sigmoid(x):
    return 1.0 / (1.0 + jnp.exp(-x))


def _silu(x):
    return x * _sigmoid(x)


def _split3(x):
    hi = x.astype(BF16)
    r1 = x - hi.astype(F32)
    mid = r1.astype(BF16)
    lo = (r1 - mid.astype(F32)).astype(BF16)
    return hi, mid, lo


def _dot_sel(x, sel, pieces=2):
    d = lambda a: jnp.dot(a, sel, preferred_element_type=F32)
    return sum(d(piece) for piece in _split3(x)[:pieces])


def _sel_dot(sel, x):
    hi, mid, lo = _split3(x)
    d = lambda a: jnp.dot(sel, a, preferred_element_type=F32)
    return d(hi) + d(mid) + d(lo)


def _dot_f32(a, b):
    ah, am, _ = _split3(a)
    bh, bm, _ = _split3(b)
    d = lambda x, y: jnp.dot(x, y, preferred_element_type=F32)
    return d(ah, bh) + d(ah, bm) + d(am, bh)


def _final_norm_kernel(nb_first, x_ref, g_ref, first_ref, second_ref):
    x = x_ref[...]
    ms = jnp.mean(x * x, axis=-1, keepdims=True)
    y = x * lax.rsqrt(ms + NORM_EPS) * g_ref[...]
    i = pl.program_id(0)

    @pl.when(i < nb_first)
    def _():
        first_ref[...] = y

    @pl.when(i >= nb_first)
    def _():
        second_ref[...] = y


def _final_norm(x, gain, n_first, tr=512):
    m, d = x.shape
    assert n_first % tr == 0 and (m - n_first) % tr == 0 and 0 < n_first < m
    nb_first = n_first // tr
    return pl.pallas_call(
        functools.partial(_final_norm_kernel, nb_first),
        grid=(m // tr,),
        in_specs=[pl.BlockSpec((tr, d), lambda i: (i, 0)),
                  pl.BlockSpec((1, d), lambda i: (0, 0))],
        out_specs=[pl.BlockSpec((tr, d), lambda i: (jnp.minimum(i, nb_first - 1), 0)),
                   pl.BlockSpec((tr, d), lambda i: (jnp.maximum(i - nb_first, 0), 0))],
        out_shape=[jax.ShapeDtypeStruct((n_first, d), x.dtype), jax.ShapeDtypeStruct((m - n_first, d), x.dtype)],
        compiler_params=_cparams(("arbitrary",), 32),
        name="final_norm",
    )(x, gain.reshape(1, d))


def _mm_kernel(epilogue, n_extra, a_ref, b_ref, *rest):
    extras = rest[:n_extra]
    o_ref = rest[n_extra]
    acc = jnp.dot(a_ref[...], b_ref[...], preferred_element_type=F32)
    if epilogue is not None:
        acc = epilogue(acc, *[e[...] for e in extras])
    o_ref[...] = acc.astype(o_ref.dtype)


def _layer_spec(w, layer, rows, tn, col_off=0):
    if w.ndim == 2:
        return pl.BlockSpec((rows, tn), lambda i, j: (0, j + col_off))
    return pl.BlockSpec((None, rows, tn), lambda i, j: (layer, 0, j + col_off))


def _matmul(a, b, *, tm, tn, out_dtype, k=None, a_kblk=0, epilogue=None, extras=(), vmem_mb=48,
            layer=None, name="matmul"):
    m = a.shape[0]
    kk, n = b.shape[-2:]
    if k is None:
        k = a.shape[1]
    assert k == kk and m % tm == 0 and n % tn == 0
    in_specs = [pl.BlockSpec((tm, k), lambda i, j: (i, a_kblk)), _layer_spec(b, layer, k, tn)]
    args = [a, b]
    for arr, off in extras:
        in_specs.append(pl.BlockSpec((tm, tn), functools.partial(lambda i, j, o: (i, j + o), o=off)))
        args.append(arr)
    return pl.pallas_call(
        functools.partial(_mm_kernel, epilogue, len(extras)),
        grid=(m // tm, n // tn),
        in_specs=in_specs,
        out_specs=pl.BlockSpec((tm, tn), lambda i, j: (i, j)),
        out_shape=jax.ShapeDtypeStruct((m, n), out_dtype),
        compiler_params=_cparams(("parallel", "parallel"), vmem_mb),
        name=name,
    )(*args)


def _merge_kernel(ya_ref, yb_ref, yc_ref, perm_ref, wa_ref, wv_ref, wg_ref, wc_ref, g0_ref, g1_ref, g2_ref,
                  o_ref, yb_scr):
    @pl.when(pl.program_id(1) == 0)
    def _():
        for b in range(yb_ref.shape[0] // S5_PERM_ROWS):
            rows = slice(b * S5_PERM_ROWS, (b + 1) * S5_PERM_ROWS)
            yb_scr[rows, :] = jnp.dot(perm_ref[1], yb_ref[rows, :], preferred_element_type=F32).astype(BF16)

    dot = lambda a, b: jnp.dot(a, b, preferred_element_type=F32)
    sigmoid = lambda v: 0.5 * jnp.tanh(0.5 * v) + 0.5
    sig = lambda ref: sigmoid(ref[...].astype(F32))
    yb = yb_scr[...]
    branch_a = dot(ya_ref[...], wa_ref[...])
    branch_b = dot(yb, wv_ref[...]) * sigmoid(dot(yb, wg_ref[...]))
    branch_c = dot(yc_ref[...], wc_ref[...])
    o_ref[...] = (sig(g0_ref) * branch_a + sig(g1_ref) * branch_b + sig(g2_ref) * branch_c).astype(o_ref.dtype)


def _merge_branches(ya, yb, yc, w_a, w_glu, w_c, layer, proj, *, tm, tn):
    m = ya.shape[0]
    n = w_a.shape[-1]
    nb = n // tn
    perm, _ = _s5_movers()
    assert tm % S5_PERM_ROWS == 0
    rows = lambda a: pl.BlockSpec((tm, a.shape[1]), lambda i, j: (i, 0))
    gate = lambda b: pl.BlockSpec((tm, tn), lambda i, j: (i, j + (COL_GATES + b * n) // tn))
    return pl.pallas_call(
        _merge_kernel,
        grid=(m // tm, nb),
        in_specs=[rows(ya), rows(yb), rows(yc), pl.BlockSpec(perm.shape, lambda i, j: (0, 0, 0)),
                  _layer_spec(w_a, layer, ya.shape[1], tn),
                  _layer_spec(w_glu, layer, yb.shape[1], tn), _layer_spec(w_glu, layer, yb.shape[1], tn, nb),
                  _layer_spec(w_c, layer, yc.shape[1], tn),
                  gate(0), gate(1), gate(2)],
        out_specs=pl.BlockSpec((tm, tn), lambda i, j: (i, j)),
        out_shape=jax.ShapeDtypeStruct((m, n), BF16),
        scratch_shapes=[pltpu.VMEM((tm, yb.shape[1]), BF16)],
        compiler_params=_cparams(("parallel", "arbitrary"), 48),
        name="merge_branches",
    )(ya, yb, yc, perm, w_a, w_glu, w_glu, w_c, proj, proj, proj)


HALO = 16


def _seq_flags(row0, nrows, seq_bounds):
    starts = [s for s, _ in seq_bounds]
    ends = [s + l for s, l in seq_bounds]
    is_start = functools.reduce(jnp.logical_or, [row0 == s for s in starts])
    is_end = functools.reduce(jnp.logical_or, [row0 + nrows == e for e in ends])
    return is_start, is_end


def _conv_taps(buf_ref, w_ref, b_ref, width, tr, cols=slice(None)):
    pad = width // 2
    acc = None
    for kk in range(width):
        term = buf_ref[pl.ds(HALO - pad + kk, tr), :] * w_ref[kk:kk + 1, cols]
        acc = term if acc is None else acc + term
    return acc + b_ref[:, cols]


def _norm_rows(x, gain):
    ms = jnp.mean(x * x, axis=-1, keepdims=True)
    return (x * lax.rsqrt(ms + NORM_EPS) * gain).astype(BF16)


def _fill_normed(hn_ref, xp_ref, xm_ref, xn_ref, gain_ref, tm, seq_bounds):
    is_start, is_end = _seq_flags(pl.program_id(0) * tm, tm, seq_bounds)
    gain = gain_ref[...]
    hn_ref[0:HALO, :] = jnp.where(is_start, 0.0, _norm_rows(xp_ref[...], gain)).astype(BF16)
    hn_ref[HALO:HALO + tm, :] = _norm_rows(xm_ref[...], gain)
    hn_ref[HALO + tm:HALO + tm + HALO, :] = jnp.where(is_end, 0.0, _norm_rows(xn_ref[...], gain)).astype(BF16)


def _row_halo_specs(tm, d, nrows):
    hb = tm // HALO
    last = nrows // HALO - 1
    return [pl.BlockSpec((HALO, d), lambda i, j: (jnp.maximum(i * hb - 1, 0), 0)),
            pl.BlockSpec((tm, d), lambda i, j: (i, 0)),
            pl.BlockSpec((HALO, d), lambda i, j: (jnp.minimum((i + 1) * hb, last), 0))]


ATT_BLOCK = 256


def _att_dilated():
    return tuple((gi, dil) for gi, (_, dil) in enumerate(ATT_PATTERNS) if dil > 1)


def _att_row_perms():
    out = []
    for _, dil in _att_dilated():
        dst = np.arange(ATT_BLOCK)[:, None]
        per = ATT_BLOCK // dil
        fwd = (dst % per) * dil + dst // per == np.arange(ATT_BLOCK)[None, :]
        out.append(np.stack([fwd, fwd.T]))
    return jnp.asarray(np.stack(out), BF16)


def _in_proj_kernel(tm, tn, seq_bounds, conv_lo, conv_hi, dil_ranges, u_range, xp_ref, xm_ref, xn_ref, gain_ref,
                    w_ref, wdt_ref, cw_ref, cb_ref, perm_ref, uperm_ref, o_ref, dt_ref, hn_ref, *acc_refs):
    j = pl.program_id(1)

    @pl.when(j == 0)
    def _():
        _fill_normed(hn_ref, xp_ref, xm_ref, xn_ref, gain_ref, tm, seq_bounds)
        dt_ref[...] = jnp.dot(hn_ref[HALO:HALO + tm, :], wdt_ref[...], preferred_element_type=F32)

    is_conv = jnp.logical_and(j >= conv_lo, j < conv_hi)

    @pl.when(is_conv)
    def _():
        part = tn // len(acc_refs)
        for c, acc_ref in enumerate(acc_refs):
            cols = slice(c * part, (c + 1) * part)
            acc_ref[...] = jnp.dot(hn_ref[...], w_ref[:, cols], preferred_element_type=F32)
            o_ref[:, cols] = _silu(_conv_taps(acc_ref, cw_ref, cb_ref, SSM_CONV, tm, cols)).astype(o_ref.dtype)

    regroup = [((lo, hi), ATT_BLOCK, functools.partial(lambda k: perm_ref[k, 0], k))
               for k, (lo, hi) in enumerate(dil_ranges)]
    regroup.append((u_range, S5_PERM_ROWS, lambda: uperm_ref[0]))
    conds = [jnp.logical_and(j >= lo, j < hi) for (lo, hi), _, _ in regroup]
    plain = jnp.logical_not(functools.reduce(jnp.logical_or, conds, is_conv))

    @pl.when(plain)
    def _():
        o_ref[...] = jnp.dot(hn_ref[HALO:HALO + tm, :], w_ref[...],
                             preferred_element_type=F32).astype(o_ref.dtype)

    for cond, (_, nrows, get_perm) in zip(conds, regroup):
        @pl.when(cond)
        def _(nrows=nrows, get_perm=get_perm):
            acc = jnp.dot(hn_ref[HALO:HALO + tm, :], w_ref[...], preferred_element_type=F32).astype(BF16)
            for b in range(tm // nrows):
                rows = slice(b * nrows, (b + 1) * nrows)
                o_ref[rows, :] = jnp.dot(get_perm(), acc[rows, :],
                                         preferred_element_type=F32).astype(o_ref.dtype)


def _in_proj(x, gain, w_main, w_dt, layer, conv_w, conv_b, seq_bounds, tm, tn=512):
    m, d = x.shape
    n = w_main.shape[-1]
    conv_lo, conv_hi = COL_XBC // tn, (COL_XBC + SSM_XBC) // tn
    cidx = lambda i, j: (0, jnp.clip(j - conv_lo, 0, conv_hi - conv_lo - 1))
    dil_ranges = tuple(((COL_QKV + gi * 3 * ATT_GW) // tn, (COL_QKV + (gi + 1) * 3 * ATT_GW) // tn)
                       for gi, _ in _att_dilated())
    u_range = (COL_U // tn, (COL_U + S5_WIDTH) // tn)
    perms = _att_row_perms()
    uperm, _ = _s5_movers()
    assert tm % S5_PERM_ROWS == 0 and tm % ATT_BLOCK == 0
    return pl.pallas_call(
        functools.partial(_in_proj_kernel, tm, tn, seq_bounds, conv_lo, conv_hi, dil_ranges, u_range),
        grid=(m // tm, n // tn),
        in_specs=_row_halo_specs(tm, d, m) + [
            pl.BlockSpec((1, d), lambda i, j: (0, 0)),
            _layer_spec(w_main, layer, d, tn),
            pl.BlockSpec((None, d, DT_COLS), lambda i, j: (layer, 0, 0)),
            pl.BlockSpec((SSM_CONV, tn), cidx),
            pl.BlockSpec((1, tn), cidx),
            pl.BlockSpec(perms.shape, lambda i, j: (0, 0, 0, 0)),
            pl.BlockSpec(uperm.shape, lambda i, j: (0, 0, 0))],
        out_specs=[pl.BlockSpec((tm, tn), lambda i, j: (i, j)),
                   pl.BlockSpec((tm, DT_COLS), lambda i, j: (i, 0))],
        out_shape=[jax.ShapeDtypeStruct((m, n), BF16), jax.ShapeDtypeStruct((m, DT_COLS), F32)],
        scratch_shapes=[pltpu.VMEM((tm + 2 * HALO, d), BF16), pltpu.VMEM((tm + 2 * HALO, tn // 2), F32),
                        pltpu.VMEM((tm + 2 * HALO, tn // 2), F32)],
        compiler_params=_cparams(("parallel", "arbitrary"), 56),
        name="in_proj",
    )(x, x, x, gain.reshape(1, d), w_main, w_dt, conv_w, conv_b.reshape(1, -1), perms, uperm)


def _ffn_up_kernel(tm, seq_bounds, xp_ref, xm_ref, xn_ref, gain_ref, wg_ref, wv_ref, cwg_ref, cbg_ref,
                   cwv_ref, cbv_ref, o_ref, hn_ref, sg_ref, sv_ref):
    @pl.when(pl.program_id(1) == 0)
    def _():
        _fill_normed(hn_ref, xp_ref, xm_ref, xn_ref, gain_ref, tm, seq_bounds)

    hn = hn_ref[...]
    sg_ref[...] = jnp.dot(hn, wg_ref[...], preferred_element_type=F32)
    sv_ref[...] = jnp.dot(hn, wv_ref[...], preferred_element_type=F32)
    gate = _conv_taps(sg_ref, cwg_ref, cbg_ref, FFN_CONV, tm)
    val = _conv_taps(sv_ref, cwv_ref, cbv_ref, FFN_CONV, tm)
    o_ref[...] = (_silu(gate) * val).astype(o_ref.dtype)


def _ffn_up(x, gain, w, layer, conv_w, conv_b, seq_bounds, tm, tn=512):
    m, d = x.shape
    nb = D_FF_PAD // tn
    col = lambda rows, off: pl.BlockSpec((rows, tn), functools.partial(lambda i, j, o: (0, j + o), o=off))
    cb2 = conv_b.reshape(1, -1)
    return pl.pallas_call(
        functools.partial(_ffn_up_kernel, tm, seq_bounds),
        grid=(m // tm, nb),
        in_specs=_row_halo_specs(tm, d, m) + [
            pl.BlockSpec((1, d), lambda i, j: (0, 0)),
            _layer_spec(w, layer, d, tn), _layer_spec(w, layer, d, tn, nb),
            col(FFN_CONV, 0), col(1, 0), col(FFN_CONV, nb), col(1, nb)],
        out_specs=pl.BlockSpec((tm, tn), lambda i, j: (i, j)),
        out_shape=jax.ShapeDtypeStruct((m, D_FF_PAD), BF16),
        scratch_shapes=[pltpu.VMEM((tm + 2 * HALO, d), BF16), pltpu.VMEM((tm + 2 * HALO, tn), F32),
                        pltpu.VMEM((tm + 2 * HALO, tn), F32)],
        compiler_params=_cparams(("parallel", "arbitrary"), 56),
        name="ffn_up",
    )(x, x, x, gain.reshape(1, d), w, w, conv_w, cb2, conv_w, cb2)


def _softplus(x):
    return jnp.maximum(x, 0.0) + jnp.log1p(jnp.exp(-jnp.abs(x)))


def _ssd_cumsums(dt_ref, bias_ref, alog_ref, d, row0=0):
    t = SSM_CHUNK
    dt = _softplus(dt_ref[row0:row0 + t, d * 128:(d + 1) * 128] + bias_ref[d])
    da = dt * (-jnp.exp(alog_ref[d]) * math.log2(math.e))
    row = lax.broadcasted_iota(jnp.int32, (t, t), 0)
    col = lax.broadcasted_iota(jnp.int32, (t, t), 1)
    tri = jnp.where((col <= row) if d == 0 else (col >= row), 1.0, 0.0).astype(BF16)
    return dt, _sel_dot(tri, da)


SSD_SCAN_CHUNKS = 2


def _ssd_state_kernel(nsteps, seq_bounds, xf_ref, bf_ref, dtf_ref, xb_ref, bb_ref, dtb_ref, bias_ref, alog_ref,
                      e_ref, hf_ref, hb_ref, h_ref):
    t = SSM_CHUNK
    n = SSM_STATE
    p = SSM_HEADDIM
    kc = SSD_SCAN_CHUNKS
    i = pl.program_id(0)

    @pl.when(i == 0)
    def _():
        h_ref[...] = jnp.zeros(h_ref.shape, F32)

    for d, (x_ref, b_ref, dt_ref, out_ref) in enumerate(((xf_ref, bf_ref, dtf_ref, hf_ref),
                                                         (xb_ref, bb_ref, dtb_ref, hb_ref))):
        blk = i if d == 0 else nsteps - 1 - i
        contrib = []
        for c in range(kc):
            r0 = c * t
            dt, cs = _ssd_cumsums(dt_ref, bias_ref, alog_ref, d, r0)
            tot = cs[t - 1:t, :] if d == 0 else cs[0:1, :]
            w = jnp.exp2(tot - cs) * dt
            etot = jnp.exp2(tot)
            spread = _dot_sel(jnp.concatenate([w, jnp.broadcast_to(etot, (SUBLANES, LANES))], axis=0), e_ref[...])
            xw = (x_ref[r0:r0 + t, :].astype(F32) * spread[0:t, :]).astype(BF16)
            per_group = []
            for g in range(SSM_GROUPS):
                cols = slice(g * SSM_HPG * p, (g + 1) * SSM_HPG * p)
                bg = b_ref[r0:r0 + t, g * n:(g + 1) * n]
                st = lax.dot_general(bg, xw[:, cols], (((0,), (0,)), ((), ())),
                                     preferred_element_type=F32)
                per_group.append((st, jnp.broadcast_to(spread[t:t + 1, cols], (n, SSM_HPG * p))))
            contrib.append(per_group)
        for c in (range(kc) if d == 0 else reversed(range(kc))):
            is_start, is_end = _seq_flags((blk * kc + c) * t, t, seq_bounds)
            reset = is_start if d == 0 else is_end
            for g in range(SSM_GROUPS):
                st, dec = contrib[c][g]
                hin = jnp.where(reset, 0.0, h_ref[d, g])
                out_ref[c, g] = hin.astype(BF16)
                h_ref[d, g] = hin * dec + st


def _ssd_states(proj, dtraw, dt_bias, a_log, seq_bounds):
    m = proj.shape[0]
    t = SSM_CHUNK
    kc = SSD_SCAN_CHUNKS
    nchunks = m // t
    nsteps = nchunks // kc
    hw = SSM_HPG * SSM_HEADDIM
    bw = SSM_GROUPS * SSM_STATE
    fwd = lambda i: i
    bwd = lambda i: nsteps - 1 - i
    xspec = lambda f: pl.BlockSpec((kc * t, SSM_INNER), lambda i: (f(i), COL_XBC // SSM_INNER))
    bspec = lambda f: pl.BlockSpec((kc * t, bw), lambda i: (f(i), (COL_XBC + SSM_INNER) // bw))
    dspec = lambda f: pl.BlockSpec((kc * t, DT_COLS), lambda i: (f(i), 0))
    hspec = lambda f: pl.BlockSpec((kc, SSM_GROUPS, SSM_STATE, hw), lambda i: (f(i), 0, 0, 0))
    const = pl.BlockSpec((2, 1, 128), lambda i: (0, 0, 0))
    spread = jnp.asarray(np.arange(LANES)[:, None] == (np.arange(SSM_INNER) // SSM_HEADDIM)[None, :], BF16)
    return pl.pallas_call(
        functools.partial(_ssd_state_kernel, nsteps, seq_bounds),
        grid=(nsteps,),
        in_specs=[xspec(fwd), bspec(fwd), dspec(fwd), xspec(bwd), bspec(bwd), dspec(bwd), const, const,
                  pl.BlockSpec(spread.shape, lambda i: (0, 0))],
        out_specs=[hspec(fwd), hspec(bwd)],
        out_shape=[jax.ShapeDtypeStruct((nchunks, SSM_GROUPS, SSM_STATE, hw), BF16)] * 2,
        scratch_shapes=[pltpu.VMEM((2, SSM_GROUPS, SSM_STATE, hw), F32)],
        compiler_params=_cparams(("arbitrary",), 32),
        name="ssd_states",
    )(proj, proj, dtraw, proj, proj, dtraw, dt_bias, a_log, spread)


def _ssd_out_kernel(x_ref, b_ref, c_ref, dt_ref, bias_ref, alog_ref, hf_ref, hb_ref, z_ref, dskip_ref,
                    gain_ref, o_ref, y_ref):
    t = SSM_CHUNK
    n = SSM_STATE
    p = SSM_HEADDIM
    dtf, csf = _ssd_cumsums(dt_ref, bias_ref, alog_ref, 0)
    dtb, csb = _ssd_cumsums(dt_ref, bias_ref, alog_ref, 1)
    csf_t, csb_t, dtf_t, dtb_t = csf.T, csb.T, dtf.T, dtb.T
    dts_t = dtf_t + dtb_t
    row = lax.broadcasted_iota(jnp.int32, (t, t), 0)
    col = lax.broadcasted_iota(jnp.int32, (t, t), 1)
    low = col <= row
    low_strict = col < row
    up_strict = col > row
    for g in range(SSM_GROUPS):
        bg = b_ref[:, g * n:(g + 1) * n]
        cg = c_ref[:, g * n:(g + 1) * n]
        cb = lax.dot_general(cg, bg, (((1,), (1,)), ((), ())), preferred_element_type=F32)
        cg_f = cg.astype(F32)
        for hh in range(SSM_HPG):
            h = g * SSM_HPG + hh
            colf = jnp.broadcast_to(csf[:, h:h + 1], (t, t))
            colb = jnp.broadcast_to(csb[:, h:h + 1], (t, t))
            seg = jnp.where(low, colf - jnp.broadcast_to(csf_t[h:h + 1, :], (t, t)),
                            colb - jnp.broadcast_to(csb_t[h:h + 1, :], (t, t)))
            coef = jnp.where(low_strict, jnp.broadcast_to(dtf_t[h:h + 1, :], (t, t)),
                             jnp.where(up_strict, jnp.broadcast_to(dtb_t[h:h + 1, :], (t, t)),
                                       jnp.broadcast_to(dts_t[h:h + 1, :], (t, t))))
            mh = jnp.exp2(seg) * cb * coef
            lhs = jnp.concatenate([mh, cg_f * jnp.exp2(colf), cg_f * jnp.exp2(colb)], axis=1).astype(BF16)
            rhs = jnp.concatenate([x_ref[:, h * p:(h + 1) * p],
                                   hf_ref[0, g, :, hh * p:(hh + 1) * p],
                                   hb_ref[0, g, :, hh * p:(hh + 1) * p]], axis=0)
            y_ref[:, h * p:(h + 1) * p] = jnp.dot(lhs, rhs, preferred_element_type=F32)
    y = y_ref[...] + x_ref[...].astype(F32) * dskip_ref[...]
    y = y * _silu(z_ref[...].astype(F32))
    ms = jnp.mean(y * y, axis=-1, keepdims=True)
    o_ref[...] = (y * lax.rsqrt(ms + NORM_EPS) * gain_ref[...]).astype(o_ref.dtype)


def _ssd_out(proj, dtraw, dt_bias, a_log, h_f, h_b, d_skip, norm_g):
    m = proj.shape[0]
    t = SSM_CHUNK
    w = SSM_INNER
    hw = SSM_HPG * SSM_HEADDIM
    bw = SSM_GROUPS * SSM_STATE
    hspec = pl.BlockSpec((1, SSM_GROUPS, SSM_STATE, hw), lambda i: (i, 0, 0, 0))
    return pl.pallas_call(
        _ssd_out_kernel,
        grid=(m // t,),
        in_specs=[pl.BlockSpec((t, w), lambda i: (i, COL_XBC // w)),
                  pl.BlockSpec((t, bw), lambda i: (i, (COL_XBC + w) // bw)),
                  pl.BlockSpec((t, bw), lambda i: (i, (COL_XBC + w) // bw + 1)),
                  pl.BlockSpec((t, DT_COLS), lambda i: (i, 0)),
                  pl.BlockSpec((2, 1, 128), lambda i: (0, 0, 0)),
                  pl.BlockSpec((2, 1, 128), lambda i: (0, 0, 0)),
                  hspec, hspec,
                  pl.BlockSpec((t, w), lambda i: (i, COL_Z // w)),
                  pl.BlockSpec((1, w), lambda i: (0, 0)),
                  pl.BlockSpec((1, w), lambda i: (0, 0))],
        out_specs=pl.BlockSpec((t, w), lambda i: (i, 0)),
        out_shape=jax.ShapeDtypeStruct((m, w), BF16),
        scratch_shapes=[pltpu.VMEM((t, w), F32)],
        compiler_params=_cparams(("parallel",), 32),
        name="ssd_out",
    )(proj, proj, proj, dtraw, dt_bias, a_log, h_f, h_b, proj,
      jnp.repeat(d_skip.astype(F32), SSM_HEADDIM).reshape(1, w), norm_g.reshape(1, w))


def _ssd(proj, dtraw, dt_bias, a_log, d_skip, norm_g, seq_bounds):
    pad = lambda v: jnp.pad(v.astype(F32), ((0, 0), (0, 128 - SSM_HEADS))).reshape(2, 1, 128)
    dt_bias, a_log = pad(dt_bias), pad(a_log)
    h_f, h_b = _ssd_states(proj, dtraw, dt_bias, a_log, seq_bounds)
    return _ssd_out(proj, dtraw, dt_bias, a_log, h_f, h_b, d_skip, norm_g)


def _s5_selectors():
    t, c = S5_CHUNK, S5_GROUP
    lag = np.arange(S5_LAGS)[:, None]
    tok = (np.arange(S5_CW) // c)[None, :]
    e_exit_f = (lag == t - 1 - tok)
    e_exit_b = (lag == tok)
    e_in_f = (lag == tok + 1)
    e_in_b = (lag == t - tok)
    lagidx = (np.arange(2 * S5_CW) // c)[None, :]
    rel = lagidx - (t - 1)
    inb = lagidx <= 2 * t - 2
    e_k_f = (lag == rel) & (rel >= 0) & inb
    e_k_b = (lag == -rel) & (rel <= 0) & inb
    tile = (np.arange(c)[:, None] == (np.arange(2 * S5_CW) % c)[None, :])
    sel = np.stack([np.concatenate([a, b], axis=1) for a, b in
                    ((e_exit_f, e_in_f), (e_exit_b, e_in_b))])
    selk = np.stack([e_k_f, e_k_b])
    return (jnp.asarray(sel, BF16), jnp.asarray(selk, BF16), jnp.asarray(tile, BF16))


def _s5_prep_kernel(acol_ref, arow_ref, ls_ref, b_ref, bt_ref, ct_ref, sel_ref, selk_ref, tile_ref,
                    sign_ref, w_ref, ws_ref, wo_ref, lpa_ref, lpb_ref):
    t = S5_CHUNK
    cw = S5_CW
    p = S5_STATE
    tile = tile_ref[...]
    lagf = lax.broadcasted_iota(jnp.int32, (p, S5_LAGS), 1).astype(F32)
    kt = jnp.zeros((S5_GROUP, 2 * cw), F32)
    for d in range(2):
        step = jnp.exp(ls_ref[0, 0, d])
        are = acol_ref[0, 0, d, 0]
        aim = acol_ref[0, 0, d, 1]
        mag = are * step
        th = aim * step
        amp = jnp.exp(lagf * mag)
        pwr = amp * jnp.cos(lagf * th)
        pwi = amp * jnp.sin(lagf * th)
        lbr = jnp.exp(mag) * jnp.cos(th)
        lbi = jnp.exp(mag) * jnp.sin(th)
        den = are * are + aim * aim
        cfr = ((lbr - 1.0) * are + lbi * aim) / den
        cfi = (lbi * are - (lbr - 1.0) * aim) / den
        bre = b_ref[0, 0, 0]
        bim = b_ref[0, 0, 1]
        bbr = cfr * bre - cfi * bim
        bbi = cfr * bim + cfi * bre
        step_r = step
        are_r = arow_ref[0, 0, d, 0:1, 0:p]
        aim_r = arow_ref[0, 0, d, 1:2, 0:p]
        mag_r = are_r * step_r
        th_r = aim_r * step_r
        lbr_r = jnp.exp(mag_r) * jnp.cos(th_r)
        lbi_r = jnp.exp(mag_r) * jnp.sin(th_r)
        den_r = are_r * are_r + aim_r * aim_r
        cfr_r = ((lbr_r - 1.0) * are_r + lbi_r * aim_r) / den_r
        cfi_r = (lbi_r * are_r - (lbr_r - 1.0) * aim_r) / den_r
        btr = bt_ref[0, 0, 0]
        bti = bt_ref[0, 0, 1]
        bbr_t = cfr_r * btr - cfi_r * bti
        bbi_t = cfr_r * bti + cfi_r * btr
        ctr = ct_ref[0, 0, d, 0]
        cti = ct_ref[0, 0, d, 1]

        sel = sel_ref[d]
        er = _dot_sel(pwr, sel)
        ei = _dot_sel(pwi, sel)
        tb_r = _dot_sel(bbr, tile[:, 0:cw])
        tb_i = _dot_sel(bbi, tile[:, 0:cw])
        tc_r = _dot_sel(ctr, tile)
        tc_i = _dot_sel(cti, tile)
        ws_ref[0, 0, d * 2 * p:d * 2 * p + p, :] = (er[:, 0:cw] * tb_r - ei[:, 0:cw] * tb_i).astype(BF16)
        ws_ref[0, 0, d * 2 * p + p:(d + 1) * 2 * p, :] = (er[:, 0:cw] * tb_i + ei[:, 0:cw] * tb_r).astype(BF16)
        zr = er[:, cw:] * tc_r[:, 0:cw] - ei[:, cw:] * tc_i[:, 0:cw]
        zi = er[:, cw:] * tc_i[:, 0:cw] + ei[:, cw:] * tc_r[:, 0:cw]
        wo_ref[0, 0, d * 2 * p:d * 2 * p + p, :] = zr.astype(BF16)
        wo_ref[0, 0, d * 2 * p + p:(d + 1) * 2 * p, :] = (-zi).astype(BF16)
        selk = selk_ref[d]
        kr, ki = _dot_sel(pwr, selk), _dot_sel(pwi, selk)
        qr = kr * tc_r - ki * tc_i
        qi = kr * tc_i + ki * tc_r
        kt = kt + _dot_f32(bbr_t, qr) - _dot_f32(bbi_t, qi)
        are2 = arow_ref[0, 0, d, 0:1, :]
        aim2 = arow_ref[0, 0, d, 1:2, :]
        ampt = jnp.exp(are2 * step_r * float(t))
        zr2 = ampt * jnp.cos(aim2 * step_r * float(t))
        zi2 = ampt * jnp.sin(aim2 * step_r * float(t))
        sign = sign_ref[...]
        for k in range(S5_SCAN_STEPS):
            lpa_ref[0, 0, d, k:k + 1, :] = zr2
            lpb_ref[0, 0, d, k:k + 1, :] = sign * zi2
            zr2, zi2 = zr2 * zr2 - zi2 * zi2, 2.0 * zr2 * zi2
    for s in range(t):
        off = (t - 1 - s) * S5_GROUP
        w_ref[0, 0, s * S5_GROUP:(s + 1) * S5_GROUP, :] = kt[:, off:off + cw].astype(BF16)


def _s5_prep(a_re, a_im, log_step, b_re, b_im, c_re, c_im):
    depth = a_re.shape[0]
    g, p, c = S5_GROUPS, S5_STATE, S5_GROUP
    a = jnp.stack([a_re, a_im], axis=2).astype(F32)
    a = a.transpose(0, 3, 1, 2, 4)
    acol = a[..., None]
    arow = jnp.concatenate([a, a], axis=-1)
    ls = log_step.astype(F32).transpose(0, 2, 1).reshape(depth, g, 2, 1, 1)
    b = jnp.stack([b_re, b_im], axis=2).astype(F32)
    bt = b.transpose(0, 1, 2, 4, 3)
    ct = jnp.stack([c_re, c_im], axis=3).astype(F32)
    ct = ct.transpose(0, 2, 1, 3, 5, 4)
    sel, selk, tile = _s5_selectors()
    sign = jnp.concatenate([-jnp.ones((1, p), F32), jnp.ones((1, p), F32)], axis=1)
    full = lambda shp: pl.BlockSpec(shp, lambda l, j: (0,) * len(shp))
    per = lambda shp: pl.BlockSpec((1, 1) + shp, lambda l, j: (l, j) + (0,) * len(shp))
    cw = S5_CW
    return pl.pallas_call(
        _s5_prep_kernel,
        grid=(depth, g),
        in_specs=[per((2, 2, p, 1)), per((2, 2, 2 * p)), per((2, 1, 1)), per((2, p, c)), per((2, c, p)),
                  per((2, 2, p, c)), full((2, S5_LAGS, 2 * cw)), full((2, S5_LAGS, 2 * cw)),
                  full((c, 2 * cw)), full((1, 2 * p))],
        out_specs=[per((cw, cw)), per((4 * p, cw)), per((4 * p, cw)),
                   per((2, S5_SCAN_STEPS, 2 * p)), per((2, S5_SCAN_STEPS, 2 * p))],
        out_shape=[jax.ShapeDtypeStruct((depth, g, cw, cw), BF16),
                   jax.ShapeDtypeStruct((depth, g, 4 * p, cw), BF16),
                   jax.ShapeDtypeStruct((depth, g, 4 * p, cw), BF16),
                   jax.ShapeDtypeStruct((depth, g, 2, S5_SCAN_STEPS, 2 * p), F32),
                   jax.ShapeDtypeStruct((depth, g, 2, S5_SCAN_STEPS, 2 * p), F32)],
        compiler_params=_cparams(("parallel", "parallel"), 32),
        name="s5_prep",
    )(acol, arow, ls, b, bt, ct, sel, selk, tile, sign)


def _gelu_tanh(x):
    return 0.5 * x * (1.0 + jnp.tanh(math.sqrt(2.0 / math.pi) * (x + 0.044715 * (x * x * x))))


LANES = 128
SUBLANES = 8


BF16_ROWS = 16
S5_PERM_ROWS = BF16_ROWS * S5_CHUNK
S5_GPB = LANES // S5_GROUP


def _s5_kernel(chunk_bounds, u_ref, pick_ref, w_ref, ws_ref, wo_ref, lpa_ref, lpb_ref,
               d_ref, o_ref, cat_ref, ug_ref, yg_ref):
    t = S5_CHUNK
    g = pl.program_id(1)
    r = u_ref.shape[0] // t
    nblk = r // BF16_ROWS
    p2 = 2 * S5_STATE
    qw = S5_GPB * LANES

    @pl.when(g == 0)
    def _():
        def body(b, carry):
            crow = pl.ds(pl.multiple_of(b * BF16_ROWS, BF16_ROWS), BF16_ROWS)
            for s in range(t):
                rows = pl.ds(pl.multiple_of(b * S5_PERM_ROWS + s * BF16_ROWS, BF16_ROWS), BF16_ROWS)
                cat_ref[crow, s * LANES:(s + 1) * LANES] = u_ref[rows, :]
            return carry
        lax.fori_loop(0, nblk, body, 0)
        for q in range(t // S5_GPB):
            picked = jnp.dot(cat_ref[:, q * qw:(q + 1) * qw], pick_ref[0],
                             preferred_element_type=F32).astype(BF16)
            for gg in range(S5_GPB):
                ug_ref[gg, :, q * LANES:(q + 1) * LANES] = picked[:, gg * LANES:(gg + 1) * LANES]

    u = ug_ref[g]
    uf = u.astype(F32)
    y = jnp.dot(u, w_ref[0, 0], preferred_element_type=F32)
    st = lax.dot_general(u, ws_ref[0, 0], (((1,), (1,)), ((), ())), preferred_element_type=F32)
    ridx = lax.broadcasted_iota(jnp.int32, (r, p2), 0)
    rloc = jnp.zeros((r, p2), jnp.int32)
    rlen = jnp.zeros((r, p2), jnp.int32)
    for s0, ln in chunk_bounds:
        inside = (ridx >= s0) & (ridx < s0 + ln)
        rloc = jnp.where(inside, ridx - s0, rloc)
        rlen = jnp.where(inside, ln, rlen)
    xin = []
    for d in range(2):
        x = st[:, d * p2:(d + 1) * p2]
        for k in range(S5_SCAN_STEPS):
            sh = 1 << k
            if d == 0:
                prev = jnp.where(rloc >= sh, pltpu.roll(x, sh, 0), 0.0)
            else:
                prev = jnp.where(rloc < rlen - sh, pltpu.roll(x, r - sh, 0), 0.0)
            x = (x + lpa_ref[0, 0, d, k:k + 1, :] * prev
                 + lpb_ref[0, 0, d, k:k + 1, :] * pltpu.roll(prev, S5_STATE, 1))
        if d == 0:
            xin.append(jnp.where(rloc >= 1, pltpu.roll(x, 1, 0), 0.0))
        else:
            xin.append(jnp.where(rloc < rlen - 1, pltpu.roll(x, r - 1, 0), 0.0))
    xin = jnp.concatenate(xin, axis=1).astype(BF16)
    y = y + jnp.dot(xin, wo_ref[0, 0], preferred_element_type=F32)
    y = _gelu_tanh(y + uf * d_ref[0])

    yg_ref[g] = y.astype(BF16)

    @pl.when(g == pl.num_programs(1) - 1)
    def _():
        for q in range(t // S5_GPB):
            lhs = jnp.concatenate([yg_ref[gg, :, q * LANES:(q + 1) * LANES] for gg in range(S5_GPB)], axis=1)
            cat_ref[:, q * qw:(q + 1) * qw] = jnp.dot(lhs, pick_ref[1],
                                                      preferred_element_type=F32).astype(BF16)

        def body(b, carry):
            crow = pl.ds(pl.multiple_of(b * BF16_ROWS, BF16_ROWS), BF16_ROWS)
            for s in range(t):
                rows = pl.ds(pl.multiple_of(b * S5_PERM_ROWS + s * BF16_ROWS, BF16_ROWS), BF16_ROWS)
                o_ref[rows, :] = cat_ref[crow, s * LANES:(s + 1) * LANES]
            return carry
        lax.fori_loop(0, nblk, body, 0)


def _s5_movers():
    dst = np.arange(S5_PERM_ROWS)[:, None]
    fwd = (dst % BF16_ROWS) * S5_CHUNK + dst // BF16_ROWS == np.arange(S5_PERM_ROWS)[None, :]
    src = np.arange(S5_GPB * LANES)
    s8, g, j = src // LANES, src % LANES // S5_GROUP, src % S5_GROUP
    pick = (g * LANES + s8 * S5_GROUP + j)[:, None] == src[None, :]
    return jnp.asarray(np.stack([fwd, fwd.T]), BF16), jnp.asarray(np.stack([pick, pick.T]), BF16)


def _s5(proj, tables, li, d_tiled, chunk_bounds):
    m = proj.shape[0]
    w, ws, wo, lpa, lpb = tables
    p = S5_STATE
    cw = S5_CW
    gpb = S5_GPB
    r = m // S5_CHUNK
    _, pick = _s5_movers()
    per = lambda shp: pl.BlockSpec((1, 1) + shp, lambda b, j: (li, b * gpb + j) + (0,) * len(shp))
    return pl.pallas_call(
        functools.partial(_s5_kernel, chunk_bounds),
        grid=(S5_GROUPS // gpb, gpb),
        in_specs=[pl.BlockSpec((m, LANES), lambda b, j: (0, COL_U // LANES + b)),
                  pl.BlockSpec((2, gpb * LANES, gpb * LANES), lambda b, j: (0, 0, 0)),
                  per((cw, cw)), per((4 * p, cw)), per((4 * p, cw)),
                  per((2, S5_SCAN_STEPS, 2 * p)), per((2, S5_SCAN_STEPS, 2 * p)),
                  pl.BlockSpec((1, 1, cw), lambda b, j: (b * gpb + j, 0, 0))],
        out_specs=pl.BlockSpec((m, LANES), lambda b, j: (0, b)),
        out_shape=jax.ShapeDtypeStruct((m, S5_WIDTH), BF16),
        scratch_shapes=[pltpu.VMEM((r, S5_CHUNK * LANES), BF16), pltpu.VMEM((gpb, r, cw), BF16),
                        pltpu.VMEM((gpb, r, cw), BF16)],
        compiler_params=_cparams(("parallel", "arbitrary"), 48),
        name="s5_mix",
    )(proj, pick, w, ws, wo, lpa, lpb, d_tiled)


def _t5_bucket(rel):
    half = REL_BUCKETS // 2
    exact = half // 2
    sign = (rel > 0).astype(np.int32) * half
    n = np.abs(rel)
    large = exact + (np.log(np.maximum(n, 1) / exact) / np.log(REL_MAX_DIST / exact)
                     * (half - exact)).astype(np.int32)
    large = np.minimum(large, half - 1)
    return sign + np.where(n < exact, n, large)


def _att_bias_kernel(gi, idx_ref, tbl_ref, o_ref):
    idx = idx_ref[...]
    for h in range(ATT_HPG):
        acc = jnp.full(idx.shape, NEG_INF, F32)
        for b in range(REL_BUCKETS):
            acc = jnp.where(idx == b, tbl_ref[b, gi * ATT_HPG + h], acc)
        o_ref[h] = acc


ATT_TQ = 256
ATT_QSUB = 128


def _att_tile(seq_bounds, dil):
    return min(ATT_TQ, min(l for _, l in seq_bounds) // dil)


def _att_bias(rel_bias, gi, tq):
    dil = ATT_PATTERNS[gi][1]
    tk = tq + 2 * ATT_HALF
    rel = (np.arange(tk)[None, :] - ATT_HALF) - np.arange(tq)[:, None]
    idx = np.where(np.abs(rel) <= ATT_HALF, _t5_bucket(rel * dil), -1).astype(np.int32)
    return pl.pallas_call(
        functools.partial(_att_bias_kernel, gi),
        in_specs=[pl.BlockSpec(memory_space=pltpu.VMEM), pl.BlockSpec(memory_space=pltpu.SMEM)],
        out_specs=pl.BlockSpec(memory_space=pltpu.VMEM),
        out_shape=jax.ShapeDtypeStruct((ATT_HPG, tq, tk), F32),
        name=f"attention_bias_{gi}",
    )(jnp.asarray(idx), rel_bias.astype(F32))


def _attn_kernel(tq, blk_bounds, q_ref, kp_ref, km_ref, kn_ref, vp_ref, vm_ref, vn_ref, bias_ref,
                 o_ref, lse_ref):
    jb = pl.program_id(1)
    is_first, is_last = _seq_flags(jb, 1, blk_bounds)
    flat = lambda ref: ref[...].reshape(-1, ref.shape[-1])
    q = flat(q_ref)
    kcat = jnp.concatenate([flat(kp_ref), flat(km_ref), flat(kn_ref)], axis=0)
    vcat = jnp.concatenate([flat(vp_ref), flat(vm_ref), flat(vn_ref)], axis=0)
    scale = ATT_HEAD_DIM ** -0.5
    qs = min(tq, ATT_QSUB)
    lane = lax.broadcasted_iota(jnp.int32, (qs, ATT_HEAD_DIM), 1)
    out_rows, lse_rows = [], []
    for a in range(0, tq, qs):
        keys = slice(a, a + qs + 2 * ATT_HALF)
        colk = a + lax.broadcasted_iota(jnp.int32, (qs, qs + 2 * ATT_HALF), 1)
        valid = jnp.logical_and(jnp.logical_or(colk >= ATT_HALF, jnp.logical_not(is_first)),
                                jnp.logical_or(colk < tq + ATT_HALF, jnp.logical_not(is_last)))
        lse_tile = jnp.zeros((qs, ATT_HEAD_DIM), F32)
        outs = []
        for h in range(ATT_HPG):
            sl = slice(h * ATT_HEAD_DIM, (h + 1) * ATT_HEAD_DIM)
            s = lax.dot_general(q[a:a + qs, sl], kcat[keys, sl], (((1,), (1,)), ((), ())),
                                preferred_element_type=F32)
            s = jnp.where(valid, s * scale + bias_ref[h, a:a + qs, keys], NEG_INF)
            mx = jnp.max(s, axis=-1, keepdims=True)
            pr = jnp.exp(s - mx)
            den = jnp.sum(pr, axis=-1, keepdims=True)
            o = jnp.dot(pr.astype(BF16), vcat[keys, sl], preferred_element_type=F32)
            outs.append((o / den).astype(o_ref.dtype))
            lse_tile = jnp.where(lane == h, mx + jnp.log(den), lse_tile)
        out_rows.append(jnp.concatenate(outs, axis=1))
        lse_rows.append(lse_tile)
    o_ref[...] = jnp.concatenate(out_rows, axis=0).reshape(o_ref.shape)
    lse_ref[...] = jnp.concatenate(lse_rows, axis=0).reshape(lse_ref.shape)


def _attention_group(proj, bias, gi, seq_bounds):
    window, dil = ATT_PATTERNS[gi]
    assert window // (2 * dil) == ATT_HALF
    m = proj.shape[0]
    tq = bias.shape[1]
    blk = ATT_BLOCK if dil > 1 else tq
    per = blk // dil
    nbq = tq // per
    hr = min(per, ATT_HALF)
    hb = ATT_HALF // hr
    nblocks = m // blk
    assert tq % per == 0 and nbq % hb == 0 and per % hr == 0
    assert all((s // blk) % nbq == 0 and (l // blk) % nbq == 0 for s, l in seq_bounds)
    blk_bounds = tuple((s // blk // nbq, l // blk // nbq) for s, l in seq_bounds)
    tk = tq + 2 * ATT_HALF
    cb = (COL_QKV + gi * 3 * ATT_GW) // ATT_GW
    src = proj.reshape(nblocks, blk, proj.shape[1])
    lasth = nblocks // hb - 1

    def main(col):
        return pl.BlockSpec((nbq, per, ATT_GW), lambda r, jb: (jb, r, cb + col))

    def prev(col):
        return pl.BlockSpec((hb, hr, ATT_GW),
                            lambda r, jb: (jnp.maximum(jb * (nbq // hb) - 1, 0), (r + 1) * (per // hr) - 1, cb + col))

    def nxt(col):
        return pl.BlockSpec((hb, hr, ATT_GW),
                            lambda r, jb: (jnp.minimum((jb + 1) * (nbq // hb), lasth), r * (per // hr), cb + col))

    o, lse = pl.pallas_call(
        functools.partial(_attn_kernel, tq, blk_bounds),
        grid=(dil, nblocks // nbq),
        in_specs=[main(0), prev(1), main(1), nxt(1), prev(2), main(2), nxt(2),
                  pl.BlockSpec((ATT_HPG, tq, tk), lambda r, jb: (0, 0, 0))],
        out_specs=[pl.BlockSpec((nbq, per, ATT_GW), lambda r, jb: (jb, r, 0)),
                   pl.BlockSpec((nbq, per, ATT_HEAD_DIM), lambda r, jb: (jb, r, 0))],
        out_shape=[jax.ShapeDtypeStruct((nblocks, blk, ATT_GW), BF16),
                   jax.ShapeDtypeStruct((nblocks, blk, ATT_HEAD_DIM), F32)],
        compiler_params=_cparams(("parallel", "parallel"), 32),
        name=f"dilated_attention_{gi}",
    )(src, src, src, src, src, src, src, bias)
    return o.reshape(m, ATT_GW), lse.reshape(m, ATT_HEAD_DIM)


def _att_combine_kernel(o0, l0, o1, l1, o2, l2, perm_ref, out_ref):
    tr = out_ref.shape[0]
    outs, lses = [o0[...]], [l0[...]]
    for k, (o_ref, l_ref) in enumerate(((o1, l1), (o2, l2))):
        inv = perm_ref[k, 1]
        ob, lb = [], []
        for b in range(tr // ATT_BLOCK):
            rows = slice(b * ATT_BLOCK, (b + 1) * ATT_BLOCK)
            ob.append(jnp.dot(inv, o_ref[rows, :], preferred_element_type=F32))
            lb.append(_sel_dot(inv, l_ref[rows, :]))
        outs.append(jnp.concatenate(ob, axis=0))
        lses.append(jnp.concatenate(lb, axis=0))
    a, b, c = lses
    mx = jnp.maximum(jnp.maximum(a, b), c)
    ea, eb, ec = jnp.exp(a - mx), jnp.exp(b - mx), jnp.exp(c - mx)
    inv = 1.0 / (ea + eb + ec)
    wa, wb, wc = ea * inv, eb * inv, ec * inv
    for h in range(ATT_HPG):
        sl = slice(h * ATT_HEAD_DIM, (h + 1) * ATT_HEAD_DIM)
        bc = lambda w: jnp.broadcast_to(w[:, h:h + 1], (tr, ATT_HEAD_DIM))
        out_ref[:, sl] = (bc(wa) * outs[0][:, sl].astype(F32) + bc(wb) * outs[1][:, sl].astype(F32)
                          + bc(wc) * outs[2][:, sl].astype(F32)).astype(out_ref.dtype)


def _att_combine(outs, lses, tr=512):
    assert [gi for gi, _ in _att_dilated()] == [1, 2]
    m, w = outs[0].shape
    ospec = pl.BlockSpec((tr, w), lambda i: (i, 0))
    lspec = pl.BlockSpec((tr, ATT_HEAD_DIM), lambda i: (i, 0))
    args = [x for pair in zip(outs, lses) for x in pair]
    perms = _att_row_perms()
    return pl.pallas_call(
        _att_combine_kernel,
        grid=(m // tr,),
        in_specs=[ospec, lspec] * 3 + [pl.BlockSpec(perms.shape, lambda i: (0, 0, 0, 0))],
        out_specs=ospec,
        out_shape=jax.ShapeDtypeStruct((m, w), BF16),
        compiler_params=_cparams(("parallel",), 32),
        name="attention_combine",
    )(*args, perms)


PACK_TILE = 512


def _pack_cols_kernel(short_tiles, *refs):
    parts, o_ref = refs[:-1], refs[-1]
    j = pl.program_id(2)
    valid = PACK_TILE
    for lo, hi, v in short_tiles:
        valid = jnp.where(jnp.logical_and(j >= lo, j < hi), v, valid)
    val = jnp.concatenate([p[0] for p in parts], axis=1)
    col = lax.broadcasted_iota(jnp.int32, val.shape, 1)
    o_ref[0] = jnp.where(col < valid, val, 0.0).astype(o_ref.dtype)


def _pack_cols(w, n_tiles, src_lane_block, short_tiles):
    depth, k, n = w.shape
    tk = min(2048, k)
    last = -(-n // LANES) - 1
    part = lambda q: pl.BlockSpec((1, tk, LANES), lambda l, i, j: (l, i, jnp.minimum(src_lane_block(j, q), last)))
    nparts = PACK_TILE // LANES
    return pl.pallas_call(
        functools.partial(_pack_cols_kernel, short_tiles),
        grid=(depth, k // tk, n_tiles),
        in_specs=[part(q) for q in range(nparts)],
        out_specs=pl.BlockSpec((1, tk, PACK_TILE), lambda l, i, j: (l, i, j)),
        out_shape=jax.ShapeDtypeStruct((depth, k, n_tiles * PACK_TILE), BF16),
        compiler_params=_cparams(("parallel", "parallel", "parallel"), 32),
        name="pack_weight_columns",
    )(*([w] * nparts))


def _pack_w_in_kernel(t_u, shift, a_ref, b_ref, o_ref, dt_ref):
    j = pl.program_id(2)

    @pl.when(j < t_u)
    def _():
        o_ref[...] = a_ref[...].T.astype(o_ref.dtype)

    @pl.when(j >= t_u)
    def _():
        rows = jnp.concatenate([a_ref[...], b_ref[...]], axis=0)[shift:shift + PACK_TILE, :]
        o_ref[...] = rows.T.astype(o_ref.dtype)

    @pl.when(j == t_u)
    def _():
        t = a_ref[0:LANES, :].T
        dt_ref[:, 0:LANES] = t.astype(dt_ref.dtype)
        dt_ref[:, LANES:2 * LANES] = pltpu.roll(t, LANES - SSM_HEADS, 1).astype(dt_ref.dtype)


def _pack_w_in(w_in):
    ng = len(ATT_PATTERNS)
    raw_dt = SSM_INNER + SSM_XBC
    shift = 2 * SSM_HEADS
    t_u = COL_U // PACK_TILE
    t_qkv = COL_QKV // PACK_TILE
    t_gates = COL_GATES // PACK_TILE
    assert raw_dt % PACK_TILE == 0 and COL_U == raw_dt

    def src_block(j):
        jj = jnp.clip(j - t_qkv, 0, 3 * ng - 1)
        return jnp.where(jnp.logical_and(j >= t_qkv, j < t_gates), t_qkv + (jj % 3) * ng + jj // 3, j)

    depth, d, n = w_in.shape
    tk = min(2048, d)
    last = -(-n // LANES) - 1
    sub = PACK_TILE // LANES
    w_t = jnp.swapaxes(w_in, 1, 2)
    return pl.pallas_call(
        functools.partial(_pack_w_in_kernel, t_u, shift),
        grid=(depth, d // tk, IN_COLS // PACK_TILE),
        in_specs=[pl.BlockSpec((None, PACK_TILE, tk), lambda l, i, j: (l, src_block(j), i)),
                  pl.BlockSpec((None, LANES, tk), lambda l, i, j: (l, jnp.minimum((src_block(j) + 1) * sub, last), i))],
        out_specs=[pl.BlockSpec((None, tk, PACK_TILE), lambda l, i, j: (l, i, j)),
                   pl.BlockSpec((None, tk, DT_COLS), lambda l, i, j: (l, i, 0))],
        out_shape=[jax.ShapeDtypeStruct((depth, d, IN_COLS), BF16), jax.ShapeDtypeStruct((depth, d, DT_COLS), BF16)],
        compiler_params=_cparams(("parallel", "parallel", "arbitrary"), 32),
        name="pack_w_in",
    )(w_t, w_t)


def _pack_ffn(w_up, conv_w, conv_b, w_down):
    nb = D_FF_PAD // PACK_TILE
    full, rem = divmod(D_FF, PACK_TILE)
    nparts = PACK_TILE // LANES
    assert nb == full + 1 and rem > 0 and D_FF % LANES == 0
    src = lambda j, q: jnp.where(j < nb, j * nparts + q, D_FF // LANES + (j - nb) * nparts + q)
    w_up_p = _pack_cols(w_up, 2 * nb, src, ((full, nb, rem), (nb + full, 2 * nb, rem)))
    padc = lambda a: jnp.pad(a, [(0, 0)] * (a.ndim - 1) + [(0, D_FF_PAD - D_FF)])
    both = lambda a: jnp.concatenate([padc(a[..., :D_FF]), padc(a[..., D_FF:])], axis=-1)
    w_down_p = jnp.pad(w_down, ((0, 0), (0, D_FF_PAD - D_FF), (0, 0))).astype(BF16)
    return w_up_p, both(conv_w).astype(F32), both(conv_b).astype(F32), w_down_p


def _trunk(x, seq_bounds, rel_bias, norm_mix, w_in, ssm_conv_w, ssm_conv_b, ssm_a_log, ssm_dt_bias, ssm_d,
           ssm_norm, ssm_w_out, s5_tables, s5_d, s5_w_glu, att_w_out, w_o, norm_ffn, w_up, ffn_conv_w,
           ffn_conv_b, w_down):
    m = x.shape[0]
    depth = w_in.shape[0]
    tm = min(1024, m)
    tm2 = min(512, m)
    chunk_bounds = tuple((s // S5_CHUNK, l // S5_CHUNK) for s, l in seq_bounds)
    att_bias = [_att_bias(rel_bias, gi, _att_tile(seq_bounds, dil)) for gi, (_, dil) in enumerate(ATT_PATTERNS)]
    resid = lambda acc, res: res + acc

    w_main, w_dt = _pack_w_in(w_in)
    w_up_p, cw_p, cb_p, w_down_p = _pack_ffn(w_up, ffn_conv_w, ffn_conv_b, w_down)
    ssm_w_out, s5_w_glu, att_w_out, w_o = (w.astype(BF16) for w in (ssm_w_out, s5_w_glu, att_w_out, w_o))

    for li in range(depth):
        proj, dtraw = _in_proj(x, norm_mix[li].astype(F32), w_main, w_dt, li, ssm_conv_w[li].astype(F32),
                               ssm_conv_b[li].astype(F32), seq_bounds, tm)

        y_a = _ssd(proj, dtraw, ssm_dt_bias[li], ssm_a_log[li], ssm_d[li], ssm_norm[li].astype(F32), seq_bounds)

        d_tiled = jnp.tile(s5_d[li].astype(F32).reshape(S5_GROUPS, 1, S5_GROUP), (1, 1, S5_CHUNK))
        y_b = _s5(proj, s5_tables, li, d_tiled, chunk_bounds)

        outs, lses = zip(*[_attention_group(proj, att_bias[gi], gi, seq_bounds)
                           for gi in range(len(ATT_PATTERNS))])
        y_c = _att_combine(outs, lses, tr=tm2)

        merged = _merge_branches(y_a, y_b, y_c, ssm_w_out, s5_w_glu, att_w_out, li, proj, tm=tm, tn=512)
        x = _matmul(merged, w_o, layer=li, tm=tm, tn=512, out_dtype=F32, epilogue=resid,
                    extras=((x, 0),), name="mix_out_proj")

        act = _ffn_up(x, norm_ffn[li].astype(F32), w_up_p, li, cw_p[li], cb_p[li], seq_bounds, tm)
        x = _matmul(act, w_down_p, layer=li, tm=tm, tn=512, out_dtype=F32, epilogue=resid, extras=((x, 0),),
                    vmem_mb=56, name="ffn_down")
    return x


def kernel(x_prompt, x_sample, rel_bias, norm_mix, w_in, ssm_conv_w, ssm_conv_b, ssm_a_log, ssm_dt_bias, ssm_d, ssm_norm, ssm_w_out, s5_a_re, s5_a_im, s5_log_step, s5_b_re, s5_b_im, s5_c_re, s5_c_im, s5_d, s5_w_glu, att_w_out, w_o, norm_ffn, w_up, ffn_conv_w, ffn_conv_b, w_down, final_norm):
    d = x_prompt.shape[-1]
    seq_bounds = []
    row = 0
    for arr in (x_prompt, x_sample):
        for _ in range(arr.shape[0]):
            seq_bounds.append((row, arr.shape[1]))
            row += arr.shape[1]
    seq_bounds = tuple(seq_bounds)
    x = jnp.concatenate([x_prompt.reshape(-1, d), x_sample.reshape(-1, d)], axis=0)
    s5_tables = _s5_prep(s5_a_re, s5_a_im, s5_log_step, s5_b_re, s5_b_im, s5_c_re, s5_c_im)
    y = _trunk(x, seq_bounds, rel_bias, norm_mix, w_in, ssm_conv_w, ssm_conv_b, ssm_a_log, ssm_dt_bias, ssm_d,
               ssm_norm, ssm_w_out, s5_tables, s5_d, s5_w_glu, att_w_out, w_o, norm_ffn, w_up, ffn_conv_w,
               ffn_conv_b, w_down)
    n_prompt = x_prompt.shape[0] * x_prompt.shape[1]
    y_prompt, y_sample = _final_norm(y, final_norm.astype(F32), n_prompt, tr=min(512, n_prompt))
    return (y_prompt.reshape(x_prompt.shape), y_sample.reshape(x_sample.shape))
```

```python
import functools
import math

import numpy as np
import jax
import jax.numpy as jnp
from jax import lax
from jax.experimental import pallas as pl
from jax.experimental.pallas import tpu as pltpu

F32 = jnp.float32
BF16 = jnp.bfloat16

D_MODEL = 2048
NORM_EPS = 1e-6
NEG_INF = -1e30

SSM_HEADDIM = 64
SSM_INNER = 1536
SSM_HEADS = 24
SSM_GROUPS = 4
SSM_HPG = SSM_HEADS // SSM_GROUPS
SSM_STATE = 128
SSM_CONV = 5
SSM_CHUNK = 128
SSM_XBC = 2560

S5_WIDTH = 1024
S5_GROUP = 16
S5_GROUPS = 64
S5_STATE = 64
S5_CHUNK = 32
S5_CW = S5_CHUNK * S5_GROUP
S5_LAGS = 128
S5_SCAN_STEPS = 8

ATT_HEAD_DIM = 128
ATT_HPG = 4
ATT_PATTERNS = ((128, 1), (512, 4), (2048, 16))
ATT_GW = ATT_HPG * ATT_HEAD_DIM
ATT_HALF = 64
REL_BUCKETS = 32
REL_MAX_DIST = 1024

D_FF = 5504
D_FF_PAD = 5632
FFN_CONV = 3

COL_Z = 0
COL_XBC = 1536
COL_U = 4096
COL_QKV = 5120
COL_GATES = 9728
IN_COLS = 15872
DT_COLS = 256

V7X_VMEM_BYTES = 64 * 1024 * 1024


def _cparams(sem, vmem_mb):
    assert vmem_mb * 1024 * 1024 <= V7X_VMEM_BYTES
    return pltpu.CompilerParams(dimension_semantics=sem, vmem_limit_bytes=vmem_mb * 1024 * 1024)


def _sigmoid(x):
    return 1.0 / (1.0 + jnp.exp(-x))


def _silu(x):
    return x * _sigmoid(x)


def _split3(x):
    hi = x.astype(BF16)
    r1 = x - hi.astype(F32)
    mid = r1.astype(BF16)
    lo = (r1 - mid.astype(F32)).astype(BF16)
    return hi, mid, lo


def _dot_sel(x, sel, pieces=2):
    d = lambda a: jnp.dot(a, sel, preferred_element_type=F32)
    return sum(d(piece) for piece in _split3(x)[:pieces])


def _sel_dot(sel, x):
    hi, mid, lo = _split3(x)
    d = lambda a: jnp.dot(sel, a, preferred_element_type=F32)
    return d(hi) + d(mid) + d(lo)


def _dot_f32(a, b):
    ah, am, _ = _split3(a)
    bh, bm, _ = _split3(b)
    d = lambda x, y: jnp.dot(x, y, preferred_element_type=F32)
    return d(ah, bh) + d(ah, bm) + d(am, bh)


def _final_norm_kernel(nb_first, x_ref, g_ref, first_ref, second_ref):
    x = x_ref[...]
    ms = jnp.mean(x * x, axis=-1, keepdims=True)
    y = x * lax.rsqrt(ms + NORM_EPS) * g_ref[...]
    i = pl.program_id(0)

    @pl.when(i < nb_first)
    def _():
        first_ref[...] = y

    @pl.when(i >= nb_first)
    def _():
        second_ref[...] = y


def _final_norm(x, gain, n_first, tr=512):
    m, d = x.shape
    assert n_first % tr == 0 and (m - n_first) % tr == 0 and 0 < n_first < m
    nb_first = n_first // tr
    return pl.pallas_call(
        functools.partial(_final_norm_kernel, nb_first),
        grid=(m // tr,),
        in_specs=[pl.BlockSpec((tr, d), lambda i: (i, 0)),
                  pl.BlockSpec((1, d), lambda i: (0, 0))],
        out_specs=[pl.BlockSpec((tr, d), lambda i: (jnp.minimum(i, nb_first - 1), 0)),
                   pl.BlockSpec((tr, d), lambda i: (jnp.maximum(i - nb_first, 0), 0))],
        out_shape=[jax.ShapeDtypeStruct((n_first, d), x.dtype), jax.ShapeDtypeStruct((m - n_first, d), x.dtype)],
        compiler_params=_cparams(("arbitrary",), 32),
        name="final_norm",
    )(x, gain.reshape(1, d))


def _mm_kernel(epilogue, n_extra, a_ref, b_ref, *rest):
    extras = rest[:n_extra]
    o_ref = rest[n_extra]
    acc = jnp.dot(a_ref[...], b_ref[...], preferred_element_type=F32)
    if epilogue is not None:
        acc = epilogue(acc, *[e[...] for e in extras])
    o_ref[...] = acc.astype(o_ref.dtype)


def _layer_spec(w, layer, rows, tn, col_off=0):
    if w.ndim == 2:
        return pl.BlockSpec((rows, tn), lambda i, j: (0, j + col_off))
    return pl.BlockSpec((None, rows, tn), lambda i, j: (layer, 0, j + col_off))


def _matmul(a, b, *, tm, tn, out_dtype, k=None, a_kblk=0, epilogue=None, extras=(), vmem_mb=48,
            layer=None, name="matmul"):
    m = a.shape[0]
    kk, n = b.shape[-2:]
    if k is None:
        k = a.shape[1]
    assert k == kk and m % tm == 0 and n % tn == 0
    in_specs = [pl.BlockSpec((tm, k), lambda i, j: (i, a_kblk)), _layer_spec(b, layer, k, tn)]
    args = [a, b]
    for arr, off in extras:
        in_specs.append(pl.BlockSpec((tm, tn), functools.partial(lambda i, j, o: (i, j + o), o=off)))
        args.append(arr)
    return pl.pallas_call(
        functools.partial(_mm_kernel, epilogue, len(extras)),
        grid=(m // tm, n // tn),
        in_specs=in_specs,
        out_specs=pl.BlockSpec((tm, tn), lambda i, j: (i, j)),
        out_shape=jax.ShapeDtypeStruct((m, n), out_dtype),
        compiler_params=_cparams(("parallel", "parallel"), vmem_mb),
        name=name,
    )(*args)


def _merge_kernel(ya_ref, yb_ref, yc_ref, perm_ref, wa_ref, wv_ref, wg_ref, wc_ref, g0_ref, g1_ref, g2_ref,
                  o_ref, yb_scr):
    @pl.when(pl.program_id(1) == 0)
    def _():
        for b in range(yb_ref.shape[0] // S5_PERM_ROWS):
            rows = slice(b * S5_PERM_ROWS, (b + 1) * S5_PERM_ROWS)
            yb_scr[rows, :] = jnp.dot(perm_ref[1], yb_ref[rows, :], preferred_element_type=F32).astype(BF16)

    dot = lambda a, b: jnp.dot(a, b, preferred_element_type=F32)
    sigmoid = lambda v: 0.5 * jnp.tanh(0.5 * v) + 0.5
    sig = lambda ref: sigmoid(ref[...].astype(F32))
    yb = yb_scr[...]
    branch_a = dot(ya_ref[...], wa_ref[...])
    branch_b = dot(yb, wv_ref[...]) * sigmoid(dot(yb, wg_ref[...]))
    branch_c = dot(yc_ref[...], wc_ref[...])
    o_ref[...] = (sig(g0_ref) * branch_a + sig(g1_ref) * branch_b + sig(g2_ref) * branch_c).astype(o_ref.dtype)


def _merge_branches(ya, yb, yc, w_a, w_glu, w_c, layer, proj, *, tm, tn):
    m = ya.shape[0]
    n = w_a.shape[-1]
    nb = n // tn
    perm, _ = _s5_movers()
    assert tm % S5_PERM_ROWS == 0
    rows = lambda a: pl.BlockSpec((tm, a.shape[1]), lambda i, j: (i, 0))
    gate = lambda b: pl.BlockSpec((tm, tn), lambda i, j: (i, j + (COL_GATES + b * n) // tn))
    return pl.pallas_call(
        _merge_kernel,
        grid=(m // tm, nb),
        in_specs=[rows(ya), rows(yb), rows(yc), pl.BlockSpec(perm.shape, lambda i, j: (0, 0, 0)),
                  _layer_spec(w_a, layer, ya.shape[1], tn),
                  _layer_spec(w_glu, layer, yb.shape[1], tn), _layer_spec(w_glu, layer, yb.shape[1], tn, nb),
                  _layer_spec(w_c, layer, yc.shape[1], tn),
                  gate(0), gate(1), gate(2)],
        out_specs=pl.BlockSpec((tm, tn), lambda i, j: (i, j)),
        out_shape=jax.ShapeDtypeStruct((m, n), BF16),
        scratch_shapes=[pltpu.VMEM((tm, yb.shape[1]), BF16)],
        compiler_params=_cparams(("parallel", "arbitrary"), 48),
        name="merge_branches",
    )(ya, yb, yc, perm, w_a, w_glu, w_glu, w_c, proj, proj, proj)


HALO = 16


def _seq_flags(row0, nrows, seq_bounds):
    starts = [s for s, _ in seq_bounds]
    ends = [s + l for s, l in seq_bounds]
    is_start = functools.reduce(jnp.logical_or, [row0 == s for s in starts])
    is_end = functools.reduce(jnp.logical_or, [row0 + nrows == e for e in ends])
    return is_start, is_end


def _conv_taps(buf_ref, w_ref, b_ref, width, tr, cols=slice(None)):
    pad = width // 2
    acc = None
    for kk in range(width):
        term = buf_ref[pl.ds(HALO - pad + kk, tr), :] * w_ref[kk:kk + 1, cols]
        acc = term if acc is None else acc + term
    return acc + b_ref[:, cols]


def _norm_rows(x, gain):
    ms = jnp.mean(x * x, axis=-1, keepdims=True)
    return (x * lax.rsqrt(ms + NORM_EPS) * gain).astype(BF16)


def _fill_normed(hn_ref, xp_ref, xm_ref, xn_ref, gain_ref, tm, seq_bounds):
    is_start, is_end = _seq_flags(pl.program_id(0) * tm, tm, seq_bounds)
    gain = gain_ref[...]
    hn_ref[0:HALO, :] = jnp.where(is_start, 0.0, _norm_rows(xp_ref[...], gain)).astype(BF16)
    hn_ref[HALO:HALO + tm, :] = _norm_rows(xm_ref[...], gain)
    hn_ref[HALO + tm:HALO + tm + HALO, :] = jnp.where(is_end, 0.0, _norm_rows(xn_ref[...], gain)).astype(BF16)


def _row_halo_specs(tm, d, nrows):
    hb = tm // HALO
    last = nrows // HALO - 1
    return [pl.BlockSpec((HALO, d), lambda i, j: (jnp.maximum(i * hb - 1, 0), 0)),
            pl.BlockSpec((tm, d), lambda i, j: (i, 0)),
            pl.BlockSpec((HALO, d), lambda i, j: (jnp.minimum((i + 1) * hb, last), 0))]


ATT_BLOCK = 256


def _att_dilated():
    return tuple((gi, dil) for gi, (_, dil) in enumerate(ATT_PATTERNS) if dil > 1)


def _att_row_perms():
    out = []
    for _, dil in _att_dilated():
        dst = np.arange(ATT_BLOCK)[:, None]
        per = ATT_BLOCK // dil
        fwd = (dst % per) * dil + dst // per == np.arange(ATT_BLOCK)[None, :]
        out.append(np.stack([fwd, fwd.T]))
    return jnp.asarray(np.stack(out), BF16)


def _in_proj_kernel(tm, tn, seq_bounds, conv_lo, conv_hi, dil_ranges, u_range, xp_ref, xm_ref, xn_ref, gain_ref,
                    w_ref, wdt_ref, cw_ref, cb_ref, perm_ref, uperm_ref, o_ref, dt_ref, hn_ref, *acc_refs):
    j = pl.program_id(1)

    @pl.when(j == 0)
    def _():
        _fill_normed(hn_ref, xp_ref, xm_ref, xn_ref, gain_ref, tm, seq_bounds)
        dt_ref[...] = jnp.dot(hn_ref[HALO:HALO + tm, :], wdt_ref[...], preferred_element_type=F32)

    is_conv = jnp.logical_and(j >= conv_lo, j < conv_hi)

    @pl.when(is_conv)
    def _():
        part = tn // len(acc_refs)
        for c, acc_ref in enumerate(acc_refs):
            cols = slice(c * part, (c + 1) * part)
            acc_ref[...] = jnp.dot(hn_ref[...], w_ref[:, cols], preferred_element_type=F32)
            o_ref[:, cols] = _silu(_conv_taps(acc_ref, cw_ref, cb_ref, SSM_CONV, tm, cols)).astype(o_ref.dtype)

    regroup = [((lo, hi), ATT_BLOCK, functools.partial(lambda k: perm_ref[k, 0], k))
               for k, (lo, hi) in enumerate(dil_ranges)]
    regroup.append((u_range, S5_PERM_ROWS, lambda: uperm_ref[0]))
    conds = [jnp.logical_and(j >= lo, j < hi) for (lo, hi), _, _ in regroup]
    plain = jnp.logical_not(functools.reduce(jnp.logical_or, conds, is_conv))

    @pl.when(plain)
    def _():
        o_ref[...] = jnp.dot(hn_ref[HALO:HALO + tm, :], w_ref[...],
                             preferred_element_type=F32).astype(o_ref.dtype)

    for cond, (_, nrows, get_perm) in zip(conds, regroup):
        @pl.when(cond)
        def _(nrows=nrows, get_perm=get_perm):
            acc = jnp.dot(hn_ref[HALO:HALO + tm, :], w_ref[...], preferred_element_type=F32).astype(BF16)
            for b in range(tm // nrows):
                rows = slice(b * nrows, (b + 1) * nrows)
                o_ref[rows, :] = jnp.dot(get_perm(), acc[rows, :],
                                         preferred_element_type=F32).astype(o_ref.dtype)


def _in_proj(x, gain, w_main, w_dt, layer, conv_w, conv_b, seq_bounds, tm, tn=512):
    m, d = x.shape
    n = w_main.shape[-1]
    conv_lo, conv_hi = COL_XBC // tn, (COL_XBC + SSM_XBC) // tn
    cidx = lambda i, j: (0, jnp.clip(j - conv_lo, 0, conv_hi - conv_lo - 1))
    dil_ranges = tuple(((COL_QKV + gi * 3 * ATT_GW) // tn, (COL_QKV + (gi + 1) * 3 * ATT_GW) // tn)
                       for gi, _ in _att_dilated())
    u_range = (COL_U // tn, (COL_U + S5_WIDTH) // tn)
    perms = _att_row_perms()
    uperm, _ = _s5_movers()
    assert tm % S5_PERM_ROWS == 0 and tm % ATT_BLOCK == 0
    return pl.pallas_call(
        functools.partial(_in_proj_kernel, tm, tn, seq_bounds, conv_lo, conv_hi, dil_ranges, u_range),
        grid=(m // tm, n // tn),
        in_specs=_row_halo_specs(tm, d, m) + [
            pl.BlockSpec((1, d), lambda i, j: (0, 0)),
            _layer_spec(w_main, layer, d, tn),
            pl.BlockSpec((None, d, DT_COLS), lambda i, j: (layer, 0, 0)),
            pl.BlockSpec((SSM_CONV, tn), cidx),
            pl.BlockSpec((1, tn), cidx),
            pl.BlockSpec(perms.shape, lambda i, j: (0, 0, 0, 0)),
            pl.BlockSpec(uperm.shape, lambda i, j: (0, 0, 0))],
        out_specs=[pl.BlockSpec((tm, tn), lambda i, j: (i, j)),
                   pl.BlockSpec((tm, DT_COLS), lambda i, j: (i, 0))],
        out_shape=[jax.ShapeDtypeStruct((m, n), BF16), jax.ShapeDtypeStruct((m, DT_COLS), F32)],
        scratch_shapes=[pltpu.VMEM((tm + 2 * HALO, d), BF16), pltpu.VMEM((tm + 2 * HALO, tn // 2), F32),
                        pltpu.VMEM((tm + 2 * HALO, tn // 2), F32)],
        compiler_params=_cparams(("parallel", "arbitrary"), 56),
        name="in_proj",
    )(x, x, x, gain.reshape(1, d), w_main, w_dt, conv_w, conv_b.reshape(1, -1), perms, uperm)


def _ffn_up_kernel(tm, seq_bounds, xp_ref, xm_ref, xn_ref, gain_ref, wg_ref, wv_ref, cwg_ref, cbg_ref,
                   cwv_ref, cbv_ref, o_ref, hn_ref, sg_ref, sv_ref):
    @pl.when(pl.program_id(1) == 0)
    def _():
        _fill_normed(hn_ref, xp_ref, xm_ref, xn_ref, gain_ref, tm, seq_bounds)

    hn = hn_ref[...]
    sg_ref[...] = jnp.dot(hn, wg_ref[...], preferred_element_type=F32)
    sv_ref[...] = jnp.dot(hn, wv_ref[...], preferred_element_type=F32)
    gate = _conv_taps(sg_ref, cwg_ref, cbg_ref, FFN_CONV, tm)
    val = _conv_taps(sv_ref, cwv_ref, cbv_ref, FFN_CONV, tm)
    o_ref[...] = (_silu(gate) * val).astype(o_ref.dtype)


def _ffn_up(x, gain, w, layer, conv_w, conv_b, seq_bounds, tm, tn=512):
    m, d = x.shape
    nb = D_FF_PAD // tn
    col = lambda rows, off: pl.BlockSpec((rows, tn), functools.partial(lambda i, j, o: (0, j + o), o=off))
    cb2 = conv_b.reshape(1, -1)
    return pl.pallas_call(
        functools.partial(_ffn_up_kernel, tm, seq_bounds),
        grid=(m // tm, nb),
        in_specs=_row_halo_specs(tm, d, m) + [
            pl.BlockSpec((1, d), lambda i, j: (0, 0)),
            _layer_spec(w, layer, d, tn), _layer_spec(w, layer, d, tn, nb),
            col(FFN_CONV, 0), col(1, 0), col(FFN_CONV, nb), col(1, nb)],
        out_specs=pl.BlockSpec((tm, tn), lambda i, j: (i, j)),
        out_shape=jax.ShapeDtypeStruct((m, D_FF_PAD), BF16),
        scratch_shapes=[pltpu.VMEM((tm + 2 * HALO, d), BF16), pltpu.VMEM((tm + 2 * HALO, tn), F32),
                        pltpu.VMEM((tm + 2 * HALO, tn), F32)],
        compiler_params=_cparams(("parallel", "arbitrary"), 56),
        name="ffn_up",
    )(x, x, x, gain.reshape(1, d), w, w, conv_w, cb2, conv_w, cb2)


def _softplus(x):
    return jnp.maximum(x, 0.0) + jnp.log1p(jnp.exp(-jnp.abs(x)))


def _ssd_cumsums(dt_ref, bias_ref, alog_ref, d, row0=0):
    t = SSM_CHUNK
    dt = _softplus(dt_ref[row0:row0 + t, d * 128:(d + 1) * 128] + bias_ref[d])
    da = dt * (-jnp.exp(alog_ref[d]) * math.log2(math.e))
    row = lax.broadcasted_iota(jnp.int32, (t, t), 0)
    col = lax.broadcasted_iota(jnp.int32, (t, t), 1)
    tri = jnp.where((col <= row) if d == 0 else (col >= row), 1.0, 0.0).astype(BF16)
    return dt, _sel_dot(tri, da)


SSD_SCAN_CHUNKS = 2
SSD_ROWS = 64


def _ssd_state_kernel(nsteps, seq_bounds, xf_ref, bf_ref, dtf_ref, xb_ref, bb_ref, dtb_ref, bias_ref, alog_ref,
                      e_ref, hf_ref, hb_ref, h_ref):
    t = SSM_CHUNK
    n = SSM_STATE
    p = SSM_HEADDIM
    kc = SSD_SCAN_CHUNKS
    i = pl.program_id(0)

    @pl.when(i == 0)
    def _():
        h_ref[...] = jnp.zeros(h_ref.shape, F32)

    for d, (x_ref, b_ref, dt_ref, out_ref) in enumerate(((xf_ref, bf_ref, dtf_ref, hf_ref),
                                                         (xb_ref, bb_ref, dtb_ref, hb_ref))):
        blk = i if d == 0 else nsteps - 1 - i
        contrib = []
        for c in range(kc):
            r0 = c * t
            dt, cs = _ssd_cumsums(dt_ref, bias_ref, alog_ref, d, r0)
            tot = cs[t - 1:t, :] if d == 0 else cs[0:1, :]
            w = jnp.exp2(tot - cs) * dt
            etot = jnp.exp2(tot)
            spread = _dot_sel(jnp.concatenate([w, jnp.broadcast_to(etot, (SUBLANES, LANES))], axis=0), e_ref[...])
            xw = (x_ref[r0:r0 + t, :].astype(F32) * spread[0:t, :]).astype(BF16)
            per_group = []
            for g in range(SSM_GROUPS):
                cols = slice(g * SSM_HPG * p, (g + 1) * SSM_HPG * p)
                bg = b_ref[r0:r0 + t, g * n:(g + 1) * n]
                st = lax.dot_general(bg, xw[:, cols], (((0,), (0,)), ((), ())),
                                     preferred_element_type=F32)
                per_group.append((st, jnp.broadcast_to(spread[t:t + 1, cols], (n, SSM_HPG * p))))
            contrib.append(per_group)
        for c in (range(kc) if d == 0 else reversed(range(kc))):
            is_start, is_end = _seq_flags((blk * kc + c) * t, t, seq_bounds)
            reset = is_start if d == 0 else is_end
            for g in range(SSM_GROUPS):
                st, dec = contrib[c][g]
                hin = jnp.where(reset, 0.0, h_ref[d, g])
                out_ref[c, g] = hin.astype(BF16)
                h_ref[d, g] = hin * dec + st


def _ssd_states(proj, dtraw, dt_bias, a_log, seq_bounds):
    m = proj.shape[0]
    t = SSM_CHUNK
    kc = SSD_SCAN_CHUNKS
    nchunks = m // t
    nsteps = nchunks // kc
    hw = SSM_HPG * SSM_HEADDIM
    bw = SSM_GROUPS * SSM_STATE
    fwd = lambda i: i
    bwd = lambda i: nsteps - 1 - i
    xspec = lambda f: pl.BlockSpec((kc * t, SSM_INNER), lambda i: (f(i), COL_XBC // SSM_INNER))
    bspec = lambda f: pl.BlockSpec((kc * t, bw), lambda i: (f(i), (COL_XBC + SSM_INNER) // bw))
    dspec = lambda f: pl.BlockSpec((kc * t, DT_COLS), lambda i: (f(i), 0))
    hspec = lambda f: pl.BlockSpec((kc, SSM_GROUPS, SSM_STATE, hw), lambda i: (f(i), 0, 0, 0))
    const = pl.BlockSpec((2, 1, 128), lambda i: (0, 0, 0))
    spread = jnp.asarray(np.arange(LANES)[:, None] == (np.arange(SSM_INNER) // SSM_HEADDIM)[None, :], BF16)
    return pl.pallas_call(
        functools.partial(_ssd_state_kernel, nsteps, seq_bounds),
        grid=(nsteps,),
        in_specs=[xspec(fwd), bspec(fwd), dspec(fwd), xspec(bwd), bspec(bwd), dspec(bwd), const, const,
                  pl.BlockSpec(spread.shape, lambda i: (0, 0))],
        out_specs=[hspec(fwd), hspec(bwd)],
        out_shape=[jax.ShapeDtypeStruct((nchunks, SSM_GROUPS, SSM_STATE, hw), BF16)] * 2,
        scratch_shapes=[pltpu.VMEM((2, SSM_GROUPS, SSM_STATE, hw), F32)],
        compiler_params=_cparams(("arbitrary",), 32),
        name="ssd_states",
    )(proj, proj, dtraw, proj, proj, dtraw, dt_bias, a_log, spread)


def _ssd_out_kernel(x_ref, b_ref, c_ref, dt_ref, bias_ref, alog_ref, hf_ref, hb_ref, z_ref, dskip_ref,
                    gain_ref, o_ref, y_ref):
    t = SSM_CHUNK
    n = SSM_STATE
    p = SSM_HEADDIM
    dtf, csf = _ssd_cumsums(dt_ref, bias_ref, alog_ref, 0)
    dtb, csb = _ssd_cumsums(dt_ref, bias_ref, alog_ref, 1)
    csf_t, csb_t, dtf_t, dtb_t = csf.T, csb.T, dtf.T, dtb.T
    dts_t = dtf_t + dtb_t
    row = lax.broadcasted_iota(jnp.int32, (t, t), 0)
    col = lax.broadcasted_iota(jnp.int32, (t, t), 1)
    low = col <= row
    low_strict = col < row
    up_strict = col > row
    for g in range(SSM_GROUPS):
        bg = b_ref[:, g * n:(g + 1) * n]
        cg = c_ref[:, g * n:(g + 1) * n]
        cb = lax.dot_general(cg, bg, (((1,), (1,)), ((), ())), preferred_element_type=F32)
        cg_f = cg.astype(F32)
        for hh in range(SSM_HPG):
            h = g * SSM_HPG + hh
            parts = []
            for r0 in range(0, t, SSD_ROWS):
                rs = slice(r0, r0 + SSD_ROWS)
                shp = (SSD_ROWS, t)
                colf = jnp.broadcast_to(csf[rs, h:h + 1], shp)
                colb = jnp.broadcast_to(csb[rs, h:h + 1], shp)
                seg = jnp.where(low[rs], colf - jnp.broadcast_to(csf_t[h:h + 1, :], shp),
                                colb - jnp.broadcast_to(csb_t[h:h + 1, :], shp))
                coef = jnp.where(low_strict[rs], jnp.broadcast_to(dtf_t[h:h + 1, :], shp),
                                 jnp.where(up_strict[rs], jnp.broadcast_to(dtb_t[h:h + 1, :], shp),
                                           jnp.broadcast_to(dts_t[h:h + 1, :], shp)))
                mh = jnp.exp2(seg) * cb[rs] * coef
                parts.append(jnp.concatenate([mh, cg_f[rs] * jnp.exp2(colf), cg_f[rs] * jnp.exp2(colb)],
                                             axis=1).astype(BF16))
            lhs = jnp.concatenate(parts, axis=0)
            rhs = jnp.concatenate([x_ref[:, h * p:(h + 1) * p],
                                   hf_ref[0, g, :, hh * p:(hh + 1) * p],
                                   hb_ref[0, g, :, hh * p:(hh + 1) * p]], axis=0)
            y_ref[:, h * p:(h + 1) * p] = jnp.dot(lhs, rhs, preferred_element_type=F32)
    y = y_ref[...] + x_ref[...].astype(F32) * dskip_ref[...]
    y = y * _silu(z_ref[...].astype(F32))
    ms = jnp.mean(y * y, axis=-1, keepdims=True)
    o_ref[...] = (y * lax.rsqrt(ms + NORM_EPS) * gain_ref[...]).astype(o_ref.dtype)


def _ssd_out(proj, dtraw, dt_bias, a_log, h_f, h_b, d_skip, norm_g):
    m = proj.shape[0]
    t = SSM_CHUNK
    w = SSM_INNER
    hw = SSM_HPG * SSM_HEADDIM
    bw = SSM_GROUPS * SSM_STATE
    hspec = pl.BlockSpec((1, SSM_GROUPS, SSM_STATE, hw), lambda i: (i, 0, 0, 0))
    return pl.pallas_call(
        _ssd_out_kernel,
        grid=(m // t,),
        in_specs=[pl.BlockSpec((t, w), lambda i: (i, COL_XBC // w)),
                  pl.BlockSpec((t, bw), lambda i: (i, (COL_XBC + w) // bw)),
                  pl.BlockSpec((t, bw), lambda i: (i, (COL_XBC + w) // bw + 1)),
                  pl.BlockSpec((t, DT_COLS), lambda i: (i, 0)),
                  pl.BlockSpec((2, 1, 128), lambda i: (0, 0, 0)),
                  pl.BlockSpec((2, 1, 128), lambda i: (0, 0, 0)),
                  hspec, hspec,
                  pl.BlockSpec((t, w), lambda i: (i, COL_Z // w)),
                  pl.BlockSpec((1, w), lambda i: (0, 0)),
                  pl.BlockSpec((1, w), lambda i: (0, 0))],
        out_specs=pl.BlockSpec((t, w), lambda i: (i, 0)),
        out_shape=jax.ShapeDtypeStruct((m, w), BF16),
        scratch_shapes=[pltpu.VMEM((t, w), F32)],
        compiler_params=_cparams(("parallel",), 32),
        name="ssd_out",
    )(proj, proj, proj, dtraw, dt_bias, a_log, h_f, h_b, proj,
      jnp.repeat(d_skip.astype(F32), SSM_HEADDIM).reshape(1, w), norm_g.reshape(1, w))


def _ssd(proj, dtraw, dt_bias, a_log, d_skip, norm_g, seq_bounds):
    pad = lambda v: jnp.pad(v.astype(F32), ((0, 0), (0, 128 - SSM_HEADS))).reshape(2, 1, 128)
    dt_bias, a_log = pad(dt_bias), pad(a_log)
    h_f, h_b = _ssd_states(proj, dtraw, dt_bias, a_log, seq_bounds)
    return _ssd_out(proj, dtraw, dt_bias, a_log, h_f, h_b, d_skip, norm_g)


def _s5_selectors():
    t, c = S5_CHUNK, S5_GROUP
    lag = np.arange(S5_LAGS)[:, None]
    tok = (np.arange(S5_CW) // c)[None, :]
    e_exit_f = (lag == t - 1 - tok)
    e_exit_b = (lag == tok)
    e_in_f = (lag == tok + 1)
    e_in_b = (lag == t - tok)
    lagidx = (np.arange(2 * S5_CW) // c)[None, :]
    rel = lagidx - (t - 1)
    inb = lagidx <= 2 * t - 2
    e_k_f = (lag == rel) & (rel >= 0) & inb
    e_k_b = (lag == -rel) & (rel <= 0) & inb
    tile = (np.arange(c)[:, None] == (np.arange(2 * S5_CW) % c)[None, :])
    sel = np.stack([np.concatenate([a, b], axis=1) for a, b in
                    ((e_exit_f, e_in_f), (e_exit_b, e_in_b))])
    selk = np.stack([e_k_f, e_k_b])
    return (jnp.asarray(sel, BF16), jnp.asarray(selk, BF16), jnp.asarray(tile, BF16))


def _s5_prep_kernel(acol_ref, arow_ref, ls_ref, b_ref, bt_ref, ct_ref, sel_ref, selk_ref, tile_ref,
                    sign_ref, w_ref, ws_ref, wo_ref, lpa_ref, lpb_ref):
    t = S5_CHUNK
    cw = S5_CW
    p = S5_STATE
    tile = tile_ref[...]
    lagf = lax.broadcasted_iota(jnp.int32, (p, S5_LAGS), 1).astype(F32)
    kt = jnp.zeros((S5_GROUP, 2 * cw), F32)
    for d in range(2):
        step = jnp.exp(ls_ref[0, 0, d])
        are = acol_ref[0, 0, d, 0]
        aim = acol_ref[0, 0, d, 1]
        mag = are * step
        th = aim * step
        amp = jnp.exp(lagf * mag)
        pwr = amp * jnp.cos(lagf * th)
        pwi = amp * jnp.sin(lagf * th)
        lbr = jnp.exp(mag) * jnp.cos(th)
        lbi = jnp.exp(mag) * jnp.sin(th)
        den = are * are + aim * aim
        cfr = ((lbr - 1.0) * are + lbi * aim) / den
        cfi = (lbi * are - (lbr - 1.0) * aim) / den
        bre = b_ref[0, 0, 0]
        bim = b_ref[0, 0, 1]
        bbr = cfr * bre - cfi * bim
        bbi = cfr * bim + cfi * bre
        step_r = step
        are_r = arow_ref[0, 0, d, 0:1, 0:p]
        aim_r = arow_ref[0, 0, d, 1:2, 0:p]
        mag_r = are_r * step_r
        th_r = aim_r * step_r
        lbr_r = jnp.exp(mag_r) * jnp.cos(th_r)
        lbi_r = jnp.exp(mag_r) * jnp.sin(th_r)
        den_r = are_r * are_r + aim_r * aim_r
        cfr_r = ((lbr_r - 1.0) * are_r + lbi_r * aim_r) / den_r
        cfi_r = (lbi_r * are_r - (lbr_r - 1.0) * aim_r) / den_r
        btr = bt_ref[0, 0, 0]
        bti = bt_ref[0, 0, 1]
        bbr_t = cfr_r * btr - cfi_r * bti
        bbi_t = cfr_r * bti + cfi_r * btr
        ctr = ct_ref[0, 0, d, 0]
        cti = ct_ref[0, 0, d, 1]

        sel = sel_ref[d]
        er = _dot_sel(pwr, sel)
        ei = _dot_sel(pwi, sel)
        tb_r = _dot_sel(bbr, tile[:, 0:cw])
        tb_i = _dot_sel(bbi, tile[:, 0:cw])
        tc_r = _dot_sel(ctr, tile)
        tc_i = _dot_sel(cti, tile)
        ws_ref[0, 0, d * 2 * p:d * 2 * p + p, :] = (er[:, 0:cw] * tb_r - ei[:, 0:cw] * tb_i).astype(BF16)
        ws_ref[0, 0, d * 2 * p + p:(d + 1) * 2 * p, :] = (er[:, 0:cw] * tb_i + ei[:, 0:cw] * tb_r).astype(BF16)
        zr = er[:, cw:] * tc_r[:, 0:cw] - ei[:, cw:] * tc_i[:, 0:cw]
        zi = er[:, cw:] * tc_i[:, 0:cw] + ei[:, cw:] * tc_r[:, 0:cw]
        wo_ref[0, 0, d * 2 * p:d * 2 * p + p, :] = zr.astype(BF16)
        wo_ref[0, 0, d * 2 * p + p:(d + 1) * 2 * p, :] = (-zi).astype(BF16)
        selk = selk_ref[d]
        kr, ki = _dot_sel(pwr, selk), _dot_sel(pwi, selk)
        qr = kr * tc_r - ki * tc_i
        qi = kr * tc_i + ki * tc_r
        kt = kt + _dot_f32(bbr_t, qr) - _dot_f32(bbi_t, qi)
        are2 = arow_ref[0, 0, d, 0:1, :]
        aim2 = arow_ref[0, 0, d, 1:2, :]
        ampt = jnp.exp(are2 * step_r * float(t))
        zr2 = ampt * jnp.cos(aim2 * step_r * float(t))
        zi2 = ampt * jnp.sin(aim2 * step_r * float(t))
        sign = sign_ref[...]
        for k in range(S5_SCAN_STEPS):
            lpa_ref[0, 0, d, k:k + 1, :] = zr2
            lpb_ref[0, 0, d, k:k + 1, :] = sign * zi2
            zr2, zi2 = zr2 * zr2 - zi2 * zi2, 2.0 * zr2 * zi2
    for s in range(t):
        off = (t - 1 - s) * S5_GROUP
        w_ref[0, 0, s * S5_GROUP:(s + 1) * S5_GROUP, :] = kt[:, off:off + cw].astype(BF16)


def _s5_prep(a_re, a_im, log_step, b_re, b_im, c_re, c_im):
    depth = a_re.shape[0]
    g, p, c = S5_GROUPS, S5_STATE, S5_GROUP
    a = jnp.stack([a_re, a_im], axis=2).astype(F32)
    a = a.transpose(0, 3, 1, 2, 4)
    acol = a[..., None]
    arow = jnp.concatenate([a, a], axis=-1)
    ls = log_step.astype(F32).transpose(0, 2, 1).reshape(depth, g, 2, 1, 1)
    b = jnp.stack([b_re, b_im], axis=2).astype(F32)
    bt = b.transpose(0, 1, 2, 4, 3)
    ct = jnp.stack([c_re, c_im], axis=3).astype(F32)
    ct = ct.transpose(0, 2, 1, 3, 5, 4)
    sel, selk, tile = _s5_selectors()
    sign = jnp.concatenate([-jnp.ones((1, p), F32), jnp.ones((1, p), F32)], axis=1)
    full = lambda shp: pl.BlockSpec(shp, lambda l, j: (0,) * len(shp))
    per = lambda shp: pl.BlockSpec((1, 1) + shp, lambda l, j: (l, j) + (0,) * len(shp))
    cw = S5_CW
    return pl.pallas_call(
        _s5_prep_kernel,
        grid=(depth, g),
        in_specs=[per((2, 2, p, 1)), per((2, 2, 2 * p)), per((2, 1, 1)), per((2, p, c)), per((2, c, p)),
                  per((2, 2, p, c)), full((2, S5_LAGS, 2 * cw)), full((2, S5_LAGS, 2 * cw)),
                  full((c, 2 * cw)), full((1, 2 * p))],
        out_specs=[per((cw, cw)), per((4 * p, cw)), per((4 * p, cw)),
                   per((2, S5_SCAN_STEPS, 2 * p)), per((2, S5_SCAN_STEPS, 2 * p))],
        out_shape=[jax.ShapeDtypeStruct((depth, g, cw, cw), BF16),
                   jax.ShapeDtypeStruct((depth, g, 4 * p, cw), BF16),
                   jax.ShapeDtypeStruct((depth, g, 4 * p, cw), BF16),
                   jax.ShapeDtypeStruct((depth, g, 2, S5_SCAN_STEPS, 2 * p), F32),
                   jax.ShapeDtypeStruct((depth, g, 2, S5_SCAN_STEPS, 2 * p), F32)],
        compiler_params=_cparams(("parallel", "parallel"), 32),
        name="s5_prep",
    )(acol, arow, ls, b, bt, ct, sel, selk, tile, sign)


def _gelu_tanh(x):
    return 0.5 * x * (1.0 + jnp.tanh(math.sqrt(2.0 / math.pi) * (x + 0.044715 * (x * x * x))))


LANES = 128
SUBLANES = 8


BF16_ROWS = 16
S5_PERM_ROWS = BF16_ROWS * S5_CHUNK
S5_GPB = LANES // S5_GROUP


def _s5_kernel(chunk_bounds, u_ref, pick_ref, w_ref, ws_ref, wo_ref, lpa_ref, lpb_ref,
               d_ref, o_ref, cat_ref, ug_ref, yg_ref):
    t = S5_CHUNK
    g = pl.program_id(1)
    r = u_ref.shape[0] // t
    nblk = r // BF16_ROWS
    p2 = 2 * S5_STATE
    qw = S5_GPB * LANES

    @pl.when(g == 0)
    def _():
        def body(b, carry):
            crow = pl.ds(pl.multiple_of(b * BF16_ROWS, BF16_ROWS), BF16_ROWS)
            for s in range(t):
                rows = pl.ds(pl.multiple_of(b * S5_PERM_ROWS + s * BF16_ROWS, BF16_ROWS), BF16_ROWS)
                cat_ref[crow, s * LANES:(s + 1) * LANES] = u_ref[rows, :]
            return carry
        lax.fori_loop(0, nblk, body, 0)
        for q in range(t // S5_GPB):
            picked = jnp.dot(cat_ref[:, q * qw:(q + 1) * qw], pick_ref[0],
                             preferred_element_type=F32).astype(BF16)
            for gg in range(S5_GPB):
                ug_ref[gg, :, q * LANES:(q + 1) * LANES] = picked[:, gg * LANES:(gg + 1) * LANES]

    u = ug_ref[g]
    uf = u.astype(F32)
    y = jnp.dot(u, w_ref[0, 0], preferred_element_type=F32)
    st = lax.dot_general(u, ws_ref[0, 0], (((1,), (1,)), ((), ())), preferred_element_type=F32)
    ridx = lax.broadcasted_iota(jnp.int32, (r, p2), 0)
    rloc = jnp.zeros((r, p2), jnp.int32)
    rlen = jnp.zeros((r, p2), jnp.int32)
    for s0, ln in chunk_bounds:
        inside = (ridx >= s0) & (ridx < s0 + ln)
        rloc = jnp.where(inside, ridx - s0, rloc)
        rlen = jnp.where(inside, ln, rlen)
    xin = []
    for d in range(2):
        x = st[:, d * p2:(d + 1) * p2]
        for k in range(S5_SCAN_STEPS):
            sh = 1 << k
            if d == 0:
                prev = jnp.where(rloc >= sh, pltpu.roll(x, sh, 0), 0.0)
            else:
                prev = jnp.where(rloc < rlen - sh, pltpu.roll(x, r - sh, 0), 0.0)
            x = (x + lpa_ref[0, 0, d, k:k + 1, :] * prev
                 + lpb_ref[0, 0, d, k:k + 1, :] * pltpu.roll(prev, S5_STATE, 1))
        if d == 0:
            xin.append(jnp.where(rloc >= 1, pltpu.roll(x, 1, 0), 0.0))
        else:
            xin.append(jnp.where(rloc < rlen - 1, pltpu.roll(x, r - 1, 0), 0.0))
    xin = jnp.concatenate(xin, axis=1).astype(BF16)
    y = y + jnp.dot(xin, wo_ref[0, 0], preferred_element_type=F32)
    y = _gelu_tanh(y + uf * d_ref[0])

    yg_ref[g] = y.astype(BF16)

    @pl.when(g == pl.num_programs(1) - 1)
    def _():
        for q in range(t // S5_GPB):
            lhs = jnp.concatenate([yg_ref[gg, :, q * LANES:(q + 1) * LANES] for gg in range(S5_GPB)], axis=1)
            cat_ref[:, q * qw:(q + 1) * qw] = jnp.dot(lhs, pick_ref[1],
                                                      preferred_element_type=F32).astype(BF16)

        def body(b, carry):
            crow = pl.ds(pl.multiple_of(b * BF16_ROWS, BF16_ROWS), BF16_ROWS)
            for s in range(t):
                rows = pl.ds(pl.multiple_of(b * S5_PERM_ROWS + s * BF16_ROWS, BF16_ROWS), BF16_ROWS)
                o_ref[rows, :] = cat_ref[crow, s * LANES:(s + 1) * LANES]
            return carry
        lax.fori_loop(0, nblk, body, 0)


def _s5_movers():
    dst = np.arange(S5_PERM_ROWS)[:, None]
    fwd = (dst % BF16_ROWS) * S5_CHUNK + dst // BF16_ROWS == np.arange(S5_PERM_ROWS)[None, :]
    src = np.arange(S5_GPB * LANES)
    s8, g, j = src // LANES, src % LANES // S5_GROUP, src % S5_GROUP
    pick = (g * LANES + s8 * S5_GROUP + j)[:, None] == src[None, :]
    return jnp.asarray(np.stack([fwd, fwd.T]), BF16), jnp.asarray(np.stack([pick, pick.T]), BF16)


def _s5(proj, tables, li, d_tiled, chunk_bounds):
    m = proj.shape[0]
    w, ws, wo, lpa, lpb = tables
    p = S5_STATE
    cw = S5_CW
    gpb = S5_GPB
    r = m // S5_CHUNK
    _, pick = _s5_movers()
    per = lambda shp: pl.BlockSpec((1, 1) + shp, lambda b, j: (li, b * gpb + j) + (0,) * len(shp))
    return pl.pallas_call(
        functools.partial(_s5_kernel, chunk_bounds),
        grid=(S5_GROUPS // gpb, gpb),
        in_specs=[pl.BlockSpec((m, LANES), lambda b, j: (0, COL_U // LANES + b)),
                  pl.BlockSpec((2, gpb * LANES, gpb * LANES), lambda b, j: (0, 0, 0)),
                  per((cw, cw)), per((4 * p, cw)), per((4 * p, cw)),
                  per((2, S5_SCAN_STEPS, 2 * p)), per((2, S5_SCAN_STEPS, 2 * p)),
                  pl.BlockSpec((1, 1, cw), lambda b, j: (b * gpb + j, 0, 0))],
        out_specs=pl.BlockSpec((m, LANES), lambda b, j: (0, b)),
        out_shape=jax.ShapeDtypeStruct((m, S5_WIDTH), BF16),
        scratch_shapes=[pltpu.VMEM((r, S5_CHUNK * LANES), BF16), pltpu.VMEM((gpb, r, cw), BF16),
                        pltpu.VMEM((gpb, r, cw), BF16)],
        compiler_params=_cparams(("parallel", "arbitrary"), 48),
        name="s5_mix",
    )(proj, pick, w, ws, wo, lpa, lpb, d_tiled)


def _t5_bucket(rel):
    half = REL_BUCKETS // 2
    exact = half // 2
    sign = (rel > 0).astype(np.int32) * half
    n = np.abs(rel)
    large = exact + (np.log(np.maximum(n, 1) / exact) / np.log(REL_MAX_DIST / exact)
                     * (half - exact)).astype(np.int32)
    large = np.minimum(large, half - 1)
    return sign + np.where(n < exact, n, large)


def _att_bias_kernel(gi, idx_ref, tbl_ref, o_ref):
    idx = idx_ref[...]
    for h in range(ATT_HPG):
        acc = jnp.full(idx.shape, NEG_INF, F32)
        for b in range(REL_BUCKETS):
            acc = jnp.where(idx == b, tbl_ref[b, gi * ATT_HPG + h], acc)
        o_ref[h] = acc


ATT_TQ = 256
ATT_QSUB = 128


def _att_tile(seq_bounds, dil):
    return min(ATT_TQ, min(l for _, l in seq_bounds) // dil)


def _att_bias(rel_bias, gi, tq):
    dil = ATT_PATTERNS[gi][1]
    tk = tq + 2 * ATT_HALF
    rel = (np.arange(tk)[None, :] - ATT_HALF) - np.arange(tq)[:, None]
    idx = np.where(np.abs(rel) <= ATT_HALF, _t5_bucket(rel * dil), -1).astype(np.int32)
    return pl.pallas_call(
        functools.partial(_att_bias_kernel, gi),
        in_specs=[pl.BlockSpec(memory_space=pltpu.VMEM), pl.BlockSpec(memory_space=pltpu.SMEM)],
        out_specs=pl.BlockSpec(memory_space=pltpu.VMEM),
        out_shape=jax.ShapeDtypeStruct((ATT_HPG, tq, tk), F32),
        name=f"attention_bias_{gi}",
    )(jnp.asarray(idx), rel_bias.astype(F32))


def _attn_kernel(tq, blk_bounds, q_ref, kp_ref, km_ref, kn_ref, vp_ref, vm_ref, vn_ref, bias_ref,
                 o_ref, lse_ref):
    jb = pl.program_id(1)
    is_first, is_last = _seq_flags(jb, 1, blk_bounds)
    flat = lambda ref: ref[...].reshape(-1, ref.shape[-1])
    q = flat(q_ref)
    kcat = jnp.concatenate([flat(kp_ref), flat(km_ref), flat(kn_ref)], axis=0)
    vcat = jnp.concatenate([flat(vp_ref), flat(vm_ref), flat(vn_ref)], axis=0)
    scale = ATT_HEAD_DIM ** -0.5
    qs = min(tq, ATT_QSUB)
    lane = lax.broadcasted_iota(jnp.int32, (qs, ATT_HEAD_DIM), 1)
    out_rows, lse_rows = [], []
    for a in range(0, tq, qs):
        keys = slice(a, a + qs + 2 * ATT_HALF)
        colk = a + lax.broadcasted_iota(jnp.int32, (qs, qs + 2 * ATT_HALF), 1)
        valid = jnp.logical_and(jnp.logical_or(colk >= ATT_HALF, jnp.logical_not(is_first)),
                                jnp.logical_or(colk < tq + ATT_HALF, jnp.logical_not(is_last)))
        lse_tile = jnp.zeros((qs, ATT_HEAD_DIM), F32)
        outs = []
        for h in range(ATT_HPG):
            sl = slice(h * ATT_HEAD_DIM, (h + 1) * ATT_HEAD_DIM)
            s = lax.dot_general(q[a:a + qs, sl], kcat[keys, sl], (((1,), (1,)), ((), ())),
                                preferred_element_type=F32)
            s = jnp.where(valid, s * scale + bias_ref[h, a:a + qs, keys], NEG_INF)
            mx = jnp.max(s, axis=-1, keepdims=True)
            pr = jnp.exp(s - mx)
            den = jnp.sum(pr, axis=-1, keepdims=True)
            o = jnp.dot(pr.astype(BF16), vcat[keys, sl], preferred_element_type=F32)
            outs.append((o / den).astype(o_ref.dtype))
            lse_tile = jnp.where(lane == h, mx + jnp.log(den), lse_tile)
        out_rows.append(jnp.concatenate(outs, axis=1))
        lse_rows.append(lse_tile)
    o_ref[...] = jnp.concatenate(out_rows, axis=0).reshape(o_ref.shape)
    lse_ref[...] = jnp.concatenate(lse_rows, axis=0).reshape(lse_ref.shape)


def _attention_group(proj, bias, gi, seq_bounds):
    window, dil = ATT_PATTERNS[gi]
    assert window // (2 * dil) == ATT_HALF
    m = proj.shape[0]
    tq = bias.shape[1]
    blk = ATT_BLOCK if dil > 1 else tq
    per = blk // dil
    nbq = tq // per
    hr = min(per, ATT_HALF)
    hb = ATT_HALF // hr
    nblocks = m // blk
    assert tq % per == 0 and nbq % hb == 0 and per % hr == 0
    assert all((s // blk) % nbq == 0 and (l // blk) % nbq == 0 for s, l in seq_bounds)
    blk_bounds = tuple((s // blk // nbq, l // blk // nbq) for s, l in seq_bounds)
    tk = tq + 2 * ATT_HALF
    cb = (COL_QKV + gi * 3 * ATT_GW) // ATT_GW
    src = proj.reshape(nblocks, blk, proj.shape[1])
    lasth = nblocks // hb - 1

    def main(col):
        return pl.BlockSpec((nbq, per, ATT_GW), lambda r, jb: (jb, r, cb + col))

    def prev(col):
        return pl.BlockSpec((hb, hr, ATT_GW),
                            lambda r, jb: (jnp.maximum(jb * (nbq // hb) - 1, 0), (r + 1) * (per // hr) - 1, cb + col))

    def nxt(col):
        return pl.BlockSpec((hb, hr, ATT_GW),
                            lambda r, jb: (jnp.minimum((jb + 1) * (nbq // hb), lasth), r * (per // hr), cb + col))

    o, lse = pl.pallas_call(
        functools.partial(_attn_kernel, tq, blk_bounds),
        grid=(dil, nblocks // nbq),
        in_specs=[main(0), prev(1), main(1), nxt(1), prev(2), main(2), nxt(2),
                  pl.BlockSpec((ATT_HPG, tq, tk), lambda r, jb: (0, 0, 0))],
        out_specs=[pl.BlockSpec((nbq, per, ATT_GW), lambda r, jb: (jb, r, 0)),
                   pl.BlockSpec((nbq, per, ATT_HEAD_DIM), lambda r, jb: (jb, r, 0))],
        out_shape=[jax.ShapeDtypeStruct((nblocks, blk, ATT_GW), BF16),
                   jax.ShapeDtypeStruct((nblocks, blk, ATT_HEAD_DIM), F32)],
        compiler_params=_cparams(("parallel", "parallel"), 32),
        name=f"dilated_attention_{gi}",
    )(src, src, src, src, src, src, src, bias)
    return o.reshape(m, ATT_GW), lse.reshape(m, ATT_HEAD_DIM)


def _att_combine_kernel(o0, l0, o1, l1, o2, l2, perm_ref, out_ref):
    tr = out_ref.shape[0]
    outs, lses = [o0[...]], [l0[...]]
    for k, (o_ref, l_ref) in enumerate(((o1, l1), (o2, l2))):
        inv = perm_ref[k, 1]
        ob, lb = [], []
        for b in range(tr // ATT_BLOCK):
            rows = slice(b * ATT_BLOCK, (b + 1) * ATT_BLOCK)
            ob.append(jnp.dot(inv, o_ref[rows, :], preferred_element_type=F32))
            lb.append(_sel_dot(inv, l_ref[rows, :]))
        outs.append(jnp.concatenate(ob, axis=0))
        lses.append(jnp.concatenate(lb, axis=0))
    a, b, c = lses
    mx = jnp.maximum(jnp.maximum(a, b), c)
    ea, eb, ec = jnp.exp(a - mx), jnp.exp(b - mx), jnp.exp(c - mx)
    inv = 1.0 / (ea + eb + ec)
    wa, wb, wc = ea * inv, eb * inv, ec * inv
    for h in range(ATT_HPG):
        sl = slice(h * ATT_HEAD_DIM, (h + 1) * ATT_HEAD_DIM)
        bc = lambda w: jnp.broadcast_to(w[:, h:h + 1], (tr, ATT_HEAD_DIM))
        out_ref[:, sl] = (bc(wa) * outs[0][:, sl].astype(F32) + bc(wb) * outs[1][:, sl].astype(F32)
                          + bc(wc) * outs[2][:, sl].astype(F32)).astype(out_ref.dtype)


def _att_combine(outs, lses, tr=512):
    assert [gi for gi, _ in _att_dilated()] == [1, 2]
    m, w = outs[0].shape
    ospec = pl.BlockSpec((tr, w), lambda i: (i, 0))
    lspec = pl.BlockSpec((tr, ATT_HEAD_DIM), lambda i: (i, 0))
    args = [x for pair in zip(outs, lses) for x in pair]
    perms = _att_row_perms()
    return pl.pallas_call(
        _att_combine_kernel,
        grid=(m // tr,),
        in_specs=[ospec, lspec] * 3 + [pl.BlockSpec(perms.shape, lambda i: (0, 0, 0, 0))],
        out_specs=ospec,
        out_shape=jax.ShapeDtypeStruct((m, w), BF16),
        compiler_params=_cparams(("parallel",), 32),
        name="attention_combine",
    )(*args, perms)


PACK_TILE = 512


def _pack_cols_kernel(short_tiles, *refs):
    parts, o_ref = refs[:-1], refs[-1]
    j = pl.program_id(2)
    valid = PACK_TILE
    for lo, hi, v in short_tiles:
        valid = jnp.where(jnp.logical_and(j >= lo, j < hi), v, valid)
    val = jnp.concatenate([p[0] for p in parts], axis=1)
    col = lax.broadcasted_iota(jnp.int32, val.shape, 1)
    o_ref[0] = jnp.where(col < valid, val, 0.0).astype(o_ref.dtype)


def _pack_cols(w, n_tiles, src_lane_block, short_tiles):
    depth, k, n = w.shape
    tk = min(2048, k)
    last = -(-n // LANES) - 1
    part = lambda q: pl.BlockSpec((1, tk, LANES), lambda l, i, j: (l, i, jnp.minimum(src_lane_block(j, q), last)))
    nparts = PACK_TILE // LANES
    return pl.pallas_call(
        functools.partial(_pack_cols_kernel, short_tiles),
        grid=(depth, k // tk, n_tiles),
        in_specs=[part(q) for q in range(nparts)],
        out_specs=pl.BlockSpec((1, tk, PACK_TILE), lambda l, i, j: (l, i, j)),
        out_shape=jax.ShapeDtypeStruct((depth, k, n_tiles * PACK_TILE), BF16),
        compiler_params=_cparams(("parallel", "parallel", "parallel"), 32),
        name="pack_weight_columns",
    )(*([w] * nparts))


def _pack_w_in_kernel(t_u, shift, a_ref, b_ref, o_ref, dt_ref):
    j = pl.program_id(2)

    @pl.when(j < t_u)
    def _():
        o_ref[...] = a_ref[...].T.astype(o_ref.dtype)

    @pl.when(j >= t_u)
    def _():
        rows = jnp.concatenate([a_ref[...], b_ref[...]], axis=0)[shift:shift + PACK_TILE, :]
        o_ref[...] = rows.T.astype(o_ref.dtype)

    @pl.when(j == t_u)
    def _():
        t = a_ref[0:LANES, :].T
        dt_ref[:, 0:LANES] = t.astype(dt_ref.dtype)
        dt_ref[:, LANES:2 * LANES] = pltpu.roll(t, LANES - SSM_HEADS, 1).astype(dt_ref.dtype)


def _pack_w_in(w_in):
    ng = len(ATT_PATTERNS)
    raw_dt = SSM_INNER + SSM_XBC
    shift = 2 * SSM_HEADS
    t_u = COL_U // PACK_TILE
    t_qkv = COL_QKV // PACK_TILE
    t_gates = COL_GATES // PACK_TILE
    assert raw_dt % PACK_TILE == 0 and COL_U == raw_dt

    def src_block(j):
        jj = jnp.clip(j - t_qkv, 0, 3 * ng - 1)
        return jnp.where(jnp.logical_and(j >= t_qkv, j < t_gates), t_qkv + (jj % 3) * ng + jj // 3, j)

    depth, d, n = w_in.shape
    tk = min(2048, d)
    last = -(-n // LANES) - 1
    sub = PACK_TILE // LANES
    w_t = jnp.swapaxes(w_in, 1, 2)
    return pl.pallas_call(
        functools.partial(_pack_w_in_kernel, t_u, shift),
        grid=(depth, d // tk, IN_COLS // PACK_TILE),
        in_specs=[pl.BlockSpec((None, PACK_TILE, tk), lambda l, i, j: (l, src_block(j), i)),
                  pl.BlockSpec((None, LANES, tk), lambda l, i, j: (l, jnp.minimum((src_block(j) + 1) * sub, last), i))],
        out_specs=[pl.BlockSpec((None, tk, PACK_TILE), lambda l, i, j: (l, i, j)),
                   pl.BlockSpec((None, tk, DT_COLS), lambda l, i, j: (l, i, 0))],
        out_shape=[jax.ShapeDtypeStruct((depth, d, IN_COLS), BF16), jax.ShapeDtypeStruct((depth, d, DT_COLS), BF16)],
        compiler_params=_cparams(("parallel", "parallel", "arbitrary"), 32),
        name="pack_w_in",
    )(w_t, w_t)


def _pack_ffn(w_up, conv_w, conv_b, w_down):
    nb = D_FF_PAD // PACK_TILE
    full, rem = divmod(D_FF, PACK_TILE)
    nparts = PACK_TILE // LANES
    assert nb == full + 1 and rem > 0 and D_FF % LANES == 0
    src = lambda j, q: jnp.where(j < nb, j * nparts + q, D_FF // LANES + (j - nb) * nparts + q)
    w_up_p = _pack_cols(w_up, 2 * nb, src, ((full, nb, rem), (nb + full, 2 * nb, rem)))
    padc = lambda a: jnp.pad(a, [(0, 0)] * (a.ndim - 1) + [(0, D_FF_PAD - D_FF)])
    both = lambda a: jnp.concatenate([padc(a[..., :D_FF]), padc(a[..., D_FF:])], axis=-1)
    w_down_p = jnp.pad(w_down, ((0, 0), (0, D_FF_PAD - D_FF), (0, 0))).astype(BF16)
    return w_up_p, both(conv_w).astype(F32), both(conv_b).astype(F32), w_down_p


def _trunk(x, seq_bounds, rel_bias, norm_mix, w_in, ssm_conv_w, ssm_conv_b, ssm_a_log, ssm_dt_bias, ssm_d,
           ssm_norm, ssm_w_out, s5_tables, s5_d, s5_w_glu, att_w_out, w_o, norm_ffn, w_up, ffn_conv_w,
           ffn_conv_b, w_down):
    m = x.shape[0]
    depth = w_in.shape[0]
    tm = min(1024, m)
    tm2 = min(512, m)
    chunk_bounds = tuple((s // S5_CHUNK, l // S5_CHUNK) for s, l in seq_bounds)
    att_bias = [_att_bias(rel_bias, gi, _att_tile(seq_bounds, dil)) for gi, (_, dil) in enumerate(ATT_PATTERNS)]
    resid = lambda acc, res: res + acc

    w_main, w_dt = _pack_w_in(w_in)
    w_up_p, cw_p, cb_p, w_down_p = _pack_ffn(w_up, ffn_conv_w, ffn_conv_b, w_down)
    ssm_w_out, s5_w_glu, att_w_out, w_o = (w.astype(BF16) for w in (ssm_w_out, s5_w_glu, att_w_out, w_o))

    for li in range(depth):
        proj, dtraw = _in_proj(x, norm_mix[li].astype(F32), w_main, w_dt, li, ssm_conv_w[li].astype(F32),
                               ssm_conv_b[li].astype(F32), seq_bounds, tm)

        y_a = _ssd(proj, dtraw, ssm_dt_bias[li], ssm_a_log[li], ssm_d[li], ssm_norm[li].astype(F32), seq_bounds)

        d_tiled = jnp.tile(s5_d[li].astype(F32).reshape(S5_GROUPS, 1, S5_GROUP), (1, 1, S5_CHUNK))
        y_b = _s5(proj, s5_tables, li, d_tiled, chunk_bounds)

        outs, lses = zip(*[_attention_group(proj, att_bias[gi], gi, seq_bounds)
                           for gi in range(len(ATT_PATTERNS))])
        y_c = _att_combine(outs, lses, tr=tm2)

        merged = _merge_branches(y_a, y_b, y_c, ssm_w_out, s5_w_glu, att_w_out, li, proj, tm=tm, tn=512)
        x = _matmul(merged, w_o, layer=li, tm=tm, tn=512, out_dtype=F32, epilogue=resid,
                    extras=((x, 0),), name="mix_out_proj")

        act = _ffn_up(x, norm_ffn[li].astype(F32), w_up_p, li, cw_p[li], cb_p[li], seq_bounds, tm)
        x = _matmul(act, w_down_p, layer=li, tm=tm, tn=512, out_dtype=F32, epilogue=resid, extras=((x, 0),),
                    vmem_mb=56, name="ffn_down")
    return x


def kernel(x_prompt, x_sample, rel_bias, norm_mix, w_in, ssm_conv_w, ssm_conv_b, ssm_a_log, ssm_dt_bias, ssm_d, ssm_norm, ssm_w_out, s5_a_re, s5_a_im, s5_log_step, s5_b_re, s5_b_im, s5_c_re, s5_c_im, s5_d, s5_w_glu, att_w_out, w_o, norm_ffn, w_up, ffn_conv_w, ffn_conv_b, w_down, final_norm):
    d = x_prompt.shape[-1]
    seq_bounds = []
    row = 0
    for arr in (x_prompt, x_sample):
        for _ in range(arr.shape[0]):
            seq_bounds.append((row, arr.shape[1]))
            row += arr.shape[1]
    seq_bounds = tuple(seq_bounds)
    x = jnp.concatenate([x_prompt.reshape(-1, d), x_sample.reshape(-1, d)], axis=0)
    s5_tables = _s5_prep(s5_a_re, s5_a_im, s5_log_step, s5_b_re, s5_b_im, s5_c_re, s5_c_im)
    y = _trunk(x, seq_bounds, rel_bias, norm_mix, w_in, ssm_conv_w, ssm_conv_b, ssm_a_log, ssm_dt_bias, ssm_d,
               ssm_norm, ssm_w_out, s5_tables, s5_d, s5_w_glu, att_w_out, w_o, norm_ffn, w_up, ffn_conv_w,
               ffn_conv_b, w_down)
    n_prompt = x_prompt.shape[0] * x_prompt.shape[1]
    y_prompt, y_sample = _final_norm(y, final_norm.astype(F32), n_prompt, tr=min(512, n_prompt))
    return (y_prompt.reshape(x_prompt.shape), y_sample.reshape(x_sample.shape))
```
